```python
import jax, jax.numpy as jnp
from jax import lax
import numpy as np

D_MODEL = 1024
BATCH = 4
SEQ = 4096
DEPTH = 4

N_MIXERS = 4
N_MEM = 256
EPS = 1e-6
NEG = -1e30
D_FF = ((8 * D_MODEL // 3 + 255) // 256) * 256

POOL_WINDOWS = (2, 4, 8, 16)
POOL_GROUP = D_MODEL // len(POOL_WINDOWS)

NSA_HEADS = 16
NSA_KV_HEADS = 4
NSA_HEAD_DIM = D_MODEL // NSA_HEADS
NSA_CMP_BLOCK = 32
NSA_CMP_STRIDE = 16
NSA_SEL_BLOCK = 64
NSA_N_SELECT = 16
NSA_WINDOW = 512
NSA_CMP_HIDDEN = 4 * NSA_HEAD_DIM
NSA_Q_BLOCK = 64
NSA_KV_WIDTH = NSA_KV_HEADS * NSA_HEAD_DIM
NSA_FORCE = 1e4
NSA_IN_WIDTH = D_MODEL + 6 * NSA_KV_WIDTH + 3 * NSA_HEADS

GLA_HEADS = 4
GLA_DK = D_MODEL // 2 // GLA_HEADS
GLA_DV = D_MODEL // GLA_HEADS
GLA_GATE_RANK = 16
GLA_TAU = 16.0
GLA_CHUNK = 64
GLA_IN_WIDTH = 2 * GLA_HEADS * GLA_DK + 2 * GLA_HEADS * GLA_DV + GLA_GATE_RANK

CONV_WIDTH = 31

XATTN_HEADS = 4
XATTN_HEAD_DIM = D_MODEL // XATTN_HEADS

kernel_name = "hybrid_pool_nsa_gla_conv_macaron"

F32 = jnp.float32


def n_layers_of(m):
    return len(range(m, DEPTH, N_MIXERS))


def rms_norm(x, g):
    xf = x.astype(F32)
    y = xf * lax.rsqrt(jnp.mean(xf * xf, axis=-1, keepdims=True) + EPS)
    return (y * g.astype(F32)).astype(x.dtype)


def swiglu_ffn(h, w_in, w_out):
    g, u = jnp.split(h @ w_in, 2, axis=-1)
    return (jax.nn.silu(g) * u) @ w_out


def pool_mixer(h, w, b, scale):
    B, S, D = h.shape
    hf = h.astype(F32)
    c = jnp.pad(jnp.cumsum(hf, axis=1), ((0, 0), (1, 0), (0, 0)))
    t = jnp.arange(S)
    groups = []
    for gi, win in enumerate(POOL_WINDOWS):
        cg = c[:, :, gi * POOL_GROUP:(gi + 1) * POOL_GROUP]
        start = jnp.maximum(t + 1 - win, 0)
        win_sum = cg[:, 1:] - jnp.take(cg, start, axis=1)
        cnt = (t + 1 - start).astype(F32)[None, :, None]
        groups.append(win_sum / cnt - hf[:, :, gi * POOL_GROUP:(gi + 1) * POOL_GROUP])
    p = jnp.stack(groups, axis=2)
    y = jnp.einsum('bsgc,gcd->bsgd', p, w.astype(F32)).reshape(B, S, D) + b.astype(F32)
    return (y * scale.astype(F32)).astype(h.dtype)


def nsa_mixer(h, w_in, cmp_pos, cmp_w1, cmp_w2, w_o):
    B, S, D = h.shape
    H, G, dh = NSA_HEADS, NSA_KV_HEADS, NSA_HEAD_DIM
    R = H // G
    SEL = NSA_SEL_BLOCK
    proj = (h @ w_in).astype(F32)
    splits = np.cumsum([D] + [NSA_KV_WIDTH] * 6).tolist()
    q, kc, vc, ks, vs, kw, vw, gates = jnp.split(proj, splits, axis=-1)
    q = q.reshape(B, S, G, R, dh).transpose(0, 2, 3, 1, 4) * dh ** -0.5
    gates = jax.nn.sigmoid(gates).reshape(B, S, G, R, 3).transpose(0, 2, 3, 1, 4)

    def kv_heads(a):
        return a.reshape(B, S, G, dh).transpose(0, 2, 1, 3)

    n_cmp = (S - NSA_CMP_BLOCK) // NSA_CMP_STRIDE + 1
    cmp_idx = np.arange(n_cmp)[:, None] * NSA_CMP_STRIDE + np.arange(NSA_CMP_BLOCK)[None, :]

    def compress(a, pos, w1, w2):
        blk = (a[:, :, cmp_idx] + pos.astype(F32)).reshape(B, G, n_cmp, NSA_CMP_BLOCK * dh)
        return jax.nn.gelu(blk @ w1.astype(F32)) @ w2.astype(F32)

    k_cmp = compress(kv_heads(kc), cmp_pos[0], cmp_w1[0], cmp_w2[0])
    v_cmp = compress(kv_heads(vc), cmp_pos[1], cmp_w1[1], cmp_w2[1])
    cmp_end = jnp.asarray(cmp_idx[:, -1])

    n_slc = S // SEL
    cs = np.arange(n_cmp) * NSA_CMP_STRIDE
    ss = np.arange(n_slc) * SEL
    ov = np.clip(np.minimum(cs[:, None] + NSA_CMP_BLOCK, ss[None, :] + SEL)
                 - np.maximum(cs[:, None], ss[None, :]), 0, None) / NSA_CMP_BLOCK
    ov = jnp.asarray(ov, F32)
    k_eff = min(NSA_N_SELECT, n_slc)
    blk_ids = jnp.arange(n_slc)

    ks_blk = kv_heads(ks).reshape(B, G, n_slc, SEL, dh)
    vs_blk = kv_heads(vs).reshape(B, G, n_slc, SEL, dh)
    kw_pad = jnp.pad(kv_heads(kw), ((0, 0), (0, 0), (NSA_WINDOW, 0), (0, 0)))
    vw_pad = jnp.pad(kv_heads(vw), ((0, 0), (0, 0), (NSA_WINDOW, 0), (0, 0)))
    bi = jnp.arange(B)[:, None, None, None]
    gi = jnp.arange(G)[None, :, None, None]
    Qb = min(NSA_Q_BLOCK, S)
    n_qb = S // Qb

    def block(qi):
        s0 = qi * Qb
        t = s0 + jnp.arange(Qb)
        qb = lax.dynamic_slice_in_dim(q, s0, Qb, axis=3)
        gb = lax.dynamic_slice_in_dim(gates, s0, Qb, axis=3)
        vis = cmp_end[None, :] <= t[:, None]
        sc = jnp.where(vis, jnp.einsum('bgrqd,bgnd->bgrqn', qb, k_cmp), NEG)
        p_cmp = jax.nn.softmax(sc, axis=-1) * jnp.any(vis, axis=-1)[:, None]
        o_cmp = jnp.einsum('bgrqn,bgnd->bgrqd', p_cmp, v_cmp)
        imp = jnp.einsum('bgrqn,nm->bgqm', p_cmp, ov)
        cur = t // SEL
        forced = (blk_ids[None, :] == 0) | (blk_ids[None, :] == cur[:, None]) | (blk_ids[None, :] == cur[:, None] - 1)
        imp = jnp.where(forced, NSA_FORCE, imp)
        imp = jnp.where(blk_ids[None, :] * SEL <= t[:, None], imp, -1.0)
        _, idx = lax.top_k(imp, k_eff)
        ksel = ks_blk[bi, gi, idx].reshape(B, G, Qb, k_eff * SEL, dh)
        vsel = vs_blk[bi, gi, idx].reshape(B, G, Qb, k_eff * SEL, dh)
        pos = (idx[..., None] * SEL + jnp.arange(SEL)).reshape(B, G, Qb, k_eff * SEL)
        sc = jnp.einsum('bgrqd,bgqkd->bgrqk', qb, ksel)
        sc = jnp.where((pos <= t[:, None])[:, :, None], sc, NEG)
        o_slc = jnp.einsum('bgrqk,bgqkd->bgrqd', jax.nn.softmax(sc, axis=-1), vsel)
        kwin = lax.dynamic_slice_in_dim(kw_pad, s0, NSA_WINDOW + Qb, axis=2)
        vwin = lax.dynamic_slice_in_dim(vw_pad, s0, NSA_WINDOW + Qb, axis=2)
        wpos = s0 - NSA_WINDOW + jnp.arange(NSA_WINDOW + Qb)
        wmask = (wpos[None, :] <= t[:, None]) & (wpos[None, :] > t[:, None] - NSA_WINDOW) & (wpos[None, :] >= 0)
        sc = jnp.where(wmask, jnp.einsum('bgrqd,bgkd->bgrqk', qb, kwin), NEG)
        o_win = jnp.einsum('bgrqk,bgkd->bgrqd', jax.nn.softmax(sc, axis=-1), vwin)
        return gb[..., 0:1] * o_cmp + gb[..., 1:2] * o_slc + gb[..., 2:3] * o_win

    o = lax.map(block, jnp.arange(n_qb))
    o = o.transpose(1, 0, 4, 2, 3, 5).reshape(B, S, H * dh)
    return o.astype(h.dtype) @ w_o


def gla_mixer(h, w_in, w_gate_up, b_gate, norm_g, w_o):
    B, S, D = h.shape
    Hh, dk, dv, C = GLA_HEADS, GLA_DK, GLA_DV, GLA_CHUNK
    proj = (h @ w_in).astype(F32)
    qw, vw = Hh * dk, Hh * dv
    q, k, v, og, gdown = jnp.split(proj, [qw, 2 * qw, 2 * qw + vw, 2 * qw + 2 * vw], axis=-1)
    log_a = jax.nn.log_sigmoid(gdown @ w_gate_up.astype(F32) + b_gate.astype(F32)) / GLA_TAU
    nc = S // C

    def heads(a, d):
        return a.reshape(B, nc, C, Hh, d).transpose(0, 3, 1, 2, 4)

    q = heads(q, dk) * dk ** -0.5
    k = heads(k, dk)
    v = heads(v, dv)
    b = jnp.cumsum(heads(log_a, dk), axis=3)
    b_last = b[:, :, :, -1:, :]
    q_t = q * jnp.exp(b)
    k_t = k * jnp.exp(-b)
    causal = jnp.tril(jnp.ones((C, C), dtype=bool))
    A = jnp.where(causal, jnp.einsum('bhncd,bhnsd->bhncs', q_t, k_t), 0.0)
    o_intra = jnp.einsum('bhncs,bhnsv->bhncv', A, v)
    kv = jnp.einsum('bhncd,bhncv->bhndv', k * jnp.exp(b_last - b), v)
    decay = jnp.exp(b_last[:, :, :, 0, :])

    def step(state, inp):
        kv_n, dec_n = inp
        return dec_n[..., None] * state + kv_n, state

    init = jnp.zeros((B, Hh, dk, dv), F32)
    _, s_prev = lax.scan(step, init, (kv.transpose(2, 0, 1, 3, 4), decay.transpose(2, 0, 1, 3)))
    s_prev = s_prev.transpose(1, 2, 0, 3, 4)
    o = o_intra + jnp.einsum('bhncd,bhndv->bhncv', q_t, s_prev)
    o = rms_norm(o.transpose(0, 2, 3, 1, 4).reshape(B, S, Hh, dv), norm_g)
    o = o.reshape(B, S, Hh * dv) * jax.nn.silu(og)
    return o.astype(h.dtype) @ w_o


def conv_mixer(h, w_in, b_in, w_dw, b_dw, ln_g, ln_b, w_out, b_out):
    D = h.shape[-1]
    a, gate = jnp.split(h @ w_in + b_in, 2, axis=-1)
    u = a * jax.nn.sigmoid(gate)
    u = jnp.pad(u, ((0, 0), (CONV_WIDTH - 1, 0), (0, 0)))
    u = lax.conv_general_dilated(u, w_dw[:, None, :].astype(u.dtype), window_strides=(1,), padding='VALID',
                                 dimension_numbers=('NWC', 'WIO', 'NWC'), feature_group_count=D) + b_dw
    uf = u.astype(F32)
    mu = jnp.mean(uf, axis=-1, keepdims=True)
    var = jnp.mean(jnp.square(uf - mu), axis=-1, keepdims=True)
    uf = (uf - mu) * lax.rsqrt(var + EPS) * ln_g.astype(F32) + ln_b.astype(F32)
    return jax.nn.silu(uf).astype(h.dtype) @ w_out + b_out


def memory_cross_attention(h, mem_n, w_q, w_kv, w_o):
    B, S, D = h.shape
    N = mem_n.shape[1]
    Hx, dh = XATTN_HEADS, XATTN_HEAD_DIM
    q = (h @ w_q).astype(F32).reshape(B, S, Hx, dh) * dh ** -0.5
    k, v = jnp.split((mem_n @ w_kv).astype(F32), 2, axis=-1)
    k = k.reshape(B, N, Hx, dh)
    v = v.reshape(B, N, Hx, dh)
    p = jax.nn.softmax(jnp.einsum('bqhd,bkhd->bhqk', q, k), axis=-1)
    o = jnp.einsum('bhqk,bkhd->bqhd', p, v).reshape(B, S, D)
    return o.astype(h.dtype) @ w_o


def setup_inputs(seed: int = 0) -> dict:
    key = jax.random.key(seed)
    keys = iter(jax.random.split(key, 64))
    D, F, L = D_MODEL, D_FF, DEPTH
    nA, nB, nC, nD = (n_layers_of(m) for m in range(N_MIXERS))

    def w(shape, fan_in):
        return jax.random.normal(next(keys), shape, F32) * fan_in ** -0.5

    def gain(shape):
        return 1.0 + 0.02 * jax.random.normal(next(keys), shape, F32)

    def small(shape, s=0.02):
        return s * jax.random.normal(next(keys), shape, F32)

    return {
        "x": jax.random.normal(next(keys), (BATCH, SEQ, D), F32),
        "mem": jax.random.normal(next(keys), (BATCH, N_MEM, D), F32),
        "ffn1_norm": gain((L, D)),
        "ffn1_w_in": w((L, D, 2 * F), D),
        "ffn1_w_out": w((L, F, D), F),
        "mix_norm": gain((L, D)),
        "xattn_norm": gain((L, D)),
        "mem_norm": gain((L, D)),
        "xattn_w_q": w((L, D, D), D),
        "xattn_w_kv": w((L, D, 2 * D), D),
        "xattn_w_o": w((L, D, D), D),
        "ffn2_norm": gain((L, D)),
        "ffn2_w_in": w((L, D, 2 * F), D),
        "ffn2_w_out": w((L, F, D), F),
        "pool_w": w((nA, len(POOL_WINDOWS), POOL_GROUP, POOL_GROUP), POOL_GROUP),
        "pool_b": small((nA, D)),
        "pool_scale": 1.0 + 0.1 * jax.random.normal(next(keys), (nA, D), F32),
        "nsa_w_in": w((nB, D, NSA_IN_WIDTH), D),
        "nsa_cmp_pos": small((nB, 2, NSA_CMP_BLOCK, NSA_HEAD_DIM), 0.1),
        "nsa_cmp_w1": w((nB, 2, NSA_CMP_BLOCK * NSA_HEAD_DIM, NSA_CMP_HIDDEN), NSA_CMP_BLOCK * NSA_HEAD_DIM),
        "nsa_cmp_w2": w((nB, 2, NSA_CMP_HIDDEN, NSA_HEAD_DIM), NSA_CMP_HIDDEN),
        "nsa_w_o": w((nB, D, D), D),
        "gla_w_in": w((nC, D, GLA_IN_WIDTH), D),
        "gla_w_gate_up": w((nC, GLA_GATE_RANK, GLA_HEADS * GLA_DK), GLA_GATE_RANK),
        "gla_b_gate": small((nC, GLA_HEADS * GLA_DK)),
        "gla_norm": gain((nC, GLA_DV)),
        "gla_w_o": w((nC, D, D), D),
        "conv_w_in": w((nD, D, 2 * D), D),
        "conv_b_in": small((nD, 2 * D)),
        "conv_dw": w((nD, CONV_WIDTH, D), CONV_WIDTH),
        "conv_b_dw": small((nD, D)),
        "conv_ln_g": gain((nD, D)),
        "conv_ln_b": small((nD, D)),
        "conv_w_out": w((nD, D, D), D),
        "conv_b_out": small((nD, D)),
        "final_norm": gain((D,)),
    }


def reference(x, mem, ffn1_norm, ffn1_w_in, ffn1_w_out, mix_norm, xattn_norm, mem_norm,
              xattn_w_q, xattn_w_kv, xattn_w_o, ffn2_norm, ffn2_w_in, ffn2_w_out,
              pool_w, pool_b, pool_scale,
              nsa_w_in, nsa_cmp_pos, nsa_cmp_w1, nsa_cmp_w2, nsa_w_o,
              gla_w_in, gla_w_gate_up, gla_b_gate, gla_norm, gla_w_o,
              conv_w_in, conv_b_in, conv_dw, conv_b_dw, conv_ln_g, conv_ln_b, conv_w_out, conv_b_out,
              final_norm):
    for i in range(DEPTH):
        m, j = i % N_MIXERS, i // N_MIXERS
        x = x + 0.5 * swiglu_ffn(rms_norm(x, ffn1_norm[i]), ffn1_w_in[i], ffn1_w_out[i])
        h = rms_norm(x, mix_norm[i])
        if m == 0:
            y = pool_mixer(h, pool_w[j], pool_b[j], pool_scale[j])
        elif m == 1:
            y = nsa_mixer(h, nsa_w_in[j], nsa_cmp_pos[j], nsa_cmp_w1[j], nsa_cmp_w2[j], nsa_w_o[j])
        elif m == 2:
            y = gla_mixer(h, gla_w_in[j], gla_w_gate_up[j], gla_b_gate[j], gla_norm[j], gla_w_o[j])
        else:
            y = conv_mixer(h, conv_w_in[j], conv_b_in[j], conv_dw[j], conv_b_dw[j],
                           conv_ln_g[j], conv_ln_b[j], conv_w_out[j], conv_b_out[j])
        x = x + y
        x = x + memory_cross_attention(rms_norm(x, xattn_norm[i]), rms_norm(mem, mem_norm[i]),
                                       xattn_w_q[i], xattn_w_kv[i], xattn_w_o[i])
        x = x + 0.5 * swiglu_ffn(rms_norm(x, ffn2_norm[i]), ffn2_w_in[i], ffn2_w_out[i])
    return rms_norm(x, final_norm)
```

```python
import functools

import numpy as np
import jax
import jax.numpy as jnp
from jax import lax
from jax.experimental import pallas as pl
from jax.experimental.pallas import tpu as pltpu

F32 = jnp.float32
BF16 = jnp.bfloat16

EPS = 1e-6
NEG = -1e30

V7X_VMEM_BYTES = 64 * 1024 * 1024
VMEM_LIMIT = V7X_VMEM_BYTES - 8 * 1024 * 1024
LANES = 128

POOL_WINDOWS = (2, 4, 8, 16)
POOL_HALO = 16

NSA_HEADS = 16
NSA_KV_HEADS = 4
NSA_REP = NSA_HEADS // NSA_KV_HEADS
NSA_HEAD_DIM = 64
NSA_CMP_BLOCK = 32
NSA_CMP_STRIDE = 16
NSA_SEL_BLOCK = 64
NSA_N_SELECT = 16
NSA_WINDOW = 512
NSA_FORCE = 1e4
NSA_QT = 256

GLA_HEADS = 4
GLA_DK = 128
GLA_DV = 256
GLA_GATE_RANK = 16
GLA_TAU = 16.0
GLA_CHUNK = 64
GLA_ROWS = 512

CONV_WIDTH = 31
CONV_HALO = 32

XATTN_HEADS = 4

ROW_TILE = 512


def _params(sem):
    return pltpu.CompilerParams(dimension_semantics=sem, vmem_limit_bytes=VMEM_LIMIT)


def _rms(x, g):
    return x * lax.rsqrt(jnp.mean(x * x, axis=-1, keepdims=True) + EPS) * g


def _dot(a, b):
    return jnp.dot(a, b, preferred_element_type=F32)


def _dot_nt(a, b):
    return lax.dot_general(a, b, (((1,), (1,)), ((), ())), preferred_element_type=F32)


def _dot_tn(a, b):
    return lax.dot_general(a, b, (((0,), (0,)), ((), ())), preferred_element_type=F32)


def _const_spec(shape):
    nd = len(shape)
    return pl.BlockSpec(shape, lambda *_: (0,) * nd, pipeline_mode=pl.Buffered(1))


def _ffn_body(x_ref, g_ref, win_ref, wout_ref, *rest, d_ff, final):
    if final:
        fg_ref, o_ref = rest
    else:
        (o_ref,) = rest
    x = x_ref[...]
    h = _rms(x, g_ref[...]).astype(BF16)
    gu = _dot(h, win_ref[...])
    g = gu[:, :d_ff]
    u = gu[:, d_ff:]
    a = (g * jax.nn.sigmoid(g) * u).astype(BF16)
    out = x + 0.5 * _dot(a, wout_ref[...])
    if final:
        out = _rms(out, fg_ref[...])
    o_ref[...] = out


def _ffn(x2, g, w_in, w_out, final_g=None, tm=ROW_TILE):
    t, d = x2.shape
    d_ff = w_out.shape[0]
    final = final_g is not None
    in_specs = [pl.BlockSpec((tm, d), lambda i: (i, 0)), _const_spec((1, d)),
                _const_spec(w_in.shape), _const_spec(w_out.shape)]
    args = [x2, g.reshape(1, d), w_in, w_out]
    if final:
        in_specs.append(_const_spec((1, d)))
        args.append(final_g.reshape(1, d))
    return pl.pallas_call(
        functools.partial(_ffn_body, d_ff=d_ff, final=final),
        out_shape=jax.ShapeDtypeStruct((t, d), F32),
        grid=(t // tm,),
        in_specs=in_specs,
        out_specs=pl.BlockSpec((tm, d), lambda i: (i, 0)),
        compiler_params=_params(("parallel",)),
        name="ffn",
    )(*args)


def _mm_res_body(a_ref, w_ref, x_ref, o_ref):
    o_ref[...] = x_ref[...] + _dot(a_ref[...], w_ref[...])


def _mm_res(a, w, x2, tm=ROW_TILE):
    t, k = a.shape
    n = w.shape[1]
    return pl.pallas_call(
        _mm_res_body,
        out_shape=jax.ShapeDtypeStruct((t, n), F32),
        grid=(t // tm,),
        in_specs=[pl.BlockSpec((tm, k), lambda i: (i, 0)), _const_spec(w.shape),
                  pl.BlockSpec((tm, n), lambda i: (i, 0))],
        out_specs=pl.BlockSpec((tm, n), lambda i: (i, 0)),
        compiler_params=_params(("parallel",)),
        name="mm_res",
    )(a, w, x2)


def _rms_mm_body(x_ref, g_ref, w_ref, o_ref):
    h = _rms(x_ref[...], g_ref[...]).astype(BF16)
    o_ref[...] = _dot(h, w_ref[...]).astype(o_ref.dtype)


def _rms_mm(x2, g, w, out_dtype, tm):
    t, d = x2.shape
    n = w.shape[1]
    return pl.pallas_call(
        _rms_mm_body,
        out_shape=jax.ShapeDtypeStruct((t, n), out_dtype),
        grid=(t // tm,),
        in_specs=[pl.BlockSpec((tm, d), lambda i: (i, 0)), _const_spec((1, d)), _const_spec(w.shape)],
        out_specs=pl.BlockSpec((tm, n), lambda i: (i, 0)),
        compiler_params=_params(("parallel",)),
        name="rms_mm",
    )(x2, g.reshape(1, d), w)


def _xattn_body(x_ref, g_ref, wq_ref, k_ref, v_ref, wo_ref, o_ref, *, heads):
    x = x_ref[0]
    d = x.shape[-1]
    dh = d // heads
    h = _rms(x, g_ref[...]).astype(BF16)
    q = (_dot(h, wq_ref[...]) * dh ** -0.5).astype(BF16)
    outs = []
    for hd in range(heads):
        sl = slice(hd * dh, (hd + 1) * dh)
        s = _dot_nt(q[:, sl], k_ref[0][:, sl])
        e = jnp.exp(s - jnp.max(s, axis=-1, keepdims=True))
        p = e / jnp.sum(e, axis=-1, keepdims=True)
        outs.append(_dot(p.astype(BF16), v_ref[0][:, sl]).astype(BF16))
    o = jnp.concatenate(outs, axis=-1)
    o_ref[0] = x + _dot(o, wo_ref[...])


def _xattn(x, g, w_q, kv, w_o, ts=ROW_TILE):
    b, s, d = x.shape
    n = kv.shape[1]
    return pl.pallas_call(
        functools.partial(_xattn_body, heads=XATTN_HEADS),
        out_shape=jax.ShapeDtypeStruct((b, s, d), F32),
        grid=(b, s // ts),
        in_specs=[pl.BlockSpec((1, ts, d), lambda bi, i: (bi, i, 0)), _const_spec((1, d)),
                  _const_spec(w_q.shape),
                  pl.BlockSpec((1, n, d), lambda bi, i: (bi, 0, 0)),
                  pl.BlockSpec((1, n, d), lambda bi, i: (bi, 0, 1)),
                  _const_spec(w_o.shape)],
        out_specs=pl.BlockSpec((1, ts, d), lambda bi, i: (bi, i, 0)),
        compiler_params=_params(("parallel", "parallel")),
        name="xattn",
    )(x, g.reshape(1, d), w_q, kv, kv, w_o)


def _pool_body(x_ref, halo_ref, g_ref, w_ref, b_ref, sc_ref, o_ref, hbuf_ref, *, ts):
    i = pl.program_id(1)
    x = x_ref[0]
    d = x.shape[-1]
    gw = d // len(POOL_WINDOWS)
    g = g_ref[...]
    h = _rms(x, g)
    hh = _rms(halo_ref[0], g)
    hbuf_ref[0:POOL_HALO, :] = jnp.where(i > 0, hh, 0.0)
    hbuf_ref[POOL_HALO:, :] = h
    t = i * ts + lax.broadcasted_iota(jnp.int32, (ts, 1), 0)
    ys = []
    for gi, win in enumerate(POOL_WINDOWS):
        cs = slice(gi * gw, (gi + 1) * gw)
        acc = h[:, cs]
        for k in range(1, win):
            acc = acc + hbuf_ref[POOL_HALO - k:POOL_HALO - k + ts, cs]
        cnt = jnp.minimum(t + 1, win).astype(F32)
        p = acc / cnt - h[:, cs]
        ys.append(_dot(p.astype(BF16), w_ref[gi]))
    y = (jnp.concatenate(ys, axis=-1) + b_ref[...]) * sc_ref[...]
    o_ref[0] = x + y


def _pool_mixer(x, g, w, bias, scale, ts=ROW_TILE):
    b, s, d = x.shape
    hb = ts // POOL_HALO
    return pl.pallas_call(
        functools.partial(_pool_body, ts=ts),
        out_shape=jax.ShapeDtypeStruct((b, s, d), F32),
        grid=(b, s // ts),
        in_specs=[pl.BlockSpec((1, ts, d), lambda bi, i: (bi, i, 0)),
                  pl.BlockSpec((1, POOL_HALO, d), lambda bi, i: (bi, jnp.maximum(i * hb - 1, 0), 0)),
                  _const_spec((1, d)), _const_spec(w.shape), _const_spec((1, d)), _const_spec((1, d))],
        out_specs=pl.BlockSpec((1, ts, d), lambda bi, i: (bi, i, 0)),
        scratch_shapes=[pltpu.VMEM((ts + POOL_HALO, d), F32)],
        compiler_params=_params(("parallel", "parallel")),
        name="pool",
    )(x, x, g.reshape(1, d), w, bias.reshape(1, d), scale.reshape(1, d))


def _conv_in_body(x_ref, g_ref, w_ref, b_ref, o_ref):
    d = x_ref.shape[-1]
    h = _rms(x_ref[...], g_ref[...]).astype(BF16)
    ag = _dot(h, w_ref[...]) + b_ref[...]
    o_ref[...] = ag[:, :d] * jax.nn.sigmoid(ag[:, d:])


def _conv_in(x2, g, w, bias, tm=ROW_TILE):
    t, d = x2.shape
    return pl.pallas_call(
        _conv_in_body,
        out_shape=jax.ShapeDtypeStruct((t, d), F32),
        grid=(t // tm,),
        in_specs=[pl.BlockSpec((tm, d), lambda i: (i, 0)), _const_spec((1, d)),
                  _const_spec(w.shape), _const_spec((1, 2 * d))],
        out_specs=pl.BlockSpec((tm, d), lambda i: (i, 0)),
        compiler_params=_params(("parallel",)),
        name="conv_in",
    )(x2, g.reshape(1, d), w, bias.reshape(1, 2 * d))


def _conv_out_body(u_ref, halo_ref, x_ref, dw_ref, bdw_ref, lng_ref, lnb_ref, w_ref, bo_ref, o_ref,
                   ubuf_ref, *, ts):
    i = pl.program_id(1)
    ubuf_ref[0:CONV_HALO, :] = jnp.where(i > 0, halo_ref[0], 0.0)
    ubuf_ref[CONV_HALO:, :] = u_ref[0]
    acc = None
    for k in range(CONV_WIDTH):
        off = CONV_HALO - (CONV_WIDTH - 1) + k
        term = ubuf_ref[off:off + ts, :] * dw_ref[k:k + 1, :]
        acc = term if acc is None else acc + term
    c = acc + bdw_ref[...]
    mu = jnp.mean(c, axis=-1, keepdims=True)
    var = jnp.mean(jnp.square(c - mu), axis=-1, keepdims=True)
    n = (c - mu) * lax.rsqrt(var + EPS) * lng_ref[...] + lnb_ref[...]
    a = (n * jax.nn.sigmoid(n)).astype(BF16)
    o_ref[0] = x_ref[0] + _dot(a, w_ref[...]) + bo_ref[...]


def _conv_out(u, x, dw, b_dw, ln_g, ln_b, w_out, b_out, ts=ROW_TILE):
    b, s, d = x.shape
    hb = ts // CONV_HALO
    dw_pad = jnp.zeros((CONV_HALO, d), F32).at[:CONV_WIDTH].set(dw)
    row = lambda v: v.reshape(1, d)
    return pl.pallas_call(
        functools.partial(_conv_out_body, ts=ts),
        out_shape=jax.ShapeDtypeStruct((b, s, d), F32),
        grid=(b, s // ts),
        in_specs=[pl.BlockSpec((1, ts, d), lambda bi, i: (bi, i, 0)),
                  pl.BlockSpec((1, CONV_HALO, d), lambda bi, i: (bi, jnp.maximum(i * hb - 1, 0), 0)),
                  pl.BlockSpec((1, ts, d), lambda bi, i: (bi, i, 0)),
                  _const_spec((CONV_HALO, d)), _const_spec((1, d)), _const_spec((1, d)), _const_spec((1, d)),
                  _const_spec(w_out.shape), _const_spec((1, d))],
        out_specs=pl.BlockSpec((1, ts, d), lambda bi, i: (bi, i, 0)),
        scratch_shapes=[pltpu.VMEM((ts + CONV_HALO, d), F32)],
        compiler_params=_params(("parallel", "parallel")),
        name="conv_out",
    )(u, u, x, dw_pad, row(b_dw), row(ln_g), row(ln_b), w_out, row(b_out))


def _gla_in_body(x_ref, g_ref, wqk_ref, wv_ref, wog_ref, wgd_ref, qk_ref, v_ref, og_ref, gd_ref):
    h = _rms(x_ref[...], g_ref[...]).astype(BF16)
    qk_ref[...] = _dot(h, wqk_ref[...])
    v_ref[...] = _dot(h, wv_ref[...]).astype(BF16)
    og_ref[...] = _dot(h, wog_ref[...])
    gd_ref[...] = _dot(h, wgd_ref[...])


def _gla_in(x2, g, w_qk, w_v, w_og, w_gd, tm=ROW_TILE):
    t, d = x2.shape
    widths = (w_qk.shape[1], w_v.shape[1], w_og.shape[1], w_gd.shape[1])
    dtypes = (F32, BF16, F32, F32)
    return pl.pallas_call(
        _gla_in_body,
        out_shape=[jax.ShapeDtypeStruct((t, n), dt) for n, dt in zip(widths, dtypes)],
        grid=(t // tm,),
        in_specs=[pl.BlockSpec((tm, d), lambda i: (i, 0)), _const_spec((1, d)),
                  _const_spec(w_qk.shape), _const_spec(w_v.shape), _const_spec(w_og.shape),
                  _const_spec(w_gd.shape)],
        out_specs=[pl.BlockSpec((tm, n), lambda i: (i, 0)) for n in widths],
        compiler_params=_params(("parallel",)),
        name="gla_in",
    )(x2, g.reshape(1, d), w_qk, w_v, w_og, w_gd)


def _gla_body(q_ref, k_ref, v_ref, og_ref, gd_ref, wup_ref, bg_ref, ng_ref, o_ref, state_ref):
    @pl.when(pl.program_id(2) == 0)
    def _():
        state_ref[...] = jnp.zeros_like(state_ref)

    c = GLA_CHUNK
    ga = _dot(gd_ref[0].astype(BF16), wup_ref[...]) + bg_ref[...]
    log_a = (jnp.minimum(ga, 0.0) - jnp.log1p(jnp.exp(-jnp.abs(ga)))) / GLA_TAU
    ri = lax.broadcasted_iota(jnp.int32, (c, c), 0)
    ci = lax.broadcasted_iota(jnp.int32, (c, c), 1)
    causal = ci <= ri
    tril = jnp.where(causal, 1.0, 0.0).astype(BF16)
    scale = GLA_DK ** -0.5
    for ch in range(GLA_ROWS // c):
        rows = slice(ch * c, (ch + 1) * c)
        la = log_a[rows]
        hi = la.astype(BF16)
        r1 = la - hi.astype(F32)
        mid = r1.astype(BF16)
        lo = (r1 - mid.astype(F32)).astype(BF16)
        bcum = _dot(tril, hi) + _dot(tril, mid) + _dot(tril, lo)
        b_last = bcum[c - 1:c, :]
        q = q_ref[0, rows, :] * scale
        k = k_ref[0, rows, :]
        v = v_ref[0, rows, :]
        q_t = (q * jnp.exp(bcum)).astype(BF16)
        k_t = (k * jnp.exp(-bcum)).astype(BF16)
        a = jnp.where(causal, _dot_nt(q_t, k_t), 0.0)
        state = state_ref[...]
        o = _dot(a.astype(BF16), v) + _dot_nt(q_t, state.astype(BF16))
        k_dec = (k * jnp.exp(b_last - bcum)).astype(BF16)
        state_ref[...] = state * jnp.exp(b_last) + _dot_tn(v, k_dec)
        o = _rms(o, ng_ref[...])
        og = og_ref[0, rows, :]
        o_ref[0, rows, :] = (o * (og * jax.nn.sigmoid(og))).astype(BF16)


def _gla_core(qk, v, og, gd, w_up, b_gate, norm_g):
    b, s, _ = v.shape
    hh, dk, dv, rt = GLA_HEADS, GLA_DK, GLA_DV, GLA_ROWS
    return pl.pallas_call(
        _gla_body,
        out_shape=jax.ShapeDtypeStruct((b, s, hh * dv), BF16),
        grid=(b, hh, s // rt),
        in_specs=[pl.BlockSpec((1, rt, dk), lambda bi, h, i: (bi, i, h)),
                  pl.BlockSpec((1, rt, dk), lambda bi, h, i: (bi, i, hh + h)),
                  pl.BlockSpec((1, rt, dv), lambda bi, h, i: (bi, i, h)),
                  pl.BlockSpec((1, rt, dv), lambda bi, h, i: (bi, i, h)),
                  pl.BlockSpec((1, rt, LANES), lambda bi, h, i: (bi, i, 0)),
                  pl.BlockSpec((LANES, dk), lambda bi, h, i: (0, h)),
                  pl.BlockSpec((1, dk), lambda bi, h, i: (0, h)),
                  _const_spec((1, dv))],
        out_specs=pl.BlockSpec((1, rt, dv), lambda bi, h, i: (bi, i, h)),
        scratch_shapes=[pltpu.VMEM((dv, dk), F32)],
        compiler_params=_params(("parallel", "parallel", "arbitrary")),
        name="gla_core",
    )(qk, qk, v, og, gd, w_up, b_gate.reshape(1, hh * dk), norm_g.reshape(1, dv))


def _gla_mixer(x, g, w_in, w_gate_up, b_gate, norm_g, w_o):
    b, s, d = x.shape
    qw, vw = GLA_HEADS * GLA_DK, GLA_HEADS * GLA_DV
    x2 = x.reshape(b * s, d)
    w_gd = jnp.zeros((d, LANES), BF16).at[:, :GLA_GATE_RANK].set(w_in[:, 2 * qw + 2 * vw:].astype(BF16))
    w_up = jnp.zeros((LANES, qw), BF16).at[:GLA_GATE_RANK].set(w_gate_up.astype(BF16))
    qk, v, og, gd = _gla_in(x2, g, w_in[:, :2 * qw].astype(BF16), w_in[:, 2 * qw:2 * qw + vw].astype(BF16),
                            w_in[:, 2 * qw + vw:2 * qw + 2 * vw].astype(BF16), w_gd)
    r3 = lambda a: a.reshape(b, s, a.shape[-1])
    o = _gla_core(r3(qk), r3(v), r3(og), r3(gd), w_up, b_gate, norm_g)
    return _mm_res(o.reshape(b * s, vw), w_o.astype(BF16), x2).reshape(b, s, d)


def _nsa_in_body(x_ref, g_ref, wq_ref, wkv_ref, wc_ref, wg_ref,
                 q_ref, kas_ref, vas_ref, kaw_ref, vaw_ref, c_ref, gt_ref, *, ts):
    i = pl.program_id(1)
    h = _rms(x_ref[0], g_ref[...]).astype(BF16)
    q_ref[0] = (_dot(h, wq_ref[...]) * NSA_HEAD_DIM ** -0.5).astype(BF16)
    kv = _dot(h, wkv_ref[...])
    gw = NSA_KV_HEADS * LANES
    lane = lax.broadcasted_iota(jnp.int32, (ts, LANES), 1)
    t = i * ts + lax.broadcasted_iota(jnp.int32, (ts, LANES), 0)
    onehot = jnp.where(lane - NSA_HEAD_DIM == t // NSA_SEL_BLOCK, 1.0, 0.0)
    ones = jnp.where(lane >= NSA_HEAD_DIM, 1.0, 0.0)
    for gi in range(NSA_KV_HEADS):
        ls = slice(gi * LANES, (gi + 1) * LANES)
        kas_ref[0, :, ls] = (kv[:, gi * LANES:(gi + 1) * LANES] + onehot).astype(BF16)
        vas_ref[0, :, ls] = (kv[:, gw + gi * LANES:gw + (gi + 1) * LANES] + ones).astype(BF16)
        kaw_ref[0, :, ls] = kv[:, 2 * gw + gi * LANES:2 * gw + (gi + 1) * LANES].astype(BF16)
        vaw_ref[0, :, ls] = (kv[:, 3 * gw + gi * LANES:3 * gw + (gi + 1) * LANES] + ones).astype(BF16)
    c_ref[0] = _dot(h, wc_ref[...])
    gt_ref[0] = _dot(h, wg_ref[...])


def _nsa_in(x, g, w_q, w_kv, w_c, w_g, ts=ROW_TILE):
    b, s, d = x.shape
    gw = NSA_KV_HEADS * LANES
    widths = (w_q.shape[1], gw, gw, gw, gw, w_c.shape[1], w_g.shape[1])
    dtypes = (BF16, BF16, BF16, BF16, BF16, F32, F32)
    return pl.pallas_call(
        functools.partial(_nsa_in_body, ts=ts),
        out_shape=[jax.ShapeDtypeStruct((b, s, n), dt) for n, dt in zip(widths, dtypes)],
        grid=(b, s // ts),
        in_specs=[pl.BlockSpec((1, ts, d), lambda bi, i: (bi, i, 0)), _const_spec((1, d)),
                  _const_spec(w_q.shape), _const_spec(w_kv.shape), _const_spec(w_c.shape),
                  _const_spec(w_g.shape)],
        out_specs=[pl.BlockSpec((1, ts, n), lambda bi, i: (bi, i, 0)) for n in widths],
        compiler_params=_params(("parallel", "parallel")),
        name="nsa_in",
    )(x, g.reshape(1, d), w_q, w_kv, w_c, w_g)


def _nsa_cmp_body(a_ref, pt_ref, pb_ref, w1_ref, w2_ref, o_ref):
    a = a_ref[0, 0, 0]
    half = a.shape[-1]
    u = _dot((a + pt_ref[0]).astype(BF16), w1_ref[0, :half, :])
    v = _dot((a + pb_ref[0]).astype(BF16), w1_ref[0, half:, :])
    nrow = a.shape[0]
    hid = jax.nn.gelu(u + pltpu.roll(v, nrow - 1, 0), approximate=True)
    o_ref[0, 0, 0] = _dot(hid.astype(BF16), w2_ref[0])


def _nsa_compress(a, pos_top, pos_bot, w1, w2):
    _, b, g, nch, feat = a.shape
    dh = w2.shape[-1]
    return pl.pallas_call(
        _nsa_cmp_body,
        out_shape=jax.ShapeDtypeStruct((2, b, g, nch, dh), F32),
        grid=(2, b, g),
        in_specs=[pl.BlockSpec((1, 1, 1, nch, feat), lambda kv, bi, gi: (kv, bi, gi, 0, 0)),
                  pl.BlockSpec((1, 1, feat), lambda kv, bi, gi: (kv, 0, 0)),
                  pl.BlockSpec((1, 1, feat), lambda kv, bi, gi: (kv, 0, 0)),
                  pl.BlockSpec((1,) + w1.shape[1:], lambda kv, bi, gi: (kv, 0, 0)),
                  pl.BlockSpec((1,) + w2.shape[1:], lambda kv, bi, gi: (kv, 0, 0))],
        out_specs=pl.BlockSpec((1, 1, 1, nch, dh), lambda kv, bi, gi: (kv, bi, gi, 0, 0)),
        compiler_params=_params(("parallel", "parallel", "parallel")),
        name="nsa_compress",
    )(a, pos_top, pos_bot, w1, w2)


def _nsa_attn_body(q_ref, kc_ref, vc_ref, kas_ref, vas_ref, kaw_ref, vaw_ref, gt_ref, ov_ref, o_ref,
                   qaug_ref, m_ref, acc_ref, *, n_slc):
    qt, rep, dh = NSA_QT, NSA_REP, NSA_HEAD_DIM
    rows = rep * qt
    qi = pl.program_id(2)
    s0 = qi * qt
    q = q_ref[0, 0].reshape(rows, dh)
    t_row = s0 + (lax.broadcasted_iota(jnp.int32, (rows, 1), 0) & (qt - 1))

    n_pad = kc_ref.shape[2]
    sc = _dot_nt(q, kc_ref[0, 0].astype(BF16))
    cmp_end = lax.broadcasted_iota(jnp.int32, (1, n_pad), 1) * NSA_CMP_STRIDE + (NSA_CMP_BLOCK - 1)
    sc = jnp.where(cmp_end <= t_row, sc, NEG)
    e = jnp.exp(sc - jnp.max(sc, axis=-1, keepdims=True))
    p_cmp = e / jnp.sum(e, axis=-1, keepdims=True)
    p_cmp = jnp.where(t_row >= NSA_CMP_BLOCK - 1, p_cmp, 0.0).astype(BF16)
    o_cmp = _dot(p_cmp, vc_ref[0, 0].astype(BF16))

    imp = _dot(p_cmp[0:qt], ov_ref[...])
    for r in range(1, rep):
        imp = imp + _dot(p_cmp[r * qt:(r + 1) * qt], ov_ref[...])
    imp_t = imp.T[:n_slc]
    blk = lax.broadcasted_iota(jnp.int32, (n_slc, 1), 0)
    t_q = s0 + lax.broadcasted_iota(jnp.int32, (1, qt), 1)
    cur = t_q // NSA_SEL_BLOCK
    forced = (blk == 0) | (blk == cur) | (blk == cur - 1)
    visible = blk * NSA_SEL_BLOCK <= t_q
    imp_t = jnp.where(forced, NSA_FORCE, imp_t)
    imp_t = jnp.where(visible, imp_t, -1.0)
    rank = jnp.zeros((n_slc, qt), F32)
    for i in range(n_slc):
        row = imp_t[i:i + 1, :]
        beats = (row > imp_t) | ((row == imp_t) & (blk > i))
        rank = rank + jnp.where(beats, 1.0, 0.0)
    sel = (rank < float(min(NSA_N_SELECT, n_slc))) & visible
    bias_parts = [jnp.zeros((dh, qt), F32), jnp.where(sel, 0.0, NEG)]
    if LANES - dh - n_slc:
        bias_parts.append(jnp.zeros((LANES - dh - n_slc, qt), F32))
    bias_t = jnp.concatenate(bias_parts, axis=0)
    bias = bias_t.T
    lane = lax.broadcasted_iota(jnp.int32, (qt, LANES), 1)
    for r in range(rep):
        q_r = jnp.concatenate([q[r * qt:(r + 1) * qt].astype(F32), jnp.zeros((qt, LANES - dh), F32)], axis=-1)
        qaug_ref[r * qt:(r + 1) * qt, :] = jnp.where(lane < dh, q_r, bias).astype(BF16)

    key_l = lax.broadcasted_iota(jnp.int32, (1, qt), 1)

    def attend(ka_ref, va_ref, j, mask_fn):
        ka = ka_ref[0, pl.ds(pl.multiple_of(j * qt, qt), qt), :]
        va = va_ref[0, pl.ds(pl.multiple_of(j * qt, qt), qt), :]
        s = _dot_nt(qaug_ref[...], ka)
        if mask_fn is not None:
            s = jnp.where(mask_fn(j * qt + key_l), s, NEG)
        m_old = m_ref[...]
        m_new = jnp.maximum(m_old, jnp.max(s, axis=-1, keepdims=True))
        p = jnp.exp(s - m_new)
        acc_ref[...] = acc_ref[...] * jnp.exp(m_old - m_new) + _dot(p.astype(BF16), va)
        m_ref[...] = m_new

    def finish():
        acc = acc_ref[...]
        return (acc / pltpu.roll(acc, dh, 1))[:, :dh]

    def reset():
        m_ref[...] = jnp.full_like(m_ref, NEG)
        acc_ref[...] = jnp.zeros_like(acc_ref)

    reset()

    def sel_step(j, carry):
        attend(kas_ref, vas_ref, j, None)
        return carry

    lax.fori_loop(0, qi, sel_step, 0)
    attend(kas_ref, vas_ref, qi, lambda kpos: kpos <= t_row)
    o_slc = finish()

    reset()
    win_mask = lambda kpos: (kpos <= t_row) & (kpos > t_row - NSA_WINDOW)
    for back in range(NSA_WINDOW // qt, 0, -1):
        @pl.when(qi >= back)
        def _(back=back):
            attend(kaw_ref, vaw_ref, qi - back, win_mask)
    attend(kaw_ref, vaw_ref, qi, win_mask)
    o_win = finish()

    gates = jax.nn.sigmoid(gt_ref[0, 0])
    for r in range(rep):
        rs = slice(r * qt, (r + 1) * qt)
        o = (gates[:, 3 * r:3 * r + 1] * o_cmp[rs] + gates[:, 3 * r + 1:3 * r + 2] * o_slc[rs]
             + gates[:, 3 * r + 2:3 * r + 3] * o_win[rs])
        o_ref[0, 0, r] = o.astype(BF16)


def _nsa_attn(q, kcmp, vcmp, kas, vas, kaw, vaw, gates, ov):
    b, g, rep, s, dh = q.shape
    qt = NSA_QT
    n_slc = s // NSA_SEL_BLOCK
    nch = kcmp.shape[2]
    kv_spec = pl.BlockSpec((1, s, LANES), lambda bi, gi, i: (bi, 0, gi))
    cmp_spec = pl.BlockSpec((1, 1, nch, dh), lambda bi, gi, i: (bi, gi, 0, 0))
    return pl.pallas_call(
        functools.partial(_nsa_attn_body, n_slc=n_slc),
        out_shape=jax.ShapeDtypeStruct((b, g, rep, s, dh), BF16),
        grid=(b, g, s // qt),
        in_specs=[pl.BlockSpec((1, 1, rep, qt, dh), lambda bi, gi, i: (bi, gi, 0, i, 0)),
                  cmp_spec, cmp_spec, kv_spec, kv_spec, kv_spec, kv_spec,
                  pl.BlockSpec((1, 1, qt, 3 * rep), lambda bi, gi, i: (bi, gi, i, 0)),
                  _const_spec(ov.shape)],
        out_specs=pl.BlockSpec((1, 1, rep, qt, dh), lambda bi, gi, i: (bi, gi, 0, i, 0)),
        scratch_shapes=[pltpu.VMEM((rep * qt, LANES), BF16), pltpu.VMEM((rep * qt, 1), F32),
                        pltpu.VMEM((rep * qt, LANES), F32)],
        compiler_params=_params(("parallel", "parallel", "parallel")),
        name="nsa_attn",
    )(q, kcmp, vcmp, kas, vas, kaw, vaw, gates, ov)


def _nsa_overlap(s):
    n_cmp = (s - NSA_CMP_BLOCK) // NSA_CMP_STRIDE + 1
    n_slc = s // NSA_SEL_BLOCK
    cs = np.arange(n_cmp) * NSA_CMP_STRIDE
    ss = np.arange(n_slc) * NSA_SEL_BLOCK
    ov = np.clip(np.minimum(cs[:, None] + NSA_CMP_BLOCK, ss[None, :] + NSA_SEL_BLOCK)
                 - np.maximum(cs[:, None], ss[None, :]), 0, None) / NSA_CMP_BLOCK
    out = np.zeros((s // NSA_CMP_STRIDE, LANES), np.float32)
    out[:n_cmp, :n_slc] = ov
    return jnp.asarray(out, BF16)


def _nsa_mixer(x, g, w_in, cmp_pos, cmp_w1, cmp_w2, w_o):
    b, s, d = x.shape
    hh, gg, rep, dh = NSA_HEADS, NSA_KV_HEADS, NSA_REP, NSA_HEAD_DIM
    kvw = gg * dh
    assert s % NSA_QT == 0 and s // NSA_SEL_BLOCK <= LANES - dh
    wb = w_in.astype(BF16)
    parts = [wb[:, d + i * kvw:d + (i + 1) * kvw].reshape(d, gg, dh) for i in range(6)]
    zeros = jnp.zeros((d, gg, LANES - dh), BF16)
    padded = lambda p: jnp.concatenate([p, zeros], axis=-1).reshape(d, gg * LANES)
    w_kv = jnp.concatenate([padded(parts[2]), padded(parts[3]), padded(parts[4]), padded(parts[5])], axis=-1)
    w_c = wb[:, d:d + 2 * kvw]
    w_g = jnp.zeros((d, LANES), BF16).at[:, :3 * hh].set(wb[:, d + 6 * kvw:])
    q, kas, vas, kaw, vaw, c, gt = _nsa_in(x, g, wb[:, :d], w_kv, w_c, w_g)

    nch = s // NSA_CMP_STRIDE
    a = c.reshape(b, nch, NSA_CMP_STRIDE, 2, gg, dh).transpose(3, 0, 4, 1, 2, 5)
    a = a.reshape(2, b, gg, nch, NSA_CMP_STRIDE * dh)
    pos = cmp_pos.reshape(2, 2, 1, NSA_CMP_STRIDE * dh)
    kvc = _nsa_compress(a, pos[:, 0], pos[:, 1], cmp_w1.astype(BF16), cmp_w2.astype(BF16))

    q5 = q.reshape(b, s, gg, rep, dh).transpose(0, 2, 3, 1, 4)
    gates = gt[:, :, :3 * hh].reshape(b, s, gg, 3 * rep).transpose(0, 2, 1, 3)
    o = _nsa_attn(q5, kvc[0], kvc[1], kas, vas, kaw, vaw, gates, _nsa_overlap(s))
    o = o.transpose(0, 3, 1, 2, 4).reshape(b * s, hh * dh)
    return _mm_res(o, w_o.astype(BF16), x.reshape(b * s, d)).reshape(b, s, d)


def kernel(x, mem, ffn1_norm, ffn1_w_in, ffn1_w_out, mix_norm, xattn_norm, mem_norm, xattn_w_q, xattn_w_kv, xattn_w_o, ffn2_norm, ffn2_w_in, ffn2_w_out, pool_w, pool_b, pool_scale, nsa_w_in, nsa_cmp_pos, nsa_cmp_w1, nsa_cmp_w2, nsa_w_o, gla_w_in, gla_w_gate_up, gla_b_gate, gla_norm, gla_w_o, conv_w_in, conv_b_in, conv_dw, conv_b_dw, conv_ln_g, conv_ln_b, conv_w_out, conv_b_out, final_norm):
    b, s, d = x.shape
    n_mem = mem.shape[1]
    depth = ffn1_norm.shape[0]
    n_mixers = 4
    flat = lambda a: a.reshape(b * s, d)
    cube = lambda a: a.reshape(b, s, d)
    mem2 = mem.reshape(b * n_mem, d)
    for i in range(depth):
        m, j = i % n_mixers, i // n_mixers
        x = cube(_ffn(flat(x), ffn1_norm[i], ffn1_w_in[i].astype(BF16), ffn1_w_out[i].astype(BF16)))
        if m == 0:
            x = _pool_mixer(x, mix_norm[i], pool_w[j].astype(BF16), pool_b[j], pool_scale[j])
        elif m == 1:
            x = _nsa_mixer(x, mix_norm[i], nsa_w_in[j], nsa_cmp_pos[j], nsa_cmp_w1[j], nsa_cmp_w2[j], nsa_w_o[j])
        elif m == 2:
            x = _gla_mixer(x, mix_norm[i], gla_w_in[j], gla_w_gate_up[j], gla_b_gate[j], gla_norm[j], gla_w_o[j])
        else:
            u = _conv_in(flat(x), mix_norm[i], conv_w_in[j].astype(BF16), conv_b_in[j])
            x = _conv_out(cube(u), x, conv_dw[j], conv_b_dw[j], conv_ln_g[j], conv_ln_b[j],
                          conv_w_out[j].astype(BF16), conv_b_out[j])
        kv = _rms_mm(mem2, mem_norm[i], xattn_w_kv[i].astype(BF16), BF16, tm=n_mem)
        x = _xattn(x, xattn_norm[i], xattn_w_q[i].astype(BF16), kv.reshape(b, n_mem, 2 * d),
                   xattn_w_o[i].astype(BF16))
        final_g = final_norm if i == depth - 1 else None
        x = cube(_ffn(flat(x), ffn2_norm[i], ffn2_w_in[i].astype(BF16), ffn2_w_out[i].astype(BF16),
                      final_g=final_g))
    return x
```

```python
import functools

import numpy as np
import jax
import jax.numpy as jnp
from jax import lax
from jax.experimental import pallas as pl
from jax.experimental.pallas import tpu as pltpu

F32 = jnp.float32
BF16 = jnp.bfloat16

EPS = 1e-6
NEG = -1e30

V7X_VMEM_BYTES = 64 * 1024 * 1024
VMEM_LIMIT = V7X_VMEM_BYTES - 8 * 1024 * 1024
LANES = 128
SUBLANES = 8

POOL_WINDOWS = (2, 4, 8, 16)
POOL_HALO = 16

NSA_HEADS = 16
NSA_KV_HEADS = 4
NSA_REP = NSA_HEADS // NSA_KV_HEADS
NSA_HEAD_DIM = 64
NSA_CMP_BLOCK = 32
NSA_CMP_STRIDE = 16
NSA_SEL_BLOCK = 64
NSA_N_SELECT = 16
NSA_WINDOW = 512
NSA_FORCE = 1e4
NSA_QT = 256
NSA_GATE_ROWS = 16

GLA_HEADS = 4
GLA_DK = 128
GLA_DV = 256
GLA_GATE_RANK = 16
GLA_TAU = 16.0
GLA_CHUNK = 64
GLA_ROWS = 256

CONV_WIDTH = 31
CONV_HALO = 32
CONV_STRIP = 128

XATTN_HEADS = 4

ROW_TILE = 512


def _params(sem):
    return pltpu.CompilerParams(dimension_semantics=sem, vmem_limit_bytes=VMEM_LIMIT)


def _rms(x, g):
    return x * lax.rsqrt(jnp.mean(x * x, axis=-1, keepdims=True) + EPS) * g


def _dot(a, b):
    return jnp.dot(a, b, preferred_element_type=F32)


def _dot_nt(a, b):
    return lax.dot_general(a, b, (((1,), (1,)), ((), ())), preferred_element_type=F32)


def _dot_tn(a, b):
    return lax.dot_general(a, b, (((0,), (0,)), ((), ())), preferred_element_type=F32)


def _const_spec(shape):
    nd = len(shape)
    return pl.BlockSpec(shape, lambda *_: (0,) * nd, pipeline_mode=pl.Buffered(1))


def _ffn_body(x_ref, g_ref, win_ref, wout_ref, *rest, d_ff, final):
    if final:
        fg_ref, o_ref = rest
    else:
        (o_ref,) = rest
    x = x_ref[...]
    h = _rms(x, g_ref[...]).astype(BF16)
    gu = _dot(h, win_ref[...])
    g = gu[:, :d_ff]
    u = gu[:, d_ff:]
    a = (g * jax.nn.sigmoid(g) * u).astype(BF16)
    out = x + 0.5 * _dot(a, wout_ref[...])
    if final:
        out = _rms(out, fg_ref[...])
    o_ref[...] = out


def _ffn(x2, g, w_in, w_out, final_g=None, tm=ROW_TILE):
    t, d = x2.shape
    d_ff = w_out.shape[0]
    final = final_g is not None
    in_specs = [pl.BlockSpec((tm, d), lambda i: (i, 0)), _const_spec((1, d)),
                _const_spec(w_in.shape), _const_spec(w_out.shape)]
    args = [x2, g.reshape(1, d), w_in, w_out]
    if final:
        in_specs.append(_const_spec((1, d)))
        args.append(final_g.reshape(1, d))
    return pl.pallas_call(
        functools.partial(_ffn_body, d_ff=d_ff, final=final),
        out_shape=jax.ShapeDtypeStruct((t, d), F32),
        grid=(t // tm,),
        in_specs=in_specs,
        out_specs=pl.BlockSpec((tm, d), lambda i: (i, 0)),
        compiler_params=_params(("parallel",)),
        name="ffn",
    )(*args)


def _mm_res_body(a_ref, w_ref, x_ref, o_ref):
    o_ref[...] = x_ref[...] + _dot(a_ref[...], w_ref[...])


def _mm_res(a, w, x2, tm=ROW_TILE):
    t, k = a.shape
    n = w.shape[1]
    return pl.pallas_call(
        _mm_res_body,
        out_shape=jax.ShapeDtypeStruct((t, n), F32),
        grid=(t // tm,),
        in_specs=[pl.BlockSpec((tm, k), lambda i: (i, 0)), _const_spec(w.shape),
                  pl.BlockSpec((tm, n), lambda i: (i, 0))],
        out_specs=pl.BlockSpec((tm, n), lambda i: (i, 0)),
        compiler_params=_params(("parallel",)),
        name="mm_res",
    )(a, w, x2)


def _rms_mm_body(x_ref, g_ref, w_ref, o_ref):
    h = _rms(x_ref[...], g_ref[...]).astype(BF16)
    o_ref[...] = _dot(h, w_ref[...]).astype(o_ref.dtype)


def _rms_mm(x2, g, w, out_dtype, tm):
    t, d = x2.shape
    n = w.shape[1]
    return pl.pallas_call(
        _rms_mm_body,
        out_shape=jax.ShapeDtypeStruct((t, n), out_dtype),
        grid=(t // tm,),
        in_specs=[pl.BlockSpec((tm, d), lambda i: (i, 0)), _const_spec((1, d)), _const_spec(w.shape)],
        out_specs=pl.BlockSpec((tm, n), lambda i: (i, 0)),
        compiler_params=_params(("parallel",)),
        name="rms_mm",
    )(x2, g.reshape(1, d), w)


def _xattn_body(x_ref, g_ref, wq_ref, k_ref, v_ref, wo_ref, o_ref, *, heads):
    x = x_ref[0]
    d = x.shape[-1]
    dh = d // heads
    h = _rms(x, g_ref[...]).astype(BF16)
    q = (_dot(h, wq_ref[...]) * dh ** -0.5).astype(BF16)
    outs = []
    for hd in range(heads):
        sl = slice(hd * dh, (hd + 1) * dh)
        s = _dot_nt(q[:, sl], k_ref[0][:, sl])
        e = jnp.exp(s - jnp.max(s, axis=-1, keepdims=True))
        p = e / jnp.sum(e, axis=-1, keepdims=True)
        outs.append(_dot(p.astype(BF16), v_ref[0][:, sl]).astype(BF16))
    o = jnp.concatenate(outs, axis=-1)
    o_ref[0] = x + _dot(o, wo_ref[...])


def _xattn(x, g, w_q, kv, w_o, ts=ROW_TILE):
    b, s, d = x.shape
    n = kv.shape[1]
    return pl.pallas_call(
        functools.partial(_xattn_body, heads=XATTN_HEADS),
        out_shape=jax.ShapeDtypeStruct((b, s, d), F32),
        grid=(b, s // ts),
        in_specs=[pl.BlockSpec((1, ts, d), lambda bi, i: (bi, i, 0)), _const_spec((1, d)),
                  _const_spec(w_q.shape),
                  pl.BlockSpec((1, n, d), lambda bi, i: (bi, 0, 0)),
                  pl.BlockSpec((1, n, d), lambda bi, i: (bi, 0, 1)),
                  _const_spec(w_o.shape)],
        out_specs=pl.BlockSpec((1, ts, d), lambda bi, i: (bi, i, 0)),
        compiler_params=_params(("parallel", "parallel")),
        name="xattn",
    )(x, g.reshape(1, d), w_q, kv, kv, w_o)


def _pool_body(x_ref, halo_ref, g_ref, w_ref, b_ref, sc_ref, o_ref, hbuf_ref, *, ts):
    i = pl.program_id(1)
    x = x_ref[0]
    d = x.shape[-1]
    gw = d // len(POOL_WINDOWS)
    g = g_ref[...]
    h = _rms(x, g)
    hh = _rms(halo_ref[0], g)
    hbuf_ref[0:POOL_HALO, :] = jnp.where(i > 0, hh, 0.0)
    hbuf_ref[POOL_HALO:, :] = h
    t = i * ts + lax.broadcasted_iota(jnp.int32, (ts, 1), 0)
    ys = []
    for gi, win in enumerate(POOL_WINDOWS):
        cs = slice(gi * gw, (gi + 1) * gw)
        acc = h[:, cs]
        for k in range(1, win):
            acc = acc + hbuf_ref[POOL_HALO - k:POOL_HALO - k + ts, cs]
        cnt = jnp.minimum(t + 1, win).astype(F32)
        p = acc / cnt - h[:, cs]
        ys.append(_dot(p.astype(BF16), w_ref[gi]))
    y = (jnp.concatenate(ys, axis=-1) + b_ref[...]) * sc_ref[...]
    o_ref[0] = x + y


def _pool_mixer(x, g, w, bias, scale, ts=ROW_TILE):
    b, s, d = x.shape
    hb = ts // POOL_HALO
    return pl.pallas_call(
        functools.partial(_pool_body, ts=ts),
        out_shape=jax.ShapeDtypeStruct((b, s, d), F32),
        grid=(b, s // ts),
        in_specs=[pl.BlockSpec((1, ts, d), lambda bi, i: (bi, i, 0)),
                  pl.BlockSpec((1, POOL_HALO, d), lambda bi, i: (bi, jnp.maximum(i * hb - 1, 0), 0)),
                  _const_spec((1, d)), _const_spec(w.shape), _const_spec((1, d)), _const_spec((1, d))],
        out_specs=pl.BlockSpec((1, ts, d), lambda bi, i: (bi, i, 0)),
        scratch_shapes=[pltpu.VMEM((ts + POOL_HALO, d), F32)],
        compiler_params=_params(("parallel", "parallel")),
        name="pool",
    )(x, x, g.reshape(1, d), w, bias.reshape(1, d), scale.reshape(1, d))


def _conv_in_body(x_ref, g_ref, w_ref, b_ref, o_ref):
    d = x_ref.shape[-1]
    h = _rms(x_ref[...], g_ref[...]).astype(BF16)
    ag = _dot(h, w_ref[...]) + b_ref[...]
    o_ref[...] = ag[:, :d] * jax.nn.sigmoid(ag[:, d:])


def _conv_in(x2, g, w, bias, tm=ROW_TILE):
    t, d = x2.shape
    return pl.pallas_call(
        _conv_in_body,
        out_shape=jax.ShapeDtypeStruct((t, d), F32),
        grid=(t // tm,),
        in_specs=[pl.BlockSpec((tm, d), lambda i: (i, 0)), _const_spec((1, d)),
                  _const_spec(w.shape), _const_spec((1, 2 * d))],
        out_specs=pl.BlockSpec((tm, d), lambda i: (i, 0)),
        compiler_params=_params(("parallel",)),
        name="conv_in",
    )(x2, g.reshape(1, d), w, bias.reshape(1, 2 * d))


def _conv_out_body(u_ref, halo_ref, x_ref, dw_ref, bdw_ref, lng_ref, lnb_ref, w_ref, bo_ref, o_ref,
                   ubuf_ref, cbuf_ref, *, ts):
    i = pl.program_id(1)
    n_slab = ubuf_ref.shape[0]
    halo = jnp.where(i > 0, halo_ref[0], 0.0)
    for lb in range(n_slab):
        ls = slice(lb * LANES, (lb + 1) * LANES)
        ubuf_ref[lb, 0:CONV_HALO, :] = halo[:, ls]
        ubuf_ref[lb, CONV_HALO:CONV_HALO + ts, :] = u_ref[0, :, ls]
        ubuf_ref[lb, CONV_HALO + ts:, :] = jnp.zeros((SUBLANES, LANES), F32)
    lead = CONV_HALO - (CONV_WIDTH - 1)

    def slab(lb, carry):
        for r0 in range(0, ts, CONV_STRIP):
            acc = None
            for sh in range(SUBLANES):
                taps = [k for k in range(CONV_WIDTH) if (lead + k) % SUBLANES == sh]
                if not taps:
                    continue
                win = ubuf_ref[lb, pl.ds(r0 + sh, CONV_STRIP + CONV_HALO), :]
                for k in taps:
                    a8 = (lead + k) // SUBLANES * SUBLANES
                    term = win[a8:a8 + CONV_STRIP] * dw_ref[lb, k:k + 1, :]
                    acc = term if acc is None else acc + term
            cbuf_ref[lb, r0:r0 + CONV_STRIP, :] = acc + bdw_ref[lb]
        return carry

    lax.fori_loop(0, n_slab, slab, 0)
    c = jnp.concatenate([cbuf_ref[lb] for lb in range(n_slab)], axis=-1)
    mu = jnp.mean(c, axis=-1, keepdims=True)
    var = jnp.mean(jnp.square(c - mu), axis=-1, keepdims=True)
    n = (c - mu) * lax.rsqrt(var + EPS) * lng_ref[...] + lnb_ref[...]
    a = (n * jax.nn.sigmoid(n)).astype(BF16)
    o_ref[0] = x_ref[0] + _dot(a, w_ref[...]) + bo_ref[...]


def _conv_out(u, x, dw, b_dw, ln_g, ln_b, w_out, b_out, ts=ROW_TILE):
    b, s, d = x.shape
    hb = ts // CONV_HALO
    n_slab = d // LANES
    dw_pad = jnp.zeros((CONV_HALO, d), F32).at[:CONV_WIDTH].set(dw)
    dw_slab = dw_pad.reshape(CONV_HALO, n_slab, LANES).transpose(1, 0, 2)
    bdw_slab = b_dw.reshape(n_slab, 1, LANES)
    row = lambda v: v.reshape(1, d)
    return pl.pallas_call(
        functools.partial(_conv_out_body, ts=ts),
        out_shape=jax.ShapeDtypeStruct((b, s, d), F32),
        grid=(b, s // ts),
        in_specs=[pl.BlockSpec((1, ts, d), lambda bi, i: (bi, i, 0)),
                  pl.BlockSpec((1, CONV_HALO, d), lambda bi, i: (bi, jnp.maximum(i * hb - 1, 0), 0)),
                  pl.BlockSpec((1, ts, d), lambda bi, i: (bi, i, 0)),
                  _const_spec(dw_slab.shape), _const_spec(bdw_slab.shape), _const_spec((1, d)),
                  _const_spec((1, d)), _const_spec(w_out.shape), _const_spec((1, d))],
        out_specs=pl.BlockSpec((1, ts, d), lambda bi, i: (bi, i, 0)),
        scratch_shapes=[pltpu.VMEM((n_slab, ts + CONV_HALO + SUBLANES, LANES), F32),
                        pltpu.VMEM((n_slab, ts, LANES), F32)],
        compiler_params=_params(("parallel", "parallel")),
        name="conv_out",
    )(u, u, x, dw_slab, bdw_slab, row(ln_g), row(ln_b), w_out, row(b_out))


def _gla_in_body(x_ref, g_ref, wqk_ref, wv_ref, wog_ref, wgd_ref, qk_ref, v_ref, og_ref, gd_ref):
    h = _rms(x_ref[...], g_ref[...]).astype(BF16)
    qk_ref[...] = _dot(h, wqk_ref[...])
    v_ref[...] = _dot(h, wv_ref[...]).astype(BF16)
    og_ref[...] = _dot(h, wog_ref[...])
    gd_ref[...] = _dot(h, wgd_ref[...])


def _gla_in(x2, g, w_qk, w_v, w_og, w_gd, tm=ROW_TILE):
    t, d = x2.shape
    widths = (w_qk.shape[1], w_v.shape[1], w_og.shape[1], w_gd.shape[1])
    dtypes = (F32, BF16, F32, F32)
    return pl.pallas_call(
        _gla_in_body,
        out_shape=[jax.ShapeDtypeStruct((t, n), dt) for n, dt in zip(widths, dtypes)],
        grid=(t // tm,),
        in_specs=[pl.BlockSpec((tm, d), lambda i: (i, 0)), _const_spec((1, d)),
                  _const_spec(w_qk.shape), _const_spec(w_v.shape), _const_spec(w_og.shape),
                  _const_spec(w_gd.shape)],
        out_specs=[pl.BlockSpec((tm, n), lambda i: (i, 0)) for n in widths],
        compiler_params=_params(("parallel",)),
        name="gla_in",
    )(x2, g.reshape(1, d), w_qk, w_v, w_og, w_gd)


def _gla_body(q_ref, k_ref, v_ref, og_ref, gd_ref, wup_ref, bg_ref, ng_ref, o_ref, state_ref):
    @pl.when(pl.program_id(1) == 0)
    def _():
        state_ref[...] = jnp.zeros_like(state_ref)

    c, dk, dv = GLA_CHUNK, GLA_DK, GLA_DV
    ga = _dot(gd_ref[0].astype(BF16), wup_ref[...]) + bg_ref[...]
    log_a = (jnp.minimum(ga, 0.0) - jnp.log1p(jnp.exp(-jnp.abs(ga)))) / GLA_TAU
    ri = lax.broadcasted_iota(jnp.int32, (c, c), 0)
    ci = lax.broadcasted_iota(jnp.int32, (c, c), 1)
    causal = ci <= ri
    tril = jnp.where(causal, 1.0, 0.0).astype(BF16)
    scale = dk ** -0.5
    for ch in range(GLA_ROWS // c):
        rows = slice(ch * c, (ch + 1) * c)
        la = log_a[rows]
        hi = la.astype(BF16)
        r1 = la - hi.astype(F32)
        mid = r1.astype(BF16)
        lo = (r1 - mid.astype(F32)).astype(BF16)
        bcum = _dot(tril, hi) + _dot(tril, mid) + _dot(tril, lo)
        b_last = bcum[c - 1:c, :]
        k = k_ref[0, rows, :]
        q_t = (q_ref[0, rows, :] * scale * jnp.exp(bcum)).astype(BF16)
        k_t = (k * jnp.exp(-bcum)).astype(BF16)
        k_dec = (k * jnp.exp(b_last - bcum)).astype(BF16)
        decay = jnp.exp(b_last)
        for h in range(GLA_HEADS):
            ks = slice(h * dk, (h + 1) * dk)
            vs = slice(h * dv, (h + 1) * dv)
            v = v_ref[0, rows, vs]
            a = jnp.where(causal, _dot_nt(q_t[:, ks], k_t[:, ks]), 0.0)
            state = state_ref[h]
            o = _dot(a.astype(BF16), v) + _dot_nt(q_t[:, ks], state.astype(BF16))
            state_ref[h] = state * decay[:, ks] + _dot_tn(v, k_dec[:, ks])
            o = _rms(o, ng_ref[...])
            og = og_ref[0, rows, vs]
            o_ref[0, rows, vs] = (o * (og * jax.nn.sigmoid(og))).astype(BF16)


def _gla_core(qk, v, og, gd, w_up, b_gate, norm_g):
    b, s, _ = v.shape
    hh, dk, dv, rt = GLA_HEADS, GLA_DK, GLA_DV, GLA_ROWS
    return pl.pallas_call(
        _gla_body,
        out_shape=jax.ShapeDtypeStruct((b, s, hh * dv), BF16),
        grid=(b, s // rt),
        in_specs=[pl.BlockSpec((1, rt, hh * dk), lambda bi, i: (bi, i, 0)),
                  pl.BlockSpec((1, rt, hh * dk), lambda bi, i: (bi, i, 1)),
                  pl.BlockSpec((1, rt, hh * dv), lambda bi, i: (bi, i, 0)),
                  pl.BlockSpec((1, rt, hh * dv), lambda bi, i: (bi, i, 0)),
                  pl.BlockSpec((1, rt, LANES), lambda bi, i: (bi, i, 0)),
                  _const_spec((LANES, hh * dk)), _const_spec((1, hh * dk)), _const_spec((1, dv))],
        out_specs=pl.BlockSpec((1, rt, hh * dv), lambda bi, i: (bi, i, 0)),
        scratch_shapes=[pltpu.VMEM((hh, dv, dk), F32)],
        compiler_params=_params(("parallel", "arbitrary")),
        name="gla_core",
    )(qk, qk, v, og, gd, w_up, b_gate.reshape(1, hh * dk), norm_g.reshape(1, dv))


def _gla_mixer(x, g, w_in, w_gate_up, b_gate, norm_g, w_o):
    b, s, d = x.shape
    qw, vw = GLA_HEADS * GLA_DK, GLA_HEADS * GLA_DV
    x2 = x.reshape(b * s, d)
    w_gd = jnp.zeros((d, LANES), BF16).at[:, :GLA_GATE_RANK].set(w_in[:, 2 * qw + 2 * vw:].astype(BF16))
    w_up = jnp.zeros((LANES, qw), BF16).at[:GLA_GATE_RANK].set(w_gate_up.astype(BF16))
    qk, v, og, gd = _gla_in(x2, g, w_in[:, :2 * qw].astype(BF16), w_in[:, 2 * qw:2 * qw + vw].astype(BF16),
                            w_in[:, 2 * qw + vw:2 * qw + 2 * vw].astype(BF16), w_gd)
    r3 = lambda a: a.reshape(b, s, a.shape[-1])
    o = _gla_core(r3(qk), r3(v), r3(og), r3(gd), w_up, b_gate, norm_g)
    return _mm_res(o.reshape(b * s, vw), w_o.astype(BF16), x2).reshape(b, s, d)


def _nsa_in_body(x_ref, g_ref, wqt_ref, wk_ref, wvt_ref, wc_ref, wgt_ref,
                 qt_ref, kas_ref, kaw_ref, vast_ref, vawt_ref, c_ref, gtt_ref, *, ts):
    i = pl.program_id(1)
    qt, groups = NSA_QT, NSA_KV_HEADS
    h = _rms(x_ref[0], g_ref[...]).astype(BF16)
    qt_ref[0] = (_dot_nt(wqt_ref[...], h) * NSA_HEAD_DIM ** -0.5).astype(BF16)
    k = _dot(h, wk_ref[...])
    gw = groups * LANES
    lane = lax.broadcasted_iota(jnp.int32, (ts, LANES), 1)
    t = i * ts + lax.broadcasted_iota(jnp.int32, (ts, LANES), 0)
    onehot = jnp.where(lane - NSA_HEAD_DIM == t // NSA_SEL_BLOCK, 1.0, 0.0)
    for gi in range(groups):
        ls = slice(gi * LANES, (gi + 1) * LANES)
        kas_ref[0, :, ls] = (k[:, ls] + onehot).astype(BF16)
        kaw_ref[0, :, ls] = k[:, gw + gi * LANES:gw + (gi + 1) * LANES].astype(BF16)
    vt = _dot_nt(wvt_ref[...], h)
    ones = jnp.where(lax.broadcasted_iota(jnp.int32, (LANES, qt), 0) >= NSA_HEAD_DIM, 1.0, 0.0)
    for gi in range(groups):
        for tt in range(ts // qt):
            cs = slice(tt * qt, (tt + 1) * qt)
            vast_ref[0, gi, tt] = (vt[gi * LANES:(gi + 1) * LANES, cs] + ones).astype(BF16)
            vawt_ref[0, gi, tt] = (vt[gw + gi * LANES:gw + (gi + 1) * LANES, cs] + ones).astype(BF16)
    c_ref[0] = _dot(h, wc_ref[...])
    gtt_ref[0] = _dot_nt(wgt_ref[...], h)


def _nsa_in(x, g, w_qt, w_k, w_vt, w_c, w_gt, ts=ROW_TILE):
    b, s, d = x.shape
    groups, qt = NSA_KV_HEADS, NSA_QT
    gw = groups * LANES
    nkt = s // qt
    tpb = ts // qt
    seq_spec = lambda n: pl.BlockSpec((1, ts, n), lambda bi, i: (bi, i, 0))
    rows_spec = lambda n: pl.BlockSpec((1, n, ts), lambda bi, i: (bi, 0, i))
    vt_spec = pl.BlockSpec((1, groups, tpb, LANES, qt), lambda bi, i: (bi, 0, i, 0, 0))
    vt_shape = jax.ShapeDtypeStruct((b, groups, nkt, LANES, qt), BF16)
    return pl.pallas_call(
        functools.partial(_nsa_in_body, ts=ts),
        out_shape=[jax.ShapeDtypeStruct((b, w_qt.shape[0], s), BF16),
                   jax.ShapeDtypeStruct((b, s, gw), BF16), jax.ShapeDtypeStruct((b, s, gw), BF16),
                   vt_shape, vt_shape,
                   jax.ShapeDtypeStruct((b, s, w_c.shape[1]), F32),
                   jax.ShapeDtypeStruct((b, w_gt.shape[0], s), F32)],
        grid=(b, s // ts),
        in_specs=[pl.BlockSpec((1, ts, d), lambda bi, i: (bi, i, 0)), _const_spec((1, d)),
                  _const_spec(w_qt.shape), _const_spec(w_k.shape), _const_spec(w_vt.shape),
                  _const_spec(w_c.shape), _const_spec(w_gt.shape)],
        out_specs=[rows_spec(w_qt.shape[0]), seq_spec(gw), seq_spec(gw), vt_spec, vt_spec,
                   seq_spec(w_c.shape[1]), rows_spec(w_gt.shape[0])],
        compiler_params=_params(("parallel", "parallel")),
        name="nsa_in",
    )(x, g.reshape(1, d), w_qt, w_k, w_vt, w_c, w_gt)


def _nsa_cmp_body(a_ref, pt_ref, pb_ref, w1_ref, w2_ref, w2t_ref, o_ref, ot_ref):
    a = a_ref[0, 0, 0]
    half = a.shape[-1]
    u = _dot((a + pt_ref[0]).astype(BF16), w1_ref[0, :half, :])
    v = _dot((a + pb_ref[0]).astype(BF16), w1_ref[0, half:, :])
    nrow = a.shape[0]
    hid = jax.nn.gelu(u + pltpu.roll(v, nrow - 1, 0), approximate=True).astype(BF16)
    o_ref[0, 0, 0] = _dot(hid, w2_ref[0])
    ot_ref[0, 0, 0] = _dot_nt(w2t_ref[0], hid)


def _nsa_compress(a, pos_top, pos_bot, w1, w2):
    _, b, g, nch, feat = a.shape
    dh = w2.shape[-1]
    w2t = w2.transpose(0, 2, 1)
    per_kv = lambda shape: pl.BlockSpec((1,) + shape, lambda kv, bi, gi: (kv,) + (0,) * len(shape))
    return pl.pallas_call(
        _nsa_cmp_body,
        out_shape=[jax.ShapeDtypeStruct((2, b, g, nch, dh), F32), jax.ShapeDtypeStruct((2, b, g, dh, nch), F32)],
        grid=(2, b, g),
        in_specs=[pl.BlockSpec((1, 1, 1, nch, feat), lambda kv, bi, gi: (kv, bi, gi, 0, 0)),
                  per_kv((1, feat)), per_kv((1, feat)), per_kv(w1.shape[1:]), per_kv(w2.shape[1:]),
                  per_kv(w2t.shape[1:])],
        out_specs=[pl.BlockSpec((1, 1, 1, nch, dh), lambda kv, bi, gi: (kv, bi, gi, 0, 0)),
                   pl.BlockSpec((1, 1, 1, dh, nch), lambda kv, bi, gi: (kv, bi, gi, 0, 0))],
        compiler_params=_params(("parallel", "parallel", "parallel")),
        name="nsa_compress",
    )(a, pos_top, pos_bot, w1, w2, w2t)


def _nsa_attn_body(qt_ref, kc_ref, vct_ref, kas_ref, vast_ref, kaw_ref, vawt_ref, gtt_ref, ovt_ref, o_ref,
                   qaug_ref, m_ref, acc_ref, p_ref, alpha_ref, *, n_slc):
    qt, rep, dh = NSA_QT, NSA_REP, NSA_HEAD_DIM
    cols = rep * qt
    qi = pl.program_id(2)
    s0 = qi * qt
    q = jnp.concatenate([qt_ref[0, r * dh:(r + 1) * dh, :] for r in range(rep)], axis=1)
    t_col = s0 + (lax.broadcasted_iota(jnp.int32, (1, cols), 1) & (qt - 1))

    n_pad = kc_ref.shape[2]
    sc = _dot(kc_ref[0, 0].astype(BF16), q)
    cmp_end = lax.broadcasted_iota(jnp.int32, (n_pad, 1), 0) * NSA_CMP_STRIDE + (NSA_CMP_BLOCK - 1)
    sc = jnp.where(cmp_end <= t_col, sc, NEG)
    e = jnp.exp(sc - jnp.max(sc, axis=0, keepdims=True))
    p_cmp = e / jnp.sum(e, axis=0, keepdims=True)
    p_cmp = jnp.where(t_col >= NSA_CMP_BLOCK - 1, p_cmp, 0.0).astype(BF16)
    o_cmp = _dot(vct_ref[0, 0].astype(BF16), p_cmp)

    imp = _dot(ovt_ref[...], p_cmp[:, 0:qt])
    for r in range(1, rep):
        imp = imp + _dot(ovt_ref[...], p_cmp[:, r * qt:(r + 1) * qt])
    imp_t = imp[:n_slc]
    blk = lax.broadcasted_iota(jnp.int32, (n_slc, 1), 0)
    t_q = s0 + lax.broadcasted_iota(jnp.int32, (1, qt), 1)
    cur = t_q // NSA_SEL_BLOCK
    forced = (blk == 0) | (blk == cur) | (blk == cur - 1)
    visible = blk * NSA_SEL_BLOCK <= t_q
    imp_t = jnp.where(forced, NSA_FORCE, imp_t)
    imp_t = jnp.where(visible, imp_t, -1.0)
    rank = jnp.zeros((n_slc, qt), F32)
    for i in range(n_slc):
        row = imp_t[i:i + 1, :]
        beats = (row > imp_t) | ((row == imp_t) & (blk > i))
        rank = rank + jnp.where(beats, 1.0, 0.0)
    sel = (rank < float(min(NSA_N_SELECT, n_slc))) & visible
    bias_parts = [jnp.where(sel, 0.0, NEG)]
    if LANES - dh - n_slc:
        bias_parts.append(jnp.zeros((LANES - dh - n_slc, qt), F32))
    bias = jnp.concatenate(bias_parts, axis=0).astype(BF16)
    qaug_ref[0:dh, :] = q
    for r in range(rep):
        qaug_ref[dh:, r * qt:(r + 1) * qt] = bias

    key_r = lax.broadcasted_iota(jnp.int32, (qt, 1), 0)

    heads = [slice(r * qt, (r + 1) * qt) for r in range(rep)]

    def flush(vat_ref, jp):
        vat = vat_ref[0, 0, jp]
        pvs = [_dot(vat, p_ref[:, cs]) for cs in heads]
        return pvs

    def accumulate(pvs):
        for cs, pv in zip(heads, pvs):
            acc_ref[:, cs] = acc_ref[:, cs] * alpha_ref[:, cs] + pv

    def step(ka_ref, vat_ref, j, jp, mask_fn):
        ka = ka_ref[0, pl.ds(pl.multiple_of(j * qt, qt), qt), :]
        scores = [_dot(ka, qaug_ref[:, cs]) for cs in heads]
        pvs = flush(vat_ref, jp)
        probs, alphas = [], []
        for cs, s in zip(heads, scores):
            if mask_fn is not None:
                s = jnp.where(mask_fn(j * qt + key_r, t_col[:, cs]), s, NEG)
            m_old = m_ref[:, cs]
            m_new = jnp.maximum(m_old, jnp.max(s, axis=0, keepdims=True))
            probs.append(jnp.exp(s - m_new).astype(BF16))
            alphas.append(jnp.exp(m_old - m_new))
            m_ref[:, cs] = m_new
        accumulate(pvs)
        for cs, p, alpha in zip(heads, probs, alphas):
            p_ref[:, cs] = p
            alpha_ref[:, cs] = alpha

    def finish(vat_ref, j_last):
        accumulate(flush(vat_ref, j_last))
        acc = acc_ref[...]
        return acc[0:dh] / acc[dh:dh + 1]

    def reset():
        m_ref[...] = jnp.full_like(m_ref, NEG)
        acc_ref[...] = jnp.zeros_like(acc_ref)
        p_ref[...] = jnp.zeros_like(p_ref)
        alpha_ref[...] = jnp.ones_like(alpha_ref)

    reset()

    def sel_step(j, carry):
        step(kas_ref, vast_ref, j, jnp.maximum(j - 1, 0), None)
        return carry

    lax.fori_loop(0, qi, sel_step, 0)
    step(kas_ref, vast_ref, qi, jnp.maximum(qi - 1, 0), lambda kpos, t: kpos <= t)
    o_slc = finish(vast_ref, qi)

    reset()
    win_mask = lambda kpos, t: (kpos <= t) & (kpos > t - NSA_WINDOW)

    def win_step(j, carry):
        step(kaw_ref, vawt_ref, j, jnp.maximum(j - 1, 0), win_mask)
        return carry

    lax.fori_loop(jnp.maximum(qi - NSA_WINDOW // qt, 0), qi + 1, win_step, 0)
    o_win = finish(vawt_ref, qi)

    gates = jax.nn.sigmoid(gtt_ref[0])
    for r in range(rep):
        cs = slice(r * qt, (r + 1) * qt)
        o = (gates[3 * r:3 * r + 1] * o_cmp[:, cs] + gates[3 * r + 1:3 * r + 2] * o_slc[:, cs]
             + gates[3 * r + 2:3 * r + 3] * o_win[:, cs])
        o_ref[0, r * dh:(r + 1) * dh, :] = o.astype(BF16)


def _nsa_attn(q_t, kcmp, vcmp_t, kas, vas_t, kaw, vaw_t, gates_t, ov_t):
    b, hd, s = q_t.shape
    g, rep, dh, qt = NSA_KV_HEADS, NSA_REP, NSA_HEAD_DIM, NSA_QT
    n_slc = s // NSA_SEL_BLOCK
    nch = kcmp.shape[2]
    nkt = s // qt
    k_spec = pl.BlockSpec((1, s, LANES), lambda bi, gi, i: (bi, 0, gi))
    vt_spec = pl.BlockSpec((1, 1, nkt, LANES, qt), lambda bi, gi, i: (bi, gi, 0, 0, 0))
    q_spec = pl.BlockSpec((1, rep * dh, qt), lambda bi, gi, i: (bi, gi, i))
    return pl.pallas_call(
        functools.partial(_nsa_attn_body, n_slc=n_slc),
        out_shape=jax.ShapeDtypeStruct((b, hd, s), BF16),
        grid=(b, g, s // qt),
        in_specs=[q_spec,
                  pl.BlockSpec((1, 1, nch, dh), lambda bi, gi, i: (bi, gi, 0, 0)),
                  pl.BlockSpec((1, 1, dh, nch), lambda bi, gi, i: (bi, gi, 0, 0)),
                  k_spec, vt_spec, k_spec, vt_spec,
                  pl.BlockSpec((1, NSA_GATE_ROWS, qt), lambda bi, gi, i: (bi, gi, i)),
                  _const_spec(ov_t.shape)],
        out_specs=q_spec,
        scratch_shapes=[pltpu.VMEM((LANES, rep * qt), BF16), pltpu.VMEM((1, rep * qt), F32),
                        pltpu.VMEM((LANES, rep * qt), F32), pltpu.VMEM((qt, rep * qt), BF16),
                        pltpu.VMEM((1, rep * qt), F32)],
        compiler_params=_params(("parallel", "parallel", "parallel")),
        name="nsa_attn",
    )(q_t, kcmp, vcmp_t, kas, vas_t, kaw, vaw_t, gates_t, ov_t)


def _nsa_overlap_t(s):
    n_cmp = (s - NSA_CMP_BLOCK) // NSA_CMP_STRIDE + 1
    n_slc = s // NSA_SEL_BLOCK
    cs = np.arange(n_cmp) * NSA_CMP_STRIDE
    ss = np.arange(n_slc) * NSA_SEL_BLOCK
    ov = np.clip(np.minimum(cs[:, None] + NSA_CMP_BLOCK, ss[None, :] + NSA_SEL_BLOCK)
                 - np.maximum(cs[:, None], ss[None, :]), 0, None) / NSA_CMP_BLOCK
    out = np.zeros((LANES, s // NSA_CMP_STRIDE), np.float32)
    out[:n_slc, :n_cmp] = ov.T
    return jnp.asarray(out, BF16)


def _mm_res_t_body(at_ref, w_ref, x_ref, o_ref):
    o_ref[0] = x_ref[0] + _dot_tn(at_ref[0], w_ref[...])


def _mm_res_t(a_t, w, x, ts=ROW_TILE):
    b, k, s = a_t.shape
    n = w.shape[1]
    return pl.pallas_call(
        _mm_res_t_body,
        out_shape=jax.ShapeDtypeStruct((b, s, n), F32),
        grid=(b, s // ts),
        in_specs=[pl.BlockSpec((1, k, ts), lambda bi, i: (bi, 0, i)), _const_spec(w.shape),
                  pl.BlockSpec((1, ts, n), lambda bi, i: (bi, i, 0))],
        out_specs=pl.BlockSpec((1, ts, n), lambda bi, i: (bi, i, 0)),
        compiler_params=_params(("parallel", "parallel")),
        name="mm_res_t",
    )(a_t, w, x)


def _nsa_mixer(x, g, w_in, cmp_pos, cmp_w1, cmp_w2, w_o):
    b, s, d = x.shape
    hh, gg, rep, dh = NSA_HEADS, NSA_KV_HEADS, NSA_REP, NSA_HEAD_DIM
    kvw = gg * dh
    assert s % ROW_TILE == 0 and s // NSA_SEL_BLOCK <= LANES - dh
    wb = w_in.astype(BF16)
    parts = [wb[:, d + i * kvw:d + (i + 1) * kvw].reshape(d, gg, dh) for i in range(6)]
    zeros = jnp.zeros((d, gg, LANES - dh), BF16)
    padded = lambda p: jnp.concatenate([p, zeros], axis=-1).reshape(d, gg * LANES)
    w_k = jnp.concatenate([padded(parts[2]), padded(parts[4])], axis=-1)
    w_vt = jnp.concatenate([padded(parts[3]), padded(parts[5])], axis=-1).T
    w_c = wb[:, d:d + 2 * kvw]
    w_g = wb[:, d + 6 * kvw:].reshape(d, gg, 3 * rep)
    w_gt = jnp.concatenate([w_g, jnp.zeros((d, gg, NSA_GATE_ROWS - 3 * rep), BF16)], axis=-1)
    w_gt = w_gt.reshape(d, gg * NSA_GATE_ROWS).T
    q_t, kas, kaw, vas_t, vaw_t, c, gates_t = _nsa_in(x, g, wb[:, :d].T, w_k, w_vt, w_c, w_gt)

    nch = s // NSA_CMP_STRIDE
    a = c.reshape(b, nch, NSA_CMP_STRIDE, 2, gg, dh).transpose(3, 0, 4, 1, 2, 5)
    a = a.reshape(2, b, gg, nch, NSA_CMP_STRIDE * dh)
    pos = cmp_pos.reshape(2, 2, 1, NSA_CMP_STRIDE * dh)
    kvc, kvc_t = _nsa_compress(a, pos[:, 0], pos[:, 1], cmp_w1.astype(BF16), cmp_w2.astype(BF16))

    o_t = _nsa_attn(q_t, kvc[0], kvc_t[1], kas, vas_t, kaw, vaw_t, gates_t, _nsa_overlap_t(s))
    return _mm_res_t(o_t, w_o.astype(BF16), x)


def kernel(x, mem, ffn1_norm, ffn1_w_in, ffn1_w_out, mix_norm, xattn_norm, mem_norm, xattn_w_q, xattn_w_kv, xattn_w_o, ffn2_norm, ffn2_w_in, ffn2_w_out, pool_w, pool_b, pool_scale, nsa_w_in, nsa_cmp_pos, nsa_cmp_w1, nsa_cmp_w2, nsa_w_o, gla_w_in, gla_w_gate_up, gla_b_gate, gla_norm, gla_w_o, conv_w_in, conv_b_in, conv_dw, conv_b_dw, conv_ln_g, conv_ln_b, conv_w_out, conv_b_out, final_norm):
    b, s, d = x.shape
    n_mem = mem.shape[1]
    depth = ffn1_norm.shape[0]
    n_mixers = 4
    flat = lambda a: a.reshape(b * s, d)
    cube = lambda a: a.reshape(b, s, d)
    mem2 = mem.reshape(b * n_mem, d)
    for i in range(depth):
        m, j = i % n_mixers, i // n_mixers
        x = cube(_ffn(flat(x), ffn1_norm[i], ffn1_w_in[i].astype(BF16), ffn1_w_out[i].astype(BF16)))
        if m == 0:
            x = _pool_mixer(x, mix_norm[i], pool_w[j].astype(BF16), pool_b[j], pool_scale[j])
        elif m == 1:
            x = _nsa_mixer(x, mix_norm[i], nsa_w_in[j], nsa_cmp_pos[j], nsa_cmp_w1[j], nsa_cmp_w2[j], nsa_w_o[j])
        elif m == 2:
            x = _gla_mixer(x, mix_norm[i], gla_w_in[j], gla_w_gate_up[j], gla_b_gate[j], gla_norm[j], gla_w_o[j])
        else:
            u = _conv_in(flat(x), mix_norm[i], conv_w_in[j].astype(BF16), conv_b_in[j])
            x = _conv_out(cube(u), x, conv_dw[j], conv_b_dw[j], conv_ln_g[j], conv_ln_b[j],
                          conv_w_out[j].astype(BF16), conv_b_out[j])
        kv = _rms_mm(mem2, mem_norm[i], xattn_w_kv[i].astype(BF16), BF16, tm=n_mem)
        x = _xattn(x, xattn_norm[i], xattn_w_q[i].astype(BF16), kv.reshape(b, n_mem, 2 * d),
                   xattn_w_o[i].astype(BF16))
        final_g = final_norm if i == depth - 1 else None
        x = cube(_ffn(flat(x), ffn2_norm[i], ffn2_w_in[i].astype(BF16), ffn2_w_out[i].astype(BF16),
                      final_g=final_g))
    return x
```

```python
import functools

import numpy as np
import jax
import jax.numpy as jnp
from jax import lax
from jax.experimental import pallas as pl
from jax.experimental.pallas import tpu as pltpu

F32 = jnp.float32
BF16 = jnp.bfloat16

EPS = 1e-6
NEG = -1e30

V7X_VMEM_BYTES = 64 * 1024 * 1024
VMEM_LIMIT = V7X_VMEM_BYTES - 8 * 1024 * 1024
LANES = 128
SUBLANES = 8

POOL_WINDOWS = (2, 4, 8, 16)
POOL_HALO = 16

NSA_HEADS = 16
NSA_KV_HEADS = 4
NSA_REP = NSA_HEADS // NSA_KV_HEADS
NSA_HEAD_DIM = 64
NSA_CMP_BLOCK = 32
NSA_CMP_STRIDE = 16
NSA_SEL_BLOCK = 64
NSA_N_SELECT = 16
NSA_WINDOW = 512
NSA_FORCE = 1e4
NSA_QT = 256
NSA_GATE_ROWS = 16

GLA_HEADS = 4
GLA_DK = 128
GLA_DV = 256
GLA_GATE_RANK = 16
GLA_TAU = 16.0
GLA_CHUNK = 64
GLA_ROWS = 256

CONV_WIDTH = 31
CONV_HALO = 32
CONV_STRIP = 128

XATTN_HEADS = 4

ROW_TILE = 512


def _params(sem):
    return pltpu.CompilerParams(dimension_semantics=sem, vmem_limit_bytes=VMEM_LIMIT)


def _rms(x, g):
    return x * lax.rsqrt(jnp.mean(x * x, axis=-1, keepdims=True) + EPS) * g


def _dot(a, b):
    return jnp.dot(a, b, preferred_element_type=F32)


def _dot_nt(a, b):
    return lax.dot_general(a, b, (((1,), (1,)), ((), ())), preferred_element_type=F32)


def _dot_tn(a, b):
    return lax.dot_general(a, b, (((0,), (0,)), ((), ())), preferred_element_type=F32)


def _const_spec(shape):
    nd = len(shape)
    return pl.BlockSpec(shape, lambda *_: (0,) * nd, pipeline_mode=pl.Buffered(1))


def _ffn_body(x_ref, g_ref, win_ref, wout_ref, *rest, d_ff, final):
    if final:
        fg_ref, o_ref = rest
    else:
        (o_ref,) = rest
    x = x_ref[...]
    h = _rms(x, g_ref[...]).astype(BF16)
    gu = _dot(h, win_ref[...])
    g = gu[:, :d_ff]
    u = gu[:, d_ff:]
    a = (g * jax.nn.sigmoid(g) * u).astype(BF16)
    out = x + 0.5 * _dot(a, wout_ref[...])
    if final:
        out = _rms(out, fg_ref[...])
    o_ref[...] = out


def _ffn(x2, g, w_in, w_out, final_g=None, tm=ROW_TILE):
    t, d = x2.shape
    d_ff = w_out.shape[0]
    final = final_g is not None
    in_specs = [pl.BlockSpec((tm, d), lambda i: (i, 0)), _const_spec((1, d)),
                _const_spec(w_in.shape), _const_spec(w_out.shape)]
    args = [x2, g.reshape(1, d), w_in, w_out]
    if final:
        in_specs.append(_const_spec((1, d)))
        args.append(final_g.reshape(1, d))
    return pl.pallas_call(
        functools.partial(_ffn_body, d_ff=d_ff, final=final),
        out_shape=jax.ShapeDtypeStruct((t, d), F32),
        grid=(t // tm,),
        in_specs=in_specs,
        out_specs=pl.BlockSpec((tm, d), lambda i: (i, 0)),
        compiler_params=_params(("parallel",)),
        name="ffn",
    )(*args)


def _rms_mm_body(x_ref, g_ref, w_ref, o_ref):
    h = _rms(x_ref[...], g_ref[...]).astype(BF16)
    o_ref[...] = _dot(h, w_ref[...]).astype(o_ref.dtype)


def _rms_mm(x2, g, w, out_dtype, tm):
    t, d = x2.shape
    n = w.shape[1]
    return pl.pallas_call(
        _rms_mm_body,
        out_shape=jax.ShapeDtypeStruct((t, n), out_dtype),
        grid=(t // tm,),
        in_specs=[pl.BlockSpec((tm, d), lambda i: (i, 0)), _const_spec((1, d)), _const_spec(w.shape)],
        out_specs=pl.BlockSpec((tm, n), lambda i: (i, 0)),
        compiler_params=_params(("parallel",)),
        name="rms_mm",
    )(x2, g.reshape(1, d), w)


def _xattn_body(x_ref, g_ref, wq_ref, k_ref, v_ref, wo_ref, o_ref, *, heads):
    x = x_ref[0]
    d = x.shape[-1]
    dh = d // heads
    h = _rms(x, g_ref[...]).astype(BF16)
    q = (_dot(h, wq_ref[...]) * dh ** -0.5).astype(BF16)
    cols = [slice(hd * dh, (hd + 1) * dh) for hd in range(heads)]
    scores = [_dot_nt(q[:, sl], k_ref[0][:, sl]) for sl in cols]
    probs = []
    for s in scores:
        e = jnp.exp(s - jnp.max(s, axis=-1, keepdims=True))
        probs.append((e * (1.0 / jnp.sum(e, axis=-1, keepdims=True))).astype(BF16))
    o = jnp.concatenate([_dot(p, v_ref[0][:, sl]).astype(BF16) for p, sl in zip(probs, cols)], axis=-1)
    o_ref[0] = x + _dot(o, wo_ref[...])


def _xattn(x, g, w_q, kv, w_o, ts=ROW_TILE):
    b, s, d = x.shape
    n = kv.shape[1]
    return pl.pallas_call(
        functools.partial(_xattn_body, heads=XATTN_HEADS),
        out_shape=jax.ShapeDtypeStruct((b, s, d), F32),
        grid=(b, s // ts),
        in_specs=[pl.BlockSpec((1, ts, d), lambda bi, i: (bi, i, 0)), _const_spec((1, d)),
                  _const_spec(w_q.shape),
                  pl.BlockSpec((1, n, d), lambda bi, i: (bi, 0, 0)),
                  pl.BlockSpec((1, n, d), lambda bi, i: (bi, 0, 1)),
                  _const_spec(w_o.shape)],
        out_specs=pl.BlockSpec((1, ts, d), lambda bi, i: (bi, i, 0)),
        compiler_params=_params(("parallel", "parallel")),
        name="xattn",
    )(x, g.reshape(1, d), w_q, kv, kv, w_o)


def _pool_body(x_ref, halo_ref, g_ref, w_ref, b_ref, sc_ref, o_ref, hbuf_ref, *, ts):
    i = pl.program_id(1)
    x = x_ref[0]
    d = x.shape[-1]
    gw = d // len(POOL_WINDOWS)
    g = g_ref[...]
    h = _rms(x, g)
    hh = _rms(halo_ref[0], g)
    hbuf_ref[0:POOL_HALO, :] = jnp.where(i > 0, hh, 0.0)
    hbuf_ref[POOL_HALO:, :] = h
    t = i * ts + lax.broadcasted_iota(jnp.int32, (ts, 1), 0)
    ys = []
    for gi, win in enumerate(POOL_WINDOWS):
        cs = slice(gi * gw, (gi + 1) * gw)
        acc = h[:, cs]
        for k in range(1, win):
            acc = acc + hbuf_ref[POOL_HALO - k:POOL_HALO - k + ts, cs]
        cnt = jnp.minimum(t + 1, win).astype(F32)
        p = acc / cnt - h[:, cs]
        ys.append(_dot(p.astype(BF16), w_ref[gi]))
    y = (jnp.concatenate(ys, axis=-1) + b_ref[...]) * sc_ref[...]
    o_ref[0] = x + y


def _pool_mixer(x, g, w, bias, scale, ts=ROW_TILE):
    b, s, d = x.shape
    hb = ts // POOL_HALO
    return pl.pallas_call(
        functools.partial(_pool_body, ts=ts),
        out_shape=jax.ShapeDtypeStruct((b, s, d), F32),
        grid=(b, s // ts),
        in_specs=[pl.BlockSpec((1, ts, d), lambda bi, i: (bi, i, 0)),
                  pl.BlockSpec((1, POOL_HALO, d), lambda bi, i: (bi, jnp.maximum(i * hb - 1, 0), 0)),
                  _const_spec((1, d)), _const_spec(w.shape), _const_spec((1, d)), _const_spec((1, d))],
        out_specs=pl.BlockSpec((1, ts, d), lambda bi, i: (bi, i, 0)),
        scratch_shapes=[pltpu.VMEM((ts + POOL_HALO, d), F32)],
        compiler_params=_params(("parallel", "parallel")),
        name="pool",
    )(x, x, g.reshape(1, d), w, bias.reshape(1, d), scale.reshape(1, d))


def _conv_in_body(x_ref, g_ref, w_ref, b_ref, o_ref):
    d = x_ref.shape[-1]
    h = _rms(x_ref[...], g_ref[...]).astype(BF16)
    ag = _dot(h, w_ref[...]) + b_ref[...]
    o_ref[...] = ag[:, :d] * jax.nn.sigmoid(ag[:, d:])


def _conv_in(x2, g, w, bias, tm=ROW_TILE):
    t, d = x2.shape
    return pl.pallas_call(
        _conv_in_body,
        out_shape=jax.ShapeDtypeStruct((t, d), F32),
        grid=(t // tm,),
        in_specs=[pl.BlockSpec((tm, d), lambda i: (i, 0)), _const_spec((1, d)),
                  _const_spec(w.shape), _const_spec((1, 2 * d))],
        out_specs=pl.BlockSpec((tm, d), lambda i: (i, 0)),
        compiler_params=_params(("parallel",)),
        name="conv_in",
    )(x2, g.reshape(1, d), w, bias.reshape(1, 2 * d))


def _conv_out_body(u_ref, halo_ref, x_ref, dw_ref, bdw_ref, lng_ref, lnb_ref, w_ref, bo_ref, o_ref,
                   ubuf_ref, cbuf_ref, *, ts):
    i = pl.program_id(1)
    n_slab = ubuf_ref.shape[0]
    halo = jnp.where(i > 0, halo_ref[0], 0.0)
    for lb in range(n_slab):
        ls = slice(lb * LANES, (lb + 1) * LANES)
        ubuf_ref[lb, 0:CONV_HALO, :] = halo[:, ls]
        ubuf_ref[lb, CONV_HALO:CONV_HALO + ts, :] = u_ref[0, :, ls]
        ubuf_ref[lb, CONV_HALO + ts:, :] = jnp.zeros((SUBLANES, LANES), F32)
    lead = CONV_HALO - (CONV_WIDTH - 1)

    def slab(lb, carry):
        for r0 in range(0, ts, CONV_STRIP):
            acc = None
            for sh in range(SUBLANES):
                taps = [k for k in range(CONV_WIDTH) if (lead + k) % SUBLANES == sh]
                if not taps:
                    continue
                win = ubuf_ref[lb, pl.ds(r0 + sh, CONV_STRIP + CONV_HALO), :]
                for k in taps:
                    a8 = (lead + k) // SUBLANES * SUBLANES
                    term = win[a8:a8 + CONV_STRIP] * dw_ref[lb, k:k + 1, :]
                    acc = term if acc is None else acc + term
            cbuf_ref[lb, r0:r0 + CONV_STRIP, :] = acc + bdw_ref[lb]
        return carry

    lax.fori_loop(0, n_slab, slab, 0)
    c = jnp.concatenate([cbuf_ref[lb] for lb in range(n_slab)], axis=-1)
    mu = jnp.mean(c, axis=-1, keepdims=True)
    var = jnp.mean(jnp.square(c - mu), axis=-1, keepdims=True)
    n = (c - mu) * lax.rsqrt(var + EPS) * lng_ref[...] + lnb_ref[...]
    a = (n * jax.nn.sigmoid(n)).astype(BF16)
    o_ref[0] = x_ref[0] + _dot(a, w_ref[...]) + bo_ref[...]


def _conv_out(u, x, dw, b_dw, ln_g, ln_b, w_out, b_out, ts=ROW_TILE):
    b, s, d = x.shape
    hb = ts // CONV_HALO
    n_slab = d // LANES
    dw_pad = jnp.zeros((CONV_HALO, d), F32).at[:CONV_WIDTH].set(dw)
    dw_slab = dw_pad.reshape(CONV_HALO, n_slab, LANES).transpose(1, 0, 2)
    bdw_slab = b_dw.reshape(n_slab, 1, LANES)
    row = lambda v: v.reshape(1, d)
    return pl.pallas_call(
        functools.partial(_conv_out_body, ts=ts),
        out_shape=jax.ShapeDtypeStruct((b, s, d), F32),
        grid=(b, s // ts),
        in_specs=[pl.BlockSpec((1, ts, d), lambda bi, i: (bi, i, 0)),
                  pl.BlockSpec((1, CONV_HALO, d), lambda bi, i: (bi, jnp.maximum(i * hb - 1, 0), 0)),
                  pl.BlockSpec((1, ts, d), lambda bi, i: (bi, i, 0)),
                  _const_spec(dw_slab.shape), _const_spec(bdw_slab.shape), _const_spec((1, d)),
                  _const_spec((1, d)), _const_spec(w_out.shape), _const_spec((1, d))],
        out_specs=pl.BlockSpec((1, ts, d), lambda bi, i: (bi, i, 0)),
        scratch_shapes=[pltpu.VMEM((n_slab, ts + CONV_HALO + SUBLANES, LANES), F32),
                        pltpu.VMEM((n_slab, ts, LANES), F32)],
        compiler_params=_params(("parallel", "parallel")),
        name="conv_out",
    )(u, u, x, dw_slab, bdw_slab, row(ln_g), row(ln_b), w_out, row(b_out))


def _gla_in_body(x_ref, g_ref, wqk_ref, wv_ref, wog_ref, wgd_ref, qk_ref, v_ref, og_ref, gd_ref):
    h = _rms(x_ref[...], g_ref[...]).astype(BF16)
    qk_ref[...] = _dot(h, wqk_ref[...])
    v_ref[...] = _dot(h, wv_ref[...]).astype(BF16)
    og_ref[...] = _dot(h, wog_ref[...])
    gd_ref[...] = _dot(h, wgd_ref[...])


def _gla_in(x2, g, w_qk, w_v, w_og, w_gd, tm=ROW_TILE):
    t, d = x2.shape
    widths = (w_qk.shape[1], w_v.shape[1], w_og.shape[1], w_gd.shape[1])
    dtypes = (F32, BF16, F32, F32)
    return pl.pallas_call(
        _gla_in_body,
        out_shape=[jax.ShapeDtypeStruct((t, n), dt) for n, dt in zip(widths, dtypes)],
        grid=(t // tm,),
        in_specs=[pl.BlockSpec((tm, d), lambda i: (i, 0)), _const_spec((1, d)),
                  _const_spec(w_qk.shape), _const_spec(w_v.shape), _const_spec(w_og.shape),
                  _const_spec(w_gd.shape)],
        out_specs=[pl.BlockSpec((tm, n), lambda i: (i, 0)) for n in widths],
        compiler_params=_params(("parallel",)),
        name="gla_in",
    )(x2, g.reshape(1, d), w_qk, w_v, w_og, w_gd)


def _gla_body(q_ref, k_ref, v_ref, og_ref, gd_ref, wup_ref, bg_ref, ng_ref, x_ref, wo_ref, o_ref,
              state_ref, obuf_ref):
    @pl.when(pl.program_id(1) == 0)
    def _():
        state_ref[...] = jnp.zeros_like(state_ref)

    c, dk, dv = GLA_CHUNK, GLA_DK, GLA_DV
    ga = _dot(gd_ref[0].astype(BF16), wup_ref[...]) + bg_ref[...]
    log_a = (jnp.minimum(ga, 0.0) - jnp.log1p(jnp.exp(-jnp.abs(ga)))) / GLA_TAU
    ri = lax.broadcasted_iota(jnp.int32, (c, c), 0)
    ci = lax.broadcasted_iota(jnp.int32, (c, c), 1)
    causal = ci <= ri
    tril = jnp.where(causal, 1.0, 0.0).astype(BF16)
    scale = dk ** -0.5
    n_chunks = GLA_ROWS // c
    chunk_rows = [slice(ch * c, (ch + 1) * c) for ch in range(n_chunks)]
    cases = [(ch, h) for ch in range(n_chunks) for h in range(GLA_HEADS)]
    hi = log_a.astype(BF16)
    r1 = log_a - hi.astype(F32)
    mid = r1.astype(BF16)
    lo = (r1 - mid.astype(F32)).astype(BF16)
    q_t, k_t, k_dec, decay = [], [], [], []
    for rows in chunk_rows:
        bcum = _dot(tril, hi[rows]) + _dot(tril, mid[rows]) + _dot(tril, lo[rows])
        b_last = bcum[c - 1:c, :]
        k = k_ref[0, rows, :]
        q_t.append((q_ref[0, rows, :] * scale * jnp.exp(bcum)).astype(BF16))
        k_t.append((k * jnp.exp(-bcum)).astype(BF16))
        k_dec.append((k * jnp.exp(b_last - bcum)).astype(BF16))
        decay.append(jnp.exp(b_last))
    ks = [slice(h * dk, (h + 1) * dk) for h in range(GLA_HEADS)]
    vs = [slice(h * dv, (h + 1) * dv) for h in range(GLA_HEADS)]
    v = {(ch, h): v_ref[0, chunk_rows[ch], vs[h]] for ch, h in cases}
    a = {(ch, h): jnp.where(causal, _dot_nt(q_t[ch][:, ks[h]], k_t[ch][:, ks[h]]), 0.0).astype(BF16)
         for ch, h in cases}
    kv = {(ch, h): _dot_tn(v[ch, h], k_dec[ch][:, ks[h]]) for ch, h in cases}
    o_intra = {(ch, h): _dot(a[ch, h], v[ch, h]) for ch, h in cases}
    state = [state_ref[h] for h in range(GLA_HEADS)]
    for ch, h in cases:
        o = o_intra[ch, h] + _dot_nt(q_t[ch][:, ks[h]], state[h].astype(BF16))
        state[h] = state[h] * decay[ch][:, ks[h]] + kv[ch, h]
        o = _rms(o, ng_ref[...])
        og = og_ref[0, chunk_rows[ch], vs[h]]
        obuf_ref[chunk_rows[ch], vs[h]] = (o * (og * jax.nn.sigmoid(og))).astype(BF16)
    for h in range(GLA_HEADS):
        state_ref[h] = state[h]
    o_ref[0] = x_ref[0] + _dot(obuf_ref[...], wo_ref[...])


def _gla_core(qk, v, og, gd, w_up, b_gate, norm_g, x, w_o):
    b, s, d = x.shape
    hh, dk, dv, rt = GLA_HEADS, GLA_DK, GLA_DV, GLA_ROWS
    return pl.pallas_call(
        _gla_body,
        out_shape=jax.ShapeDtypeStruct((b, s, d), F32),
        grid=(b, s // rt),
        in_specs=[pl.BlockSpec((1, rt, hh * dk), lambda bi, i: (bi, i, 0)),
                  pl.BlockSpec((1, rt, hh * dk), lambda bi, i: (bi, i, 1)),
                  pl.BlockSpec((1, rt, hh * dv), lambda bi, i: (bi, i, 0)),
                  pl.BlockSpec((1, rt, hh * dv), lambda bi, i: (bi, i, 0)),
                  pl.BlockSpec((1, rt, LANES), lambda bi, i: (bi, i, 0)),
                  _const_spec((LANES, hh * dk)), _const_spec((1, hh * dk)), _const_spec((1, dv)),
                  pl.BlockSpec((1, rt, d), lambda bi, i: (bi, i, 0)), _const_spec(w_o.shape)],
        out_specs=pl.BlockSpec((1, rt, d), lambda bi, i: (bi, i, 0)),
        scratch_shapes=[pltpu.VMEM((hh, dv, dk), F32), pltpu.VMEM((rt, hh * dv), BF16)],
        compiler_params=_params(("parallel", "arbitrary")),
        name="gla_core",
    )(qk, qk, v, og, gd, w_up, b_gate.reshape(1, hh * dk), norm_g.reshape(1, dv), x, w_o)


def _gla_mixer(x, g, w_in, w_gate_up, b_gate, norm_g, w_o):
    b, s, d = x.shape
    qw, vw = GLA_HEADS * GLA_DK, GLA_HEADS * GLA_DV
    x2 = x.reshape(b * s, d)
    w_gd = jnp.zeros((d, LANES), BF16).at[:, :GLA_GATE_RANK].set(w_in[:, 2 * qw + 2 * vw:].astype(BF16))
    w_up = jnp.zeros((LANES, qw), BF16).at[:GLA_GATE_RANK].set(w_gate_up.astype(BF16))
    qk, v, og, gd = _gla_in(x2, g, w_in[:, :2 * qw].astype(BF16), w_in[:, 2 * qw:2 * qw + vw].astype(BF16),
                            w_in[:, 2 * qw + vw:2 * qw + 2 * vw].astype(BF16), w_gd)
    r3 = lambda a: a.reshape(b, s, a.shape[-1])
    return _gla_core(r3(qk), r3(v), r3(og), r3(gd), w_up, b_gate, norm_g, x, w_o.astype(BF16))


def _nsa_in_body(x_ref, g_ref, wqt_ref, wk_ref, wvt_ref, wc_ref, wgt_ref,
                 qt_ref, kas_ref, kaw_ref, vast_ref, vawt_ref, c_ref, gtt_ref, *, ts):
    i = pl.program_id(1)
    qt, groups = NSA_QT, NSA_KV_HEADS
    h = _rms(x_ref[0], g_ref[...]).astype(BF16)
    qt_ref[0] = (_dot_nt(wqt_ref[...], h) * NSA_HEAD_DIM ** -0.5).astype(BF16)
    k = _dot(h, wk_ref[...])
    gw = groups * LANES
    lane = lax.broadcasted_iota(jnp.int32, (ts, LANES), 1)
    t = i * ts + lax.broadcasted_iota(jnp.int32, (ts, LANES), 0)
    onehot = jnp.where(lane - NSA_HEAD_DIM == t // NSA_SEL_BLOCK, 1.0, 0.0)
    for gi in range(groups):
        ls = slice(gi * LANES, (gi + 1) * LANES)
        kas_ref[0, :, ls] = (k[:, ls] + onehot).astype(BF16)
        kaw_ref[0, :, ls] = k[:, gw + gi * LANES:gw + (gi + 1) * LANES].astype(BF16)
    vt = _dot_nt(wvt_ref[...], h)
    ones = jnp.where(lax.broadcasted_iota(jnp.int32, (LANES, qt), 0) >= NSA_HEAD_DIM, 1.0, 0.0)
    for gi in range(groups):
        for tt in range(ts // qt):
            cs = slice(tt * qt, (tt + 1) * qt)
            vast_ref[0, gi, tt] = (vt[gi * LANES:(gi + 1) * LANES, cs] + ones).astype(BF16)
            vawt_ref[0, gi, tt] = (vt[gw + gi * LANES:gw + (gi + 1) * LANES, cs] + ones).astype(BF16)
    c_ref[0] = _dot(h, wc_ref[...])
    gtt_ref[0] = _dot_nt(wgt_ref[...], h)


def _nsa_in(x, g, w_qt, w_k, w_vt, w_c, w_gt, ts=ROW_TILE):
    b, s, d = x.shape
    groups, qt = NSA_KV_HEADS, NSA_QT
    gw = groups * LANES
    nkt = s // qt
    tpb = ts // qt
    seq_spec = lambda n: pl.BlockSpec((1, ts, n), lambda bi, i: (bi, i, 0))
    rows_spec = lambda n: pl.BlockSpec((1, n, ts), lambda bi, i: (bi, 0, i))
    vt_spec = pl.BlockSpec((1, groups, tpb, LANES, qt), lambda bi, i: (bi, 0, i, 0, 0))
    vt_shape = jax.ShapeDtypeStruct((b, groups, nkt, LANES, qt), BF16)
    return pl.pallas_call(
        functools.partial(_nsa_in_body, ts=ts),
        out_shape=[jax.ShapeDtypeStruct((b, w_qt.shape[0], s), BF16),
                   jax.ShapeDtypeStruct((b, s, gw), BF16), jax.ShapeDtypeStruct((b, s, gw), BF16),
                   vt_shape, vt_shape,
                   jax.ShapeDtypeStruct((b, s, w_c.shape[1]), F32),
                   jax.ShapeDtypeStruct((b, w_gt.shape[0], s), F32)],
        grid=(b, s // ts),
        in_specs=[pl.BlockSpec((1, ts, d), lambda bi, i: (bi, i, 0)), _const_spec((1, d)),
                  _const_spec(w_qt.shape), _const_spec(w_k.shape), _const_spec(w_vt.shape),
                  _const_spec(w_c.shape), _const_spec(w_gt.shape)],
        out_specs=[rows_spec(w_qt.shape[0]), seq_spec(gw), seq_spec(gw), vt_spec, vt_spec,
                   seq_spec(w_c.shape[1]), rows_spec(w_gt.shape[0])],
        compiler_params=_params(("parallel", "parallel")),
        name="nsa_in",
    )(x, g.reshape(1, d), w_qt, w_k, w_vt, w_c, w_gt)


def _nsa_cmp_body(a_ref, pt_ref, pb_ref, w1_ref, w2_ref, w2t_ref, o_ref, ot_ref):
    a = a_ref[0, 0, 0]
    half = a.shape[-1]
    u = _dot((a + pt_ref[0]).astype(BF16), w1_ref[0, :half, :])
    v = _dot((a + pb_ref[0]).astype(BF16), w1_ref[0, half:, :])
    nrow = a.shape[0]
    hid = jax.nn.gelu(u + pltpu.roll(v, nrow - 1, 0), approximate=True).astype(BF16)
    o_ref[0, 0, 0] = _dot(hid, w2_ref[0])
    ot_ref[0, 0, 0] = _dot_nt(w2t_ref[0], hid)


def _nsa_compress(a, pos_top, pos_bot, w1, w2):
    _, b, g, nch, feat = a.shape
    dh = w2.shape[-1]
    w2t = w2.transpose(0, 2, 1)
    per_kv = lambda shape: pl.BlockSpec((1,) + shape, lambda kv, bi, gi: (kv,) + (0,) * len(shape))
    return pl.pallas_call(
        _nsa_cmp_body,
        out_shape=[jax.ShapeDtypeStruct((2, b, g, nch, dh), F32), jax.ShapeDtypeStruct((2, b, g, dh, nch), F32)],
        grid=(2, b, g),
        in_specs=[pl.BlockSpec((1, 1, 1, nch, feat), lambda kv, bi, gi: (kv, bi, gi, 0, 0)),
                  per_kv((1, feat)), per_kv((1, feat)), per_kv(w1.shape[1:]), per_kv(w2.shape[1:]),
                  per_kv(w2t.shape[1:])],
        out_specs=[pl.BlockSpec((1, 1, 1, nch, dh), lambda kv, bi, gi: (kv, bi, gi, 0, 0)),
                   pl.BlockSpec((1, 1, 1, dh, nch), lambda kv, bi, gi: (kv, bi, gi, 0, 0))],
        compiler_params=_params(("parallel", "parallel", "parallel")),
        name="nsa_compress",
    )(a, pos_top, pos_bot, w1, w2, w2t)


def _nsa_attn_body(qt_ref, kc_ref, vct_ref, kas_ref, vast_ref, kaw_ref, vawt_ref, gtt_ref, ovt_ref, o_ref,
                   qaug_ref, m_ref, acc_ref, p_ref, alpha_ref, s_ref, *, n_slc):
    qt, rep, dh = NSA_QT, NSA_REP, NSA_HEAD_DIM
    cols = rep * qt
    qi = pl.program_id(2)
    s0 = qi * qt
    q = jnp.concatenate([qt_ref[0, r * dh:(r + 1) * dh, :] for r in range(rep)], axis=1)
    t_col = s0 + (lax.broadcasted_iota(jnp.int32, (1, cols), 1) & (qt - 1))

    n_pad = kc_ref.shape[2]
    sc = _dot(kc_ref[0, 0].astype(BF16), q)
    cmp_end = lax.broadcasted_iota(jnp.int32, (n_pad, 1), 0) * NSA_CMP_STRIDE + (NSA_CMP_BLOCK - 1)
    sc = jnp.where(cmp_end <= t_col, sc, NEG)
    e = jnp.exp(sc - jnp.max(sc, axis=0, keepdims=True))
    p_cmp = e * (1.0 / jnp.sum(e, axis=0, keepdims=True))
    p_cmp = jnp.where(t_col >= NSA_CMP_BLOCK - 1, p_cmp, 0.0).astype(BF16)
    o_cmp = _dot(vct_ref[0, 0].astype(BF16), p_cmp)

    imp = _dot(ovt_ref[...], p_cmp[:, 0:qt])
    for r in range(1, rep):
        imp = imp + _dot(ovt_ref[...], p_cmp[:, r * qt:(r + 1) * qt])
    imp_t = imp[:n_slc]
    blk = lax.broadcasted_iota(jnp.int32, (n_slc, 1), 0)
    t_q = s0 + lax.broadcasted_iota(jnp.int32, (1, qt), 1)
    cur = t_q // NSA_SEL_BLOCK
    forced = (blk == 0) | (blk == cur) | (blk == cur - 1)
    visible = blk * NSA_SEL_BLOCK <= t_q
    imp_t = jnp.where(forced, NSA_FORCE, imp_t)
    imp_t = jnp.where(visible, imp_t, -1.0)
    rank = jnp.zeros((n_slc, qt), F32)
    for i in range(n_slc):
        row = imp_t[i:i + 1, :]
        beats = (row > imp_t) | ((row == imp_t) & (blk > i))
        rank = rank + jnp.where(beats, 1.0, 0.0)
    sel = (rank < float(min(NSA_N_SELECT, n_slc))) & visible
    bias_parts = [jnp.where(sel, 0.0, NEG)]
    if LANES - dh - n_slc:
        bias_parts.append(jnp.zeros((LANES - dh - n_slc, qt), F32))
    bias = jnp.concatenate(bias_parts, axis=0).astype(BF16)
    qaug_ref[0:dh, :] = q
    for r in range(rep):
        qaug_ref[dh:, r * qt:(r + 1) * qt] = bias

    key_r = lax.broadcasted_iota(jnp.int32, (qt, 1), 0)

    heads = [slice(r * qt, (r + 1) * qt) for r in range(rep)]

    def score(ka_ref, j):
        start = j * qt
        if not isinstance(start, int):
            start = pl.multiple_of(start, qt)
        ka = ka_ref[0, pl.ds(start, qt), :]
        return [_dot(ka, qaug_ref[:, cs]) for cs in heads]

    def flush(vat_ref, jp):
        vat = vat_ref[0, 0, jp]
        return [_dot(vat, p_ref[:, cs]) for cs in heads]

    def accumulate(pvs):
        for cs, pv in zip(heads, pvs):
            acc_ref[:, cs] = acc_ref[:, cs] * alpha_ref[:, cs] + pv

    def step(ka_ref, vat_ref, j, mask_fn, has_next):
        nxt = score(ka_ref, j + 1) if has_next else None
        pvs = flush(vat_ref, jnp.maximum(j - 1, 0))
        probs, alphas = [], []
        for cs in heads:
            s = s_ref[:, cs]
            if mask_fn is not None:
                s = jnp.where(mask_fn(j * qt + key_r, t_col[:, cs]), s, NEG)
            m_old = m_ref[:, cs]
            m_new = jnp.maximum(m_old, jnp.max(s, axis=0, keepdims=True))
            probs.append(jnp.exp(s - m_new).astype(BF16))
            alphas.append(jnp.exp(m_old - m_new))
            m_ref[:, cs] = m_new
        accumulate(pvs)
        for cs, p, alpha in zip(heads, probs, alphas):
            p_ref[:, cs] = p
            alpha_ref[:, cs] = alpha
        if has_next:
            for cs, s in zip(heads, nxt):
                s_ref[:, cs] = s

    def attend(ka_ref, vat_ref, lo, mask_fn, last_mask_fn):
        m_ref[...] = jnp.full_like(m_ref, NEG)
        acc_ref[...] = jnp.zeros_like(acc_ref)
        p_ref[...] = jnp.zeros_like(p_ref)
        alpha_ref[...] = jnp.ones_like(alpha_ref)
        for cs, s in zip(heads, score(ka_ref, lo)):
            s_ref[:, cs] = s

        def body(j, carry):
            step(ka_ref, vat_ref, j, mask_fn, True)
            return carry

        lax.fori_loop(lo, qi, body, 0)
        step(ka_ref, vat_ref, qi, last_mask_fn, False)
        accumulate(flush(vat_ref, qi))
        acc = acc_ref[...]
        return acc[0:dh] / acc[dh:dh + 1]

    o_slc = attend(kas_ref, vast_ref, 0, None, lambda kpos, t: kpos <= t)

    o_win = attend(kaw_ref, vawt_ref, jnp.maximum(qi - NSA_WINDOW // qt, 0),
                   lambda kpos, t: kpos > t - NSA_WINDOW, lambda kpos, t: kpos <= t)

    gates = jax.nn.sigmoid(gtt_ref[0])
    for r in range(rep):
        cs = slice(r * qt, (r + 1) * qt)
        o = (gates[3 * r:3 * r + 1] * o_cmp[:, cs] + gates[3 * r + 1:3 * r + 2] * o_slc[:, cs]
             + gates[3 * r + 2:3 * r + 3] * o_win[:, cs])
        o_ref[0, r * dh:(r + 1) * dh, :] = o.astype(BF16)


def _nsa_attn(q_t, kcmp, vcmp_t, kas, vas_t, kaw, vaw_t, gates_t, ov_t):
    b, hd, s = q_t.shape
    g, rep, dh, qt = NSA_KV_HEADS, NSA_REP, NSA_HEAD_DIM, NSA_QT
    n_slc = s // NSA_SEL_BLOCK
    nch = kcmp.shape[2]
    nkt = s // qt
    k_spec = pl.BlockSpec((1, s, LANES), lambda bi, gi, i: (bi, 0, gi))
    vt_spec = pl.BlockSpec((1, 1, nkt, LANES, qt), lambda bi, gi, i: (bi, gi, 0, 0, 0))
    q_spec = pl.BlockSpec((1, rep * dh, qt), lambda bi, gi, i: (bi, gi, i))
    return pl.pallas_call(
        functools.partial(_nsa_attn_body, n_slc=n_slc),
        out_shape=jax.ShapeDtypeStruct((b, hd, s), BF16),
        grid=(b, g, s // qt),
        in_specs=[q_spec,
                  pl.BlockSpec((1, 1, nch, dh), lambda bi, gi, i: (bi, gi, 0, 0)),
                  pl.BlockSpec((1, 1, dh, nch), lambda bi, gi, i: (bi, gi, 0, 0)),
                  k_spec, vt_spec, k_spec, vt_spec,
                  pl.BlockSpec((1, NSA_GATE_ROWS, qt), lambda bi, gi, i: (bi, gi, i)),
                  _const_spec(ov_t.shape)],
        out_specs=q_spec,
        scratch_shapes=[pltpu.VMEM((LANES, rep * qt), BF16), pltpu.VMEM((1, rep * qt), F32),
                        pltpu.VMEM((LANES, rep * qt), F32), pltpu.VMEM((qt, rep * qt), BF16),
                        pltpu.VMEM((1, rep * qt), F32), pltpu.VMEM((qt, rep * qt), F32)],
        compiler_params=_params(("parallel", "parallel", "parallel")),
        name="nsa_attn",
    )(q_t, kcmp, vcmp_t, kas, vas_t, kaw, vaw_t, gates_t, ov_t)


def _nsa_overlap_t(s):
    n_cmp = (s - NSA_CMP_BLOCK) // NSA_CMP_STRIDE + 1
    n_slc = s // NSA_SEL_BLOCK
    cs = np.arange(n_cmp) * NSA_CMP_STRIDE
    ss = np.arange(n_slc) * NSA_SEL_BLOCK
    ov = np.clip(np.minimum(cs[:, None] + NSA_CMP_BLOCK, ss[None, :] + NSA_SEL_BLOCK)
                 - np.maximum(cs[:, None], ss[None, :]), 0, None) / NSA_CMP_BLOCK
    out = np.zeros((LANES, s // NSA_CMP_STRIDE), np.float32)
    out[:n_slc, :n_cmp] = ov.T
    return jnp.asarray(out, BF16)


def _mm_res_t_body(at_ref, w_ref, x_ref, o_ref):
    o_ref[0] = x_ref[0] + _dot_tn(at_ref[0], w_ref[...])


def _mm_res_t(a_t, w, x, ts=ROW_TILE):
    b, k, s = a_t.shape
    n = w.shape[1]
    return pl.pallas_call(
        _mm_res_t_body,
        out_shape=jax.ShapeDtypeStruct((b, s, n), F32),
        grid=(b, s // ts),
        in_specs=[pl.BlockSpec((1, k, ts), lambda bi, i: (bi, 0, i)), _const_spec(w.shape),
                  pl.BlockSpec((1, ts, n), lambda bi, i: (bi, i, 0))],
        out_specs=pl.BlockSpec((1, ts, n), lambda bi, i: (bi, i, 0)),
        compiler_params=_params(("parallel", "parallel")),
        name="mm_res_t",
    )(a_t, w, x)


def _nsa_mixer(x, g, w_in, cmp_pos, cmp_w1, cmp_w2, w_o):
    b, s, d = x.shape
    hh, gg, rep, dh = NSA_HEADS, NSA_KV_HEADS, NSA_REP, NSA_HEAD_DIM
    kvw = gg * dh
    assert s % ROW_TILE == 0 and s // NSA_SEL_BLOCK <= LANES - dh
    wb = w_in.astype(BF16)
    parts = [wb[:, d + i * kvw:d + (i + 1) * kvw].reshape(d, gg, dh) for i in range(6)]
    zeros = jnp.zeros((d, gg, LANES - dh), BF16)
    padded = lambda p: jnp.concatenate([p, zeros], axis=-1).reshape(d, gg * LANES)
    w_k = jnp.concatenate([padded(parts[2]), padded(parts[4])], axis=-1)
    w_vt = jnp.concatenate([padded(parts[3]), padded(parts[5])], axis=-1).T
    w_c = wb[:, d:d + 2 * kvw]
    w_g = wb[:, d + 6 * kvw:].reshape(d, gg, 3 * rep)
    w_gt = jnp.concatenate([w_g, jnp.zeros((d, gg, NSA_GATE_ROWS - 3 * rep), BF16)], axis=-1)
    w_gt = w_gt.reshape(d, gg * NSA_GATE_ROWS).T
    q_t, kas, kaw, vas_t, vaw_t, c, gates_t = _nsa_in(x, g, wb[:, :d].T, w_k, w_vt, w_c, w_gt)

    nch = s // NSA_CMP_STRIDE
    a = c.reshape(b, nch, NSA_CMP_STRIDE, 2, gg, dh).transpose(3, 0, 4, 1, 2, 5)
    a = a.reshape(2, b, gg, nch, NSA_CMP_STRIDE * dh)
    pos = cmp_pos.reshape(2, 2, 1, NSA_CMP_STRIDE * dh)
    kvc, kvc_t = _nsa_compress(a, pos[:, 0], pos[:, 1], cmp_w1.astype(BF16), cmp_w2.astype(BF16))

    o_t = _nsa_attn(q_t, kvc[0], kvc_t[1], kas, vas_t, kaw, vaw_t, gates_t, _nsa_overlap_t(s))
    return _mm_res_t(o_t, w_o.astype(BF16), x)


def kernel(x, mem, ffn1_norm, ffn1_w_in, ffn1_w_out, mix_norm, xattn_norm, mem_norm, xattn_w_q, xattn_w_kv, xattn_w_o, ffn2_norm, ffn2_w_in, ffn2_w_out, pool_w, pool_b, pool_scale, nsa_w_in, nsa_cmp_pos, nsa_cmp_w1, nsa_cmp_w2, nsa_w_o, gla_w_in, gla_w_gate_up, gla_b_gate, gla_norm, gla_w_o, conv_w_in, conv_b_in, conv_dw, conv_b_dw, conv_ln_g, conv_ln_b, conv_w_out, conv_b_out, final_norm):
    b, s, d = x.shape
    n_mem = mem.shape[1]
    depth = ffn1_norm.shape[0]
    n_mixers = 4
    flat = lambda a: a.reshape(b * s, d)
    cube = lambda a: a.reshape(b, s, d)
    mem2 = mem.reshape(b * n_mem, d)
    for i in range(depth):
        m, j = i % n_mixers, i // n_mixers
        x = cube(_ffn(flat(x), ffn1_norm[i], ffn1_w_in[i].astype(BF16), ffn1_w_out[i].astype(BF16)))
        if m == 0:
            x = _pool_mixer(x, mix_norm[i], pool_w[j].astype(BF16), pool_b[j], pool_scale[j])
        elif m == 1:
            x = _nsa_mixer(x, mix_norm[i], nsa_w_in[j], nsa_cmp_pos[j], nsa_cmp_w1[j], nsa_cmp_w2[j], nsa_w_o[j])
        elif m == 2:
            x = _gla_mixer(x, mix_norm[i], gla_w_in[j], gla_w_gate_up[j], gla_b_gate[j], gla_norm[j], gla_w_o[j])
        else:
            u = _conv_in(flat(x), mix_norm[i], conv_w_in[j].astype(BF16), conv_b_in[j])
            x = _conv_out(cube(u), x, conv_dw[j], conv_b_dw[j], conv_ln_g[j], conv_ln_b[j],
                          conv_w_out[j].astype(BF16), conv_b_out[j])
        kv = _rms_mm(mem2, mem_norm[i], xattn_w_kv[i].astype(BF16), BF16, tm=n_mem)
        x = _xattn(x, xattn_norm[i], xattn_w_q[i].astype(BF16), kv.reshape(b, n_mem, 2 * d),
                   xattn_w_o[i].astype(BF16))
        final_g = final_norm if i == depth - 1 else None
        x = cube(_ffn(flat(x), ffn2_norm[i], ffn2_w_in[i].astype(BF16), ffn2_w_out[i].astype(BF16),
                      final_g=final_g))
    return x
```

```python
import functools

import numpy as np
import jax
import jax.numpy as jnp
from jax import lax
from jax.experimental import pallas as pl
from jax.experimental.pallas import tpu as pltpu

F32 = jnp.float32
BF16 = jnp.bfloat16

EPS = 1e-6
NEG = -1e30

V7X_VMEM_BYTES = 64 * 1024 * 1024
VMEM_LIMIT = V7X_VMEM_BYTES - 8 * 1024 * 1024
LANES = 128
SUBLANES = 8

POOL_WINDOWS = (2, 4, 8, 16)
POOL_HALO = 16

NSA_HEADS = 16
NSA_KV_HEADS = 4
NSA_REP = NSA_HEADS // NSA_KV_HEADS
NSA_HEAD_DIM = 64
NSA_CMP_BLOCK = 32
NSA_CMP_STRIDE = 16
NSA_SEL_BLOCK = 64
NSA_N_SELECT = 16
NSA_WINDOW = 512
NSA_FORCE = 1e4
NSA_QT = 256
NSA_GATE_ROWS = 16

GLA_HEADS = 4
GLA_DK = 128
GLA_DV = 256
GLA_GATE_RANK = 16
GLA_TAU = 16.0
GLA_CHUNK = 64
GLA_ROWS = 256

CONV_WIDTH = 31
CONV_HALO = 32
CONV_STRIP = 128

XATTN_HEADS = 4

ROW_TILE = 512


def _params(sem):
    return pltpu.CompilerParams(dimension_semantics=sem, vmem_limit_bytes=VMEM_LIMIT)


def _rms(x, g):
    return x * lax.rsqrt(jnp.mean(x * x, axis=-1, keepdims=True) + EPS) * g


def _dot(a, b):
    return jnp.dot(a, b, preferred_element_type=F32)


def _dot_nt(a, b):
    return lax.dot_general(a, b, (((1,), (1,)), ((), ())), preferred_element_type=F32)


def _dot_tn(a, b):
    return lax.dot_general(a, b, (((0,), (0,)), ((), ())), preferred_element_type=F32)


def _const_spec(shape):
    nd = len(shape)
    return pl.BlockSpec(shape, lambda *_: (0,) * nd, pipeline_mode=pl.Buffered(1))


def _ffn_body(x_ref, g_ref, win_ref, wout_ref, *rest, d_ff, final):
    if final:
        fg_ref, o_ref = rest
    else:
        (o_ref,) = rest
    x = x_ref[...]
    h = _rms(x, g_ref[...]).astype(BF16)
    gu = _dot(h, win_ref[...])
    g = gu[:, :d_ff]
    u = gu[:, d_ff:]
    a = (g * jax.nn.sigmoid(g) * u).astype(BF16)
    out = x + 0.5 * _dot(a, wout_ref[...])
    if final:
        out = _rms(out, fg_ref[...])
    o_ref[...] = out


def _ffn(x2, g, w_in, w_out, final_g=None, tm=ROW_TILE):
    t, d = x2.shape
    d_ff = w_out.shape[0]
    final = final_g is not None
    in_specs = [pl.BlockSpec((tm, d), lambda i: (i, 0)), _const_spec((1, d)),
                _const_spec(w_in.shape), _const_spec(w_out.shape)]
    args = [x2, g.reshape(1, d), w_in, w_out]
    if final:
        in_specs.append(_const_spec((1, d)))
        args.append(final_g.reshape(1, d))
    return pl.pallas_call(
        functools.partial(_ffn_body, d_ff=d_ff, final=final),
        out_shape=jax.ShapeDtypeStruct((t, d), F32),
        grid=(t // tm,),
        in_specs=in_specs,
        out_specs=pl.BlockSpec((tm, d), lambda i: (i, 0)),
        compiler_params=_params(("parallel",)),
        name="ffn",
    )(*args)


def _rms_mm_body(x_ref, g_ref, w_ref, o_ref):
    h = _rms(x_ref[...], g_ref[...]).astype(BF16)
    o_ref[...] = _dot(h, w_ref[...]).astype(o_ref.dtype)


def _rms_mm(x2, g, w, out_dtype, tm):
    t, d = x2.shape
    n = w.shape[1]
    return pl.pallas_call(
        _rms_mm_body,
        out_shape=jax.ShapeDtypeStruct((t, n), out_dtype),
        grid=(t // tm,),
        in_specs=[pl.BlockSpec((tm, d), lambda i: (i, 0)), _const_spec((1, d)), _const_spec(w.shape)],
        out_specs=pl.BlockSpec((tm, n), lambda i: (i, 0)),
        compiler_params=_params(("parallel",)),
        name="rms_mm",
    )(x2, g.reshape(1, d), w)


def _xattn_body(x_ref, g_ref, wq_ref, k_ref, v_ref, wo_ref, o_ref, *, heads):
    x = x_ref[0]
    d = x.shape[-1]
    dh = d // heads
    h = _rms(x, g_ref[...]).astype(BF16)
    q = (_dot(h, wq_ref[...]) * dh ** -0.5).astype(BF16)
    cols = [slice(hd * dh, (hd + 1) * dh) for hd in range(heads)]
    scores = [_dot_nt(q[:, sl], k_ref[0][:, sl]) for sl in cols]
    probs = []
    for s in scores:
        e = jnp.exp(s - jnp.max(s, axis=-1, keepdims=True))
        probs.append((e * (1.0 / jnp.sum(e, axis=-1, keepdims=True))).astype(BF16))
    o = jnp.concatenate([_dot(p, v_ref[0][:, sl]).astype(BF16) for p, sl in zip(probs, cols)], axis=-1)
    o_ref[0] = x + _dot(o, wo_ref[...])


def _xattn(x, g, w_q, kv, w_o, ts=ROW_TILE):
    b, s, d = x.shape
    n = kv.shape[1]
    return pl.pallas_call(
        functools.partial(_xattn_body, heads=XATTN_HEADS),
        out_shape=jax.ShapeDtypeStruct((b, s, d), F32),
        grid=(b, s // ts),
        in_specs=[pl.BlockSpec((1, ts, d), lambda bi, i: (bi, i, 0)), _const_spec((1, d)),
                  _const_spec(w_q.shape),
                  pl.BlockSpec((1, n, d), lambda bi, i: (bi, 0, 0)),
                  pl.BlockSpec((1, n, d), lambda bi, i: (bi, 0, 1)),
                  _const_spec(w_o.shape)],
        out_specs=pl.BlockSpec((1, ts, d), lambda bi, i: (bi, i, 0)),
        compiler_params=_params(("parallel", "parallel")),
        name="xattn",
    )(x, g.reshape(1, d), w_q, kv, kv, w_o)


def _pool_body(x_ref, halo_ref, g_ref, w_ref, b_ref, sc_ref, o_ref, hbuf_ref, *, ts):
    i = pl.program_id(1)
    x = x_ref[0]
    d = x.shape[-1]
    gw = d // len(POOL_WINDOWS)
    g = g_ref[...]
    h = _rms(x, g)
    hh = _rms(halo_ref[0], g)
    hbuf_ref[0:POOL_HALO, :] = jnp.where(i > 0, hh, 0.0)
    hbuf_ref[POOL_HALO:, :] = h
    t = i * ts + lax.broadcasted_iota(jnp.int32, (ts, 1), 0)
    ys = []
    for gi, win in enumerate(POOL_WINDOWS):
        cs = slice(gi * gw, (gi + 1) * gw)
        acc = h[:, cs]
        for k in range(1, win):
            acc = acc + hbuf_ref[POOL_HALO - k:POOL_HALO - k + ts, cs]
        cnt = jnp.minimum(t + 1, win).astype(F32)
        p = acc / cnt - h[:, cs]
        ys.append(_dot(p.astype(BF16), w_ref[gi]))
    y = (jnp.concatenate(ys, axis=-1) + b_ref[...]) * sc_ref[...]
    o_ref[0] = x + y


def _pool_mixer(x, g, w, bias, scale, ts=ROW_TILE):
    b, s, d = x.shape
    hb = ts // POOL_HALO
    return pl.pallas_call(
        functools.partial(_pool_body, ts=ts),
        out_shape=jax.ShapeDtypeStruct((b, s, d), F32),
        grid=(b, s // ts),
        in_specs=[pl.BlockSpec((1, ts, d), lambda bi, i: (bi, i, 0)),
                  pl.BlockSpec((1, POOL_HALO, d), lambda bi, i: (bi, jnp.maximum(i * hb - 1, 0), 0)),
                  _const_spec((1, d)), _const_spec(w.shape), _const_spec((1, d)), _const_spec((1, d))],
        out_specs=pl.BlockSpec((1, ts, d), lambda bi, i: (bi, i, 0)),
        scratch_shapes=[pltpu.VMEM((ts + POOL_HALO, d), F32)],
        compiler_params=_params(("parallel", "parallel")),
        name="pool",
    )(x, x, g.reshape(1, d), w, bias.reshape(1, d), scale.reshape(1, d))


def _conv_in_body(x_ref, g_ref, w_ref, b_ref, o_ref):
    d = x_ref.shape[-1]
    h = _rms(x_ref[...], g_ref[...]).astype(BF16)
    ag = _dot(h, w_ref[...]) + b_ref[...]
    o_ref[...] = ag[:, :d] * jax.nn.sigmoid(ag[:, d:])


def _conv_in(x2, g, w, bias, tm=ROW_TILE):
    t, d = x2.shape
    return pl.pallas_call(
        _conv_in_body,
        out_shape=jax.ShapeDtypeStruct((t, d), F32),
        grid=(t // tm,),
        in_specs=[pl.BlockSpec((tm, d), lambda i: (i, 0)), _const_spec((1, d)),
                  _const_spec(w.shape), _const_spec((1, 2 * d))],
        out_specs=pl.BlockSpec((tm, d), lambda i: (i, 0)),
        compiler_params=_params(("parallel",)),
        name="conv_in",
    )(x2, g.reshape(1, d), w, bias.reshape(1, 2 * d))


def _conv_out_body(u_ref, halo_ref, x_ref, dw_ref, bdw_ref, lng_ref, lnb_ref, w_ref, bo_ref, o_ref,
                   ubuf_ref, cbuf_ref, *, ts):
    i = pl.program_id(1)
    n_slab = ubuf_ref.shape[0]
    halo = jnp.where(i > 0, halo_ref[0], 0.0)
    for lb in range(n_slab):
        ls = slice(lb * LANES, (lb + 1) * LANES)
        ubuf_ref[lb, 0:CONV_HALO, :] = halo[:, ls]
        ubuf_ref[lb, CONV_HALO:CONV_HALO + ts, :] = u_ref[0, :, ls]
        ubuf_ref[lb, CONV_HALO + ts:, :] = jnp.zeros((SUBLANES, LANES), F32)
    lead = CONV_HALO - (CONV_WIDTH - 1)

    def slab(lb, carry):
        for r0 in range(0, ts, CONV_STRIP):
            acc = None
            for sh in range(SUBLANES):
                taps = [k for k in range(CONV_WIDTH) if (lead + k) % SUBLANES == sh]
                if not taps:
                    continue
                win = ubuf_ref[lb, pl.ds(r0 + sh, CONV_STRIP + CONV_HALO), :]
                for k in taps:
                    a8 = (lead + k) // SUBLANES * SUBLANES
                    term = win[a8:a8 + CONV_STRIP] * dw_ref[lb, k:k + 1, :]
                    acc = term if acc is None else acc + term
            cbuf_ref[lb, r0:r0 + CONV_STRIP, :] = acc + bdw_ref[lb]
        return carry

    lax.fori_loop(0, n_slab, slab, 0)
    c = jnp.concatenate([cbuf_ref[lb] for lb in range(n_slab)], axis=-1)
    mu = jnp.mean(c, axis=-1, keepdims=True)
    var = jnp.mean(jnp.square(c - mu), axis=-1, keepdims=True)
    n = (c - mu) * lax.rsqrt(var + EPS) * lng_ref[...] + lnb_ref[...]
    a = (n * jax.nn.sigmoid(n)).astype(BF16)
    o_ref[0] = x_ref[0] + _dot(a, w_ref[...]) + bo_ref[...]


def _conv_out(u, x, dw, b_dw, ln_g, ln_b, w_out, b_out, ts=ROW_TILE):
    b, s, d = x.shape
    hb = ts // CONV_HALO
    n_slab = d // LANES
    dw_pad = jnp.zeros((CONV_HALO, d), F32).at[:CONV_WIDTH].set(dw)
    dw_slab = dw_pad.reshape(CONV_HALO, n_slab, LANES).transpose(1, 0, 2)
    bdw_slab = b_dw.reshape(n_slab, 1, LANES)
    row = lambda v: v.reshape(1, d)
    return pl.pallas_call(
        functools.partial(_conv_out_body, ts=ts),
        out_shape=jax.ShapeDtypeStruct((b, s, d), F32),
        grid=(b, s // ts),
        in_specs=[pl.BlockSpec((1, ts, d), lambda bi, i: (bi, i, 0)),
                  pl.BlockSpec((1, CONV_HALO, d), lambda bi, i: (bi, jnp.maximum(i * hb - 1, 0), 0)),
                  pl.BlockSpec((1, ts, d), lambda bi, i: (bi, i, 0)),
                  _const_spec(dw_slab.shape), _const_spec(bdw_slab.shape), _const_spec((1, d)),
                  _const_spec((1, d)), _const_spec(w_out.shape), _const_spec((1, d))],
        out_specs=pl.BlockSpec((1, ts, d), lambda bi, i: (bi, i, 0)),
        scratch_shapes=[pltpu.VMEM((n_slab, ts + CONV_HALO + SUBLANES, LANES), F32),
                        pltpu.VMEM((n_slab, ts, LANES), F32)],
        compiler_params=_params(("parallel", "parallel")),
        name="conv_out",
    )(u, u, x, dw_slab, bdw_slab, row(ln_g), row(ln_b), w_out, row(b_out))


def _gla_in_body(x_ref, g_ref, wqk_ref, wv_ref, wog_ref, wgd_ref, qk_ref, v_ref, og_ref, gd_ref):
    h = _rms(x_ref[...], g_ref[...]).astype(BF16)
    qk_ref[...] = _dot(h, wqk_ref[...])
    v_ref[...] = _dot(h, wv_ref[...]).astype(BF16)
    og_ref[...] = _dot(h, wog_ref[...])
    gd_ref[...] = _dot(h, wgd_ref[...])


def _gla_in(x2, g, w_qk, w_v, w_og, w_gd, tm=ROW_TILE):
    t, d = x2.shape
    widths = (w_qk.shape[1], w_v.shape[1], w_og.shape[1], w_gd.shape[1])
    dtypes = (F32, BF16, F32, F32)
    return pl.pallas_call(
        _gla_in_body,
        out_shape=[jax.ShapeDtypeStruct((t, n), dt) for n, dt in zip(widths, dtypes)],
        grid=(t // tm,),
        in_specs=[pl.BlockSpec((tm, d), lambda i: (i, 0)), _const_spec((1, d)),
                  _const_spec(w_qk.shape), _const_spec(w_v.shape), _const_spec(w_og.shape),
                  _const_spec(w_gd.shape)],
        out_specs=[pl.BlockSpec((tm, n), lambda i: (i, 0)) for n in widths],
        compiler_params=_params(("parallel",)),
        name="gla_in",
    )(x2, g.reshape(1, d), w_qk, w_v, w_og, w_gd)


def _gla_body(q_ref, k_ref, v_ref, og_ref, gd_ref, wup_ref, bg_ref, ng_ref, x_ref, wo_ref, o_ref,
              state_ref, obuf_ref):
    @pl.when(pl.program_id(1) == 0)
    def _():
        state_ref[...] = jnp.zeros_like(state_ref)

    c, dk, dv = GLA_CHUNK, GLA_DK, GLA_DV
    ga = _dot(gd_ref[0].astype(BF16), wup_ref[...]) + bg_ref[...]
    log_a = (jnp.minimum(ga, 0.0) - jnp.log1p(jnp.exp(-jnp.abs(ga)))) / GLA_TAU
    ri = lax.broadcasted_iota(jnp.int32, (c, c), 0)
    ci = lax.broadcasted_iota(jnp.int32, (c, c), 1)
    causal = ci <= ri
    tril = jnp.where(causal, 1.0, 0.0).astype(BF16)
    scale = dk ** -0.5
    n_chunks = GLA_ROWS // c
    chunk_rows = [slice(ch * c, (ch + 1) * c) for ch in range(n_chunks)]
    cases = [(ch, h) for ch in range(n_chunks) for h in range(GLA_HEADS)]
    hi = log_a.astype(BF16)
    r1 = log_a - hi.astype(F32)
    mid = r1.astype(BF16)
    lo = (r1 - mid.astype(F32)).astype(BF16)
    q_t, k_t, k_dec, decay = [], [], [], []
    for rows in chunk_rows:
        bcum = _dot(tril, hi[rows]) + _dot(tril, mid[rows]) + _dot(tril, lo[rows])
        b_last = bcum[c - 1:c, :]
        k = k_ref[0, rows, :]
        q_t.append((q_ref[0, rows, :] * scale * jnp.exp(bcum)).astype(BF16))
        k_t.append((k * jnp.exp(-bcum)).astype(BF16))
        k_dec.append((k * jnp.exp(b_last - bcum)).astype(BF16))
        decay.append(jnp.exp(b_last))
    ks = [slice(h * dk, (h + 1) * dk) for h in range(GLA_HEADS)]
    vs = [slice(h * dv, (h + 1) * dv) for h in range(GLA_HEADS)]
    v = {(ch, h): v_ref[0, chunk_rows[ch], vs[h]] for ch, h in cases}
    a = {(ch, h): jnp.where(causal, _dot_nt(q_t[ch][:, ks[h]], k_t[ch][:, ks[h]]), 0.0).astype(BF16)
         for ch, h in cases}
    kv = {(ch, h): _dot_tn(v[ch, h], k_dec[ch][:, ks[h]]) for ch, h in cases}
    o_intra = {(ch, h): _dot(a[ch, h], v[ch, h]) for ch, h in cases}
    state = [state_ref[h] for h in range(GLA_HEADS)]
    for ch, h in cases:
        o = o_intra[ch, h] + _dot_nt(q_t[ch][:, ks[h]], state[h].astype(BF16))
        state[h] = state[h] * decay[ch][:, ks[h]] + kv[ch, h]
        o = _rms(o, ng_ref[...])
        og = og_ref[0, chunk_rows[ch], vs[h]]
        obuf_ref[chunk_rows[ch], vs[h]] = (o * (og * jax.nn.sigmoid(og))).astype(BF16)
    for h in range(GLA_HEADS):
        state_ref[h] = state[h]
    o_ref[0] = x_ref[0] + _dot(obuf_ref[...], wo_ref[...])


def _gla_core(qk, v, og, gd, w_up, b_gate, norm_g, x, w_o):
    b, s, d = x.shape
    hh, dk, dv, rt = GLA_HEADS, GLA_DK, GLA_DV, GLA_ROWS
    return pl.pallas_call(
        _gla_body,
        out_shape=jax.ShapeDtypeStruct((b, s, d), F32),
        grid=(b, s // rt),
        in_specs=[pl.BlockSpec((1, rt, hh * dk), lambda bi, i: (bi, i, 0)),
                  pl.BlockSpec((1, rt, hh * dk), lambda bi, i: (bi, i, 1)),
                  pl.BlockSpec((1, rt, hh * dv), lambda bi, i: (bi, i, 0)),
                  pl.BlockSpec((1, rt, hh * dv), lambda bi, i: (bi, i, 0)),
                  pl.BlockSpec((1, rt, LANES), lambda bi, i: (bi, i, 0)),
                  _const_spec((LANES, hh * dk)), _const_spec((1, hh * dk)), _const_spec((1, dv)),
                  pl.BlockSpec((1, rt, d), lambda bi, i: (bi, i, 0)), _const_spec(w_o.shape)],
        out_specs=pl.BlockSpec((1, rt, d), lambda bi, i: (bi, i, 0)),
        scratch_shapes=[pltpu.VMEM((hh, dv, dk), F32), pltpu.VMEM((rt, hh * dv), BF16)],
        compiler_params=_params(("parallel", "arbitrary")),
        name="gla_core",
    )(qk, qk, v, og, gd, w_up, b_gate.reshape(1, hh * dk), norm_g.reshape(1, dv), x, w_o)


def _gla_mixer(x, g, w_in, w_gate_up, b_gate, norm_g, w_o):
    b, s, d = x.shape
    qw, vw = GLA_HEADS * GLA_DK, GLA_HEADS * GLA_DV
    x2 = x.reshape(b * s, d)
    w_gd = jnp.zeros((d, LANES), BF16).at[:, :GLA_GATE_RANK].set(w_in[:, 2 * qw + 2 * vw:].astype(BF16))
    w_up = jnp.zeros((LANES, qw), BF16).at[:GLA_GATE_RANK].set(w_gate_up.astype(BF16))
    qk, v, og, gd = _gla_in(x2, g, w_in[:, :2 * qw].astype(BF16), w_in[:, 2 * qw:2 * qw + vw].astype(BF16),
                            w_in[:, 2 * qw + vw:2 * qw + 2 * vw].astype(BF16), w_gd)
    r3 = lambda a: a.reshape(b, s, a.shape[-1])
    return _gla_core(r3(qk), r3(v), r3(og), r3(gd), w_up, b_gate, norm_g, x, w_o.astype(BF16))


def _nsa_in_body(x_ref, g_ref, wqt_ref, wk_ref, wvt_ref, wc_ref, wgt_ref,
                 qt_ref, kas_ref, kaw_ref, vast_ref, vawt_ref, c_ref, gtt_ref, *, ts):
    i = pl.program_id(1)
    qt, groups = NSA_QT, NSA_KV_HEADS
    h = _rms(x_ref[0], g_ref[...]).astype(BF16)
    q_all = (_dot_nt(wqt_ref[...], h) * NSA_HEAD_DIM ** -0.5).astype(BF16)
    hw = NSA_REP * NSA_HEAD_DIM
    for gi in range(groups):
        for tt in range(ts // qt):
            qt_ref[0, gi, tt] = q_all[gi * hw:(gi + 1) * hw, tt * qt:(tt + 1) * qt]
    k = _dot(h, wk_ref[...])
    gw = groups * LANES
    lane = lax.broadcasted_iota(jnp.int32, (ts, LANES), 1)
    t = i * ts + lax.broadcasted_iota(jnp.int32, (ts, LANES), 0)
    onehot = jnp.where(lane - NSA_HEAD_DIM == t // NSA_SEL_BLOCK, 1.0, 0.0)
    for gi in range(groups):
        ls = slice(gi * LANES, (gi + 1) * LANES)
        kas_ref[0, gi] = (k[:, ls] + onehot).astype(BF16)
        kaw_ref[0, gi] = k[:, gw + gi * LANES:gw + (gi + 1) * LANES].astype(BF16)
    vt = _dot_nt(wvt_ref[...], h)
    ones = jnp.where(lax.broadcasted_iota(jnp.int32, (LANES, qt), 0) >= NSA_HEAD_DIM, 1.0, 0.0)
    for gi in range(groups):
        for tt in range(ts // qt):
            cs = slice(tt * qt, (tt + 1) * qt)
            vast_ref[0, gi, tt] = (vt[gi * LANES:(gi + 1) * LANES, cs] + ones).astype(BF16)
            vawt_ref[0, gi, tt] = (vt[gw + gi * LANES:gw + (gi + 1) * LANES, cs] + ones).astype(BF16)
    c_ref[0] = _dot(h, wc_ref[...])
    gtt_ref[0] = _dot_nt(wgt_ref[...], h)


def _nsa_in(x, g, w_qt, w_k, w_vt, w_c, w_gt, ts=ROW_TILE):
    b, s, d = x.shape
    groups, qt = NSA_KV_HEADS, NSA_QT
    gw = groups * LANES
    nkt = s // qt
    tpb = ts // qt
    seq_spec = lambda n: pl.BlockSpec((1, ts, n), lambda bi, i: (bi, i, 0))
    rows_spec = lambda n: pl.BlockSpec((1, n, ts), lambda bi, i: (bi, 0, i))
    vt_spec = pl.BlockSpec((1, groups, tpb, LANES, qt), lambda bi, i: (bi, 0, i, 0, 0))
    vt_shape = jax.ShapeDtypeStruct((b, groups, nkt, LANES, qt), BF16)
    k_spec = pl.BlockSpec((1, groups, ts, LANES), lambda bi, i: (bi, 0, i, 0))
    k_shape = jax.ShapeDtypeStruct((b, groups, s, LANES), BF16)
    return pl.pallas_call(
        functools.partial(_nsa_in_body, ts=ts),
        out_shape=[jax.ShapeDtypeStruct((b, groups, nkt, w_qt.shape[0] // groups, qt), BF16),
                   k_shape, k_shape,
                   vt_shape, vt_shape,
                   jax.ShapeDtypeStruct((b, s, w_c.shape[1]), F32),
                   jax.ShapeDtypeStruct((b, w_gt.shape[0], s), F32)],
        grid=(b, s // ts),
        in_specs=[pl.BlockSpec((1, ts, d), lambda bi, i: (bi, i, 0)), _const_spec((1, d)),
                  _const_spec(w_qt.shape), _const_spec(w_k.shape), _const_spec(w_vt.shape),
                  _const_spec(w_c.shape), _const_spec(w_gt.shape)],
        out_specs=[pl.BlockSpec((1, groups, tpb, w_qt.shape[0] // groups, qt), lambda bi, i: (bi, 0, i, 0, 0)),
                   k_spec, k_spec, vt_spec, vt_spec,
                   seq_spec(w_c.shape[1]), rows_spec(w_gt.shape[0])],
        compiler_params=_params(("parallel", "parallel")),
        name="nsa_in",
    )(x, g.reshape(1, d), w_qt, w_k, w_vt, w_c, w_gt)


def _nsa_cmp_body(a_ref, pt_ref, pb_ref, w1_ref, w2_ref, w2t_ref, o_ref, ot_ref):
    a = a_ref[0, 0, 0]
    half = a.shape[-1]
    u = _dot((a + pt_ref[0]).astype(BF16), w1_ref[0, :half, :])
    v = _dot((a + pb_ref[0]).astype(BF16), w1_ref[0, half:, :])
    nrow = a.shape[0]
    hid = jax.nn.gelu(u + pltpu.roll(v, nrow - 1, 0), approximate=True).astype(BF16)
    o_ref[0, 0, 0] = _dot(hid, w2_ref[0])
    ot_ref[0, 0, 0] = _dot_nt(w2t_ref[0], hid)


def _nsa_compress(a, pos_top, pos_bot, w1, w2):
    _, b, g, nch, feat = a.shape
    dh = w2.shape[-1]
    w2t = w2.transpose(0, 2, 1)
    per_kv = lambda shape: pl.BlockSpec((1,) + shape, lambda kv, bi, gi: (kv,) + (0,) * len(shape))
    return pl.pallas_call(
        _nsa_cmp_body,
        out_shape=[jax.ShapeDtypeStruct((2, b, g, nch, dh), F32), jax.ShapeDtypeStruct((2, b, g, dh, nch), F32)],
        grid=(2, b, g),
        in_specs=[pl.BlockSpec((1, 1, 1, nch, feat), lambda kv, bi, gi: (kv, bi, gi, 0, 0)),
                  per_kv((1, feat)), per_kv((1, feat)), per_kv(w1.shape[1:]), per_kv(w2.shape[1:]),
                  per_kv(w2t.shape[1:])],
        out_specs=[pl.BlockSpec((1, 1, 1, nch, dh), lambda kv, bi, gi: (kv, bi, gi, 0, 0)),
                   pl.BlockSpec((1, 1, 1, dh, nch), lambda kv, bi, gi: (kv, bi, gi, 0, 0))],
        compiler_params=_params(("parallel", "parallel", "parallel")),
        name="nsa_compress",
    )(a, pos_top, pos_bot, w1, w2, w2t)


def _nsa_attn_body(qt_ref, kc_ref, vct_ref, kas_ref, vast_ref, kaw_ref, vawt_ref, gtt_ref, ovt_ref, o_ref,
                   qaug_ref, m_ref, acc_ref, p_ref, alpha_ref, s_ref, *, n_slc):
    qt, rep, dh = NSA_QT, NSA_REP, NSA_HEAD_DIM
    cols = rep * qt
    qi = pl.program_id(2)
    s0 = qi * qt
    q = jnp.concatenate([qt_ref[0, 0, 0, r * dh:(r + 1) * dh, :] for r in range(rep)], axis=1)
    t_col = s0 + (lax.broadcasted_iota(jnp.int32, (1, cols), 1) & (qt - 1))

    n_pad = kc_ref.shape[2]
    sc = _dot(kc_ref[0, 0].astype(BF16), q)
    cmp_end = lax.broadcasted_iota(jnp.int32, (n_pad, 1), 0) * NSA_CMP_STRIDE + (NSA_CMP_BLOCK - 1)
    sc = jnp.where(cmp_end <= t_col, sc, NEG)
    e = jnp.exp(sc - jnp.max(sc, axis=0, keepdims=True))
    p_cmp = e * (1.0 / jnp.sum(e, axis=0, keepdims=True))
    p_cmp = jnp.where(t_col >= NSA_CMP_BLOCK - 1, p_cmp, 0.0).astype(BF16)
    o_cmp = _dot(vct_ref[0, 0].astype(BF16), p_cmp)

    imp = _dot(ovt_ref[...], p_cmp[:, 0:qt])
    for r in range(1, rep):
        imp = imp + _dot(ovt_ref[...], p_cmp[:, r * qt:(r + 1) * qt])
    imp_t = imp[:n_slc]
    blk = lax.broadcasted_iota(jnp.int32, (n_slc, 1), 0)
    t_q = s0 + lax.broadcasted_iota(jnp.int32, (1, qt), 1)
    cur = t_q // NSA_SEL_BLOCK
    forced = (blk == 0) | (blk == cur) | (blk == cur - 1)
    visible = blk * NSA_SEL_BLOCK <= t_q
    imp_t = jnp.where(forced, NSA_FORCE, imp_t)
    imp_t = jnp.where(visible, imp_t, -1.0)
    rank = jnp.zeros((n_slc, qt), F32)
    for i in range(n_slc):
        row = imp_t[i:i + 1, :]
        beats = (row > imp_t) | ((row == imp_t) & (blk > i))
        rank = rank + jnp.where(beats, 1.0, 0.0)
    sel = (rank < float(min(NSA_N_SELECT, n_slc))) & visible
    bias_parts = [jnp.where(sel, 0.0, NEG)]
    if LANES - dh - n_slc:
        bias_parts.append(jnp.zeros((LANES - dh - n_slc, qt), F32))
    bias = jnp.concatenate(bias_parts, axis=0).astype(BF16)
    qaug_ref[0:dh, :] = q
    for r in range(rep):
        qaug_ref[dh:, r * qt:(r + 1) * qt] = bias

    key_r = lax.broadcasted_iota(jnp.int32, (qt, 1), 0)

    heads = [slice(r * qt, (r + 1) * qt) for r in range(rep)]

    def score(ka_ref, j):
        start = j * qt
        if not isinstance(start, int):
            start = pl.multiple_of(start, qt)
        ka = ka_ref[0, 0, pl.ds(start, qt), :]
        return [_dot(ka, qaug_ref[:, cs]) for cs in heads]

    def flush(vat_ref, jp):
        vat = vat_ref[0, 0, jp]
        return [_dot(vat, p_ref[:, cs]) for cs in heads]

    def accumulate(pvs):
        for cs, pv in zip(heads, pvs):
            acc_ref[:, cs] = acc_ref[:, cs] * alpha_ref[:, cs] + pv

    def step(ka_ref, vat_ref, j, mask_fn, has_next):
        nxt = score(ka_ref, j + 1) if has_next else None
        pvs = flush(vat_ref, jnp.maximum(j - 1, 0))
        probs, alphas = [], []
        for cs in heads:
            s = s_ref[:, cs]
            if mask_fn is not None:
                s = jnp.where(mask_fn(j * qt + key_r, t_col[:, cs]), s, NEG)
            m_old = m_ref[:, cs]
            m_new = jnp.maximum(m_old, jnp.max(s, axis=0, keepdims=True))
            probs.append(jnp.exp(s - m_new).astype(BF16))
            alphas.append(jnp.exp(m_old - m_new))
            m_ref[:, cs] = m_new
        accumulate(pvs)
        for cs, p, alpha in zip(heads, probs, alphas):
            p_ref[:, cs] = p
            alpha_ref[:, cs] = alpha
        if has_next:
            for cs, s in zip(heads, nxt):
                s_ref[:, cs] = s

    def attend(ka_ref, vat_ref, lo, mask_fn, last_mask_fn):
        m_ref[...] = jnp.full_like(m_ref, NEG)
        acc_ref[...] = jnp.zeros_like(acc_ref)
        p_ref[...] = jnp.zeros_like(p_ref)
        alpha_ref[...] = jnp.ones_like(alpha_ref)
        for cs, s in zip(heads, score(ka_ref, lo)):
            s_ref[:, cs] = s

        def body(j, carry):
            step(ka_ref, vat_ref, j, mask_fn, True)
            return carry

        lax.fori_loop(lo, qi, body, 0)
        step(ka_ref, vat_ref, qi, last_mask_fn, False)
        accumulate(flush(vat_ref, qi))
        acc = acc_ref[...]
        return acc[0:dh] / acc[dh:dh + 1]

    o_slc = attend(kas_ref, vast_ref, 0, None, lambda kpos, t: kpos <= t)

    o_win = attend(kaw_ref, vawt_ref, jnp.maximum(qi - NSA_WINDOW // qt, 0),
                   lambda kpos, t: kpos > t - NSA_WINDOW, lambda kpos, t: kpos <= t)

    gates = jax.nn.sigmoid(gtt_ref[0])
    for r in range(rep):
        cs = slice(r * qt, (r + 1) * qt)
        o = (gates[3 * r:3 * r + 1] * o_cmp[:, cs] + gates[3 * r + 1:3 * r + 2] * o_slc[:, cs]
             + gates[3 * r + 2:3 * r + 3] * o_win[:, cs])
        o_ref[0, 0, 0, r * dh:(r + 1) * dh, :] = o.astype(BF16)


def _nsa_attn(q_t, kcmp, vcmp_t, kas, vas_t, kaw, vaw_t, gates_t, ov_t):
    b, g, nkt, _, qt = q_t.shape
    rep, dh = NSA_REP, NSA_HEAD_DIM
    s = nkt * qt
    n_slc = s // NSA_SEL_BLOCK
    nch = kcmp.shape[2]
    k_spec = pl.BlockSpec((1, 1, s, LANES), lambda bi, gi, i: (bi, gi, 0, 0))
    vt_spec = pl.BlockSpec((1, 1, nkt, LANES, qt), lambda bi, gi, i: (bi, gi, 0, 0, 0))
    q_spec = pl.BlockSpec((1, 1, 1, rep * dh, qt), lambda bi, gi, i: (bi, gi, i, 0, 0))
    return pl.pallas_call(
        functools.partial(_nsa_attn_body, n_slc=n_slc),
        out_shape=jax.ShapeDtypeStruct(q_t.shape, BF16),
        grid=(b, g, s // qt),
        in_specs=[q_spec,
                  pl.BlockSpec((1, 1, nch, dh), lambda bi, gi, i: (bi, gi, 0, 0)),
                  pl.BlockSpec((1, 1, dh, nch), lambda bi, gi, i: (bi, gi, 0, 0)),
                  k_spec, vt_spec, k_spec, vt_spec,
                  pl.BlockSpec((1, NSA_GATE_ROWS, qt), lambda bi, gi, i: (bi, gi, i)),
                  _const_spec(ov_t.shape)],
        out_specs=q_spec,
        scratch_shapes=[pltpu.VMEM((LANES, rep * qt), BF16), pltpu.VMEM((1, rep * qt), F32),
                        pltpu.VMEM((LANES, rep * qt), F32), pltpu.VMEM((qt, rep * qt), BF16),
                        pltpu.VMEM((1, rep * qt), F32), pltpu.VMEM((qt, rep * qt), F32)],
        compiler_params=_params(("parallel", "parallel", "parallel")),
        name="nsa_attn",
    )(q_t, kcmp, vcmp_t, kas, vas_t, kaw, vaw_t, gates_t, ov_t)


def _nsa_overlap_t(s):
    n_cmp = (s - NSA_CMP_BLOCK) // NSA_CMP_STRIDE + 1
    n_slc = s // NSA_SEL_BLOCK
    cs = np.arange(n_cmp) * NSA_CMP_STRIDE
    ss = np.arange(n_slc) * NSA_SEL_BLOCK
    ov = np.clip(np.minimum(cs[:, None] + NSA_CMP_BLOCK, ss[None, :] + NSA_SEL_BLOCK)
                 - np.maximum(cs[:, None], ss[None, :]), 0, None) / NSA_CMP_BLOCK
    out = np.zeros((LANES, s // NSA_CMP_STRIDE), np.float32)
    out[:n_slc, :n_cmp] = ov.T
    return jnp.asarray(out, BF16)


def _mm_res_t_body(at_ref, w_ref, x_ref, o_ref):
    groups, tiles, kw, qt = at_ref.shape[1:]
    for tt in range(tiles):
        rows = slice(tt * qt, (tt + 1) * qt)
        y = x_ref[0, rows, :]
        for gi in range(groups):
            y = y + _dot_tn(at_ref[0, gi, tt], w_ref[gi * kw:(gi + 1) * kw, :])
        o_ref[0, rows, :] = y


def _mm_res_t(a_t, w, x, ts=ROW_TILE):
    b, groups, nkt, kw, qt = a_t.shape
    s, n = nkt * qt, w.shape[1]
    tpb = ts // qt
    return pl.pallas_call(
        _mm_res_t_body,
        out_shape=jax.ShapeDtypeStruct((b, s, n), F32),
        grid=(b, s // ts),
        in_specs=[pl.BlockSpec((1, groups, tpb, kw, qt), lambda bi, i: (bi, 0, i, 0, 0)),
                  _const_spec(w.shape), pl.BlockSpec((1, ts, n), lambda bi, i: (bi, i, 0))],
        out_specs=pl.BlockSpec((1, ts, n), lambda bi, i: (bi, i, 0)),
        compiler_params=_params(("parallel", "parallel")),
        name="mm_res_t",
    )(a_t, w, x)


def _nsa_mixer(x, g, w_in, cmp_pos, cmp_w1, cmp_w2, w_o):
    b, s, d = x.shape
    hh, gg, rep, dh = NSA_HEADS, NSA_KV_HEADS, NSA_REP, NSA_HEAD_DIM
    kvw = gg * dh
    assert s % ROW_TILE == 0 and s // NSA_SEL_BLOCK <= LANES - dh
    wb = w_in.astype(BF16)
    parts = [wb[:, d + i * kvw:d + (i + 1) * kvw].reshape(d, gg, dh) for i in range(6)]
    zeros = jnp.zeros((d, gg, LANES - dh), BF16)
    padded = lambda p: jnp.concatenate([p, zeros], axis=-1).reshape(d, gg * LANES)
    w_k = jnp.concatenate([padded(parts[2]), padded(parts[4])], axis=-1)
    w_vt = jnp.concatenate([padded(parts[3]), padded(parts[5])], axis=-1).T
    w_c = wb[:, d:d + 2 * kvw]
    w_g = wb[:, d + 6 * kvw:].reshape(d, gg, 3 * rep)
    w_gt = jnp.concatenate([w_g, jnp.zeros((d, gg, NSA_GATE_ROWS - 3 * rep), BF16)], axis=-1)
    w_gt = w_gt.reshape(d, gg * NSA_GATE_ROWS).T
    q_t, kas, kaw, vas_t, vaw_t, c, gates_t = _nsa_in(x, g, wb[:, :d].T, w_k, w_vt, w_c, w_gt)

    nch = s // NSA_CMP_STRIDE
    a = c.reshape(b, nch, NSA_CMP_STRIDE, 2, gg, dh).transpose(3, 0, 4, 1, 2, 5)
    a = a.reshape(2, b, gg, nch, NSA_CMP_STRIDE * dh)
    pos = cmp_pos.reshape(2, 2, 1, NSA_CMP_STRIDE * dh)
    kvc, kvc_t = _nsa_compress(a, pos[:, 0], pos[:, 1], cmp_w1.astype(BF16), cmp_w2.astype(BF16))

    o_t = _nsa_attn(q_t, kvc[0], kvc_t[1], kas, vas_t, kaw, vaw_t, gates_t, _nsa_overlap_t(s))
    return _mm_res_t(o_t, w_o.astype(BF16), x)


def kernel(x, mem, ffn1_norm, ffn1_w_in, ffn1_w_out, mix_norm, xattn_norm, mem_norm, xattn_w_q, xattn_w_kv, xattn_w_o, ffn2_norm, ffn2_w_in, ffn2_w_out, pool_w, pool_b, pool_scale, nsa_w_in, nsa_cmp_pos, nsa_cmp_w1, nsa_cmp_w2, nsa_w_o, gla_w_in, gla_w_gate_up, gla_b_gate, gla_norm, gla_w_o, conv_w_in, conv_b_in, conv_dw, conv_b_dw, conv_ln_g, conv_ln_b, conv_w_out, conv_b_out, final_norm):
    b, s, d = x.shape
    n_mem = mem.shape[1]
    depth = ffn1_norm.shape[0]
    n_mixers = 4
    flat = lambda a: a.reshape(b * s, d)
    cube = lambda a: a.reshape(b, s, d)
    mem2 = mem.reshape(b * n_mem, d)
    for i in range(depth):
        m, j = i % n_mixers, i // n_mixers
        x = cube(_ffn(flat(x), ffn1_norm[i], ffn1_w_in[i].astype(BF16), ffn1_w_out[i].astype(BF16)))
        if m == 0:
            x = _pool_mixer(x, mix_norm[i], pool_w[j].astype(BF16), pool_b[j], pool_scale[j])
        elif m == 1:
            x = _nsa_mixer(x, mix_norm[i], nsa_w_in[j], nsa_cmp_pos[j], nsa_cmp_w1[j], nsa_cmp_w2[j], nsa_w_o[j])
        elif m == 2:
            x = _gla_mixer(x, mix_norm[i], gla_w_in[j], gla_w_gate_up[j], gla_b_gate[j], gla_norm[j], gla_w_o[j])
        else:
            u = _conv_in(flat(x), mix_norm[i], conv_w_in[j].astype(BF16), conv_b_in[j])
            x = _conv_out(cube(u), x, conv_dw[j], conv_b_dw[j], conv_ln_g[j], conv_ln_b[j],
                          conv_w_out[j].astype(BF16), conv_b_out[j])
        kv = _rms_mm(mem2, mem_norm[i], xattn_w_kv[i].astype(BF16), BF16, tm=n_mem)
        x = _xattn(x, xattn_norm[i], xattn_w_q[i].astype(BF16), kv.reshape(b, n_mem, 2 * d),
                   xattn_w_o[i].astype(BF16))
        final_g = final_norm if i == depth - 1 else None
        x = cube(_ffn(flat(x), ffn2_norm[i], ffn2_w_in[i].astype(BF16), ffn2_w_out[i].astype(BF16),
                      final_g=final_g))
    return x
```

```python
import functools

import numpy as np
import jax
import jax.numpy as jnp
from jax import lax
from jax.experimental import pallas as pl
from jax.experimental.pallas import tpu as pltpu

F32 = jnp.float32
BF16 = jnp.bfloat16

EPS = 1e-6
NEG = -1e30

V7X_VMEM_BYTES = 64 * 1024 * 1024
VMEM_LIMIT = V7X_VMEM_BYTES - 8 * 1024 * 1024
LANES = 128
SUBLANES = 8

POOL_WINDOWS = (2, 4, 8, 16)
POOL_HALO = 16

NSA_HEADS = 16
NSA_KV_HEADS = 4
NSA_REP = NSA_HEADS // NSA_KV_HEADS
NSA_HEAD_DIM = 64
NSA_CMP_BLOCK = 32
NSA_CMP_STRIDE = 16
NSA_SEL_BLOCK = 64
NSA_N_SELECT = 16
NSA_WINDOW = 512
NSA_FORCE = 1e4
NSA_QT = 256
NSA_GATE_ROWS = 16

GLA_HEADS = 4
GLA_DK = 128
GLA_DV = 256
GLA_GATE_RANK = 16
GLA_TAU = 16.0
GLA_CHUNK = 64
GLA_ROWS = 256

CONV_WIDTH = 31
CONV_HALO = 32
CONV_STRIP = 128

XATTN_HEADS = 4

ROW_TILE = 512


def _params(sem):
    return pltpu.CompilerParams(dimension_semantics=sem, vmem_limit_bytes=VMEM_LIMIT)


def _rms(x, g):
    return x * lax.rsqrt(jnp.mean(x * x, axis=-1, keepdims=True) + EPS) * g


def _dot(a, b):
    return jnp.dot(a, b, preferred_element_type=F32)


def _dot_nt(a, b):
    return lax.dot_general(a, b, (((1,), (1,)), ((), ())), preferred_element_type=F32)


def _dot_tn(a, b):
    return lax.dot_general(a, b, (((0,), (0,)), ((), ())), preferred_element_type=F32)


def _const_spec(shape):
    nd = len(shape)
    return pl.BlockSpec(shape, lambda *_: (0,) * nd, pipeline_mode=pl.Buffered(1))


def _ffn_body(x_ref, g_ref, win_ref, wout_ref, *rest, d_ff, final):
    if final:
        fg_ref, o_ref = rest
    else:
        (o_ref,) = rest
    x = x_ref[...]
    h = _rms(x, g_ref[...]).astype(BF16)
    gu = _dot(h, win_ref[...])
    g = gu[:, :d_ff]
    u = gu[:, d_ff:]
    a = (g * jax.nn.sigmoid(g) * u).astype(BF16)
    out = x + 0.5 * _dot(a, wout_ref[...])
    if final:
        out = _rms(out, fg_ref[...])
    o_ref[...] = out


def _ffn(x2, g, w_in, w_out, final_g=None, tm=ROW_TILE):
    t, d = x2.shape
    d_ff = w_out.shape[0]
    final = final_g is not None
    in_specs = [pl.BlockSpec((tm, d), lambda i: (i, 0)), _const_spec((1, d)),
                _const_spec(w_in.shape), _const_spec(w_out.shape)]
    args = [x2, g.reshape(1, d), w_in, w_out]
    if final:
        in_specs.append(_const_spec((1, d)))
        args.append(final_g.reshape(1, d))
    return pl.pallas_call(
        functools.partial(_ffn_body, d_ff=d_ff, final=final),
        out_shape=jax.ShapeDtypeStruct((t, d), F32),
        grid=(t // tm,),
        in_specs=in_specs,
        out_specs=pl.BlockSpec((tm, d), lambda i: (i, 0)),
        compiler_params=_params(("parallel",)),
        name="ffn",
    )(*args)


def _rms_mm_body(x_ref, g_ref, w_ref, o_ref):
    h = _rms(x_ref[...], g_ref[...]).astype(BF16)
    o_ref[...] = _dot(h, w_ref[...]).astype(o_ref.dtype)


def _rms_mm(x2, g, w, out_dtype, tm):
    t, d = x2.shape
    n = w.shape[1]
    return pl.pallas_call(
        _rms_mm_body,
        out_shape=jax.ShapeDtypeStruct((t, n), out_dtype),
        grid=(t // tm,),
        in_specs=[pl.BlockSpec((tm, d), lambda i: (i, 0)), _const_spec((1, d)), _const_spec(w.shape)],
        out_specs=pl.BlockSpec((tm, n), lambda i: (i, 0)),
        compiler_params=_params(("parallel",)),
        name="rms_mm",
    )(x2, g.reshape(1, d), w)


def _xattn_body(x_ref, g_ref, wq_ref, k_ref, v_ref, wo_ref, o_ref, *, heads):
    x = x_ref[0]
    d = x.shape[-1]
    dh = d // heads
    h = _rms(x, g_ref[...]).astype(BF16)
    q = (_dot(h, wq_ref[...]) * dh ** -0.5).astype(BF16)
    cols = [slice(hd * dh, (hd + 1) * dh) for hd in range(heads)]
    scores = [_dot_nt(q[:, sl], k_ref[0][:, sl]) for sl in cols]
    probs = []
    for s in scores:
        e = jnp.exp(s - jnp.max(s, axis=-1, keepdims=True))
        probs.append((e * (1.0 / jnp.sum(e, axis=-1, keepdims=True))).astype(BF16))
    o = jnp.concatenate([_dot(p, v_ref[0][:, sl]).astype(BF16) for p, sl in zip(probs, cols)], axis=-1)
    o_ref[0] = x + _dot(o, wo_ref[...])


def _xattn(x, g, w_q, kv, w_o, ts=ROW_TILE):
    b, s, d = x.shape
    n = kv.shape[1]
    return pl.pallas_call(
        functools.partial(_xattn_body, heads=XATTN_HEADS),
        out_shape=jax.ShapeDtypeStruct((b, s, d), F32),
        grid=(b, s // ts),
        in_specs=[pl.BlockSpec((1, ts, d), lambda bi, i: (bi, i, 0)), _const_spec((1, d)),
                  _const_spec(w_q.shape),
                  pl.BlockSpec((1, n, d), lambda bi, i: (bi, 0, 0)),
                  pl.BlockSpec((1, n, d), lambda bi, i: (bi, 0, 1)),
                  _const_spec(w_o.shape)],
        out_specs=pl.BlockSpec((1, ts, d), lambda bi, i: (bi, i, 0)),
        compiler_params=_params(("parallel", "parallel")),
        name="xattn",
    )(x, g.reshape(1, d), w_q, kv, kv, w_o)


def _pool_body(x_ref, halo_ref, g_ref, w_ref, b_ref, sc_ref, o_ref, hbuf_ref, *, ts):
    i = pl.program_id(1)
    x = x_ref[0]
    d = x.shape[-1]
    gw = d // len(POOL_WINDOWS)
    g = g_ref[...]
    h = _rms(x, g)
    hh = _rms(halo_ref[0], g)
    hbuf_ref[0:POOL_HALO, :] = jnp.where(i > 0, hh, 0.0)
    hbuf_ref[POOL_HALO:, :] = h
    t = i * ts + lax.broadcasted_iota(jnp.int32, (ts, 1), 0)
    ys = []
    for gi, win in enumerate(POOL_WINDOWS):
        cs = slice(gi * gw, (gi + 1) * gw)
        acc = h[:, cs]
        for k in range(1, win):
            acc = acc + hbuf_ref[POOL_HALO - k:POOL_HALO - k + ts, cs]
        cnt = jnp.minimum(t + 1, win).astype(F32)
        p = acc / cnt - h[:, cs]
        ys.append(_dot(p.astype(BF16), w_ref[gi]))
    y = (jnp.concatenate(ys, axis=-1) + b_ref[...]) * sc_ref[...]
    o_ref[0] = x + y


def _pool_mixer(x, g, w, bias, scale, ts=ROW_TILE):
    b, s, d = x.shape
    hb = ts // POOL_HALO
    return pl.pallas_call(
        functools.partial(_pool_body, ts=ts),
        out_shape=jax.ShapeDtypeStruct((b, s, d), F32),
        grid=(b, s // ts),
        in_specs=[pl.BlockSpec((1, ts, d), lambda bi, i: (bi, i, 0)),
                  pl.BlockSpec((1, POOL_HALO, d), lambda bi, i: (bi, jnp.maximum(i * hb - 1, 0), 0)),
                  _const_spec((1, d)), _const_spec(w.shape), _const_spec((1, d)), _const_spec((1, d))],
        out_specs=pl.BlockSpec((1, ts, d), lambda bi, i: (bi, i, 0)),
        scratch_shapes=[pltpu.VMEM((ts + POOL_HALO, d), F32)],
        compiler_params=_params(("parallel", "parallel")),
        name="pool",
    )(x, x, g.reshape(1, d), w, bias.reshape(1, d), scale.reshape(1, d))


def _conv_in_body(x_ref, g_ref, w_ref, b_ref, o_ref):
    d = x_ref.shape[-1]
    h = _rms(x_ref[...], g_ref[...]).astype(BF16)
    ag = _dot(h, w_ref[...]) + b_ref[...]
    o_ref[...] = ag[:, :d] * jax.nn.sigmoid(ag[:, d:])


def _conv_in(x2, g, w, bias, tm=ROW_TILE):
    t, d = x2.shape
    return pl.pallas_call(
        _conv_in_body,
        out_shape=jax.ShapeDtypeStruct((t, d), F32),
        grid=(t // tm,),
        in_specs=[pl.BlockSpec((tm, d), lambda i: (i, 0)), _const_spec((1, d)),
                  _const_spec(w.shape), _const_spec((1, 2 * d))],
        out_specs=pl.BlockSpec((tm, d), lambda i: (i, 0)),
        compiler_params=_params(("parallel",)),
        name="conv_in",
    )(x2, g.reshape(1, d), w, bias.reshape(1, 2 * d))


def _conv_out_body(u_ref, halo_ref, x_ref, dw_ref, bdw_ref, lng_ref, lnb_ref, w_ref, bo_ref, o_ref,
                   ubuf_ref, cbuf_ref, *, ts):
    i = pl.program_id(1)
    n_slab = ubuf_ref.shape[0]
    halo = jnp.where(i > 0, halo_ref[0], 0.0)
    for lb in range(n_slab):
        ls = slice(lb * LANES, (lb + 1) * LANES)
        ubuf_ref[lb, 0:CONV_HALO, :] = halo[:, ls]
        ubuf_ref[lb, CONV_HALO:CONV_HALO + ts, :] = u_ref[0, :, ls]
        ubuf_ref[lb, CONV_HALO + ts:, :] = jnp.zeros((SUBLANES, LANES), F32)
    lead = CONV_HALO - (CONV_WIDTH - 1)

    def slab(lb, carry):
        for r0 in range(0, ts, CONV_STRIP):
            acc = None
            for sh in range(SUBLANES):
                taps = [k for k in range(CONV_WIDTH) if (lead + k) % SUBLANES == sh]
                if not taps:
                    continue
                win = ubuf_ref[lb, pl.ds(r0 + sh, CONV_STRIP + CONV_HALO), :]
                for k in taps:
                    a8 = (lead + k) // SUBLANES * SUBLANES
                    term = win[a8:a8 + CONV_STRIP] * dw_ref[lb, k:k + 1, :]
                    acc = term if acc is None else acc + term
            cbuf_ref[lb, r0:r0 + CONV_STRIP, :] = acc + bdw_ref[lb]
        return carry

    lax.fori_loop(0, n_slab, slab, 0)
    c = jnp.concatenate([cbuf_ref[lb] for lb in range(n_slab)], axis=-1)
    mu = jnp.mean(c, axis=-1, keepdims=True)
    var = jnp.mean(jnp.square(c - mu), axis=-1, keepdims=True)
    n = (c - mu) * lax.rsqrt(var + EPS) * lng_ref[...] + lnb_ref[...]
    a = (n * jax.nn.sigmoid(n)).astype(BF16)
    o_ref[0] = x_ref[0] + _dot(a, w_ref[...]) + bo_ref[...]


def _conv_out(u, x, dw, b_dw, ln_g, ln_b, w_out, b_out, ts=ROW_TILE):
    b, s, d = x.shape
    hb = ts // CONV_HALO
    n_slab = d // LANES
    dw_pad = jnp.zeros((CONV_HALO, d), F32).at[:CONV_WIDTH].set(dw)
    dw_slab = dw_pad.reshape(CONV_HALO, n_slab, LANES).transpose(1, 0, 2)
    bdw_slab = b_dw.reshape(n_slab, 1, LANES)
    row = lambda v: v.reshape(1, d)
    return pl.pallas_call(
        functools.partial(_conv_out_body, ts=ts),
        out_shape=jax.ShapeDtypeStruct((b, s, d), F32),
        grid=(b, s // ts),
        in_specs=[pl.BlockSpec((1, ts, d), lambda bi, i: (bi, i, 0)),
                  pl.BlockSpec((1, CONV_HALO, d), lambda bi, i: (bi, jnp.maximum(i * hb - 1, 0), 0)),
                  pl.BlockSpec((1, ts, d), lambda bi, i: (bi, i, 0)),
                  _const_spec(dw_slab.shape), _const_spec(bdw_slab.shape), _const_spec((1, d)),
                  _const_spec((1, d)), _const_spec(w_out.shape), _const_spec((1, d))],
        out_specs=pl.BlockSpec((1, ts, d), lambda bi, i: (bi, i, 0)),
        scratch_shapes=[pltpu.VMEM((n_slab, ts + CONV_HALO + SUBLANES, LANES), F32),
                        pltpu.VMEM((n_slab, ts, LANES), F32)],
        compiler_params=_params(("parallel", "parallel")),
        name="conv_out",
    )(u, u, x, dw_slab, bdw_slab, row(ln_g), row(ln_b), w_out, row(b_out))


def _gla_in_body(x_ref, g_ref, wqk_ref, wv_ref, wog_ref, wgd_ref, qk_ref, v_ref, og_ref, gd_ref):
    h = _rms(x_ref[...], g_ref[...]).astype(BF16)
    qk_ref[...] = _dot(h, wqk_ref[...])
    v_ref[...] = _dot(h, wv_ref[...]).astype(BF16)
    og_ref[...] = _dot(h, wog_ref[...])
    gd_ref[...] = _dot(h, wgd_ref[...])


def _gla_in(x2, g, w_qk, w_v, w_og, w_gd, tm=ROW_TILE):
    t, d = x2.shape
    widths = (w_qk.shape[1], w_v.shape[1], w_og.shape[1], w_gd.shape[1])
    dtypes = (F32, BF16, F32, F32)
    return pl.pallas_call(
        _gla_in_body,
        out_shape=[jax.ShapeDtypeStruct((t, n), dt) for n, dt in zip(widths, dtypes)],
        grid=(t // tm,),
        in_specs=[pl.BlockSpec((tm, d), lambda i: (i, 0)), _const_spec((1, d)),
                  _const_spec(w_qk.shape), _const_spec(w_v.shape), _const_spec(w_og.shape),
                  _const_spec(w_gd.shape)],
        out_specs=[pl.BlockSpec((tm, n), lambda i: (i, 0)) for n in widths],
        compiler_params=_params(("parallel",)),
        name="gla_in",
    )(x2, g.reshape(1, d), w_qk, w_v, w_og, w_gd)


def _gla_body(q_ref, k_ref, v_ref, og_ref, gd_ref, wup_ref, bg_ref, ng_ref, x_ref, wo_ref, o_ref,
              state_ref, obuf_ref):
    @pl.when(pl.program_id(1) == 0)
    def _():
        state_ref[...] = jnp.zeros_like(state_ref)

    c, dk, dv = GLA_CHUNK, GLA_DK, GLA_DV
    ga = _dot(gd_ref[0].astype(BF16), wup_ref[...]) + bg_ref[...]
    log_a = (jnp.minimum(ga, 0.0) - jnp.log1p(jnp.exp(-jnp.abs(ga)))) / GLA_TAU
    ri = lax.broadcasted_iota(jnp.int32, (c, c), 0)
    ci = lax.broadcasted_iota(jnp.int32, (c, c), 1)
    causal = ci <= ri
    tril = jnp.where(causal, 1.0, 0.0).astype(BF16)
    scale = dk ** -0.5
    n_chunks = GLA_ROWS // c
    chunk_rows = [slice(ch * c, (ch + 1) * c) for ch in range(n_chunks)]
    cases = [(ch, h) for ch in range(n_chunks) for h in range(GLA_HEADS)]
    hi = log_a.astype(BF16)
    r1 = log_a - hi.astype(F32)
    mid = r1.astype(BF16)
    lo = (r1 - mid.astype(F32)).astype(BF16)
    q_t, k_t, k_dec, decay = [], [], [], []
    for rows in chunk_rows:
        bcum = _dot(tril, hi[rows]) + _dot(tril, mid[rows]) + _dot(tril, lo[rows])
        b_last = bcum[c - 1:c, :]
        k = k_ref[0, rows, :]
        q_t.append((q_ref[0, rows, :] * scale * jnp.exp(bcum)).astype(BF16))
        k_t.append((k * jnp.exp(-bcum)).astype(BF16))
        k_dec.append((k * jnp.exp(b_last - bcum)).astype(BF16))
        decay.append(jnp.exp(b_last))
    ks = [slice(h * dk, (h + 1) * dk) for h in range(GLA_HEADS)]
    vs = [slice(h * dv, (h + 1) * dv) for h in range(GLA_HEADS)]
    v = {(ch, h): v_ref[0, chunk_rows[ch], vs[h]] for ch, h in cases}
    a = {(ch, h): jnp.where(causal, _dot_nt(q_t[ch][:, ks[h]], k_t[ch][:, ks[h]]), 0.0).astype(BF16)
         for ch, h in cases}
    kv = {(ch, h): _dot_tn(v[ch, h], k_dec[ch][:, ks[h]]) for ch, h in cases}
    o_intra = {(ch, h): _dot(a[ch, h], v[ch, h]) for ch, h in cases}
    state = [state_ref[h] for h in range(GLA_HEADS)]
    for ch, h in cases:
        o = o_intra[ch, h] + _dot_nt(q_t[ch][:, ks[h]], state[h].astype(BF16))
        state[h] = state[h] * decay[ch][:, ks[h]] + kv[ch, h]
        o = _rms(o, ng_ref[...])
        og = og_ref[0, chunk_rows[ch], vs[h]]
        obuf_ref[chunk_rows[ch], vs[h]] = (o * (og * jax.nn.sigmoid(og))).astype(BF16)
    for h in range(GLA_HEADS):
        state_ref[h] = state[h]
    o_ref[0] = x_ref[0] + _dot(obuf_ref[...], wo_ref[...])


def _gla_core(qk, v, og, gd, w_up, b_gate, norm_g, x, w_o):
    b, s, d = x.shape
    hh, dk, dv, rt = GLA_HEADS, GLA_DK, GLA_DV, GLA_ROWS
    return pl.pallas_call(
        _gla_body,
        out_shape=jax.ShapeDtypeStruct((b, s, d), F32),
        grid=(b, s // rt),
        in_specs=[pl.BlockSpec((1, rt, hh * dk), lambda bi, i: (bi, i, 0)),
                  pl.BlockSpec((1, rt, hh * dk), lambda bi, i: (bi, i, 1)),
                  pl.BlockSpec((1, rt, hh * dv), lambda bi, i: (bi, i, 0)),
                  pl.BlockSpec((1, rt, hh * dv), lambda bi, i: (bi, i, 0)),
                  pl.BlockSpec((1, rt, LANES), lambda bi, i: (bi, i, 0)),
                  _const_spec((LANES, hh * dk)), _const_spec((1, hh * dk)), _const_spec((1, dv)),
                  pl.BlockSpec((1, rt, d), lambda bi, i: (bi, i, 0)), _const_spec(w_o.shape)],
        out_specs=pl.BlockSpec((1, rt, d), lambda bi, i: (bi, i, 0)),
        scratch_shapes=[pltpu.VMEM((hh, dv, dk), F32), pltpu.VMEM((rt, hh * dv), BF16)],
        compiler_params=_params(("parallel", "arbitrary")),
        name="gla_core",
    )(qk, qk, v, og, gd, w_up, b_gate.reshape(1, hh * dk), norm_g.reshape(1, dv), x, w_o)


def _gla_mixer(x, g, w_in, w_gate_up, b_gate, norm_g, w_o):
    b, s, d = x.shape
    qw, vw = GLA_HEADS * GLA_DK, GLA_HEADS * GLA_DV
    x2 = x.reshape(b * s, d)
    w_gd = jnp.zeros((d, LANES), BF16).at[:, :GLA_GATE_RANK].set(w_in[:, 2 * qw + 2 * vw:].astype(BF16))
    w_up = jnp.zeros((LANES, qw), BF16).at[:GLA_GATE_RANK].set(w_gate_up.astype(BF16))
    qk, v, og, gd = _gla_in(x2, g, w_in[:, :2 * qw].astype(BF16), w_in[:, 2 * qw:2 * qw + vw].astype(BF16),
                            w_in[:, 2 * qw + vw:2 * qw + 2 * vw].astype(BF16), w_gd)
    r3 = lambda a: a.reshape(b, s, a.shape[-1])
    return _gla_core(r3(qk), r3(v), r3(og), r3(gd), w_up, b_gate, norm_g, x, w_o.astype(BF16))


def _nsa_in_body(x_ref, g_ref, wqt_ref, wk_ref, wvt_ref, wc_ref, wgt_ref,
                 qt_ref, kas_ref, kaw_ref, vast_ref, vawt_ref, c_ref, gtt_ref, *, ts):
    i = pl.program_id(1)
    qt, groups = NSA_QT, NSA_KV_HEADS
    h = _rms(x_ref[0], g_ref[...]).astype(BF16)
    q_all = (_dot_nt(wqt_ref[...], h) * NSA_HEAD_DIM ** -0.5).astype(BF16)
    hw = NSA_REP * NSA_HEAD_DIM
    for gi in range(groups):
        for tt in range(ts // qt):
            qt_ref[0, gi, tt] = q_all[gi * hw:(gi + 1) * hw, tt * qt:(tt + 1) * qt]
    k = _dot(h, wk_ref[...])
    gw = groups * LANES
    lane = lax.broadcasted_iota(jnp.int32, (ts, LANES), 1)
    t = i * ts + lax.broadcasted_iota(jnp.int32, (ts, LANES), 0)
    onehot = jnp.where(lane - NSA_HEAD_DIM == t // NSA_SEL_BLOCK, 1.0, 0.0)
    for gi in range(groups):
        ls = slice(gi * LANES, (gi + 1) * LANES)
        kas_ref[0, gi] = (k[:, ls] + onehot).astype(BF16)
        kaw_ref[0, gi] = k[:, gw + gi * LANES:gw + (gi + 1) * LANES].astype(BF16)
    vt = _dot_nt(wvt_ref[...], h).astype(BF16)
    dh = NSA_HEAD_DIM
    for gi in range(groups):
        for tt in range(ts // qt):
            cs = slice(tt * qt, (tt + 1) * qt)
            vast_ref[0, gi, tt] = vt[gi * dh:(gi + 1) * dh, cs]
            vawt_ref[0, gi, tt] = vt[(groups + gi) * dh:(groups + gi + 1) * dh, cs]
    c_ref[0] = _dot(h, wc_ref[...])
    gtt_ref[0] = _dot_nt(wgt_ref[...], h)


def _nsa_in(x, g, w_qt, w_k, w_vt, w_c, w_gt, ts=ROW_TILE):
    b, s, d = x.shape
    groups, qt = NSA_KV_HEADS, NSA_QT
    gw = groups * LANES
    nkt = s // qt
    tpb = ts // qt
    seq_spec = lambda n: pl.BlockSpec((1, ts, n), lambda bi, i: (bi, i, 0))
    rows_spec = lambda n: pl.BlockSpec((1, n, ts), lambda bi, i: (bi, 0, i))
    vt_spec = pl.BlockSpec((1, groups, tpb, NSA_HEAD_DIM, qt), lambda bi, i: (bi, 0, i, 0, 0))
    vt_shape = jax.ShapeDtypeStruct((b, groups, nkt, NSA_HEAD_DIM, qt), BF16)
    k_spec = pl.BlockSpec((1, groups, ts, LANES), lambda bi, i: (bi, 0, i, 0))
    k_shape = jax.ShapeDtypeStruct((b, groups, s, LANES), BF16)
    return pl.pallas_call(
        functools.partial(_nsa_in_body, ts=ts),
        out_shape=[jax.ShapeDtypeStruct((b, groups, nkt, w_qt.shape[0] // groups, qt), BF16),
                   k_shape, k_shape,
                   vt_shape, vt_shape,
                   jax.ShapeDtypeStruct((b, s, w_c.shape[1]), F32),
                   jax.ShapeDtypeStruct((b, w_gt.shape[0], s), F32)],
        grid=(b, s // ts),
        in_specs=[pl.BlockSpec((1, ts, d), lambda bi, i: (bi, i, 0)), _const_spec((1, d)),
                  _const_spec(w_qt.shape), _const_spec(w_k.shape), _const_spec(w_vt.shape),
                  _const_spec(w_c.shape), _const_spec(w_gt.shape)],
        out_specs=[pl.BlockSpec((1, groups, tpb, w_qt.shape[0] // groups, qt), lambda bi, i: (bi, 0, i, 0, 0)),
                   k_spec, k_spec, vt_spec, vt_spec,
                   seq_spec(w_c.shape[1]), rows_spec(w_gt.shape[0])],
        compiler_params=_params(("parallel", "parallel")),
        name="nsa_in",
    )(x, g.reshape(1, d), w_qt, w_k, w_vt, w_c, w_gt)


def _nsa_cmp_body(a_ref, pt_ref, pb_ref, w1_ref, w2_ref, w2t_ref, o_ref, ot_ref):
    a = a_ref[0, 0, 0]
    half = a.shape[-1]
    u = _dot((a + pt_ref[0]).astype(BF16), w1_ref[0, :half, :])
    v = _dot((a + pb_ref[0]).astype(BF16), w1_ref[0, half:, :])
    nrow = a.shape[0]
    hid = jax.nn.gelu(u + pltpu.roll(v, nrow - 1, 0), approximate=True).astype(BF16)
    o_ref[0, 0, 0] = _dot(hid, w2_ref[0])
    ot_ref[0, 0, 0] = _dot_nt(w2t_ref[0], hid)


def _nsa_compress(a, pos_top, pos_bot, w1, w2):
    _, b, g, nch, feat = a.shape
    dh = w2.shape[-1]
    w2t = w2.transpose(0, 2, 1)
    per_kv = lambda shape: pl.BlockSpec((1,) + shape, lambda kv, bi, gi: (kv,) + (0,) * len(shape))
    return pl.pallas_call(
        _nsa_cmp_body,
        out_shape=[jax.ShapeDtypeStruct((2, b, g, nch, dh), F32), jax.ShapeDtypeStruct((2, b, g, dh, nch), F32)],
        grid=(2, b, g),
        in_specs=[pl.BlockSpec((1, 1, 1, nch, feat), lambda kv, bi, gi: (kv, bi, gi, 0, 0)),
                  per_kv((1, feat)), per_kv((1, feat)), per_kv(w1.shape[1:]), per_kv(w2.shape[1:]),
                  per_kv(w2t.shape[1:])],
        out_specs=[pl.BlockSpec((1, 1, 1, nch, dh), lambda kv, bi, gi: (kv, bi, gi, 0, 0)),
                   pl.BlockSpec((1, 1, 1, dh, nch), lambda kv, bi, gi: (kv, bi, gi, 0, 0))],
        compiler_params=_params(("parallel", "parallel", "parallel")),
        name="nsa_compress",
    )(a, pos_top, pos_bot, w1, w2, w2t)


def _nsa_attn_body(qt_ref, kc_ref, vct_ref, kas_ref, vast_ref, kaw_ref, vawt_ref, gtt_ref, ovt_ref, o_ref,
                   qaug_ref, m_ref, l_ref, acc_ref, p_ref, alpha_ref, s_ref, *, n_slc):
    qt, rep, dh = NSA_QT, NSA_REP, NSA_HEAD_DIM
    cols = rep * qt
    qi = pl.program_id(2)
    s0 = qi * qt
    q = jnp.concatenate([qt_ref[0, 0, 0, r * dh:(r + 1) * dh, :] for r in range(rep)], axis=1)
    t_col = s0 + (lax.broadcasted_iota(jnp.int32, (1, cols), 1) & (qt - 1))

    n_pad = kc_ref.shape[2]
    sc = _dot(kc_ref[0, 0].astype(BF16), q)
    cmp_end = lax.broadcasted_iota(jnp.int32, (n_pad, 1), 0) * NSA_CMP_STRIDE + (NSA_CMP_BLOCK - 1)
    sc = jnp.where(cmp_end <= t_col, sc, NEG)
    e = jnp.exp(sc - jnp.max(sc, axis=0, keepdims=True))
    p_cmp = e * (1.0 / jnp.sum(e, axis=0, keepdims=True))
    p_cmp = jnp.where(t_col >= NSA_CMP_BLOCK - 1, p_cmp, 0.0).astype(BF16)
    o_cmp = _dot(vct_ref[0, 0].astype(BF16), p_cmp)

    imp = _dot(ovt_ref[...], p_cmp[:, 0:qt])
    for r in range(1, rep):
        imp = imp + _dot(ovt_ref[...], p_cmp[:, r * qt:(r + 1) * qt])
    imp_t = imp[:n_slc]
    blk = lax.broadcasted_iota(jnp.int32, (n_slc, 1), 0)
    t_q = s0 + lax.broadcasted_iota(jnp.int32, (1, qt), 1)
    cur = t_q // NSA_SEL_BLOCK
    forced = (blk == 0) | (blk == cur) | (blk == cur - 1)
    visible = blk * NSA_SEL_BLOCK <= t_q
    imp_t = jnp.where(forced, NSA_FORCE, imp_t)
    imp_t = jnp.where(visible, imp_t, -1.0)
    rank = jnp.zeros((n_slc, qt), F32)
    for i in range(n_slc):
        row = imp_t[i:i + 1, :]
        beats = (row > imp_t) | ((row == imp_t) & (blk > i))
        rank = rank + jnp.where(beats, 1.0, 0.0)
    sel = (rank < float(min(NSA_N_SELECT, n_slc))) & visible
    bias_parts = [jnp.where(sel, 0.0, NEG)]
    if LANES - dh - n_slc:
        bias_parts.append(jnp.zeros((LANES - dh - n_slc, qt), F32))
    bias = jnp.concatenate(bias_parts, axis=0).astype(BF16)
    qaug_ref[0:dh, :] = q
    for r in range(rep):
        qaug_ref[dh:, r * qt:(r + 1) * qt] = bias

    key_r = lax.broadcasted_iota(jnp.int32, (qt, 1), 0)

    heads = [slice(r * qt, (r + 1) * qt) for r in range(rep)]

    def score(ka_ref, j):
        start = j * qt
        if not isinstance(start, int):
            start = pl.multiple_of(start, qt)
        ka = ka_ref[0, 0, pl.ds(start, qt), :]
        return [_dot(ka, qaug_ref[:, cs]) for cs in heads]

    def flush(vat_ref, jp):
        vat = vat_ref[0, 0, jp]
        return [_dot(vat, p_ref[:, cs]) for cs in heads]

    def accumulate(pvs):
        for cs, pv in zip(heads, pvs):
            acc_ref[:, cs] = acc_ref[:, cs] * alpha_ref[:, cs] + pv

    def step(ka_ref, vat_ref, j, mask_fn, has_next):
        nxt = score(ka_ref, j + 1) if has_next else None
        pvs = flush(vat_ref, jnp.maximum(j - 1, 0))
        probs, alphas = [], []
        for cs in heads:
            s = s_ref[:, cs]
            if mask_fn is not None:
                s = jnp.where(mask_fn(j * qt + key_r, t_col[:, cs]), s, NEG)
            m_old = m_ref[:, cs]
            m_new = jnp.maximum(m_old, jnp.max(s, axis=0, keepdims=True))
            p = jnp.exp(s - m_new)
            alpha = jnp.exp(m_old - m_new)
            l_ref[:, cs] = l_ref[:, cs] * alpha + jnp.sum(p, axis=0, keepdims=True)
            probs.append(p.astype(BF16))
            alphas.append(alpha)
            m_ref[:, cs] = m_new
        accumulate(pvs)
        for cs, p, alpha in zip(heads, probs, alphas):
            p_ref[:, cs] = p
            alpha_ref[:, cs] = alpha
        if has_next:
            for cs, s in zip(heads, nxt):
                s_ref[:, cs] = s

    def attend(ka_ref, vat_ref, lo, mask_fn, last_mask_fn):
        m_ref[...] = jnp.full_like(m_ref, NEG)
        acc_ref[...] = jnp.zeros_like(acc_ref)
        l_ref[...] = jnp.zeros_like(l_ref)
        p_ref[...] = jnp.zeros_like(p_ref)
        alpha_ref[...] = jnp.ones_like(alpha_ref)
        for cs, s in zip(heads, score(ka_ref, lo)):
            s_ref[:, cs] = s

        def body(j, carry):
            step(ka_ref, vat_ref, j, mask_fn, True)
            return carry

        lax.fori_loop(lo, qi, body, 0)
        step(ka_ref, vat_ref, qi, last_mask_fn, False)
        accumulate(flush(vat_ref, qi))
        return acc_ref[...] / l_ref[...]

    o_slc = attend(kas_ref, vast_ref, 0, None, lambda kpos, t: kpos <= t)

    o_win = attend(kaw_ref, vawt_ref, jnp.maximum(qi - NSA_WINDOW // qt, 0),
                   lambda kpos, t: kpos > t - NSA_WINDOW, lambda kpos, t: kpos <= t)

    gates = jax.nn.sigmoid(gtt_ref[0])
    for r in range(rep):
        cs = slice(r * qt, (r + 1) * qt)
        o = (gates[3 * r:3 * r + 1] * o_cmp[:, cs] + gates[3 * r + 1:3 * r + 2] * o_slc[:, cs]
             + gates[3 * r + 2:3 * r + 3] * o_win[:, cs])
        o_ref[0, 0, 0, r * dh:(r + 1) * dh, :] = o.astype(BF16)


def _nsa_attn(q_t, kcmp, vcmp_t, kas, vas_t, kaw, vaw_t, gates_t, ov_t):
    b, g, nkt, _, qt = q_t.shape
    rep, dh = NSA_REP, NSA_HEAD_DIM
    s = nkt * qt
    n_slc = s // NSA_SEL_BLOCK
    nch = kcmp.shape[2]
    k_spec = pl.BlockSpec((1, 1, s, LANES), lambda bi, gi, i: (bi, gi, 0, 0))
    vt_spec = pl.BlockSpec((1, 1, nkt, dh, qt), lambda bi, gi, i: (bi, gi, 0, 0, 0))
    q_spec = pl.BlockSpec((1, 1, 1, rep * dh, qt), lambda bi, gi, i: (bi, gi, i, 0, 0))
    return pl.pallas_call(
        functools.partial(_nsa_attn_body, n_slc=n_slc),
        out_shape=jax.ShapeDtypeStruct(q_t.shape, BF16),
        grid=(b, g, s // qt),
        in_specs=[q_spec,
                  pl.BlockSpec((1, 1, nch, dh), lambda bi, gi, i: (bi, gi, 0, 0)),
                  pl.BlockSpec((1, 1, dh, nch), lambda bi, gi, i: (bi, gi, 0, 0)),
                  k_spec, vt_spec, k_spec, vt_spec,
                  pl.BlockSpec((1, NSA_GATE_ROWS, qt), lambda bi, gi, i: (bi, gi, i)),
                  _const_spec(ov_t.shape)],
        out_specs=q_spec,
        scratch_shapes=[pltpu.VMEM((LANES, rep * qt), BF16), pltpu.VMEM((1, rep * qt), F32),
                        pltpu.VMEM((1, rep * qt), F32),
                        pltpu.VMEM((dh, rep * qt), F32), pltpu.VMEM((qt, rep * qt), BF16),
                        pltpu.VMEM((1, rep * qt), F32), pltpu.VMEM((qt, rep * qt), F32)],
        compiler_params=_params(("parallel", "parallel", "parallel")),
        name="nsa_attn",
    )(q_t, kcmp, vcmp_t, kas, vas_t, kaw, vaw_t, gates_t, ov_t)


def _nsa_overlap_t(s):
    n_cmp = (s - NSA_CMP_BLOCK) // NSA_CMP_STRIDE + 1
    n_slc = s // NSA_SEL_BLOCK
    cs = np.arange(n_cmp) * NSA_CMP_STRIDE
    ss = np.arange(n_slc) * NSA_SEL_BLOCK
    ov = np.clip(np.minimum(cs[:, None] + NSA_CMP_BLOCK, ss[None, :] + NSA_SEL_BLOCK)
                 - np.maximum(cs[:, None], ss[None, :]), 0, None) / NSA_CMP_BLOCK
    out = np.zeros((LANES, s // NSA_CMP_STRIDE), np.float32)
    out[:n_slc, :n_cmp] = ov.T
    return jnp.asarray(out, BF16)


def _mm_res_t_body(at_ref, w_ref, x_ref, o_ref):
    groups, tiles, kw, qt = at_ref.shape[1:]
    for tt in range(tiles):
        rows = slice(tt * qt, (tt + 1) * qt)
        y = x_ref[0, rows, :]
        for gi in range(groups):
            y = y + _dot_tn(at_ref[0, gi, tt], w_ref[gi * kw:(gi + 1) * kw, :])
        o_ref[0, rows, :] = y


def _mm_res_t(a_t, w, x, ts=ROW_TILE):
    b, groups, nkt, kw, qt = a_t.shape
    s, n = nkt * qt, w.shape[1]
    tpb = ts // qt
    return pl.pallas_call(
        _mm_res_t_body,
        out_shape=jax.ShapeDtypeStruct((b, s, n), F32),
        grid=(b, s // ts),
        in_specs=[pl.BlockSpec((1, groups, tpb, kw, qt), lambda bi, i: (bi, 0, i, 0, 0)),
                  _const_spec(w.shape), pl.BlockSpec((1, ts, n), lambda bi, i: (bi, i, 0))],
        out_specs=pl.BlockSpec((1, ts, n), lambda bi, i: (bi, i, 0)),
        compiler_params=_params(("parallel", "parallel")),
        name="mm_res_t",
    )(a_t, w, x)


def _nsa_mixer(x, g, w_in, cmp_pos, cmp_w1, cmp_w2, w_o):
    b, s, d = x.shape
    hh, gg, rep, dh = NSA_HEADS, NSA_KV_HEADS, NSA_REP, NSA_HEAD_DIM
    kvw = gg * dh
    assert s % ROW_TILE == 0 and s // NSA_SEL_BLOCK <= LANES - dh
    wb = w_in.astype(BF16)
    parts = [wb[:, d + i * kvw:d + (i + 1) * kvw].reshape(d, gg, dh) for i in range(6)]
    zeros = jnp.zeros((d, gg, LANES - dh), BF16)
    padded = lambda p: jnp.concatenate([p, zeros], axis=-1).reshape(d, gg * LANES)
    w_k = jnp.concatenate([padded(parts[2]), padded(parts[4])], axis=-1)
    w_vt = jnp.concatenate([parts[3].reshape(d, kvw), parts[5].reshape(d, kvw)], axis=-1).T
    w_c = wb[:, d:d + 2 * kvw]
    w_g = wb[:, d + 6 * kvw:].reshape(d, gg, 3 * rep)
    w_gt = jnp.concatenate([w_g, jnp.zeros((d, gg, NSA_GATE_ROWS - 3 * rep), BF16)], axis=-1)
    w_gt = w_gt.reshape(d, gg * NSA_GATE_ROWS).T
    q_t, kas, kaw, vas_t, vaw_t, c, gates_t = _nsa_in(x, g, wb[:, :d].T, w_k, w_vt, w_c, w_gt)

    nch = s // NSA_CMP_STRIDE
    a = c.reshape(b, nch, NSA_CMP_STRIDE, 2, gg, dh).transpose(3, 0, 4, 1, 2, 5)
    a = a.reshape(2, b, gg, nch, NSA_CMP_STRIDE * dh)
    pos = cmp_pos.reshape(2, 2, 1, NSA_CMP_STRIDE * dh)
    kvc, kvc_t = _nsa_compress(a, pos[:, 0], pos[:, 1], cmp_w1.astype(BF16), cmp_w2.astype(BF16))

    o_t = _nsa_attn(q_t, kvc[0], kvc_t[1], kas, vas_t, kaw, vaw_t, gates_t, _nsa_overlap_t(s))
    return _mm_res_t(o_t, w_o.astype(BF16), x)


def kernel(x, mem, ffn1_norm, ffn1_w_in, ffn1_w_out, mix_norm, xattn_norm, mem_norm, xattn_w_q, xattn_w_kv, xattn_w_o, ffn2_norm, ffn2_w_in, ffn2_w_out, pool_w, pool_b, pool_scale, nsa_w_in, nsa_cmp_pos, nsa_cmp_w1, nsa_cmp_w2, nsa_w_o, gla_w_in, gla_w_gate_up, gla_b_gate, gla_norm, gla_w_o, conv_w_in, conv_b_in, conv_dw, conv_b_dw, conv_ln_g, conv_ln_b, conv_w_out, conv_b_out, final_norm):
    b, s, d = x.shape
    n_mem = mem.shape[1]
    depth = ffn1_norm.shape[0]
    n_mixers = 4
    flat = lambda a: a.reshape(b * s, d)
    cube = lambda a: a.reshape(b, s, d)
    mem2 = mem.reshape(b * n_mem, d)
    for i in range(depth):
        m, j = i % n_mixers, i // n_mixers
        x = cube(_ffn(flat(x), ffn1_norm[i], ffn1_w_in[i].astype(BF16), ffn1_w_out[i].astype(BF16)))
        if m == 0:
            x = _pool_mixer(x, mix_norm[i], pool_w[j].astype(BF16), pool_b[j], pool_scale[j])
        elif m == 1:
            x = _nsa_mixer(x, mix_norm[i], nsa_w_in[j], nsa_cmp_pos[j], nsa_cmp_w1[j], nsa_cmp_w2[j], nsa_w_o[j])
        elif m == 2:
            x = _gla_mixer(x, mix_norm[i], gla_w_in[j], gla_w_gate_up[j], gla_b_gate[j], gla_norm[j], gla_w_o[j])
        else:
            u = _conv_in(flat(x), mix_norm[i], conv_w_in[j].astype(BF16), conv_b_in[j])
            x = _conv_out(cube(u), x, conv_dw[j], conv_b_dw[j], conv_ln_g[j], conv_ln_b[j],
                          conv_w_out[j].astype(BF16), conv_b_out[j])
        kv = _rms_mm(mem2, mem_norm[i], xattn_w_kv[i].astype(BF16), BF16, tm=n_mem)
        x = _xattn(x, xattn_norm[i], xattn_w_q[i].astype(BF16), kv.reshape(b, n_mem, 2 * d),
                   xattn_w_o[i].astype(BF16))
        final_g = final_norm if i == depth - 1 else None
        x = cube(_ffn(flat(x), ffn2_norm[i], ffn2_w_in[i].astype(BF16), ffn2_w_out[i].astype(BF16),
                      final_g=final_g))
    return x
```

```python
import functools

import numpy as np
import jax
import jax.numpy as jnp
from jax import lax
from jax.experimental import pallas as pl
from jax.experimental.pallas import tpu as pltpu

F32 = jnp.float32
BF16 = jnp.bfloat16

EPS = 1e-6
NEG = -1e30

V7X_VMEM_BYTES = 64 * 1024 * 1024
VMEM_LIMIT = V7X_VMEM_BYTES - 8 * 1024 * 1024
LANES = 128
SUBLANES = 8

POOL_WINDOWS = (2, 4, 8, 16)
POOL_HALO = 16

NSA_HEADS = 16
NSA_KV_HEADS = 4
NSA_REP = NSA_HEADS // NSA_KV_HEADS
NSA_HEAD_DIM = 64
NSA_CMP_BLOCK = 32
NSA_CMP_STRIDE = 16
NSA_SEL_BLOCK = 64
NSA_N_SELECT = 16
NSA_WINDOW = 512
NSA_FORCE = 1e4
NSA_QT = 256
NSA_GATE_ROWS = 16
NSA_SEL_SPAN = 2
LOG2E = 1.4426950408889634

GLA_HEADS = 4
GLA_DK = 128
GLA_DV = 256
GLA_GATE_RANK = 16
GLA_TAU = 16.0
GLA_CHUNK = 64
GLA_ROWS = 256

CONV_WIDTH = 31
CONV_HALO = 32
CONV_STRIP = 128

XATTN_HEADS = 4

ROW_TILE = 512


def _params(sem):
    return pltpu.CompilerParams(dimension_semantics=sem, vmem_limit_bytes=VMEM_LIMIT)


def _rms(x, g):
    return x * lax.rsqrt(jnp.mean(x * x, axis=-1, keepdims=True) + EPS) * g


def _dot(a, b):
    return jnp.dot(a, b, preferred_element_type=F32)


def _dot_nt(a, b):
    return lax.dot_general(a, b, (((1,), (1,)), ((), ())), preferred_element_type=F32)


def _dot_tn(a, b):
    return lax.dot_general(a, b, (((0,), (0,)), ((), ())), preferred_element_type=F32)


def _const_spec(shape):
    nd = len(shape)
    return pl.BlockSpec(shape, lambda *_: (0,) * nd, pipeline_mode=pl.Buffered(1))


def _ffn_body(x_ref, g_ref, win_ref, wout_ref, *rest, d_ff, final):
    if final:
        fg_ref, o_ref = rest
    else:
        (o_ref,) = rest
    x = x_ref[...]
    h = _rms(x, g_ref[...]).astype(BF16)
    gu = _dot(h, win_ref[...])
    g = gu[:, :d_ff]
    u = gu[:, d_ff:]
    a = (g * jax.nn.sigmoid(g) * u).astype(BF16)
    out = x + 0.5 * _dot(a, wout_ref[...])
    if final:
        out = _rms(out, fg_ref[...])
    o_ref[...] = out


def _ffn(x2, g, w_in, w_out, final_g=None, tm=ROW_TILE):
    t, d = x2.shape
    d_ff = w_out.shape[0]
    final = final_g is not None
    in_specs = [pl.BlockSpec((tm, d), lambda i: (i, 0)), _const_spec((1, d)),
                _const_spec(w_in.shape), _const_spec(w_out.shape)]
    args = [x2, g.reshape(1, d), w_in, w_out]
    if final:
        in_specs.append(_const_spec((1, d)))
        args.append(final_g.reshape(1, d))
    return pl.pallas_call(
        functools.partial(_ffn_body, d_ff=d_ff, final=final),
        out_shape=jax.ShapeDtypeStruct((t, d), F32),
        grid=(t // tm,),
        in_specs=in_specs,
        out_specs=pl.BlockSpec((tm, d), lambda i: (i, 0)),
        compiler_params=_params(("parallel",)),
        name="ffn",
    )(*args)


def _rms_mm_body(x_ref, g_ref, w_ref, o_ref):
    h = _rms(x_ref[...], g_ref[...]).astype(BF16)
    o_ref[...] = _dot(h, w_ref[...]).astype(o_ref.dtype)


def _rms_mm(x2, g, w, out_dtype, tm):
    t, d = x2.shape
    n = w.shape[1]
    return pl.pallas_call(
        _rms_mm_body,
        out_shape=jax.ShapeDtypeStruct((t, n), out_dtype),
        grid=(t // tm,),
        in_specs=[pl.BlockSpec((tm, d), lambda i: (i, 0)), _const_spec((1, d)), _const_spec(w.shape)],
        out_specs=pl.BlockSpec((tm, n), lambda i: (i, 0)),
        compiler_params=_params(("parallel",)),
        name="rms_mm",
    )(x2, g.reshape(1, d), w)


def _xattn_body(x_ref, g_ref, wq_ref, k_ref, v_ref, wo_ref, o_ref, *, heads):
    x = x_ref[0]
    d = x.shape[-1]
    dh = d // heads
    h = _rms(x, g_ref[...]).astype(BF16)
    q = (_dot(h, wq_ref[...]) * dh ** -0.5).astype(BF16)
    cols = [slice(hd * dh, (hd + 1) * dh) for hd in range(heads)]
    scores = [_dot_nt(q[:, sl], k_ref[0][:, sl]) for sl in cols]
    probs = []
    for s in scores:
        e = jnp.exp(s - jnp.max(s, axis=-1, keepdims=True))
        probs.append((e * (1.0 / jnp.sum(e, axis=-1, keepdims=True))).astype(BF16))
    o = jnp.concatenate([_dot(p, v_ref[0][:, sl]).astype(BF16) for p, sl in zip(probs, cols)], axis=-1)
    o_ref[0] = x + _dot(o, wo_ref[...])


def _xattn(x, g, w_q, kv, w_o, ts=ROW_TILE):
    b, s, d = x.shape
    n = kv.shape[1]
    return pl.pallas_call(
        functools.partial(_xattn_body, heads=XATTN_HEADS),
        out_shape=jax.ShapeDtypeStruct((b, s, d), F32),
        grid=(b, s // ts),
        in_specs=[pl.BlockSpec((1, ts, d), lambda bi, i: (bi, i, 0)), _const_spec((1, d)),
                  _const_spec(w_q.shape),
                  pl.BlockSpec((1, n, d), lambda bi, i: (bi, 0, 0)),
                  pl.BlockSpec((1, n, d), lambda bi, i: (bi, 0, 1)),
                  _const_spec(w_o.shape)],
        out_specs=pl.BlockSpec((1, ts, d), lambda bi, i: (bi, i, 0)),
        compiler_params=_params(("parallel", "parallel")),
        name="xattn",
    )(x, g.reshape(1, d), w_q, kv, kv, w_o)


def _pool_body(x_ref, halo_ref, g_ref, w_ref, b_ref, sc_ref, o_ref, hbuf_ref, *, ts):
    i = pl.program_id(1)
    x = x_ref[0]
    d = x.shape[-1]
    gw = d // len(POOL_WINDOWS)
    g = g_ref[...]
    h = _rms(x, g)
    hh = _rms(halo_ref[0], g)
    hbuf_ref[0:POOL_HALO, :] = jnp.where(i > 0, hh, 0.0)
    hbuf_ref[POOL_HALO:, :] = h
    t = i * ts + lax.broadcasted_iota(jnp.int32, (ts, 1), 0)
    ys = []
    for gi, win in enumerate(POOL_WINDOWS):
        cs = slice(gi * gw, (gi + 1) * gw)
        acc = h[:, cs]
        for k in range(1, win):
            acc = acc + hbuf_ref[POOL_HALO - k:POOL_HALO - k + ts, cs]
        cnt = jnp.minimum(t + 1, win).astype(F32)
        p = acc / cnt - h[:, cs]
        ys.append(_dot(p.astype(BF16), w_ref[gi]))
    y = (jnp.concatenate(ys, axis=-1) + b_ref[...]) * sc_ref[...]
    o_ref[0] = x + y


def _pool_mixer(x, g, w, bias, scale, ts=ROW_TILE):
    b, s, d = x.shape
    hb = ts // POOL_HALO
    return pl.pallas_call(
        functools.partial(_pool_body, ts=ts),
        out_shape=jax.ShapeDtypeStruct((b, s, d), F32),
        grid=(b, s // ts),
        in_specs=[pl.BlockSpec((1, ts, d), lambda bi, i: (bi, i, 0)),
                  pl.BlockSpec((1, POOL_HALO, d), lambda bi, i: (bi, jnp.maximum(i * hb - 1, 0), 0)),
                  _const_spec((1, d)), _const_spec(w.shape), _const_spec((1, d)), _const_spec((1, d))],
        out_specs=pl.BlockSpec((1, ts, d), lambda bi, i: (bi, i, 0)),
        scratch_shapes=[pltpu.VMEM((ts + POOL_HALO, d), F32)],
        compiler_params=_params(("parallel", "parallel")),
        name="pool",
    )(x, x, g.reshape(1, d), w, bias.reshape(1, d), scale.reshape(1, d))


def _conv_in_body(x_ref, g_ref, w_ref, b_ref, o_ref):
    d = x_ref.shape[-1]
    h = _rms(x_ref[...], g_ref[...]).astype(BF16)
    ag = _dot(h, w_ref[...]) + b_ref[...]
    o_ref[...] = ag[:, :d] * jax.nn.sigmoid(ag[:, d:])


def _conv_in(x2, g, w, bias, tm=ROW_TILE):
    t, d = x2.shape
    return pl.pallas_call(
        _conv_in_body,
        out_shape=jax.ShapeDtypeStruct((t, d), F32),
        grid=(t // tm,),
        in_specs=[pl.BlockSpec((tm, d), lambda i: (i, 0)), _const_spec((1, d)),
                  _const_spec(w.shape), _const_spec((1, 2 * d))],
        out_specs=pl.BlockSpec((tm, d), lambda i: (i, 0)),
        compiler_params=_params(("parallel",)),
        name="conv_in",
    )(x2, g.reshape(1, d), w, bias.reshape(1, 2 * d))


def _conv_out_body(u_ref, halo_ref, x_ref, dw_ref, bdw_ref, lng_ref, lnb_ref, w_ref, bo_ref, o_ref,
                   ubuf_ref, cbuf_ref, *, ts):
    i = pl.program_id(1)
    n_slab = ubuf_ref.shape[0]
    halo = jnp.where(i > 0, halo_ref[0], 0.0)
    for lb in range(n_slab):
        ls = slice(lb * LANES, (lb + 1) * LANES)
        ubuf_ref[lb, 0:CONV_HALO, :] = halo[:, ls]
        ubuf_ref[lb, CONV_HALO:CONV_HALO + ts, :] = u_ref[0, :, ls]
        ubuf_ref[lb, CONV_HALO + ts:, :] = jnp.zeros((SUBLANES, LANES), F32)
    lead = CONV_HALO - (CONV_WIDTH - 1)

    def slab(lb, carry):
        for r0 in range(0, ts, CONV_STRIP):
            acc = None
            for sh in range(SUBLANES):
                taps = [k for k in range(CONV_WIDTH) if (lead + k) % SUBLANES == sh]
                if not taps:
                    continue
                win = ubuf_ref[lb, pl.ds(r0 + sh, CONV_STRIP + CONV_HALO), :]
                for k in taps:
                    a8 = (lead + k) // SUBLANES * SUBLANES
                    term = win[a8:a8 + CONV_STRIP] * dw_ref[lb, k:k + 1, :]
                    acc = term if acc is None else acc + term
            cbuf_ref[lb, r0:r0 + CONV_STRIP, :] = acc + bdw_ref[lb]
        return carry

    lax.fori_loop(0, n_slab, slab, 0)
    c = jnp.concatenate([cbuf_ref[lb] for lb in range(n_slab)], axis=-1)
    mu = jnp.mean(c, axis=-1, keepdims=True)
    var = jnp.mean(jnp.square(c - mu), axis=-1, keepdims=True)
    n = (c - mu) * lax.rsqrt(var + EPS) * lng_ref[...] + lnb_ref[...]
    a = (n * jax.nn.sigmoid(n)).astype(BF16)
    o_ref[0] = x_ref[0] + _dot(a, w_ref[...]) + bo_ref[...]


def _conv_out(u, x, dw, b_dw, ln_g, ln_b, w_out, b_out, ts=ROW_TILE):
    b, s, d = x.shape
    hb = ts // CONV_HALO
    n_slab = d // LANES
    dw_pad = jnp.zeros((CONV_HALO, d), F32).at[:CONV_WIDTH].set(dw)
    dw_slab = dw_pad.reshape(CONV_HALO, n_slab, LANES).transpose(1, 0, 2)
    bdw_slab = b_dw.reshape(n_slab, 1, LANES)
    row = lambda v: v.reshape(1, d)
    return pl.pallas_call(
        functools.partial(_conv_out_body, ts=ts),
        out_shape=jax.ShapeDtypeStruct((b, s, d), F32),
        grid=(b, s // ts),
        in_specs=[pl.BlockSpec((1, ts, d), lambda bi, i: (bi, i, 0)),
                  pl.BlockSpec((1, CONV_HALO, d), lambda bi, i: (bi, jnp.maximum(i * hb - 1, 0), 0)),
                  pl.BlockSpec((1, ts, d), lambda bi, i: (bi, i, 0)),
                  _const_spec(dw_slab.shape), _const_spec(bdw_slab.shape), _const_spec((1, d)),
                  _const_spec((1, d)), _const_spec(w_out.shape), _const_spec((1, d))],
        out_specs=pl.BlockSpec((1, ts, d), lambda bi, i: (bi, i, 0)),
        scratch_shapes=[pltpu.VMEM((n_slab, ts + CONV_HALO + SUBLANES, LANES), F32),
                        pltpu.VMEM((n_slab, ts, LANES), F32)],
        compiler_params=_params(("parallel", "parallel")),
        name="conv_out",
    )(u, u, x, dw_slab, bdw_slab, row(ln_g), row(ln_b), w_out, row(b_out))


def _gla_in_body(x_ref, g_ref, wqk_ref, wv_ref, wog_ref, wgd_ref, qk_ref, v_ref, og_ref, gd_ref):
    h = _rms(x_ref[...], g_ref[...]).astype(BF16)
    qk_ref[...] = _dot(h, wqk_ref[...])
    v_ref[...] = _dot(h, wv_ref[...]).astype(BF16)
    og_ref[...] = _dot(h, wog_ref[...])
    gd_ref[...] = _dot(h, wgd_ref[...])


def _gla_in(x2, g, w_qk, w_v, w_og, w_gd, tm=ROW_TILE):
    t, d = x2.shape
    widths = (w_qk.shape[1], w_v.shape[1], w_og.shape[1], w_gd.shape[1])
    dtypes = (F32, BF16, F32, F32)
    return pl.pallas_call(
        _gla_in_body,
        out_shape=[jax.ShapeDtypeStruct((t, n), dt) for n, dt in zip(widths, dtypes)],
        grid=(t // tm,),
        in_specs=[pl.BlockSpec((tm, d), lambda i: (i, 0)), _const_spec((1, d)),
                  _const_spec(w_qk.shape), _const_spec(w_v.shape), _const_spec(w_og.shape),
                  _const_spec(w_gd.shape)],
        out_specs=[pl.BlockSpec((tm, n), lambda i: (i, 0)) for n in widths],
        compiler_params=_params(("parallel",)),
        name="gla_in",
    )(x2, g.reshape(1, d), w_qk, w_v, w_og, w_gd)


def _gla_body(q_ref, k_ref, v_ref, og_ref, gd_ref, wup_ref, bg_ref, ng_ref, x_ref, wo_ref, o_ref,
              state_ref, obuf_ref):
    @pl.when(pl.program_id(1) == 0)
    def _():
        state_ref[...] = jnp.zeros_like(state_ref)

    c, dk, dv = GLA_CHUNK, GLA_DK, GLA_DV
    ga = _dot(gd_ref[0].astype(BF16), wup_ref[...]) + bg_ref[...]
    log_a = (jnp.minimum(ga, 0.0) - jnp.log1p(jnp.exp(-jnp.abs(ga)))) / GLA_TAU
    ri = lax.broadcasted_iota(jnp.int32, (c, c), 0)
    ci = lax.broadcasted_iota(jnp.int32, (c, c), 1)
    causal = ci <= ri
    tril = jnp.where(causal, 1.0, 0.0).astype(BF16)
    scale = dk ** -0.5
    n_chunks = GLA_ROWS // c
    chunk_rows = [slice(ch * c, (ch + 1) * c) for ch in range(n_chunks)]
    cases = [(ch, h) for ch in range(n_chunks) for h in range(GLA_HEADS)]
    hi = log_a.astype(BF16)
    r1 = log_a - hi.astype(F32)
    mid = r1.astype(BF16)
    lo = (r1 - mid.astype(F32)).astype(BF16)
    q_t, k_t, k_dec, decay = [], [], [], []
    for rows in chunk_rows:
        bcum = _dot(tril, hi[rows]) + _dot(tril, mid[rows]) + _dot(tril, lo[rows])
        b_last = bcum[c - 1:c, :]
        k = k_ref[0, rows, :]
        q_t.append((q_ref[0, rows, :] * scale * jnp.exp(bcum)).astype(BF16))
        k_t.append((k * jnp.exp(-bcum)).astype(BF16))
        k_dec.append((k * jnp.exp(b_last - bcum)).astype(BF16))
        decay.append(jnp.exp(b_last))
    ks = [slice(h * dk, (h + 1) * dk) for h in range(GLA_HEADS)]
    vs = [slice(h * dv, (h + 1) * dv) for h in range(GLA_HEADS)]
    v = {(ch, h): v_ref[0, chunk_rows[ch], vs[h]] for ch, h in cases}
    a = {(ch, h): jnp.where(causal, _dot_nt(q_t[ch][:, ks[h]], k_t[ch][:, ks[h]]), 0.0).astype(BF16)
         for ch, h in cases}
    kv = {(ch, h): _dot_tn(v[ch, h], k_dec[ch][:, ks[h]]) for ch, h in cases}
    o_intra = {(ch, h): _dot(a[ch, h], v[ch, h]) for ch, h in cases}
    state = [state_ref[h] for h in range(GLA_HEADS)]
    for ch, h in cases:
        o = o_intra[ch, h] + _dot_nt(q_t[ch][:, ks[h]], state[h].astype(BF16))
        state[h] = state[h] * decay[ch][:, ks[h]] + kv[ch, h]
        o = _rms(o, ng_ref[...])
        og = og_ref[0, chunk_rows[ch], vs[h]]
        obuf_ref[chunk_rows[ch], vs[h]] = (o * (og * jax.nn.sigmoid(og))).astype(BF16)
    for h in range(GLA_HEADS):
        state_ref[h] = state[h]
    o_ref[0] = x_ref[0] + _dot(obuf_ref[...], wo_ref[...])


def _gla_core(qk, v, og, gd, w_up, b_gate, norm_g, x, w_o):
    b, s, d = x.shape
    hh, dk, dv, rt = GLA_HEADS, GLA_DK, GLA_DV, GLA_ROWS
    return pl.pallas_call(
        _gla_body,
        out_shape=jax.ShapeDtypeStruct((b, s, d), F32),
        grid=(b, s // rt),
        in_specs=[pl.BlockSpec((1, rt, hh * dk), lambda bi, i: (bi, i, 0)),
                  pl.BlockSpec((1, rt, hh * dk), lambda bi, i: (bi, i, 1)),
                  pl.BlockSpec((1, rt, hh * dv), lambda bi, i: (bi, i, 0)),
                  pl.BlockSpec((1, rt, hh * dv), lambda bi, i: (bi, i, 0)),
                  pl.BlockSpec((1, rt, LANES), lambda bi, i: (bi, i, 0)),
                  _const_spec((LANES, hh * dk)), _const_spec((1, hh * dk)), _const_spec((1, dv)),
                  pl.BlockSpec((1, rt, d), lambda bi, i: (bi, i, 0)), _const_spec(w_o.shape)],
        out_specs=pl.BlockSpec((1, rt, d), lambda bi, i: (bi, i, 0)),
        scratch_shapes=[pltpu.VMEM((hh, dv, dk), F32), pltpu.VMEM((rt, hh * dv), BF16)],
        compiler_params=_params(("parallel", "arbitrary")),
        name="gla_core",
    )(qk, qk, v, og, gd, w_up, b_gate.reshape(1, hh * dk), norm_g.reshape(1, dv), x, w_o)


def _gla_mixer(x, g, w_in, w_gate_up, b_gate, norm_g, w_o):
    b, s, d = x.shape
    qw, vw = GLA_HEADS * GLA_DK, GLA_HEADS * GLA_DV
    x2 = x.reshape(b * s, d)
    w_gd = jnp.zeros((d, LANES), BF16).at[:, :GLA_GATE_RANK].set(w_in[:, 2 * qw + 2 * vw:].astype(BF16))
    w_up = jnp.zeros((LANES, qw), BF16).at[:GLA_GATE_RANK].set(w_gate_up.astype(BF16))
    qk, v, og, gd = _gla_in(x2, g, w_in[:, :2 * qw].astype(BF16), w_in[:, 2 * qw:2 * qw + vw].astype(BF16),
                            w_in[:, 2 * qw + vw:2 * qw + 2 * vw].astype(BF16), w_gd)
    r3 = lambda a: a.reshape(b, s, a.shape[-1])
    return _gla_core(r3(qk), r3(v), r3(og), r3(gd), w_up, b_gate, norm_g, x, w_o.astype(BF16))


def _nsa_in_body(x_ref, g_ref, wqt_ref, wk_ref, wvt_ref, wc_ref, wgt_ref,
                 qt_ref, kas_ref, kaw_ref, vast_ref, vawt_ref, c_ref, gtt_ref, *, ts):
    i = pl.program_id(1)
    qt, groups = NSA_QT, NSA_KV_HEADS
    h = _rms(x_ref[0], g_ref[...]).astype(BF16)
    q_all = (_dot_nt(wqt_ref[...], h) * (NSA_HEAD_DIM ** -0.5 * LOG2E)).astype(BF16)
    hw = NSA_REP * NSA_HEAD_DIM
    for gi in range(groups):
        for tt in range(ts // qt):
            qt_ref[0, gi, tt] = q_all[gi * hw:(gi + 1) * hw, tt * qt:(tt + 1) * qt]
    k = _dot(h, wk_ref[...])
    gw = groups * LANES
    lane = lax.broadcasted_iota(jnp.int32, (ts, LANES), 1)
    t = i * ts + lax.broadcasted_iota(jnp.int32, (ts, LANES), 0)
    onehot = jnp.where(lane - NSA_HEAD_DIM == t // NSA_SEL_BLOCK, 1.0, 0.0)
    for gi in range(groups):
        ls = slice(gi * LANES, (gi + 1) * LANES)
        kas_ref[0, gi] = (k[:, ls] + onehot).astype(BF16)
        kaw_ref[0, gi] = k[:, gw + gi * LANES:gw + (gi + 1) * LANES].astype(BF16)
    vt = _dot_nt(wvt_ref[...], h)
    ones = jnp.where(lax.broadcasted_iota(jnp.int32, (LANES, qt), 0) >= NSA_HEAD_DIM, 1.0, 0.0)
    for gi in range(groups):
        for tt in range(ts // qt):
            cs = slice(tt * qt, (tt + 1) * qt)
            vast_ref[0, gi, tt] = (vt[gi * LANES:(gi + 1) * LANES, cs] + ones).astype(BF16)
            vawt_ref[0, gi, tt] = (vt[gw + gi * LANES:gw + (gi + 1) * LANES, cs] + ones).astype(BF16)
    c_ref[0] = _dot(h, wc_ref[...])
    gtt_ref[0] = _dot_nt(wgt_ref[...], h)


def _nsa_in(x, g, w_qt, w_k, w_vt, w_c, w_gt, ts=ROW_TILE):
    b, s, d = x.shape
    groups, qt = NSA_KV_HEADS, NSA_QT
    gw = groups * LANES
    nkt = s // qt
    tpb = ts // qt
    seq_spec = lambda n: pl.BlockSpec((1, ts, n), lambda bi, i: (bi, i, 0))
    rows_spec = lambda n: pl.BlockSpec((1, n, ts), lambda bi, i: (bi, 0, i))
    vt_spec = pl.BlockSpec((1, groups, tpb, LANES, qt), lambda bi, i: (bi, 0, i, 0, 0))
    vt_shape = jax.ShapeDtypeStruct((b, groups, nkt, LANES, qt), BF16)
    k_spec = pl.BlockSpec((1, groups, ts, LANES), lambda bi, i: (bi, 0, i, 0))
    k_shape = jax.ShapeDtypeStruct((b, groups, s, LANES), BF16)
    return pl.pallas_call(
        functools.partial(_nsa_in_body, ts=ts),
        out_shape=[jax.ShapeDtypeStruct((b, groups, nkt, w_qt.shape[0] // groups, qt), BF16),
                   k_shape, k_shape,
                   vt_shape, vt_shape,
                   jax.ShapeDtypeStruct((b, s, w_c.shape[1]), F32),
                   jax.ShapeDtypeStruct((b, w_gt.shape[0], s), F32)],
        grid=(b, s // ts),
        in_specs=[pl.BlockSpec((1, ts, d), lambda bi, i: (bi, i, 0)), _const_spec((1, d)),
                  _const_spec(w_qt.shape), _const_spec(w_k.shape), _const_spec(w_vt.shape),
                  _const_spec(w_c.shape), _const_spec(w_gt.shape)],
        out_specs=[pl.BlockSpec((1, groups, tpb, w_qt.shape[0] // groups, qt), lambda bi, i: (bi, 0, i, 0, 0)),
                   k_spec, k_spec, vt_spec, vt_spec,
                   seq_spec(w_c.shape[1]), rows_spec(w_gt.shape[0])],
        compiler_params=_params(("parallel", "parallel")),
        name="nsa_in",
    )(x, g.reshape(1, d), w_qt, w_k, w_vt, w_c, w_gt)


def _nsa_cmp_body(a_ref, pt_ref, pb_ref, w1_ref, w2_ref, w2t_ref, o_ref, ot_ref):
    a = a_ref[0, 0, 0]
    half = a.shape[-1]
    u = _dot((a + pt_ref[0]).astype(BF16), w1_ref[0, :half, :])
    v = _dot((a + pb_ref[0]).astype(BF16), w1_ref[0, half:, :])
    nrow = a.shape[0]
    hid = jax.nn.gelu(u + pltpu.roll(v, nrow - 1, 0), approximate=True).astype(BF16)
    o_ref[0, 0, 0] = _dot(hid, w2_ref[0])
    ot_ref[0, 0, 0] = _dot_nt(w2t_ref[0], hid)


def _nsa_compress(a, pos_top, pos_bot, w1, w2):
    _, b, g, nch, feat = a.shape
    dh = w2.shape[-1]
    w2t = w2.transpose(0, 2, 1)
    per_kv = lambda shape: pl.BlockSpec((1,) + shape, lambda kv, bi, gi: (kv,) + (0,) * len(shape))
    return pl.pallas_call(
        _nsa_cmp_body,
        out_shape=[jax.ShapeDtypeStruct((2, b, g, nch, dh), F32), jax.ShapeDtypeStruct((2, b, g, dh, nch), F32)],
        grid=(2, b, g),
        in_specs=[pl.BlockSpec((1, 1, 1, nch, feat), lambda kv, bi, gi: (kv, bi, gi, 0, 0)),
                  per_kv((1, feat)), per_kv((1, feat)), per_kv(w1.shape[1:]), per_kv(w2.shape[1:]),
                  per_kv(w2t.shape[1:])],
        out_specs=[pl.BlockSpec((1, 1, 1, nch, dh), lambda kv, bi, gi: (kv, bi, gi, 0, 0)),
                   pl.BlockSpec((1, 1, 1, dh, nch), lambda kv, bi, gi: (kv, bi, gi, 0, 0))],
        compiler_params=_params(("parallel", "parallel", "parallel")),
        name="nsa_compress",
    )(a, pos_top, pos_bot, w1, w2, w2t)


def _nsa_attn_body(qt_ref, kc_ref, vct_ref, kas_ref, vast_ref, kaw_ref, vawt_ref, gtt_ref, ovt_ref, o_ref,
                   qaug_ref, m_ref, acc_ref, p_ref, alpha_ref, s_ref, *, n_slc):
    qt, rep, dh = NSA_QT, NSA_REP, NSA_HEAD_DIM
    cols = rep * qt
    qi = pl.program_id(2)
    s0 = qi * qt
    q = jnp.concatenate([qt_ref[0, 0, 0, r * dh:(r + 1) * dh, :] for r in range(rep)], axis=1)
    t_col = s0 + (lax.broadcasted_iota(jnp.int32, (1, cols), 1) & (qt - 1))

    n_pad = kc_ref.shape[2]
    sc = _dot(kc_ref[0, 0].astype(BF16), q)
    cmp_end = lax.broadcasted_iota(jnp.int32, (n_pad, 1), 0) * NSA_CMP_STRIDE + (NSA_CMP_BLOCK - 1)
    sc = jnp.where(cmp_end <= t_col, sc, NEG)
    e = jnp.exp2(sc - jnp.max(sc, axis=0, keepdims=True))
    p_cmp = e * (1.0 / jnp.sum(e, axis=0, keepdims=True))
    p_cmp = jnp.where(t_col >= NSA_CMP_BLOCK - 1, p_cmp, 0.0).astype(BF16)
    o_cmp = _dot(vct_ref[0, 0].astype(BF16), p_cmp)

    imp = _dot(ovt_ref[...], p_cmp[:, 0:qt])
    for r in range(1, rep):
        imp = imp + _dot(ovt_ref[...], p_cmp[:, r * qt:(r + 1) * qt])
    imp_t = imp[:n_slc]
    blk = lax.broadcasted_iota(jnp.int32, (n_slc, 1), 0)
    t_q = s0 + lax.broadcasted_iota(jnp.int32, (1, qt), 1)
    cur = t_q // NSA_SEL_BLOCK
    forced = (blk == 0) | (blk == cur) | (blk == cur - 1)
    visible = blk * NSA_SEL_BLOCK <= t_q
    imp_t = jnp.where(forced, NSA_FORCE, imp_t)
    imp_t = jnp.where(visible, imp_t, -1.0)
    rank = jnp.zeros((n_slc, qt), F32)
    for i in range(n_slc):
        row = imp_t[i:i + 1, :]
        beats = (row > imp_t) | ((row == imp_t) & (blk > i))
        rank = rank + jnp.where(beats, 1.0, 0.0)
    sel = (rank < float(min(NSA_N_SELECT, n_slc))) & visible
    bias_parts = [jnp.where(sel, 0.0, NEG)]
    if LANES - dh - n_slc:
        bias_parts.append(jnp.zeros((LANES - dh - n_slc, qt), F32))
    bias = jnp.concatenate(bias_parts, axis=0).astype(BF16)
    qaug_ref[0:dh, :] = q
    for r in range(rep):
        qaug_ref[dh:, r * qt:(r + 1) * qt] = bias

    heads = [slice(r * qt, (r + 1) * qt) for r in range(rep)]

    def score(ka_ref, j, nt):
        start = j * (nt * qt)
        if not isinstance(start, int):
            start = pl.multiple_of(start, nt * qt)
        ka = ka_ref[0, 0, pl.ds(start, nt * qt), :]
        return [_dot(ka, qaug_ref[:, cs]) for cs in heads]

    def flush(vat_ref, jp, nt):
        pvs = []
        for cs in heads:
            pv = _dot(vat_ref[0, 0, jp * nt], p_ref[0:qt, cs])
            for u in range(1, nt):
                pv = pv + _dot(vat_ref[0, 0, jp * nt + u], p_ref[u * qt:(u + 1) * qt, cs])
            pvs.append(pv)
        return pvs

    def accumulate(pvs):
        for cs, pv in zip(heads, pvs):
            acc_ref[:, cs] = acc_ref[:, cs] * alpha_ref[:, cs] + pv

    def step(ka_ref, vat_ref, j, nt, mask_fn, has_next):
        rows = nt * qt
        nxt = score(ka_ref, j + 1, nt) if has_next else None
        pvs = flush(vat_ref, jnp.maximum(j - 1, 0), nt)
        probs, alphas = [], []
        for cs in heads:
            s = s_ref[0:rows, cs]
            if mask_fn is not None:
                kpos = j * rows + lax.broadcasted_iota(jnp.int32, (rows, 1), 0)
                s = jnp.where(mask_fn(kpos, t_col[:, cs]), s, NEG)
            m_old = m_ref[:, cs]
            m_new = jnp.maximum(m_old, jnp.max(s, axis=0, keepdims=True))
            probs.append(jnp.exp2(s - m_new).astype(BF16))
            alphas.append(jnp.exp2(m_old - m_new))
            m_ref[:, cs] = m_new
        accumulate(pvs)
        for cs, p, alpha in zip(heads, probs, alphas):
            p_ref[0:rows, cs] = p
            alpha_ref[:, cs] = alpha
        if has_next:
            for cs, s in zip(heads, nxt):
                s_ref[0:rows, cs] = s

    def attend(ka_ref, vat_ref, nt, lo, hi, mask_fn, last_mask_fn):
        rows = nt * qt
        m_ref[...] = jnp.full_like(m_ref, NEG)
        acc_ref[...] = jnp.zeros_like(acc_ref)
        p_ref[0:rows, :] = jnp.zeros((rows, cols), BF16)
        alpha_ref[...] = jnp.ones_like(alpha_ref)
        for cs, s in zip(heads, score(ka_ref, lo, nt)):
            s_ref[0:rows, cs] = s

        def body(j, carry):
            step(ka_ref, vat_ref, j, nt, mask_fn, True)
            return carry

        lax.fori_loop(lo, hi, body, 0)
        step(ka_ref, vat_ref, hi, nt, last_mask_fn, False)
        accumulate(flush(vat_ref, hi, nt))
        acc = acc_ref[...]
        return acc[0:dh] / acc[dh:dh + 1]

    o_slc = attend(kas_ref, vast_ref, NSA_SEL_SPAN, 0, qi // NSA_SEL_SPAN, None, lambda kpos, t: kpos <= t)

    o_win = attend(kaw_ref, vawt_ref, 1, jnp.maximum(qi - NSA_WINDOW // qt, 0), qi,
                   lambda kpos, t: kpos > t - NSA_WINDOW, lambda kpos, t: kpos <= t)

    gates = jax.nn.sigmoid(gtt_ref[0])
    for r in range(rep):
        cs = slice(r * qt, (r + 1) * qt)
        o = (gates[3 * r:3 * r + 1] * o_cmp[:, cs] + gates[3 * r + 1:3 * r + 2] * o_slc[:, cs]
             + gates[3 * r + 2:3 * r + 3] * o_win[:, cs])
        o_ref[0, 0, 0, r * dh:(r + 1) * dh, :] = o.astype(BF16)


def _nsa_attn(q_t, kcmp, vcmp_t, kas, vas_t, kaw, vaw_t, gates_t, ov_t):
    b, g, nkt, _, qt = q_t.shape
    rep, dh = NSA_REP, NSA_HEAD_DIM
    s = nkt * qt
    n_slc = s // NSA_SEL_BLOCK
    nch = kcmp.shape[2]
    k_spec = pl.BlockSpec((1, 1, s, LANES), lambda bi, gi, i: (bi, gi, 0, 0))
    vt_spec = pl.BlockSpec((1, 1, nkt, LANES, qt), lambda bi, gi, i: (bi, gi, 0, 0, 0))
    q_spec = pl.BlockSpec((1, 1, 1, rep * dh, qt), lambda bi, gi, i: (bi, gi, i, 0, 0))
    return pl.pallas_call(
        functools.partial(_nsa_attn_body, n_slc=n_slc),
        out_shape=jax.ShapeDtypeStruct(q_t.shape, BF16),
        grid=(b, g, s // qt),
        in_specs=[q_spec,
                  pl.BlockSpec((1, 1, nch, dh), lambda bi, gi, i: (bi, gi, 0, 0)),
                  pl.BlockSpec((1, 1, dh, nch), lambda bi, gi, i: (bi, gi, 0, 0)),
                  k_spec, vt_spec, k_spec, vt_spec,
                  pl.BlockSpec((1, NSA_GATE_ROWS, qt), lambda bi, gi, i: (bi, gi, i)),
                  _const_spec(ov_t.shape)],
        out_specs=q_spec,
        scratch_shapes=[pltpu.VMEM((LANES, rep * qt), BF16), pltpu.VMEM((1, rep * qt), F32),
                        pltpu.VMEM((LANES, rep * qt), F32), pltpu.VMEM((NSA_SEL_SPAN * qt, rep * qt), BF16),
                        pltpu.VMEM((1, rep * qt), F32), pltpu.VMEM((NSA_SEL_SPAN * qt, rep * qt), F32)],
        compiler_params=_params(("parallel", "parallel", "parallel")),
        name="nsa_attn",
    )(q_t, kcmp, vcmp_t, kas, vas_t, kaw, vaw_t, gates_t, ov_t)


def _nsa_overlap_t(s):
    n_cmp = (s - NSA_CMP_BLOCK) // NSA_CMP_STRIDE + 1
    n_slc = s // NSA_SEL_BLOCK
    cs = np.arange(n_cmp) * NSA_CMP_STRIDE
    ss = np.arange(n_slc) * NSA_SEL_BLOCK
    ov = np.clip(np.minimum(cs[:, None] + NSA_CMP_BLOCK, ss[None, :] + NSA_SEL_BLOCK)
                 - np.maximum(cs[:, None], ss[None, :]), 0, None) / NSA_CMP_BLOCK
    out = np.zeros((LANES, s // NSA_CMP_STRIDE), np.float32)
    out[:n_slc, :n_cmp] = ov.T
    return jnp.asarray(out, BF16)


def _mm_res_t_body(at_ref, w_ref, x_ref, o_ref):
    groups, tiles, kw, qt = at_ref.shape[1:]
    for tt in range(tiles):
        rows = slice(tt * qt, (tt + 1) * qt)
        y = x_ref[0, rows, :]
        for gi in range(groups):
            y = y + _dot_tn(at_ref[0, gi, tt], w_ref[gi * kw:(gi + 1) * kw, :])
        o_ref[0, rows, :] = y


def _mm_res_t(a_t, w, x, ts=ROW_TILE):
    b, groups, nkt, kw, qt = a_t.shape
    s, n = nkt * qt, w.shape[1]
    tpb = ts // qt
    return pl.pallas_call(
        _mm_res_t_body,
        out_shape=jax.ShapeDtypeStruct((b, s, n), F32),
        grid=(b, s // ts),
        in_specs=[pl.BlockSpec((1, groups, tpb, kw, qt), lambda bi, i: (bi, 0, i, 0, 0)),
                  _const_spec(w.shape), pl.BlockSpec((1, ts, n), lambda bi, i: (bi, i, 0))],
        out_specs=pl.BlockSpec((1, ts, n), lambda bi, i: (bi, i, 0)),
        compiler_params=_params(("parallel", "parallel")),
        name="mm_res_t",
    )(a_t, w, x)


def _nsa_mixer(x, g, w_in, cmp_pos, cmp_w1, cmp_w2, w_o):
    b, s, d = x.shape
    hh, gg, rep, dh = NSA_HEADS, NSA_KV_HEADS, NSA_REP, NSA_HEAD_DIM
    kvw = gg * dh
    assert s % (NSA_SEL_SPAN * NSA_QT) == 0 and s % ROW_TILE == 0 and s // NSA_SEL_BLOCK <= LANES - dh
    wb = w_in.astype(BF16)
    parts = [wb[:, d + i * kvw:d + (i + 1) * kvw].reshape(d, gg, dh) for i in range(6)]
    zeros = jnp.zeros((d, gg, LANES - dh), BF16)
    padded = lambda p: jnp.concatenate([p, zeros], axis=-1).reshape(d, gg * LANES)
    w_k = jnp.concatenate([padded(parts[2]), padded(parts[4])], axis=-1)
    w_vt = jnp.concatenate([padded(parts[3]), padded(parts[5])], axis=-1).T
    w_c = wb[:, d:d + 2 * kvw]
    w_g = wb[:, d + 6 * kvw:].reshape(d, gg, 3 * rep)
    w_gt = jnp.concatenate([w_g, jnp.zeros((d, gg, NSA_GATE_ROWS - 3 * rep), BF16)], axis=-1)
    w_gt = w_gt.reshape(d, gg * NSA_GATE_ROWS).T
    q_t, kas, kaw, vas_t, vaw_t, c, gates_t = _nsa_in(x, g, wb[:, :d].T, w_k, w_vt, w_c, w_gt)

    nch = s // NSA_CMP_STRIDE
    a = c.reshape(b, nch, NSA_CMP_STRIDE, 2, gg, dh).transpose(3, 0, 4, 1, 2, 5)
    a = a.reshape(2, b, gg, nch, NSA_CMP_STRIDE * dh)
    pos = cmp_pos.reshape(2, 2, 1, NSA_CMP_STRIDE * dh)
    kvc, kvc_t = _nsa_compress(a, pos[:, 0], pos[:, 1], cmp_w1.astype(BF16), cmp_w2.astype(BF16))

    o_t = _nsa_attn(q_t, kvc[0], kvc_t[1], kas, vas_t, kaw, vaw_t, gates_t, _nsa_overlap_t(s))
    return _mm_res_t(o_t, w_o.astype(BF16), x)


def kernel(x, mem, ffn1_norm, ffn1_w_in, ffn1_w_out, mix_norm, xattn_norm, mem_norm, xattn_w_q, xattn_w_kv, xattn_w_o, ffn2_norm, ffn2_w_in, ffn2_w_out, pool_w, pool_b, pool_scale, nsa_w_in, nsa_cmp_pos, nsa_cmp_w1, nsa_cmp_w2, nsa_w_o, gla_w_in, gla_w_gate_up, gla_b_gate, gla_norm, gla_w_o, conv_w_in, conv_b_in, conv_dw, conv_b_dw, conv_ln_g, conv_ln_b, conv_w_out, conv_b_out, final_norm):
    b, s, d = x.shape
    n_mem = mem.shape[1]
    depth = ffn1_norm.shape[0]
    n_mixers = 4
    flat = lambda a: a.reshape(b * s, d)
    cube = lambda a: a.reshape(b, s, d)
    mem2 = mem.reshape(b * n_mem, d)
    for i in range(depth):
        m, j = i % n_mixers, i // n_mixers
        x = cube(_ffn(flat(x), ffn1_norm[i], ffn1_w_in[i].astype(BF16), ffn1_w_out[i].astype(BF16)))
        if m == 0:
            x = _pool_mixer(x, mix_norm[i], pool_w[j].astype(BF16), pool_b[j], pool_scale[j])
        elif m == 1:
            x = _nsa_mixer(x, mix_norm[i], nsa_w_in[j], nsa_cmp_pos[j], nsa_cmp_w1[j], nsa_cmp_w2[j], nsa_w_o[j])
        elif m == 2:
            x = _gla_mixer(x, mix_norm[i], gla_w_in[j], gla_w_gate_up[j], gla_b_gate[j], gla_norm[j], gla_w_o[j])
        else:
            u = _conv_in(flat(x), mix_norm[i], conv_w_in[j].astype(BF16), conv_b_in[j])
            x = _conv_out(cube(u), x, conv_dw[j], conv_b_dw[j], conv_ln_g[j], conv_ln_b[j],
                          conv_w_out[j].astype(BF16), conv_b_out[j])
        kv = _rms_mm(mem2, mem_norm[i], xattn_w_kv[i].astype(BF16), BF16, tm=n_mem)
        x = _xattn(x, xattn_norm[i], xattn_w_q[i].astype(BF16), kv.reshape(b, n_mem, 2 * d),
                   xattn_w_o[i].astype(BF16))
        final_g = final_norm if i == depth - 1 else None
        x = cube(_ffn(flat(x), ffn2_norm[i], ffn2_w_in[i].astype(BF16), ffn2_w_out[i].astype(BF16),
                      final_g=final_g))
    return x
```

```python
import functools

import numpy as np
import jax
import jax.numpy as jnp
from jax import lax
from jax.experimental import pallas as pl
from jax.experimental.pallas import tpu as pltpu

F32 = jnp.float32
BF16 = jnp.bfloat16

EPS = 1e-6
NEG = -1e30

V7X_VMEM_BYTES = 64 * 1024 * 1024
VMEM_LIMIT = V7X_VMEM_BYTES - 8 * 1024 * 1024
LANES = 128
SUBLANES = 8

POOL_WINDOWS = (2, 4, 8, 16)
POOL_HALO = 16

NSA_HEADS = 16
NSA_KV_HEADS = 4
NSA_REP = NSA_HEADS // NSA_KV_HEADS
NSA_HEAD_DIM = 64
NSA_CMP_BLOCK = 32
NSA_CMP_STRIDE = 16
NSA_SEL_BLOCK = 64
NSA_N_SELECT = 16
NSA_WINDOW = 512
NSA_FORCE = 1e4
NSA_QT = 256
NSA_GATE_ROWS = 16
NSA_SEL_SPAN = 2
LOG2E = 1.4426950408889634

GLA_HEADS = 4
GLA_DK = 128
GLA_DV = 256
GLA_GATE_RANK = 16
GLA_TAU = 16.0
GLA_CHUNK = 64
GLA_ROWS = 256

CONV_WIDTH = 31
CONV_HALO = 32
CONV_STRIP = 128

XATTN_HEADS = 4

ROW_TILE = 512


def _params(sem):
    return pltpu.CompilerParams(dimension_semantics=sem, vmem_limit_bytes=VMEM_LIMIT)


def _rms(x, g):
    return x * lax.rsqrt(jnp.mean(x * x, axis=-1, keepdims=True) + EPS) * g


def _dot(a, b):
    return jnp.dot(a, b, preferred_element_type=F32)


def _dot_nt(a, b):
    return lax.dot_general(a, b, (((1,), (1,)), ((), ())), preferred_element_type=F32)


def _dot_tn(a, b):
    return lax.dot_general(a, b, (((0,), (0,)), ((), ())), preferred_element_type=F32)


def _const_spec(shape):
    nd = len(shape)
    return pl.BlockSpec(shape, lambda *_: (0,) * nd, pipeline_mode=pl.Buffered(1))


def _layer_spec(stack_shape, layer):
    return pl.BlockSpec((None,) + tuple(stack_shape[1:]), lambda *_: (layer, 0, 0), pipeline_mode=pl.Buffered(1))


def _ffn_body(x_ref, g_ref, win_ref, wout_ref, *rest, d_ff, final):
    if final:
        fg_ref, o_ref = rest
    else:
        (o_ref,) = rest
    x = x_ref[...]
    h = _rms(x, g_ref[...]).astype(BF16)
    gu = _dot(h, win_ref[...])
    g = gu[:, :d_ff]
    u = gu[:, d_ff:]
    a = (g * jax.nn.sigmoid(g) * u).astype(BF16)
    out = x + 0.5 * _dot(a, wout_ref[...])
    if final:
        out = _rms(out, fg_ref[...])
    o_ref[...] = out


def _ffn(x2, g, w_in, w_out, layer, final_g=None, tm=ROW_TILE):
    t, d = x2.shape
    d_ff = w_out.shape[1]
    final = final_g is not None
    in_specs = [pl.BlockSpec((tm, d), lambda i: (i, 0)), _const_spec((1, d)),
                _layer_spec(w_in.shape, layer), _layer_spec(w_out.shape, layer)]
    args = [x2, g.reshape(1, d), w_in, w_out]
    if final:
        in_specs.append(_const_spec((1, d)))
        args.append(final_g.reshape(1, d))
    return pl.pallas_call(
        functools.partial(_ffn_body, d_ff=d_ff, final=final),
        out_shape=jax.ShapeDtypeStruct((t, d), F32),
        grid=(t // tm,),
        in_specs=in_specs,
        out_specs=pl.BlockSpec((tm, d), lambda i: (i, 0)),
        compiler_params=_params(("parallel",)),
        name="ffn",
    )(*args)


def _rms_mm_body(x_ref, g_ref, w_ref, o_ref):
    h = _rms(x_ref[...], g_ref[...]).astype(BF16)
    o_ref[...] = _dot(h, w_ref[...]).astype(o_ref.dtype)


def _rms_mm(x2, g, w, layer, out_dtype, tm):
    t, d = x2.shape
    n = w.shape[2]
    return pl.pallas_call(
        _rms_mm_body,
        out_shape=jax.ShapeDtypeStruct((t, n), out_dtype),
        grid=(t // tm,),
        in_specs=[pl.BlockSpec((tm, d), lambda i: (i, 0)), _const_spec((1, d)), _layer_spec(w.shape, layer)],
        out_specs=pl.BlockSpec((tm, n), lambda i: (i, 0)),
        compiler_params=_params(("parallel",)),
        name="rms_mm",
    )(x2, g.reshape(1, d), w)


def _xattn_body(x_ref, g_ref, wq_ref, k_ref, v_ref, wo_ref, o_ref, *, heads):
    x = x_ref[0]
    d = x.shape[-1]
    dh = d // heads
    h = _rms(x, g_ref[...]).astype(BF16)
    q = (_dot(h, wq_ref[...]) * dh ** -0.5).astype(BF16)
    cols = [slice(hd * dh, (hd + 1) * dh) for hd in range(heads)]
    scores = [_dot_nt(q[:, sl], k_ref[0][:, sl]) for sl in cols]
    probs = []
    for s in scores:
        e = jnp.exp(s - jnp.max(s, axis=-1, keepdims=True))
        probs.append((e * (1.0 / jnp.sum(e, axis=-1, keepdims=True))).astype(BF16))
    o = jnp.concatenate([_dot(p, v_ref[0][:, sl]).astype(BF16) for p, sl in zip(probs, cols)], axis=-1)
    o_ref[0] = x + _dot(o, wo_ref[...])


def _xattn(x, g, w_q, kv, w_o, layer, ts=ROW_TILE):
    b, s, d = x.shape
    n = kv.shape[1]
    return pl.pallas_call(
        functools.partial(_xattn_body, heads=XATTN_HEADS),
        out_shape=jax.ShapeDtypeStruct((b, s, d), F32),
        grid=(b, s // ts),
        in_specs=[pl.BlockSpec((1, ts, d), lambda bi, i: (bi, i, 0)), _const_spec((1, d)),
                  _layer_spec(w_q.shape, layer),
                  pl.BlockSpec((1, n, d), lambda bi, i: (bi, 0, 0)),
                  pl.BlockSpec((1, n, d), lambda bi, i: (bi, 0, 1)),
                  _layer_spec(w_o.shape, layer)],
        out_specs=pl.BlockSpec((1, ts, d), lambda bi, i: (bi, i, 0)),
        compiler_params=_params(("parallel", "parallel")),
        name="xattn",
    )(x, g.reshape(1, d), w_q, kv, kv, w_o)


def _pool_body(x_ref, halo_ref, g_ref, w_ref, b_ref, sc_ref, o_ref, hbuf_ref, *, ts):
    i = pl.program_id(1)
    x = x_ref[0]
    d = x.shape[-1]
    gw = d // len(POOL_WINDOWS)
    g = g_ref[...]
    h = _rms(x, g)
    hh = _rms(halo_ref[0], g)
    hbuf_ref[0:POOL_HALO, :] = jnp.where(i > 0, hh, 0.0)
    hbuf_ref[POOL_HALO:, :] = h
    t = i * ts + lax.broadcasted_iota(jnp.int32, (ts, 1), 0)
    ys = []
    for gi, win in enumerate(POOL_WINDOWS):
        cs = slice(gi * gw, (gi + 1) * gw)
        acc = h[:, cs]
        for k in range(1, win):
            acc = acc + hbuf_ref[POOL_HALO - k:POOL_HALO - k + ts, cs]
        cnt = jnp.minimum(t + 1, win).astype(F32)
        p = acc / cnt - h[:, cs]
        ys.append(_dot(p.astype(BF16), w_ref[gi]))
    y = (jnp.concatenate(ys, axis=-1) + b_ref[...]) * sc_ref[...]
    o_ref[0] = x + y


def _pool_mixer(x, g, w, bias, scale, ts=ROW_TILE):
    b, s, d = x.shape
    hb = ts // POOL_HALO
    return pl.pallas_call(
        functools.partial(_pool_body, ts=ts),
        out_shape=jax.ShapeDtypeStruct((b, s, d), F32),
        grid=(b, s // ts),
        in_specs=[pl.BlockSpec((1, ts, d), lambda bi, i: (bi, i, 0)),
                  pl.BlockSpec((1, POOL_HALO, d), lambda bi, i: (bi, jnp.maximum(i * hb - 1, 0), 0)),
                  _const_spec((1, d)), _const_spec(w.shape), _const_spec((1, d)), _const_spec((1, d))],
        out_specs=pl.BlockSpec((1, ts, d), lambda bi, i: (bi, i, 0)),
        scratch_shapes=[pltpu.VMEM((ts + POOL_HALO, d), F32)],
        compiler_params=_params(("parallel", "parallel")),
        name="pool",
    )(x, x, g.reshape(1, d), w, bias.reshape(1, d), scale.reshape(1, d))


def _conv_in_body(x_ref, g_ref, w_ref, b_ref, o_ref):
    d = x_ref.shape[-1]
    h = _rms(x_ref[...], g_ref[...]).astype(BF16)
    ag = _dot(h, w_ref[...]) + b_ref[...]
    o_ref[...] = ag[:, :d] * jax.nn.sigmoid(ag[:, d:])


def _conv_in(x2, g, w, bias, tm=ROW_TILE):
    t, d = x2.shape
    return pl.pallas_call(
        _conv_in_body,
        out_shape=jax.ShapeDtypeStruct((t, d), F32),
        grid=(t // tm,),
        in_specs=[pl.BlockSpec((tm, d), lambda i: (i, 0)), _const_spec((1, d)),
                  _const_spec(w.shape), _const_spec((1, 2 * d))],
        out_specs=pl.BlockSpec((tm, d), lambda i: (i, 0)),
        compiler_params=_params(("parallel",)),
        name="conv_in",
    )(x2, g.reshape(1, d), w, bias.reshape(1, 2 * d))


def _conv_out_body(u_ref, halo_ref, x_ref, dw_ref, bdw_ref, lng_ref, lnb_ref, w_ref, bo_ref, o_ref,
                   ubuf_ref, cbuf_ref, *, ts):
    i = pl.program_id(1)
    n_slab = ubuf_ref.shape[0]
    halo = jnp.where(i > 0, halo_ref[0], 0.0)
    for lb in range(n_slab):
        ls = slice(lb * LANES, (lb + 1) * LANES)
        ubuf_ref[lb, 0:CONV_HALO, :] = halo[:, ls]
        ubuf_ref[lb, CONV_HALO:CONV_HALO + ts, :] = u_ref[0, :, ls]
        ubuf_ref[lb, CONV_HALO + ts:, :] = jnp.zeros((SUBLANES, LANES), F32)
    lead = CONV_HALO - (CONV_WIDTH - 1)

    def slab(lb, carry):
        for r0 in range(0, ts, CONV_STRIP):
            acc = None
            for sh in range(SUBLANES):
                taps = [k for k in range(CONV_WIDTH) if (lead + k) % SUBLANES == sh]
                if not taps:
                    continue
                win = ubuf_ref[lb, pl.ds(r0 + sh, CONV_STRIP + CONV_HALO), :]
                for k in taps:
                    a8 = (lead + k) // SUBLANES * SUBLANES
                    term = win[a8:a8 + CONV_STRIP] * dw_ref[lb, k:k + 1, :]
                    acc = term if acc is None else acc + term
            cbuf_ref[lb, r0:r0 + CONV_STRIP, :] = acc + bdw_ref[lb]
        return carry

    lax.fori_loop(0, n_slab, slab, 0)
    c = jnp.concatenate([cbuf_ref[lb] for lb in range(n_slab)], axis=-1)
    mu = jnp.mean(c, axis=-1, keepdims=True)
    var = jnp.mean(jnp.square(c - mu), axis=-1, keepdims=True)
    n = (c - mu) * lax.rsqrt(var + EPS) * lng_ref[...] + lnb_ref[...]
    a = (n * jax.nn.sigmoid(n)).astype(BF16)
    o_ref[0] = x_ref[0] + _dot(a, w_ref[...]) + bo_ref[...]


def _conv_out(u, x, dw, b_dw, ln_g, ln_b, w_out, b_out, ts=ROW_TILE):
    b, s, d = x.shape
    hb = ts // CONV_HALO
    n_slab = d // LANES
    dw_pad = jnp.zeros((CONV_HALO, d), F32).at[:CONV_WIDTH].set(dw)
    dw_slab = dw_pad.reshape(CONV_HALO, n_slab, LANES).transpose(1, 0, 2)
    bdw_slab = b_dw.reshape(n_slab, 1, LANES)
    row = lambda v: v.reshape(1, d)
    return pl.pallas_call(
        functools.partial(_conv_out_body, ts=ts),
        out_shape=jax.ShapeDtypeStruct((b, s, d), F32),
        grid=(b, s // ts),
        in_specs=[pl.BlockSpec((1, ts, d), lambda bi, i: (bi, i, 0)),
                  pl.BlockSpec((1, CONV_HALO, d), lambda bi, i: (bi, jnp.maximum(i * hb - 1, 0), 0)),
                  pl.BlockSpec((1, ts, d), lambda bi, i: (bi, i, 0)),
                  _const_spec(dw_slab.shape), _const_spec(bdw_slab.shape), _const_spec((1, d)),
                  _const_spec((1, d)), _const_spec(w_out.shape), _const_spec((1, d))],
        out_specs=pl.BlockSpec((1, ts, d), lambda bi, i: (bi, i, 0)),
        scratch_shapes=[pltpu.VMEM((n_slab, ts + CONV_HALO + SUBLANES, LANES), F32),
                        pltpu.VMEM((n_slab, ts, LANES), F32)],
        compiler_params=_params(("parallel", "parallel")),
        name="conv_out",
    )(u, u, x, dw_slab, bdw_slab, row(ln_g), row(ln_b), w_out, row(b_out))


def _gla_in_body(x_ref, g_ref, wqk_ref, wv_ref, wog_ref, wgd_ref, qk_ref, v_ref, og_ref, gd_ref):
    h = _rms(x_ref[...], g_ref[...]).astype(BF16)
    qk_ref[...] = _dot(h, wqk_ref[...])
    v_ref[...] = _dot(h, wv_ref[...]).astype(BF16)
    og_ref[...] = _dot(h, wog_ref[...])
    gd_ref[...] = _dot(h, wgd_ref[...])


def _gla_in(x2, g, w_qk, w_v, w_og, w_gd, tm=ROW_TILE):
    t, d = x2.shape
    widths = (w_qk.shape[1], w_v.shape[1], w_og.shape[1], w_gd.shape[1])
    dtypes = (F32, BF16, F32, F32)
    return pl.pallas_call(
        _gla_in_body,
        out_shape=[jax.ShapeDtypeStruct((t, n), dt) for n, dt in zip(widths, dtypes)],
        grid=(t // tm,),
        in_specs=[pl.BlockSpec((tm, d), lambda i: (i, 0)), _const_spec((1, d)),
                  _const_spec(w_qk.shape), _const_spec(w_v.shape), _const_spec(w_og.shape),
                  _const_spec(w_gd.shape)],
        out_specs=[pl.BlockSpec((tm, n), lambda i: (i, 0)) for n in widths],
        compiler_params=_params(("parallel",)),
        name="gla_in",
    )(x2, g.reshape(1, d), w_qk, w_v, w_og, w_gd)


def _gla_body(q_ref, k_ref, v_ref, og_ref, gd_ref, wup_ref, bg_ref, ng_ref, x_ref, wo_ref, o_ref,
              state_ref, obuf_ref):
    @pl.when(pl.program_id(1) == 0)
    def _():
        state_ref[...] = jnp.zeros_like(state_ref)

    c, dk, dv = GLA_CHUNK, GLA_DK, GLA_DV
    ga = _dot(gd_ref[0].astype(BF16), wup_ref[...]) + bg_ref[...]
    log_a = (jnp.minimum(ga, 0.0) - jnp.log1p(jnp.exp(-jnp.abs(ga)))) / GLA_TAU
    ri = lax.broadcasted_iota(jnp.int32, (c, c), 0)
    ci = lax.broadcasted_iota(jnp.int32, (c, c), 1)
    causal = ci <= ri
    tril = jnp.where(causal, 1.0, 0.0).astype(BF16)
    scale = dk ** -0.5
    n_chunks = GLA_ROWS // c
    chunk_rows = [slice(ch * c, (ch + 1) * c) for ch in range(n_chunks)]
    cases = [(ch, h) for ch in range(n_chunks) for h in range(GLA_HEADS)]
    hi = log_a.astype(BF16)
    r1 = log_a - hi.astype(F32)
    mid = r1.astype(BF16)
    lo = (r1 - mid.astype(F32)).astype(BF16)
    q_t, k_t, k_dec, decay = [], [], [], []
    for rows in chunk_rows:
        bcum = _dot(tril, hi[rows]) + _dot(tril, mid[rows]) + _dot(tril, lo[rows])
        b_last = bcum[c - 1:c, :]
        k = k_ref[0, rows, :]
        q_t.append((q_ref[0, rows, :] * scale * jnp.exp(bcum)).astype(BF16))
        k_t.append((k * jnp.exp(-bcum)).astype(BF16))
        k_dec.append((k * jnp.exp(b_last - bcum)).astype(BF16))
        decay.append(jnp.exp(b_last))
    ks = [slice(h * dk, (h + 1) * dk) for h in range(GLA_HEADS)]
    vs = [slice(h * dv, (h + 1) * dv) for h in range(GLA_HEADS)]
    v = {(ch, h): v_ref[0, chunk_rows[ch], vs[h]] for ch, h in cases}
    a = {(ch, h): jnp.where(causal, _dot_nt(q_t[ch][:, ks[h]], k_t[ch][:, ks[h]]), 0.0).astype(BF16)
         for ch, h in cases}
    kv = {(ch, h): _dot_tn(v[ch, h], k_dec[ch][:, ks[h]]) for ch, h in cases}
    o_intra = {(ch, h): _dot(a[ch, h], v[ch, h]) for ch, h in cases}
    state = [state_ref[h] for h in range(GLA_HEADS)]
    for ch, h in cases:
        o = o_intra[ch, h] + _dot_nt(q_t[ch][:, ks[h]], state[h].astype(BF16))
        state[h] = state[h] * decay[ch][:, ks[h]] + kv[ch, h]
        o = _rms(o, ng_ref[...])
        og = og_ref[0, chunk_rows[ch], vs[h]]
        obuf_ref[chunk_rows[ch], vs[h]] = (o * (og * jax.nn.sigmoid(og))).astype(BF16)
    for h in range(GLA_HEADS):
        state_ref[h] = state[h]
    o_ref[0] = x_ref[0] + _dot(obuf_ref[...], wo_ref[...])


def _gla_core(qk, v, og, gd, w_up, b_gate, norm_g, x, w_o):
    b, s, d = x.shape
    hh, dk, dv, rt = GLA_HEADS, GLA_DK, GLA_DV, GLA_ROWS
    return pl.pallas_call(
        _gla_body,
        out_shape=jax.ShapeDtypeStruct((b, s, d), F32),
        grid=(b, s // rt),
        in_specs=[pl.BlockSpec((1, rt, hh * dk), lambda bi, i: (bi, i, 0)),
                  pl.BlockSpec((1, rt, hh * dk), lambda bi, i: (bi, i, 1)),
                  pl.BlockSpec((1, rt, hh * dv), lambda bi, i: (bi, i, 0)),
                  pl.BlockSpec((1, rt, hh * dv), lambda bi, i: (bi, i, 0)),
                  pl.BlockSpec((1, rt, LANES), lambda bi, i: (bi, i, 0)),
                  _const_spec((LANES, hh * dk)), _const_spec((1, hh * dk)), _const_spec((1, dv)),
                  pl.BlockSpec((1, rt, d), lambda bi, i: (bi, i, 0)), _const_spec(w_o.shape)],
        out_specs=pl.BlockSpec((1, rt, d), lambda bi, i: (bi, i, 0)),
        scratch_shapes=[pltpu.VMEM((hh, dv, dk), F32), pltpu.VMEM((rt, hh * dv), BF16)],
        compiler_params=_params(("parallel", "arbitrary")),
        name="gla_core",
    )(qk, qk, v, og, gd, w_up, b_gate.reshape(1, hh * dk), norm_g.reshape(1, dv), x, w_o)


def _gla_mixer(x, g, w_in, w_gate_up, b_gate, norm_g, w_o):
    b, s, d = x.shape
    qw, vw = GLA_HEADS * GLA_DK, GLA_HEADS * GLA_DV
    x2 = x.reshape(b * s, d)
    w_gd = jnp.zeros((d, LANES), BF16).at[:, :GLA_GATE_RANK].set(w_in[:, 2 * qw + 2 * vw:].astype(BF16))
    w_up = jnp.zeros((LANES, qw), BF16).at[:GLA_GATE_RANK].set(w_gate_up.astype(BF16))
    qk, v, og, gd = _gla_in(x2, g, w_in[:, :2 * qw].astype(BF16), w_in[:, 2 * qw:2 * qw + vw].astype(BF16),
                            w_in[:, 2 * qw + vw:2 * qw + 2 * vw].astype(BF16), w_gd)
    r3 = lambda a: a.reshape(b, s, a.shape[-1])
    return _gla_core(r3(qk), r3(v), r3(og), r3(gd), w_up, b_gate, norm_g, x, w_o.astype(BF16))


def _nsa_in_body(x_ref, g_ref, wqt_ref, wk_ref, wvt_ref, wc_ref, wgt_ref,
                 qt_ref, kas_ref, kaw_ref, vast_ref, vawt_ref, c_ref, gtt_ref, *, ts):
    i = pl.program_id(1)
    qt, groups = NSA_QT, NSA_KV_HEADS
    h = _rms(x_ref[0], g_ref[...]).astype(BF16)
    q_all = (_dot_nt(wqt_ref[...], h) * (NSA_HEAD_DIM ** -0.5 * LOG2E)).astype(BF16)
    hw = NSA_REP * NSA_HEAD_DIM
    for gi in range(groups):
        for tt in range(ts // qt):
            qt_ref[0, gi, tt] = q_all[gi * hw:(gi + 1) * hw, tt * qt:(tt + 1) * qt]
    k = _dot(h, wk_ref[...])
    gw = groups * LANES
    lane = lax.broadcasted_iota(jnp.int32, (ts, LANES), 1)
    t = i * ts + lax.broadcasted_iota(jnp.int32, (ts, LANES), 0)
    onehot = jnp.where(lane - NSA_HEAD_DIM == t // NSA_SEL_BLOCK, 1.0, 0.0)
    for gi in range(groups):
        ls = slice(gi * LANES, (gi + 1) * LANES)
        kas_ref[0, gi] = (k[:, ls] + onehot).astype(BF16)
        kaw_ref[0, gi] = k[:, gw + gi * LANES:gw + (gi + 1) * LANES].astype(BF16)
    vt = _dot_nt(wvt_ref[...], h)
    ones = jnp.where(lax.broadcasted_iota(jnp.int32, (LANES, qt), 0) >= NSA_HEAD_DIM, 1.0, 0.0)
    for gi in range(groups):
        for tt in range(ts // qt):
            cs = slice(tt * qt, (tt + 1) * qt)
            vast_ref[0, gi, tt] = (vt[gi * LANES:(gi + 1) * LANES, cs] + ones).astype(BF16)
            vawt_ref[0, gi, tt] = (vt[gw + gi * LANES:gw + (gi + 1) * LANES, cs] + ones).astype(BF16)
    c = _dot(h, wc_ref[...])
    for gi in range(groups):
        c_ref[0, gi] = c[:, gi * LANES:(gi + 1) * LANES]
    gtt_ref[0] = _dot_nt(wgt_ref[...], h)


def _nsa_in(x, g, w_qt, w_k, w_vt, w_c, w_gt, ts=ROW_TILE):
    b, s, d = x.shape
    groups, qt = NSA_KV_HEADS, NSA_QT
    gw = groups * LANES
    nkt = s // qt
    tpb = ts // qt
    seq_spec = lambda n: pl.BlockSpec((1, ts, n), lambda bi, i: (bi, i, 0))
    rows_spec = lambda n: pl.BlockSpec((1, n, ts), lambda bi, i: (bi, 0, i))
    vt_spec = pl.BlockSpec((1, groups, tpb, LANES, qt), lambda bi, i: (bi, 0, i, 0, 0))
    vt_shape = jax.ShapeDtypeStruct((b, groups, nkt, LANES, qt), BF16)
    k_spec = pl.BlockSpec((1, groups, ts, LANES), lambda bi, i: (bi, 0, i, 0))
    k_shape = jax.ShapeDtypeStruct((b, groups, s, LANES), BF16)
    return pl.pallas_call(
        functools.partial(_nsa_in_body, ts=ts),
        out_shape=[jax.ShapeDtypeStruct((b, groups, nkt, w_qt.shape[0] // groups, qt), BF16),
                   k_shape, k_shape,
                   vt_shape, vt_shape,
                   jax.ShapeDtypeStruct((b, groups, s, LANES), F32),
                   jax.ShapeDtypeStruct((b, w_gt.shape[0], s), F32)],
        grid=(b, s // ts),
        in_specs=[pl.BlockSpec((1, ts, d), lambda bi, i: (bi, i, 0)), _const_spec((1, d)),
                  _const_spec(w_qt.shape), _const_spec(w_k.shape), _const_spec(w_vt.shape),
                  _const_spec(w_c.shape), _const_spec(w_gt.shape)],
        out_specs=[pl.BlockSpec((1, groups, tpb, w_qt.shape[0] // groups, qt), lambda bi, i: (bi, 0, i, 0, 0)),
                   k_spec, k_spec, vt_spec, vt_spec,
                   k_spec, rows_spec(w_gt.shape[0])],
        compiler_params=_params(("parallel", "parallel")),
        name="nsa_in",
    )(x, g.reshape(1, d), w_qt, w_k, w_vt, w_c, w_gt)


def _nsa_cmp_body(c_ref, pos_ref, w1t_ref, w1b_ref, w2k_ref, w2vt_ref, kc_ref, vct_ref):
    stride = NSA_CMP_STRIDE
    nch = c_ref.shape[2] // stride
    u = v = None
    for p in range(stride):
        x = c_ref[0, 0, pl.ds(p, nch, stride=stride), :]
        up = _dot((x + pos_ref[p:p + 1, :]).astype(BF16), w1t_ref[p])
        vp = _dot((x + pos_ref[stride + p:stride + p + 1, :]).astype(BF16), w1b_ref[p])
        u = up if u is None else u + up
        v = vp if v is None else v + vp
    hid = jax.nn.gelu(u + pltpu.roll(v, nch - 1, 0), approximate=True).astype(BF16)
    hw = hid.shape[-1] // 2
    kc_ref[0, 0] = _dot(hid[:, :hw], w2k_ref[...])
    vct_ref[0, 0] = _dot_nt(w2vt_ref[...], hid[:, hw:])


def _nsa_compress(c, cmp_pos, w1, w2):
    b, g, s, _ = c.shape
    dh, hidden = w2.shape[-1], w2.shape[1]
    stride = NSA_CMP_STRIDE
    nch = s // stride
    pos = jnp.concatenate([cmp_pos[0], cmp_pos[1]], axis=-1)
    w1r = w1.astype(BF16).reshape(2, NSA_CMP_BLOCK, dh, hidden)
    zero = jnp.zeros((NSA_CMP_BLOCK, dh, hidden), BF16)
    w1bd = jnp.concatenate([jnp.concatenate([w1r[0], zero], axis=-1),
                            jnp.concatenate([zero, w1r[1]], axis=-1)], axis=1)
    return pl.pallas_call(
        _nsa_cmp_body,
        out_shape=[jax.ShapeDtypeStruct((b, g, nch, dh), F32), jax.ShapeDtypeStruct((b, g, dh, nch), F32)],
        grid=(b, g),
        in_specs=[pl.BlockSpec((1, 1, s, 2 * dh), lambda bi, gi: (bi, gi, 0, 0)),
                  _const_spec(pos.shape), _const_spec((stride, 2 * dh, 2 * hidden)),
                  _const_spec((stride, 2 * dh, 2 * hidden)), _const_spec((hidden, dh)), _const_spec((dh, hidden))],
        out_specs=[pl.BlockSpec((1, 1, nch, dh), lambda bi, gi: (bi, gi, 0, 0)),
                   pl.BlockSpec((1, 1, dh, nch), lambda bi, gi: (bi, gi, 0, 0))],
        compiler_params=_params(("parallel", "parallel")),
        name="nsa_compress",
    )(c, pos, w1bd[:stride], w1bd[stride:], w2[0].astype(BF16), w2[1].astype(BF16).T)


def _nsa_attn_body(qt_ref, kc_ref, vct_ref, kas_ref, vast_ref, kaw_ref, vawt_ref, gtt_ref, ovt_ref, o_ref,
                   qaug_ref, m_ref, acc_ref, p_ref, alpha_ref, s_ref, *, n_slc):
    qt, rep, dh = NSA_QT, NSA_REP, NSA_HEAD_DIM
    cols = rep * qt
    qi = pl.program_id(2)
    s0 = qi * qt
    q = jnp.concatenate([qt_ref[0, 0, 0, r * dh:(r + 1) * dh, :] for r in range(rep)], axis=1)
    t_col = s0 + (lax.broadcasted_iota(jnp.int32, (1, cols), 1) & (qt - 1))

    n_pad = kc_ref.shape[2]
    sc = _dot(kc_ref[0, 0].astype(BF16), q)
    cmp_end = lax.broadcasted_iota(jnp.int32, (n_pad, 1), 0) * NSA_CMP_STRIDE + (NSA_CMP_BLOCK - 1)
    sc = jnp.where(cmp_end <= t_col, sc, NEG)
    e = jnp.exp2(sc - jnp.max(sc, axis=0, keepdims=True))
    p_cmp = e * (1.0 / jnp.sum(e, axis=0, keepdims=True))
    p_cmp = jnp.where(t_col >= NSA_CMP_BLOCK - 1, p_cmp, 0.0).astype(BF16)
    o_cmp = _dot(vct_ref[0, 0].astype(BF16), p_cmp)

    imp = _dot(ovt_ref[...], p_cmp[:, 0:qt])
    for r in range(1, rep):
        imp = imp + _dot(ovt_ref[...], p_cmp[:, r * qt:(r + 1) * qt])
    imp_t = imp[:n_slc]
    blk = lax.broadcasted_iota(jnp.int32, (n_slc, 1), 0)
    t_q = s0 + lax.broadcasted_iota(jnp.int32, (1, qt), 1)
    cur = t_q // NSA_SEL_BLOCK
    forced = (blk == 0) | (blk == cur) | (blk == cur - 1)
    visible = blk * NSA_SEL_BLOCK <= t_q
    imp_t = jnp.where(forced, NSA_FORCE, imp_t)
    imp_t = jnp.where(visible, imp_t, -1.0)
    groups = [imp_t[k:k + SUBLANES] for k in range(0, n_slc, SUBLANES)]
    ranks = [jnp.zeros((SUBLANES, qt), F32) for _ in groups]
    sub = lax.broadcasted_iota(jnp.int32, (SUBLANES, 1), 0)
    for i in range(n_slc):
        gi, si = divmod(i, SUBLANES)
        row = groups[gi][si:si + 1, :]
        for k, blk_imp in enumerate(groups):
            if k < gi:
                inc = jnp.where(row > blk_imp, 1.0, 0.0)
            elif k > gi:
                inc = jnp.where(row >= blk_imp, 1.0, 0.0)
            else:
                inc = jnp.where(sub > si, jnp.where(row >= blk_imp, 1.0, 0.0), jnp.where(row > blk_imp, 1.0, 0.0))
            ranks[k] = ranks[k] + inc
    rank = jnp.concatenate(ranks, axis=0)
    sel = (rank < float(min(NSA_N_SELECT, n_slc))) & visible
    bias_parts = [jnp.where(sel, 0.0, NEG)]
    if LANES - dh - n_slc:
        bias_parts.append(jnp.zeros((LANES - dh - n_slc, qt), F32))
    bias = jnp.concatenate(bias_parts, axis=0).astype(BF16)
    qaug_ref[0:dh, :] = q
    for r in range(rep):
        qaug_ref[dh:, r * qt:(r + 1) * qt] = bias

    heads = [slice(r * qt, (r + 1) * qt) for r in range(rep)]

    def score(ka_ref, j, nt):
        start = j * (nt * qt)
        if not isinstance(start, int):
            start = pl.multiple_of(start, nt * qt)
        ka = ka_ref[0, 0, pl.ds(start, nt * qt), :]
        return [_dot(ka, qaug_ref[:, cs]) for cs in heads]

    def flush(vat_ref, jp, nt):
        pvs = []
        for cs in heads:
            pv = _dot(vat_ref[0, 0, jp * nt], p_ref[0:qt, cs])
            for u in range(1, nt):
                pv = pv + _dot(vat_ref[0, 0, jp * nt + u], p_ref[u * qt:(u + 1) * qt, cs])
            pvs.append(pv)
        return pvs

    def accumulate(pvs):
        for cs, pv in zip(heads, pvs):
            acc_ref[:, cs] = acc_ref[:, cs] * alpha_ref[:, cs] + pv

    def step(ka_ref, vat_ref, j, nt, mask_fn, has_next):
        rows = nt * qt
        nxt = score(ka_ref, j + 1, nt) if has_next else None
        pvs = flush(vat_ref, jnp.maximum(j - 1, 0), nt)
        probs, alphas = [], []
        for cs in heads:
            s = s_ref[0:rows, cs]
            if mask_fn is not None:
                kpos = j * rows + lax.broadcasted_iota(jnp.int32, (rows, 1), 0)
                s = jnp.where(mask_fn(kpos, t_col[:, cs]), s, NEG)
            m_old = m_ref[:, cs]
            m_new = jnp.maximum(m_old, jnp.max(s, axis=0, keepdims=True))
            probs.append(jnp.exp2(s - m_new).astype(BF16))
            alphas.append(jnp.exp2(m_old - m_new))
            m_ref[:, cs] = m_new
        accumulate(pvs)
        for cs, p, alpha in zip(heads, probs, alphas):
            p_ref[0:rows, cs] = p
            alpha_ref[:, cs] = alpha
        if has_next:
            for cs, s in zip(heads, nxt):
                s_ref[0:rows, cs] = s

    def attend(ka_ref, vat_ref, nt, lo, hi, mask_fn, last_mask_fn):
        rows = nt * qt
        m_ref[...] = jnp.full_like(m_ref, NEG)
        acc_ref[...] = jnp.zeros_like(acc_ref)
        p_ref[0:rows, :] = jnp.zeros((rows, cols), BF16)
        alpha_ref[...] = jnp.ones_like(alpha_ref)
        for cs, s in zip(heads, score(ka_ref, lo, nt)):
            s_ref[0:rows, cs] = s

        def body(j, carry):
            step(ka_ref, vat_ref, j, nt, mask_fn, True)
            return carry

        lax.fori_loop(lo, hi, body, 0)
        step(ka_ref, vat_ref, hi, nt, last_mask_fn, False)
        accumulate(flush(vat_ref, hi, nt))
        acc = acc_ref[...]
        return acc[0:dh] / acc[dh:dh + 1]

    o_slc = attend(kas_ref, vast_ref, NSA_SEL_SPAN, 0, qi // NSA_SEL_SPAN, None, lambda kpos, t: kpos <= t)

    wt = NSA_WINDOW // qt + 1
    lo = jnp.clip(qi - (wt - 1), 0, kaw_ref.shape[2] // qt - wt)
    kw = kaw_ref[0, 0, pl.ds(pl.multiple_of(lo * qt, qt), wt * qt), :]
    kpos = lo * qt + lax.broadcasted_iota(jnp.int32, (wt * qt, 1), 0)
    t_q = t_col[:, 0:qt]
    wbias = jnp.where((kpos <= t_q) & (kpos > t_q - NSA_WINDOW), 0.0, NEG)
    scores = [_dot(kw, qaug_ref[:, cs]) for cs in heads]
    probs = []
    for s in scores:
        s = s + wbias
        probs.append(jnp.exp2(s - jnp.max(s, axis=0, keepdims=True)).astype(BF16))
    o_win = []
    for p in probs:
        pv = _dot(vawt_ref[0, 0, lo], p[0:qt])
        for u in range(1, wt):
            pv = pv + _dot(vawt_ref[0, 0, lo + u], p[u * qt:(u + 1) * qt])
        o_win.append(pv[0:dh] / pv[dh:dh + 1])
    o_win = jnp.concatenate(o_win, axis=1)

    gates = jax.nn.sigmoid(gtt_ref[0])
    for r in range(rep):
        cs = slice(r * qt, (r + 1) * qt)
        o = (gates[3 * r:3 * r + 1] * o_cmp[:, cs] + gates[3 * r + 1:3 * r + 2] * o_slc[:, cs]
             + gates[3 * r + 2:3 * r + 3] * o_win[:, cs])
        o_ref[0, 0, 0, r * dh:(r + 1) * dh, :] = o.astype(BF16)


def _nsa_attn(q_t, kcmp, vcmp_t, kas, vas_t, kaw, vaw_t, gates_t, ov_t):
    b, g, nkt, _, qt = q_t.shape
    rep, dh = NSA_REP, NSA_HEAD_DIM
    s = nkt * qt
    n_slc = s // NSA_SEL_BLOCK
    nch = kcmp.shape[2]
    k_spec = pl.BlockSpec((1, 1, s, LANES), lambda bi, gi, i: (bi, gi, 0, 0))
    vt_spec = pl.BlockSpec((1, 1, nkt, LANES, qt), lambda bi, gi, i: (bi, gi, 0, 0, 0))
    q_spec = pl.BlockSpec((1, 1, 1, rep * dh, qt), lambda bi, gi, i: (bi, gi, i, 0, 0))
    return pl.pallas_call(
        functools.partial(_nsa_attn_body, n_slc=n_slc),
        out_shape=jax.ShapeDtypeStruct(q_t.shape, BF16),
        grid=(b, g, s // qt),
        in_specs=[q_spec,
                  pl.BlockSpec((1, 1, nch, dh), lambda bi, gi, i: (bi, gi, 0, 0)),
                  pl.BlockSpec((1, 1, dh, nch), lambda bi, gi, i: (bi, gi, 0, 0)),
                  k_spec, vt_spec, k_spec, vt_spec,
                  pl.BlockSpec((1, NSA_GATE_ROWS, qt), lambda bi, gi, i: (bi, gi, i)),
                  _const_spec(ov_t.shape)],
        out_specs=q_spec,
        scratch_shapes=[pltpu.VMEM((LANES, rep * qt), BF16), pltpu.VMEM((1, rep * qt), F32),
                        pltpu.VMEM((LANES, rep * qt), F32), pltpu.VMEM((NSA_SEL_SPAN * qt, rep * qt), BF16),
                        pltpu.VMEM((1, rep * qt), F32), pltpu.VMEM((NSA_SEL_SPAN * qt, rep * qt), F32)],
        compiler_params=_params(("parallel", "parallel", "parallel")),
        name="nsa_attn",
    )(q_t, kcmp, vcmp_t, kas, vas_t, kaw, vaw_t, gates_t, ov_t)


def _nsa_overlap_t(s):
    n_cmp = (s - NSA_CMP_BLOCK) // NSA_CMP_STRIDE + 1
    n_slc = s // NSA_SEL_BLOCK
    cs = np.arange(n_cmp) * NSA_CMP_STRIDE
    ss = np.arange(n_slc) * NSA_SEL_BLOCK
    ov = np.clip(np.minimum(cs[:, None] + NSA_CMP_BLOCK, ss[None, :] + NSA_SEL_BLOCK)
                 - np.maximum(cs[:, None], ss[None, :]), 0, None) / NSA_CMP_BLOCK
    out = np.zeros((LANES, s // NSA_CMP_STRIDE), np.float32)
    out[:n_slc, :n_cmp] = ov.T
    return jnp.asarray(out, BF16)


def _mm_res_t_body(at_ref, w_ref, x_ref, o_ref):
    groups, tiles, kw, qt = at_ref.shape[1:]
    for tt in range(tiles):
        rows = slice(tt * qt, (tt + 1) * qt)
        y = x_ref[0, rows, :]
        for gi in range(groups):
            y = y + _dot_tn(at_ref[0, gi, tt], w_ref[gi * kw:(gi + 1) * kw, :])
        o_ref[0, rows, :] = y


def _mm_res_t(a_t, w, x, ts=ROW_TILE):
    b, groups, nkt, kw, qt = a_t.shape
    s, n = nkt * qt, w.shape[1]
    tpb = ts // qt
    return pl.pallas_call(
        _mm_res_t_body,
        out_shape=jax.ShapeDtypeStruct((b, s, n), F32),
        grid=(b, s // ts),
        in_specs=[pl.BlockSpec((1, groups, tpb, kw, qt), lambda bi, i: (bi, 0, i, 0, 0)),
                  _const_spec(w.shape), pl.BlockSpec((1, ts, n), lambda bi, i: (bi, i, 0))],
        out_specs=pl.BlockSpec((1, ts, n), lambda bi, i: (bi, i, 0)),
        compiler_params=_params(("parallel", "parallel")),
        name="mm_res_t",
    )(a_t, w, x)


def _nsa_mixer(x, g, w_in, cmp_pos, cmp_w1, cmp_w2, w_o):
    b, s, d = x.shape
    hh, gg, rep, dh = NSA_HEADS, NSA_KV_HEADS, NSA_REP, NSA_HEAD_DIM
    kvw = gg * dh
    assert s % (NSA_SEL_SPAN * NSA_QT) == 0 and s % ROW_TILE == 0
    assert s // NSA_SEL_BLOCK <= LANES - dh and (s // NSA_SEL_BLOCK) % SUBLANES == 0
    wb = w_in.astype(BF16)
    parts = [wb[:, d + i * kvw:d + (i + 1) * kvw].reshape(d, gg, dh) for i in range(6)]
    zeros = jnp.zeros((d, gg, LANES - dh), BF16)
    padded = lambda p: jnp.concatenate([p, zeros], axis=-1).reshape(d, gg * LANES)
    w_k = jnp.concatenate([padded(parts[2]), padded(parts[4])], axis=-1)
    w_vt = jnp.concatenate([padded(parts[3]), padded(parts[5])], axis=-1).T
    w_c = jnp.concatenate([parts[0], parts[1]], axis=-1).reshape(d, gg * 2 * dh)
    w_g = wb[:, d + 6 * kvw:].reshape(d, gg, 3 * rep)
    w_gt = jnp.concatenate([w_g, jnp.zeros((d, gg, NSA_GATE_ROWS - 3 * rep), BF16)], axis=-1)
    w_gt = w_gt.reshape(d, gg * NSA_GATE_ROWS).T
    q_t, kas, kaw, vas_t, vaw_t, c, gates_t = _nsa_in(x, g, wb[:, :d].T, w_k, w_vt, w_c, w_gt)

    k_cmp, v_cmp_t = _nsa_compress(c, cmp_pos, cmp_w1, cmp_w2)
    o_t = _nsa_attn(q_t, k_cmp, v_cmp_t, kas, vas_t, kaw, vaw_t, gates_t, _nsa_overlap_t(s))
    return _mm_res_t(o_t, w_o.astype(BF16), x)


def kernel(x, mem, ffn1_norm, ffn1_w_in, ffn1_w_out, mix_norm, xattn_norm, mem_norm, xattn_w_q, xattn_w_kv, xattn_w_o, ffn2_norm, ffn2_w_in, ffn2_w_out, pool_w, pool_b, pool_scale, nsa_w_in, nsa_cmp_pos, nsa_cmp_w1, nsa_cmp_w2, nsa_w_o, gla_w_in, gla_w_gate_up, gla_b_gate, gla_norm, gla_w_o, conv_w_in, conv_b_in, conv_dw, conv_b_dw, conv_ln_g, conv_ln_b, conv_w_out, conv_b_out, final_norm):
    b, s, d = x.shape
    n_mem = mem.shape[1]
    depth = ffn1_norm.shape[0]
    n_mixers = 4
    flat = lambda a: a.reshape(b * s, d)
    cube = lambda a: a.reshape(b, s, d)
    mem2 = mem.reshape(b * n_mem, d)
    ffn1_w_in, ffn1_w_out, ffn2_w_in, ffn2_w_out, xattn_w_q, xattn_w_kv, xattn_w_o = (
        w.astype(BF16) for w in (ffn1_w_in, ffn1_w_out, ffn2_w_in, ffn2_w_out, xattn_w_q, xattn_w_kv, xattn_w_o))
    for i in range(depth):
        m, j = i % n_mixers, i // n_mixers
        x = cube(_ffn(flat(x), ffn1_norm[i], ffn1_w_in, ffn1_w_out, i))
        if m == 0:
            x = _pool_mixer(x, mix_norm[i], pool_w[j].astype(BF16), pool_b[j], pool_scale[j])
        elif m == 1:
            x = _nsa_mixer(x, mix_norm[i], nsa_w_in[j], nsa_cmp_pos[j], nsa_cmp_w1[j], nsa_cmp_w2[j], nsa_w_o[j])
        elif m == 2:
            x = _gla_mixer(x, mix_norm[i], gla_w_in[j], gla_w_gate_up[j], gla_b_gate[j], gla_norm[j], gla_w_o[j])
        else:
            u = _conv_in(flat(x), mix_norm[i], conv_w_in[j].astype(BF16), conv_b_in[j])
            x = _conv_out(cube(u), x, conv_dw[j], conv_b_dw[j], conv_ln_g[j], conv_ln_b[j],
                          conv_w_out[j].astype(BF16), conv_b_out[j])
        kv = _rms_mm(mem2, mem_norm[i], xattn_w_kv, i, BF16, tm=n_mem)
        x = _xattn(x, xattn_norm[i], xattn_w_q, kv.reshape(b, n_mem, 2 * d), xattn_w_o, i)
        final_g = final_norm if i == depth - 1 else None
        x = cube(_ffn(flat(x), ffn2_norm[i], ffn2_w_in, ffn2_w_out, i, final_g=final_g))
    return x
```

```python
import functools

import numpy as np
import jax
import jax.numpy as jnp
from jax import lax
from jax.experimental import pallas as pl
from jax.experimental.pallas import tpu as pltpu

F32 = jnp.float32
BF16 = jnp.bfloat16

EPS = 1e-6
NEG = -1e30

V7X_VMEM_BYTES = 64 * 1024 * 1024
VMEM_LIMIT = V7X_VMEM_BYTES - 8 * 1024 * 1024
LANES = 128
SUBLANES = 8

POOL_WINDOWS = (2, 4, 8, 16)
POOL_HALO = 16

NSA_HEADS = 16
NSA_KV_HEADS = 4
NSA_REP = NSA_HEADS // NSA_KV_HEADS
NSA_HEAD_DIM = 64
NSA_CMP_BLOCK = 32
NSA_CMP_STRIDE = 16
NSA_SEL_BLOCK = 64
NSA_N_SELECT = 16
NSA_WINDOW = 512
NSA_FORCE = 1e4
NSA_QT = 256
NSA_GATE_ROWS = 16
NSA_SEL_SPAN = 2
LOG2E = 1.4426950408889634

GLA_HEADS = 4
GLA_DK = 128
GLA_DV = 256
GLA_GATE_RANK = 16
GLA_TAU = 16.0
GLA_CHUNK = 64
GLA_ROWS = 512

CONV_WIDTH = 31
CONV_HALO = 32
CONV_STRIP = 128

XATTN_HEADS = 4

ROW_TILE = 512
FFN_ROW_TILE = 512
FFN_CHUNKS = 1


def _params(sem):
    return pltpu.CompilerParams(dimension_semantics=sem, vmem_limit_bytes=VMEM_LIMIT)


def _rms(x, g):
    return x * lax.rsqrt(jnp.mean(x * x, axis=-1, keepdims=True) + EPS) * g


def _dot(a, b):
    return jnp.dot(a, b, preferred_element_type=F32)


def _dot_nt(a, b):
    return lax.dot_general(a, b, (((1,), (1,)), ((), ())), preferred_element_type=F32)


def _dot_tn(a, b):
    return lax.dot_general(a, b, (((0,), (0,)), ((), ())), preferred_element_type=F32)


def _const_spec(shape):
    nd = len(shape)
    return pl.BlockSpec(shape, lambda *_: (0,) * nd, pipeline_mode=pl.Buffered(1))


def _layer_spec(stack_shape, layer):
    return pl.BlockSpec((None,) + tuple(stack_shape[1:]), lambda *_: (layer, 0, 0), pipeline_mode=pl.Buffered(1))


def _ffn_body(x_ref, g_ref, win_ref, wout_ref, *rest, d_ff, final):
    if final:
        fg_ref, o_ref = rest
    else:
        (o_ref,) = rest
    x = x_ref[...]
    h = _rms(x, g_ref[...]).astype(BF16)
    fc = d_ff // FFN_CHUNKS
    y = None
    for c in range(FFN_CHUNKS):
        g = _dot(h, win_ref[:, c * fc:(c + 1) * fc])
        u = _dot(h, win_ref[:, d_ff + c * fc:d_ff + (c + 1) * fc])
        a = (g * jax.nn.sigmoid(g) * u).astype(BF16)
        yc = _dot(a, wout_ref[c * fc:(c + 1) * fc, :])
        y = yc if y is None else y + yc
    out = x + 0.5 * y
    if final:
        out = _rms(out, fg_ref[...])
    o_ref[...] = out


def _ffn(x2, g, w_in, w_out, layer, final_g=None, tm=FFN_ROW_TILE):
    t, d = x2.shape
    d_ff = w_out.shape[1]
    final = final_g is not None
    in_specs = [pl.BlockSpec((tm, d), lambda i: (i, 0)), _const_spec((1, d)),
                _layer_spec(w_in.shape, layer), _layer_spec(w_out.shape, layer)]
    args = [x2, g.reshape(1, d), w_in, w_out]
    if final:
        in_specs.append(_const_spec((1, d)))
        args.append(final_g.reshape(1, d))
    return pl.pallas_call(
        functools.partial(_ffn_body, d_ff=d_ff, final=final),
        out_shape=jax.ShapeDtypeStruct((t, d), F32),
        grid=(t // tm,),
        in_specs=in_specs,
        out_specs=pl.BlockSpec((tm, d), lambda i: (i, 0)),
        compiler_params=_params(("parallel",)),
        name="ffn",
    )(*args)


def _rms_mm_body(x_ref, g_ref, w_ref, o_ref):
    h = _rms(x_ref[...], g_ref[...]).astype(BF16)
    o_ref[...] = _dot(h, w_ref[...]).astype(o_ref.dtype)


def _rms_mm(x2, g, w, layer, out_dtype, tm):
    t, d = x2.shape
    n = w.shape[2]
    return pl.pallas_call(
        _rms_mm_body,
        out_shape=jax.ShapeDtypeStruct((t, n), out_dtype),
        grid=(t // tm,),
        in_specs=[pl.BlockSpec((tm, d), lambda i: (i, 0)), _const_spec((1, d)), _layer_spec(w.shape, layer)],
        out_specs=pl.BlockSpec((tm, n), lambda i: (i, 0)),
        compiler_params=_params(("parallel",)),
        name="rms_mm",
    )(x2, g.reshape(1, d), w)


def _xattn_body(x_ref, g_ref, wq_ref, k_ref, v_ref, wo_ref, o_ref, *, heads):
    x = x_ref[0]
    d = x.shape[-1]
    dh = d // heads
    h = _rms(x, g_ref[...]).astype(BF16)
    q = (_dot(h, wq_ref[...]) * dh ** -0.5).astype(BF16)
    cols = [slice(hd * dh, (hd + 1) * dh) for hd in range(heads)]
    scores = [_dot_nt(q[:, sl], k_ref[0][:, sl]) for sl in cols]
    probs = []
    for s in scores:
        e = jnp.exp(s - jnp.max(s, axis=-1, keepdims=True))
        probs.append((e * (1.0 / jnp.sum(e, axis=-1, keepdims=True))).astype(BF16))
    o = jnp.concatenate([_dot(p, v_ref[0][:, sl]).astype(BF16) for p, sl in zip(probs, cols)], axis=-1)
    o_ref[0] = x + _dot(o, wo_ref[...])


def _xattn(x, g, w_q, kv, w_o, layer, ts=ROW_TILE):
    b, s, d = x.shape
    n = kv.shape[1]
    return pl.pallas_call(
        functools.partial(_xattn_body, heads=XATTN_HEADS),
        out_shape=jax.ShapeDtypeStruct((b, s, d), F32),
        grid=(b, s // ts),
        in_specs=[pl.BlockSpec((1, ts, d), lambda bi, i: (bi, i, 0)), _const_spec((1, d)),
                  _layer_spec(w_q.shape, layer),
                  pl.BlockSpec((1, n, d), lambda bi, i: (bi, 0, 0)),
                  pl.BlockSpec((1, n, d), lambda bi, i: (bi, 0, 1)),
                  _layer_spec(w_o.shape, layer)],
        out_specs=pl.BlockSpec((1, ts, d), lambda bi, i: (bi, i, 0)),
        compiler_params=_params(("parallel", "parallel")),
        name="xattn",
    )(x, g.reshape(1, d), w_q, kv, kv, w_o)


def _pool_body(x_ref, halo_ref, g_ref, w_ref, b_ref, sc_ref, o_ref, hbuf_ref, *, ts):
    i = pl.program_id(1)
    x = x_ref[0]
    d = x.shape[-1]
    gw = d // len(POOL_WINDOWS)
    g = g_ref[...]
    h = _rms(x, g)
    hh = _rms(halo_ref[0], g)
    hbuf_ref[0:POOL_HALO, :] = jnp.where(i > 0, hh, 0.0)
    hbuf_ref[POOL_HALO:, :] = h
    t = i * ts + lax.broadcasted_iota(jnp.int32, (ts, 1), 0)
    ys = []
    for gi, win in enumerate(POOL_WINDOWS):
        cs = slice(gi * gw, (gi + 1) * gw)
        acc = h[:, cs]
        for k in range(1, win):
            acc = acc + hbuf_ref[POOL_HALO - k:POOL_HALO - k + ts, cs]
        cnt = jnp.minimum(t + 1, win).astype(F32)
        p = acc / cnt - h[:, cs]
        ys.append(_dot(p.astype(BF16), w_ref[gi]))
    y = (jnp.concatenate(ys, axis=-1) + b_ref[...]) * sc_ref[...]
    o_ref[0] = x + y


def _pool_mixer(x, g, w, bias, scale, ts=ROW_TILE):
    b, s, d = x.shape
    hb = ts // POOL_HALO
    return pl.pallas_call(
        functools.partial(_pool_body, ts=ts),
        out_shape=jax.ShapeDtypeStruct((b, s, d), F32),
        grid=(b, s // ts),
        in_specs=[pl.BlockSpec((1, ts, d), lambda bi, i: (bi, i, 0)),
                  pl.BlockSpec((1, POOL_HALO, d), lambda bi, i: (bi, jnp.maximum(i * hb - 1, 0), 0)),
                  _const_spec((1, d)), _const_spec(w.shape), _const_spec((1, d)), _const_spec((1, d))],
        out_specs=pl.BlockSpec((1, ts, d), lambda bi, i: (bi, i, 0)),
        scratch_shapes=[pltpu.VMEM((ts + POOL_HALO, d), F32)],
        compiler_params=_params(("parallel", "parallel")),
        name="pool",
    )(x, x, g.reshape(1, d), w, bias.reshape(1, d), scale.reshape(1, d))


def _conv_in_body(x_ref, g_ref, w_ref, b_ref, o_ref):
    d = x_ref.shape[-1]
    h = _rms(x_ref[...], g_ref[...]).astype(BF16)
    ag = _dot(h, w_ref[...]) + b_ref[...]
    o_ref[...] = ag[:, :d] * jax.nn.sigmoid(ag[:, d:])


def _conv_in(x2, g, w, bias, tm=ROW_TILE):
    t, d = x2.shape
    return pl.pallas_call(
        _conv_in_body,
        out_shape=jax.ShapeDtypeStruct((t, d), F32),
        grid=(t // tm,),
        in_specs=[pl.BlockSpec((tm, d), lambda i: (i, 0)), _const_spec((1, d)),
                  _const_spec(w.shape), _const_spec((1, 2 * d))],
        out_specs=pl.BlockSpec((tm, d), lambda i: (i, 0)),
        compiler_params=_params(("parallel",)),
        name="conv_in",
    )(x2, g.reshape(1, d), w, bias.reshape(1, 2 * d))


def _conv_out_body(u_ref, halo_ref, x_ref, dw_ref, bdw_ref, lng_ref, lnb_ref, w_ref, bo_ref, o_ref,
                   ubuf_ref, cbuf_ref, *, ts):
    i = pl.program_id(1)
    n_slab = ubuf_ref.shape[0]
    halo = jnp.where(i > 0, halo_ref[0], 0.0)
    for lb in range(n_slab):
        ls = slice(lb * LANES, (lb + 1) * LANES)
        ubuf_ref[lb, 0:CONV_HALO, :] = halo[:, ls]
        ubuf_ref[lb, CONV_HALO:CONV_HALO + ts, :] = u_ref[0, :, ls]
        ubuf_ref[lb, CONV_HALO + ts:, :] = jnp.zeros((SUBLANES, LANES), F32)
    lead = CONV_HALO - (CONV_WIDTH - 1)

    def slab(lb, carry):
        for r0 in range(0, ts, CONV_STRIP):
            acc = None
            for sh in range(SUBLANES):
                taps = [k for k in range(CONV_WIDTH) if (lead + k) % SUBLANES == sh]
                if not taps:
                    continue
                win = ubuf_ref[lb, pl.ds(r0 + sh, CONV_STRIP + CONV_HALO), :]
                for k in taps:
                    a8 = (lead + k) // SUBLANES * SUBLANES
                    term = win[a8:a8 + CONV_STRIP] * dw_ref[lb, k:k + 1, :]
                    acc = term if acc is None else acc + term
            cbuf_ref[lb, r0:r0 + CONV_STRIP, :] = acc + bdw_ref[lb]
        return carry

    lax.fori_loop(0, n_slab, slab, 0)
    c = jnp.concatenate([cbuf_ref[lb] for lb in range(n_slab)], axis=-1)
    mu = jnp.mean(c, axis=-1, keepdims=True)
    var = jnp.mean(jnp.square(c - mu), axis=-1, keepdims=True)
    n = (c - mu) * lax.rsqrt(var + EPS) * lng_ref[...] + lnb_ref[...]
    a = (n * jax.nn.sigmoid(n)).astype(BF16)
    o_ref[0] = x_ref[0] + _dot(a, w_ref[...]) + bo_ref[...]


def _conv_out(u, x, dw, b_dw, ln_g, ln_b, w_out, b_out, ts=ROW_TILE):
    b, s, d = x.shape
    hb = ts // CONV_HALO
    n_slab = d // LANES
    dw_pad = jnp.zeros((CONV_HALO, d), F32).at[:CONV_WIDTH].set(dw)
    dw_slab = dw_pad.reshape(CONV_HALO, n_slab, LANES).transpose(1, 0, 2)
    bdw_slab = b_dw.reshape(n_slab, 1, LANES)
    row = lambda v: v.reshape(1, d)
    return pl.pallas_call(
        functools.partial(_conv_out_body, ts=ts),
        out_shape=jax.ShapeDtypeStruct((b, s, d), F32),
        grid=(b, s // ts),
        in_specs=[pl.BlockSpec((1, ts, d), lambda bi, i: (bi, i, 0)),
                  pl.BlockSpec((1, CONV_HALO, d), lambda bi, i: (bi, jnp.maximum(i * hb - 1, 0), 0)),
                  pl.BlockSpec((1, ts, d), lambda bi, i: (bi, i, 0)),
                  _const_spec(dw_slab.shape), _const_spec(bdw_slab.shape), _const_spec((1, d)),
                  _const_spec((1, d)), _const_spec(w_out.shape), _const_spec((1, d))],
        out_specs=pl.BlockSpec((1, ts, d), lambda bi, i: (bi, i, 0)),
        scratch_shapes=[pltpu.VMEM((n_slab, ts + CONV_HALO + SUBLANES, LANES), F32),
                        pltpu.VMEM((n_slab, ts, LANES), F32)],
        compiler_params=_params(("parallel", "parallel")),
        name="conv_out",
    )(u, u, x, dw_slab, bdw_slab, row(ln_g), row(ln_b), w_out, row(b_out))


def _gla_in_body(x_ref, g_ref, wqk_ref, wv_ref, wog_ref, wgd_ref, qk_ref, v_ref, og_ref, gd_ref):
    h = _rms(x_ref[...], g_ref[...]).astype(BF16)
    qk_ref[...] = _dot(h, wqk_ref[...])
    v_ref[...] = _dot(h, wv_ref[...]).astype(BF16)
    og_ref[...] = _dot(h, wog_ref[...])
    gd_ref[...] = _dot(h, wgd_ref[...])


def _gla_in(x2, g, w_qk, w_v, w_og, w_gd, tm=ROW_TILE):
    t, d = x2.shape
    widths = (w_qk.shape[1], w_v.shape[1], w_og.shape[1], w_gd.shape[1])
    dtypes = (F32, BF16, F32, F32)
    return pl.pallas_call(
        _gla_in_body,
        out_shape=[jax.ShapeDtypeStruct((t, n), dt) for n, dt in zip(widths, dtypes)],
        grid=(t // tm,),
        in_specs=[pl.BlockSpec((tm, d), lambda i: (i, 0)), _const_spec((1, d)),
                  _const_spec(w_qk.shape), _const_spec(w_v.shape), _const_spec(w_og.shape),
                  _const_spec(w_gd.shape)],
        out_specs=[pl.BlockSpec((tm, n), lambda i: (i, 0)) for n in widths],
        compiler_params=_params(("parallel",)),
        name="gla_in",
    )(x2, g.reshape(1, d), w_qk, w_v, w_og, w_gd)


def _gla_body(q_ref, k_ref, v_ref, og_ref, gd_ref, wup_ref, bg_ref, ng_ref, x_ref, wo_ref, o_ref,
              state_ref, obuf_ref):
    @pl.when(pl.program_id(1) == 0)
    def _():
        state_ref[...] = jnp.zeros_like(state_ref)

    c, dk, dv = GLA_CHUNK, GLA_DK, GLA_DV
    ga = _dot(gd_ref[0].astype(BF16), wup_ref[...]) + bg_ref[...]
    log_a = (jnp.minimum(ga, 0.0) - jnp.log1p(jnp.exp(-jnp.abs(ga)))) / GLA_TAU
    ri = lax.broadcasted_iota(jnp.int32, (c, c), 0)
    ci = lax.broadcasted_iota(jnp.int32, (c, c), 1)
    causal = ci <= ri
    tril = jnp.where(causal, 1.0, 0.0).astype(BF16)
    scale = dk ** -0.5
    n_chunks = GLA_ROWS // c
    chunk_rows = [slice(ch * c, (ch + 1) * c) for ch in range(n_chunks)]
    cases = [(ch, h) for ch in range(n_chunks) for h in range(GLA_HEADS)]
    hi = log_a.astype(BF16)
    r1 = log_a - hi.astype(F32)
    mid = r1.astype(BF16)
    lo = (r1 - mid.astype(F32)).astype(BF16)
    q_t, k_t, k_dec, decay = [], [], [], []
    for rows in chunk_rows:
        bcum = _dot(tril, hi[rows]) + _dot(tril, mid[rows]) + _dot(tril, lo[rows])
        b_last = bcum[c - 1:c, :]
        k = k_ref[0, rows, :]
        q_t.append((q_ref[0, rows, :] * scale * jnp.exp(bcum)).astype(BF16))
        k_t.append((k * jnp.exp(-bcum)).astype(BF16))
        k_dec.append((k * jnp.exp(b_last - bcum)).astype(BF16))
        decay.append(jnp.exp(b_last))
    ks = [slice(h * dk, (h + 1) * dk) for h in range(GLA_HEADS)]
    vs = [slice(h * dv, (h + 1) * dv) for h in range(GLA_HEADS)]
    v = {(ch, h): v_ref[0, chunk_rows[ch], vs[h]] for ch, h in cases}
    a = {(ch, h): jnp.where(causal, _dot_nt(q_t[ch][:, ks[h]], k_t[ch][:, ks[h]]), 0.0).astype(BF16)
         for ch, h in cases}
    kv = {(ch, h): _dot_tn(v[ch, h], k_dec[ch][:, ks[h]]) for ch, h in cases}
    o_intra = {(ch, h): _dot(a[ch, h], v[ch, h]) for ch, h in cases}
    state = [state_ref[h] for h in range(GLA_HEADS)]
    for ch, h in cases:
        o = o_intra[ch, h] + _dot_nt(q_t[ch][:, ks[h]], state[h].astype(BF16))
        state[h] = state[h] * decay[ch][:, ks[h]] + kv[ch, h]
        o = _rms(o, ng_ref[...])
        og = og_ref[0, chunk_rows[ch], vs[h]]
        obuf_ref[chunk_rows[ch], vs[h]] = (o * (og * jax.nn.sigmoid(og))).astype(BF16)
    for h in range(GLA_HEADS):
        state_ref[h] = state[h]
    o_ref[0] = x_ref[0] + _dot(obuf_ref[...], wo_ref[...])


def _gla_core(qk, v, og, gd, w_up, b_gate, norm_g, x, w_o):
    b, s, d = x.shape
    hh, dk, dv, rt = GLA_HEADS, GLA_DK, GLA_DV, GLA_ROWS
    return pl.pallas_call(
        _gla_body,
        out_shape=jax.ShapeDtypeStruct((b, s, d), F32),
        grid=(b, s // rt),
        in_specs=[pl.BlockSpec((1, rt, hh * dk), lambda bi, i: (bi, i, 0)),
                  pl.BlockSpec((1, rt, hh * dk), lambda bi, i: (bi, i, 1)),
                  pl.BlockSpec((1, rt, hh * dv), lambda bi, i: (bi, i, 0)),
                  pl.BlockSpec((1, rt, hh * dv), lambda bi, i: (bi, i, 0)),
                  pl.BlockSpec((1, rt, LANES), lambda bi, i: (bi, i, 0)),
                  _const_spec((LANES, hh * dk)), _const_spec((1, hh * dk)), _const_spec((1, dv)),
                  pl.BlockSpec((1, rt, d), lambda bi, i: (bi, i, 0)), _const_spec(w_o.shape)],
        out_specs=pl.BlockSpec((1, rt, d), lambda bi, i: (bi, i, 0)),
        scratch_shapes=[pltpu.VMEM((hh, dv, dk), F32), pltpu.VMEM((rt, hh * dv), BF16)],
        compiler_params=_params(("parallel", "arbitrary")),
        name="gla_core",
    )(qk, qk, v, og, gd, w_up, b_gate.reshape(1, hh * dk), norm_g.reshape(1, dv), x, w_o)


def _gla_mixer(x, g, w_in, w_gate_up, b_gate, norm_g, w_o):
    b, s, d = x.shape
    qw, vw = GLA_HEADS * GLA_DK, GLA_HEADS * GLA_DV
    x2 = x.reshape(b * s, d)
    w_gd = jnp.zeros((d, LANES), BF16).at[:, :GLA_GATE_RANK].set(w_in[:, 2 * qw + 2 * vw:].astype(BF16))
    w_up = jnp.zeros((LANES, qw), BF16).at[:GLA_GATE_RANK].set(w_gate_up.astype(BF16))
    qk, v, og, gd = _gla_in(x2, g, w_in[:, :2 * qw].astype(BF16), w_in[:, 2 * qw:2 * qw + vw].astype(BF16),
                            w_in[:, 2 * qw + vw:2 * qw + 2 * vw].astype(BF16), w_gd)
    r3 = lambda a: a.reshape(b, s, a.shape[-1])
    return _gla_core(r3(qk), r3(v), r3(og), r3(gd), w_up, b_gate, norm_g, x, w_o.astype(BF16))


def _nsa_in_body(x_ref, g_ref, wqt_ref, wk_ref, wvt_ref, wc_ref, wgt_ref,
                 qt_ref, kas_ref, kaw_ref, vast_ref, vawt_ref, c_ref, gtt_ref, *, ts):
    i = pl.program_id(1)
    qt, groups = NSA_QT, NSA_KV_HEADS
    h = _rms(x_ref[0], g_ref[...]).astype(BF16)
    q_all = (_dot_nt(wqt_ref[...], h) * (NSA_HEAD_DIM ** -0.5 * LOG2E)).astype(BF16)
    hw = NSA_REP * NSA_HEAD_DIM
    for gi in range(groups):
        for tt in range(ts // qt):
            qt_ref[0, gi, tt] = q_all[gi * hw:(gi + 1) * hw, tt * qt:(tt + 1) * qt]
    k = _dot(h, wk_ref[...])
    gw = groups * LANES
    lane = lax.broadcasted_iota(jnp.int32, (ts, LANES), 1)
    t = i * ts + lax.broadcasted_iota(jnp.int32, (ts, LANES), 0)
    onehot = jnp.where(lane - NSA_HEAD_DIM == t // NSA_SEL_BLOCK, 1.0, 0.0)
    for gi in range(groups):
        ls = slice(gi * LANES, (gi + 1) * LANES)
        kas_ref[0, gi] = (k[:, ls] + onehot).astype(BF16)
        kaw_ref[0, gi] = k[:, gw + gi * LANES:gw + (gi + 1) * LANES].astype(BF16)
    vt = _dot_nt(wvt_ref[...], h)
    ones = jnp.where(lax.broadcasted_iota(jnp.int32, (LANES, qt), 0) >= NSA_HEAD_DIM, 1.0, 0.0)
    for gi in range(groups):
        for tt in range(ts // qt):
            cs = slice(tt * qt, (tt + 1) * qt)
            vast_ref[0, gi, tt] = (vt[gi * LANES:(gi + 1) * LANES, cs] + ones).astype(BF16)
            vawt_ref[0, gi, tt] = (vt[gw + gi * LANES:gw + (gi + 1) * LANES, cs] + ones).astype(BF16)
    c = _dot(h, wc_ref[...])
    for gi in range(groups):
        c_ref[0, gi] = c[:, gi * LANES:(gi + 1) * LANES]
    gtt_ref[0] = _dot_nt(wgt_ref[...], h)


def _nsa_in(x, g, w_qt, w_k, w_vt, w_c, w_gt, ts=ROW_TILE):
    b, s, d = x.shape
    groups, qt = NSA_KV_HEADS, NSA_QT
    gw = groups * LANES
    nkt = s // qt
    tpb = ts // qt
    seq_spec = lambda n: pl.BlockSpec((1, ts, n), lambda bi, i: (bi, i, 0))
    rows_spec = lambda n: pl.BlockSpec((1, n, ts), lambda bi, i: (bi, 0, i))
    vt_spec = pl.BlockSpec((1, groups, tpb, LANES, qt), lambda bi, i: (bi, 0, i, 0, 0))
    vt_shape = jax.ShapeDtypeStruct((b, groups, nkt, LANES, qt), BF16)
    k_spec = pl.BlockSpec((1, groups, ts, LANES), lambda bi, i: (bi, 0, i, 0))
    k_shape = jax.ShapeDtypeStruct((b, groups, s, LANES), BF16)
    return pl.pallas_call(
        functools.partial(_nsa_in_body, ts=ts),
        out_shape=[jax.ShapeDtypeStruct((b, groups, nkt, w_qt.shape[0] // groups, qt), BF16),
                   k_shape, k_shape,
                   vt_shape, vt_shape,
                   jax.ShapeDtypeStruct((b, groups, s, LANES), F32),
                   jax.ShapeDtypeStruct((b, w_gt.shape[0], s), F32)],
        grid=(b, s // ts),
        in_specs=[pl.BlockSpec((1, ts, d), lambda bi, i: (bi, i, 0)), _const_spec((1, d)),
                  _const_spec(w_qt.shape), _const_spec(w_k.shape), _const_spec(w_vt.shape),
                  _const_spec(w_c.shape), _const_spec(w_gt.shape)],
        out_specs=[pl.BlockSpec((1, groups, tpb, w_qt.shape[0] // groups, qt), lambda bi, i: (bi, 0, i, 0, 0)),
                   k_spec, k_spec, vt_spec, vt_spec,
                   k_spec, rows_spec(w_gt.shape[0])],
        compiler_params=_params(("parallel", "parallel")),
        name="nsa_in",
    )(x, g.reshape(1, d), w_qt, w_k, w_vt, w_c, w_gt)


def _nsa_cmp_body(c_ref, pos_ref, w1t_ref, w1b_ref, w2k_ref, w2vt_ref, kc_ref, vct_ref):
    stride = NSA_CMP_STRIDE
    nch = c_ref.shape[2] // stride
    u = v = None
    for p in range(stride):
        x = c_ref[0, 0, pl.ds(p, nch, stride=stride), :]
        up = _dot((x + pos_ref[p:p + 1, :]).astype(BF16), w1t_ref[p])
        vp = _dot((x + pos_ref[stride + p:stride + p + 1, :]).astype(BF16), w1b_ref[p])
        u = up if u is None else u + up
        v = vp if v is None else v + vp
    hid = jax.nn.gelu(u + pltpu.roll(v, nch - 1, 0), approximate=True).astype(BF16)
    hw = hid.shape[-1] // 2
    kc_ref[0, 0] = _dot(hid[:, :hw], w2k_ref[...])
    vct_ref[0, 0] = _dot_nt(w2vt_ref[...], hid[:, hw:])


def _nsa_compress(c, cmp_pos, w1, w2):
    b, g, s, _ = c.shape
    dh, hidden = w2.shape[-1], w2.shape[1]
    stride = NSA_CMP_STRIDE
    nch = s // stride
    pos = jnp.concatenate([cmp_pos[0], cmp_pos[1]], axis=-1)
    w1r = w1.astype(BF16).reshape(2, NSA_CMP_BLOCK, dh, hidden)
    zero = jnp.zeros((NSA_CMP_BLOCK, dh, hidden), BF16)
    w1bd = jnp.concatenate([jnp.concatenate([w1r[0], zero], axis=-1),
                            jnp.concatenate([zero, w1r[1]], axis=-1)], axis=1)
    return pl.pallas_call(
        _nsa_cmp_body,
        out_shape=[jax.ShapeDtypeStruct((b, g, nch, dh), F32), jax.ShapeDtypeStruct((b, g, dh, nch), F32)],
        grid=(b, g),
        in_specs=[pl.BlockSpec((1, 1, s, 2 * dh), lambda bi, gi: (bi, gi, 0, 0)),
                  _const_spec(pos.shape), _const_spec((stride, 2 * dh, 2 * hidden)),
                  _const_spec((stride, 2 * dh, 2 * hidden)), _const_spec((hidden, dh)), _const_spec((dh, hidden))],
        out_specs=[pl.BlockSpec((1, 1, nch, dh), lambda bi, gi: (bi, gi, 0, 0)),
                   pl.BlockSpec((1, 1, dh, nch), lambda bi, gi: (bi, gi, 0, 0))],
        compiler_params=_params(("parallel", "parallel")),
        name="nsa_compress",
    )(c, pos, w1bd[:stride], w1bd[stride:], w2[0].astype(BF16), w2[1].astype(BF16).T)


def _nsa_attn_body(qt_ref, kc_ref, vct_ref, kas_ref, vast_ref, kaw_ref, vawt_ref, gtt_ref, ovt_ref, o_ref,
                   qaug_ref, m_ref, acc_ref, p_ref, alpha_ref, s_ref, rank_ref, *, n_slc):
    qt, rep, dh = NSA_QT, NSA_REP, NSA_HEAD_DIM
    cols = rep * qt
    qi = pl.program_id(2)
    s0 = qi * qt
    q = jnp.concatenate([qt_ref[0, 0, 0, r * dh:(r + 1) * dh, :] for r in range(rep)], axis=1)
    t_col = s0 + (lax.broadcasted_iota(jnp.int32, (1, cols), 1) & (qt - 1))

    n_pad = kc_ref.shape[2]
    sc = _dot(kc_ref[0, 0].astype(BF16), q)
    cmp_end = lax.broadcasted_iota(jnp.int32, (n_pad, 1), 0) * NSA_CMP_STRIDE + (NSA_CMP_BLOCK - 1)
    sc = jnp.where(cmp_end <= t_col, sc, NEG)
    e = jnp.exp2(sc - jnp.max(sc, axis=0, keepdims=True))
    p_cmp = e * (1.0 / jnp.sum(e, axis=0, keepdims=True))
    p_cmp = jnp.where(t_col >= NSA_CMP_BLOCK - 1, p_cmp, 0.0).astype(BF16)
    o_cmp = _dot(vct_ref[0, 0].astype(BF16), p_cmp)

    heads = [slice(r * qt, (r + 1) * qt) for r in range(rep)]

    wt = NSA_WINDOW // qt + 1
    lo = jnp.clip(qi - (wt - 1), 0, kaw_ref.shape[2] // qt - wt)
    kw = kaw_ref[0, 0, pl.ds(pl.multiple_of(lo * qt, qt), wt * qt), 0:dh]
    kpos = lo * qt + lax.broadcasted_iota(jnp.int32, (wt * qt, 1), 0)
    t_q = t_col[:, 0:qt]
    wbias = jnp.where((kpos <= t_q) & (kpos > t_q - NSA_WINDOW), 0.0, NEG)
    w_scores = [_dot(kw, q[:, cs]) for cs in heads]
    w_probs = []
    for s in w_scores:
        s = s + wbias
        w_probs.append(jnp.exp2(s - jnp.max(s, axis=0, keepdims=True)).astype(BF16))
    o_win = []
    for p in w_probs:
        pv = _dot(vawt_ref[0, 0, lo], p[0:qt])
        for u in range(1, wt):
            pv = pv + _dot(vawt_ref[0, 0, lo + u], p[u * qt:(u + 1) * qt])
        o_win.append(pv[0:dh] / pv[dh:dh + 1])
    o_win = jnp.concatenate(o_win, axis=1)

    imp = _dot(ovt_ref[...], p_cmp[:, 0:qt])
    for r in range(1, rep):
        imp = imp + _dot(ovt_ref[...], p_cmp[:, r * qt:(r + 1) * qt])
    imp_t = imp[:n_slc]
    blk = lax.broadcasted_iota(jnp.int32, (n_slc, 1), 0)
    cur = t_q // NSA_SEL_BLOCK
    forced = (blk == 0) | (blk == cur) | (blk == cur - 1)
    visible = blk * NSA_SEL_BLOCK <= t_q
    imp_t = jnp.where(forced, NSA_FORCE, imp_t)
    imp_t = jnp.where(visible, imp_t, -1.0)
    groups = [imp_t[k:k + SUBLANES] for k in range(0, n_slc, SUBLANES)]
    sub = lax.broadcasted_iota(jnp.int32, (SUBLANES, 1), 0)
    rank_ref[...] = jnp.zeros_like(rank_ref)
    last_blk = (s0 + qt - 1) // NSA_SEL_BLOCK
    for gi, own in enumerate(groups):
        @pl.when(gi * SUBLANES <= last_blk)
        def _(gi=gi, own=own):
            incs = [jnp.zeros((SUBLANES, qt), F32) for _ in groups]
            for si in range(SUBLANES):
                row = own[si:si + 1, :]
                for k, blk_imp in enumerate(groups):
                    if k < gi:
                        inc = jnp.where(row > blk_imp, 1.0, 0.0)
                    elif k > gi:
                        inc = jnp.where(row >= blk_imp, 1.0, 0.0)
                    else:
                        inc = jnp.where(sub > si, jnp.where(row >= blk_imp, 1.0, 0.0),
                                        jnp.where(row > blk_imp, 1.0, 0.0))
                    incs[k] = incs[k] + inc
            for k, inc in enumerate(incs):
                rank_ref[k * SUBLANES:(k + 1) * SUBLANES, :] += inc
    sel = (rank_ref[0:n_slc, :] < float(min(NSA_N_SELECT, n_slc))) & visible
    bias_parts = [jnp.where(sel, 0.0, NEG)]
    if LANES - dh - n_slc:
        bias_parts.append(jnp.zeros((LANES - dh - n_slc, qt), F32))
    bias = jnp.concatenate(bias_parts, axis=0).astype(BF16)
    qaug_ref[0:dh, :] = q
    for r in range(rep):
        qaug_ref[dh:, r * qt:(r + 1) * qt] = bias

    def score(ka_ref, j, nt):
        start = j * (nt * qt)
        if not isinstance(start, int):
            start = pl.multiple_of(start, nt * qt)
        ka = ka_ref[0, 0, pl.ds(start, nt * qt), :]
        return [_dot(ka, qaug_ref[:, cs]) for cs in heads]

    def flush(vat_ref, jp, nt):
        pvs = []
        for cs in heads:
            pv = _dot(vat_ref[0, 0, jp * nt], p_ref[0:qt, cs])
            for u in range(1, nt):
                pv = pv + _dot(vat_ref[0, 0, jp * nt + u], p_ref[u * qt:(u + 1) * qt, cs])
            pvs.append(pv)
        return pvs

    def accumulate(pvs):
        for cs, pv in zip(heads, pvs):
            acc_ref[:, cs] = acc_ref[:, cs] * alpha_ref[:, cs] + pv

    def step(ka_ref, vat_ref, j, nt, mask_fn, has_next):
        rows = nt * qt
        nxt = score(ka_ref, j + 1, nt) if has_next else None
        pvs = flush(vat_ref, jnp.maximum(j - 1, 0), nt)
        probs, alphas = [], []
        for cs in heads:
            s = s_ref[0:rows, cs]
            if mask_fn is not None:
                kpos = j * rows + lax.broadcasted_iota(jnp.int32, (rows, 1), 0)
                s = jnp.where(mask_fn(kpos, t_col[:, cs]), s, NEG)
            m_old = m_ref[:, cs]
            m_new = jnp.maximum(m_old, jnp.max(s, axis=0, keepdims=True))
            probs.append(jnp.exp2(s - m_new).astype(BF16))
            alphas.append(jnp.exp2(m_old - m_new))
            m_ref[:, cs] = m_new
        accumulate(pvs)
        for cs, p, alpha in zip(heads, probs, alphas):
            p_ref[0:rows, cs] = p
            alpha_ref[:, cs] = alpha
        if has_next:
            for cs, s in zip(heads, nxt):
                s_ref[0:rows, cs] = s

    def attend(ka_ref, vat_ref, nt, lo, hi, mask_fn, last_mask_fn):
        rows = nt * qt
        m_ref[...] = jnp.full_like(m_ref, NEG)
        acc_ref[...] = jnp.zeros_like(acc_ref)
        p_ref[0:rows, :] = jnp.zeros((rows, cols), BF16)
        alpha_ref[...] = jnp.ones_like(alpha_ref)
        for cs, s in zip(heads, score(ka_ref, lo, nt)):
            s_ref[0:rows, cs] = s

        def body(j, carry):
            step(ka_ref, vat_ref, j, nt, mask_fn, True)
            return carry

        lax.fori_loop(lo, hi, body, 0)
        step(ka_ref, vat_ref, hi, nt, last_mask_fn, False)
        accumulate(flush(vat_ref, hi, nt))
        acc = acc_ref[...]
        return acc[0:dh] / acc[dh:dh + 1]

    o_slc = attend(kas_ref, vast_ref, NSA_SEL_SPAN, 0, qi // NSA_SEL_SPAN, None, lambda kpos, t: kpos <= t)

    gates = jax.nn.sigmoid(gtt_ref[0])
    for r in range(rep):
        cs = slice(r * qt, (r + 1) * qt)
        o = (gates[3 * r:3 * r + 1] * o_cmp[:, cs] + gates[3 * r + 1:3 * r + 2] * o_slc[:, cs]
             + gates[3 * r + 2:3 * r + 3] * o_win[:, cs])
        o_ref[0, 0, 0, r * dh:(r + 1) * dh, :] = o.astype(BF16)


def _nsa_attn(q_t, kcmp, vcmp_t, kas, vas_t, kaw, vaw_t, gates_t, ov_t):
    b, g, nkt, _, qt = q_t.shape
    rep, dh = NSA_REP, NSA_HEAD_DIM
    s = nkt * qt
    n_slc = s // NSA_SEL_BLOCK
    nch = kcmp.shape[2]
    k_spec = pl.BlockSpec((1, 1, s, LANES), lambda bi, gi, i: (bi, gi, 0, 0))
    vt_spec = pl.BlockSpec((1, 1, nkt, LANES, qt), lambda bi, gi, i: (bi, gi, 0, 0, 0))
    q_spec = pl.BlockSpec((1, 1, 1, rep * dh, qt), lambda bi, gi, i: (bi, gi, i, 0, 0))
    return pl.pallas_call(
        functools.partial(_nsa_attn_body, n_slc=n_slc),
        out_shape=jax.ShapeDtypeStruct(q_t.shape, BF16),
        grid=(b, g, s // qt),
        in_specs=[q_spec,
                  pl.BlockSpec((1, 1, nch, dh), lambda bi, gi, i: (bi, gi, 0, 0)),
                  pl.BlockSpec((1, 1, dh, nch), lambda bi, gi, i: (bi, gi, 0, 0)),
                  k_spec, vt_spec, k_spec, vt_spec,
                  pl.BlockSpec((1, NSA_GATE_ROWS, qt), lambda bi, gi, i: (bi, gi, i)),
                  _const_spec(ov_t.shape)],
        out_specs=q_spec,
        scratch_shapes=[pltpu.VMEM((LANES, rep * qt), BF16), pltpu.VMEM((1, rep * qt), F32),
                        pltpu.VMEM((LANES, rep * qt), F32), pltpu.VMEM((NSA_SEL_SPAN * qt, rep * qt), BF16),
                        pltpu.VMEM((1, rep * qt), F32), pltpu.VMEM((NSA_SEL_SPAN * qt, rep * qt), F32),
                        pltpu.VMEM((LANES - dh, qt), F32)],
        compiler_params=_params(("parallel", "parallel", "parallel")),
        name="nsa_attn",
    )(q_t, kcmp, vcmp_t, kas, vas_t, kaw, vaw_t, gates_t, ov_t)


def _nsa_overlap_t(s):
    n_cmp = (s - NSA_CMP_BLOCK) // NSA_CMP_STRIDE + 1
    n_slc = s // NSA_SEL_BLOCK
    cs = np.arange(n_cmp) * NSA_CMP_STRIDE
    ss = np.arange(n_slc) * NSA_SEL_BLOCK
    ov = np.clip(np.minimum(cs[:, None] + NSA_CMP_BLOCK, ss[None, :] + NSA_SEL_BLOCK)
                 - np.maximum(cs[:, None], ss[None, :]), 0, None) / NSA_CMP_BLOCK
    out = np.zeros((LANES, s // NSA_CMP_STRIDE), np.float32)
    out[:n_slc, :n_cmp] = ov.T
    return jnp.asarray(out, BF16)


def _mm_res_t_body(at_ref, w_ref, x_ref, o_ref):
    groups, tiles, kw, qt = at_ref.shape[1:]
    for tt in range(tiles):
        rows = slice(tt * qt, (tt + 1) * qt)
        y = x_ref[0, rows, :]
        for gi in range(groups):
            y = y + _dot_tn(at_ref[0, gi, tt], w_ref[gi * kw:(gi + 1) * kw, :])
        o_ref[0, rows, :] = y


def _mm_res_t(a_t, w, x, ts=ROW_TILE):
    b, groups, nkt, kw, qt = a_t.shape
    s, n = nkt * qt, w.shape[1]
    tpb = ts // qt
    return pl.pallas_call(
        _mm_res_t_body,
        out_shape=jax.ShapeDtypeStruct((b, s, n), F32),
        grid=(b, s // ts),
        in_specs=[pl.BlockSpec((1, groups, tpb, kw, qt), lambda bi, i: (bi, 0, i, 0, 0)),
                  _const_spec(w.shape), pl.BlockSpec((1, ts, n), lambda bi, i: (bi, i, 0))],
        out_specs=pl.BlockSpec((1, ts, n), lambda bi, i: (bi, i, 0)),
        compiler_params=_params(("parallel", "parallel")),
        name="mm_res_t",
    )(a_t, w, x)


def _nsa_mixer(x, g, w_in, cmp_pos, cmp_w1, cmp_w2, w_o):
    b, s, d = x.shape
    hh, gg, rep, dh = NSA_HEADS, NSA_KV_HEADS, NSA_REP, NSA_HEAD_DIM
    kvw = gg * dh
    assert s % (NSA_SEL_SPAN * NSA_QT) == 0 and s % ROW_TILE == 0
    assert s // NSA_SEL_BLOCK <= LANES - dh and (s // NSA_SEL_BLOCK) % SUBLANES == 0
    wb = w_in.astype(BF16)
    parts = [wb[:, d + i * kvw:d + (i + 1) * kvw].reshape(d, gg, dh) for i in range(6)]
    zeros = jnp.zeros((d, gg, LANES - dh), BF16)
    padded = lambda p: jnp.concatenate([p, zeros], axis=-1).reshape(d, gg * LANES)
    w_k = jnp.concatenate([padded(parts[2]), padded(parts[4])], axis=-1)
    w_vt = jnp.concatenate([padded(parts[3]), padded(parts[5])], axis=-1).T
    w_c = jnp.concatenate([parts[0], parts[1]], axis=-1).reshape(d, gg * 2 * dh)
    w_g = wb[:, d + 6 * kvw:].reshape(d, gg, 3 * rep)
    w_gt = jnp.concatenate([w_g, jnp.zeros((d, gg, NSA_GATE_ROWS - 3 * rep), BF16)], axis=-1)
    w_gt = w_gt.reshape(d, gg * NSA_GATE_ROWS).T
    q_t, kas, kaw, vas_t, vaw_t, c, gates_t = _nsa_in(x, g, wb[:, :d].T, w_k, w_vt, w_c, w_gt)

    k_cmp, v_cmp_t = _nsa_compress(c, cmp_pos, cmp_w1, cmp_w2)
    o_t = _nsa_attn(q_t, k_cmp, v_cmp_t, kas, vas_t, kaw, vaw_t, gates_t, _nsa_overlap_t(s))
    return _mm_res_t(o_t, w_o.astype(BF16), x)


def kernel(x, mem, ffn1_norm, ffn1_w_in, ffn1_w_out, mix_norm, xattn_norm, mem_norm, xattn_w_q, xattn_w_kv, xattn_w_o, ffn2_norm, ffn2_w_in, ffn2_w_out, pool_w, pool_b, pool_scale, nsa_w_in, nsa_cmp_pos, nsa_cmp_w1, nsa_cmp_w2, nsa_w_o, gla_w_in, gla_w_gate_up, gla_b_gate, gla_norm, gla_w_o, conv_w_in, conv_b_in, conv_dw, conv_b_dw, conv_ln_g, conv_ln_b, conv_w_out, conv_b_out, final_norm):
    b, s, d = x.shape
    n_mem = mem.shape[1]
    depth = ffn1_norm.shape[0]
    n_mixers = 4
    flat = lambda a: a.reshape(b * s, d)
    cube = lambda a: a.reshape(b, s, d)
    mem2 = mem.reshape(b * n_mem, d)
    ffn1_w_in, ffn1_w_out, ffn2_w_in, ffn2_w_out, xattn_w_q, xattn_w_kv, xattn_w_o = (
        w.astype(BF16) for w in (ffn1_w_in, ffn1_w_out, ffn2_w_in, ffn2_w_out, xattn_w_q, xattn_w_kv, xattn_w_o))
    for i in range(depth):
        m, j = i % n_mixers, i // n_mixers
        x = cube(_ffn(flat(x), ffn1_norm[i], ffn1_w_in, ffn1_w_out, i))
        if m == 0:
            x = _pool_mixer(x, mix_norm[i], pool_w[j].astype(BF16), pool_b[j], pool_scale[j])
        elif m == 1:
            x = _nsa_mixer(x, mix_norm[i], nsa_w_in[j], nsa_cmp_pos[j], nsa_cmp_w1[j], nsa_cmp_w2[j], nsa_w_o[j])
        elif m == 2:
            x = _gla_mixer(x, mix_norm[i], gla_w_in[j], gla_w_gate_up[j], gla_b_gate[j], gla_norm[j], gla_w_o[j])
        else:
            u = _conv_in(flat(x), mix_norm[i], conv_w_in[j].astype(BF16), conv_b_in[j])
            x = _conv_out(cube(u), x, conv_dw[j], conv_b_dw[j], conv_ln_g[j], conv_ln_b[j],
                          conv_w_out[j].astype(BF16), conv_b_out[j])
        kv = _rms_mm(mem2, mem_norm[i], xattn_w_kv, i, BF16, tm=n_mem)
        x = _xattn(x, xattn_norm[i], xattn_w_q, kv.reshape(b, n_mem, 2 * d), xattn_w_o, i)
        final_g = final_norm if i == depth - 1 else None
        x = cube(_ffn(flat(x), ffn2_norm[i], ffn2_w_in, ffn2_w_out, i, final_g=final_g))
    return x
```

```python
import functools

import numpy as np
import jax
import jax.numpy as jnp
from jax import lax
from jax.experimental import pallas as pl
from jax.experimental.pallas import tpu as pltpu

F32 = jnp.float32
BF16 = jnp.bfloat16

EPS = 1e-6
NEG = -1e30

V7X_VMEM_BYTES = 64 * 1024 * 1024
VMEM_LIMIT = V7X_VMEM_BYTES - 8 * 1024 * 1024
LANES = 128
SUBLANES = 8

POOL_WINDOWS = (2, 4, 8, 16)
POOL_HALO = 16

NSA_HEADS = 16
NSA_KV_HEADS = 4
NSA_REP = NSA_HEADS // NSA_KV_HEADS
NSA_HEAD_DIM = 64
NSA_CMP_BLOCK = 32
NSA_CMP_STRIDE = 16
NSA_SEL_BLOCK = 64
NSA_N_SELECT = 16
NSA_WINDOW = 512
NSA_FORCE = 1e4
NSA_QT = 256
NSA_GATE_ROWS = 16
NSA_SEL_SPAN = 2
LOG2E = 1.4426950408889634

GLA_HEADS = 4
GLA_DK = 128
GLA_DV = 256
GLA_GATE_RANK = 16
GLA_TAU = 16.0
GLA_CHUNK = 64
GLA_ROWS = 512

CONV_WIDTH = 31
CONV_HALO = 32
CONV_STRIP = 128

XATTN_HEADS = 4

ROW_TILE = 512
FFN_ROW_TILE = 512
FFN_CHUNKS = 1


def _params(sem):
    return pltpu.CompilerParams(dimension_semantics=sem, vmem_limit_bytes=VMEM_LIMIT)


def _rms(x, g):
    return x * lax.rsqrt(jnp.mean(x * x, axis=-1, keepdims=True) + EPS) * g


def _dot(a, b):
    return jnp.dot(a, b, preferred_element_type=F32)


def _dot_nt(a, b):
    return lax.dot_general(a, b, (((1,), (1,)), ((), ())), preferred_element_type=F32)


def _dot_tn(a, b):
    return lax.dot_general(a, b, (((0,), (0,)), ((), ())), preferred_element_type=F32)


def _const_spec(shape):
    nd = len(shape)
    return pl.BlockSpec(shape, lambda *_: (0,) * nd, pipeline_mode=pl.Buffered(1))


def _layer_spec(stack_shape, layer):
    return pl.BlockSpec((None,) + tuple(stack_shape[1:]), lambda *_: (layer, 0, 0), pipeline_mode=pl.Buffered(1))


def _ffn_body(x_ref, g_ref, win_ref, wout_ref, *rest, d_ff, final):
    if final:
        fg_ref, o_ref = rest
    else:
        (o_ref,) = rest
    x = x_ref[...]
    h = _rms(x, g_ref[...]).astype(BF16)
    fc = d_ff // FFN_CHUNKS
    y = None
    for c in range(FFN_CHUNKS):
        g = _dot(h, win_ref[:, c * fc:(c + 1) * fc])
        u = _dot(h, win_ref[:, d_ff + c * fc:d_ff + (c + 1) * fc])
        a = (g * jax.nn.sigmoid(g) * u).astype(BF16)
        yc = _dot(a, wout_ref[c * fc:(c + 1) * fc, :])
        y = yc if y is None else y + yc
    out = x + 0.5 * y
    if final:
        out = _rms(out, fg_ref[...])
    o_ref[...] = out


def _ffn(x2, g, w_in, w_out, layer, final_g=None, tm=FFN_ROW_TILE):
    t, d = x2.shape
    d_ff = w_out.shape[1]
    final = final_g is not None
    in_specs = [pl.BlockSpec((tm, d), lambda i: (i, 0)), _const_spec((1, d)),
                _layer_spec(w_in.shape, layer), _layer_spec(w_out.shape, layer)]
    args = [x2, g.reshape(1, d), w_in, w_out]
    if final:
        in_specs.append(_const_spec((1, d)))
        args.append(final_g.reshape(1, d))
    return pl.pallas_call(
        functools.partial(_ffn_body, d_ff=d_ff, final=final),
        out_shape=jax.ShapeDtypeStruct((t, d), F32),
        grid=(t // tm,),
        in_specs=in_specs,
        out_specs=pl.BlockSpec((tm, d), lambda i: (i, 0)),
        compiler_params=_params(("parallel",)),
        name="ffn",
    )(*args)


def _rms_mm_body(x_ref, g_ref, w_ref, o_ref):
    h = _rms(x_ref[...], g_ref[...]).astype(BF16)
    o_ref[...] = _dot(h, w_ref[...]).astype(o_ref.dtype)


def _rms_mm(x2, g, w, layer, out_dtype, tm):
    t, d = x2.shape
    n = w.shape[2]
    return pl.pallas_call(
        _rms_mm_body,
        out_shape=jax.ShapeDtypeStruct((t, n), out_dtype),
        grid=(t // tm,),
        in_specs=[pl.BlockSpec((tm, d), lambda i: (i, 0)), _const_spec((1, d)), _layer_spec(w.shape, layer)],
        out_specs=pl.BlockSpec((tm, n), lambda i: (i, 0)),
        compiler_params=_params(("parallel",)),
        name="rms_mm",
    )(x2, g.reshape(1, d), w)


def _xattn_body(x_ref, g_ref, wq_ref, k_ref, v_ref, wo_ref, o_ref, *, heads):
    x = x_ref[0]
    d = x.shape[-1]
    dh = d // heads
    h = _rms(x, g_ref[...]).astype(BF16)
    q = (_dot(h, wq_ref[...]) * dh ** -0.5).astype(BF16)
    cols = [slice(hd * dh, (hd + 1) * dh) for hd in range(heads)]
    scores = [_dot_nt(q[:, sl], k_ref[0][:, sl]) for sl in cols]
    probs = []
    for s in scores:
        e = jnp.exp(s - jnp.max(s, axis=-1, keepdims=True))
        probs.append((e * (1.0 / jnp.sum(e, axis=-1, keepdims=True))).astype(BF16))
    o = jnp.concatenate([_dot(p, v_ref[0][:, sl]).astype(BF16) for p, sl in zip(probs, cols)], axis=-1)
    o_ref[0] = x + _dot(o, wo_ref[...])


def _xattn(x, g, w_q, kv, w_o, layer, ts=ROW_TILE):
    b, s, d = x.shape
    n = kv.shape[1]
    return pl.pallas_call(
        functools.partial(_xattn_body, heads=XATTN_HEADS),
        out_shape=jax.ShapeDtypeStruct((b, s, d), F32),
        grid=(b, s // ts),
        in_specs=[pl.BlockSpec((1, ts, d), lambda bi, i: (bi, i, 0)), _const_spec((1, d)),
                  _layer_spec(w_q.shape, layer),
                  pl.BlockSpec((1, n, d), lambda bi, i: (bi, 0, 0)),
                  pl.BlockSpec((1, n, d), lambda bi, i: (bi, 0, 1)),
                  _layer_spec(w_o.shape, layer)],
        out_specs=pl.BlockSpec((1, ts, d), lambda bi, i: (bi, i, 0)),
        compiler_params=_params(("parallel", "parallel")),
        name="xattn",
    )(x, g.reshape(1, d), w_q, kv, kv, w_o)


def _pool_body(x_ref, halo_ref, g_ref, w_ref, b_ref, sc_ref, o_ref, hbuf_ref, *, ts):
    i = pl.program_id(1)
    x = x_ref[0]
    d = x.shape[-1]
    gw = d // len(POOL_WINDOWS)
    g = g_ref[...]
    h = _rms(x, g)
    hh = _rms(halo_ref[0], g)
    hbuf_ref[0:POOL_HALO, :] = jnp.where(i > 0, hh, 0.0)
    hbuf_ref[POOL_HALO:, :] = h
    t = i * ts + lax.broadcasted_iota(jnp.int32, (ts, 1), 0)
    ys = []
    for gi, win in enumerate(POOL_WINDOWS):
        cs = slice(gi * gw, (gi + 1) * gw)
        acc = h[:, cs]
        for k in range(1, win):
            acc = acc + hbuf_ref[POOL_HALO - k:POOL_HALO - k + ts, cs]
        cnt = jnp.minimum(t + 1, win).astype(F32)
        p = acc / cnt - h[:, cs]
        ys.append(_dot(p.astype(BF16), w_ref[gi]))
    y = (jnp.concatenate(ys, axis=-1) + b_ref[...]) * sc_ref[...]
    o_ref[0] = x + y


def _pool_mixer(x, g, w, bias, scale, ts=ROW_TILE):
    b, s, d = x.shape
    hb = ts // POOL_HALO
    return pl.pallas_call(
        functools.partial(_pool_body, ts=ts),
        out_shape=jax.ShapeDtypeStruct((b, s, d), F32),
        grid=(b, s // ts),
        in_specs=[pl.BlockSpec((1, ts, d), lambda bi, i: (bi, i, 0)),
                  pl.BlockSpec((1, POOL_HALO, d), lambda bi, i: (bi, jnp.maximum(i * hb - 1, 0), 0)),
                  _const_spec((1, d)), _const_spec(w.shape), _const_spec((1, d)), _const_spec((1, d))],
        out_specs=pl.BlockSpec((1, ts, d), lambda bi, i: (bi, i, 0)),
        scratch_shapes=[pltpu.VMEM((ts + POOL_HALO, d), F32)],
        compiler_params=_params(("parallel", "parallel")),
        name="pool",
    )(x, x, g.reshape(1, d), w, bias.reshape(1, d), scale.reshape(1, d))


def _conv_in_body(x_ref, g_ref, w_ref, b_ref, o_ref):
    d = x_ref.shape[-1]
    h = _rms(x_ref[...], g_ref[...]).astype(BF16)
    ag = _dot(h, w_ref[...]) + b_ref[...]
    o_ref[...] = ag[:, :d] * jax.nn.sigmoid(ag[:, d:])


def _conv_in(x2, g, w, bias, tm=ROW_TILE):
    t, d = x2.shape
    return pl.pallas_call(
        _conv_in_body,
        out_shape=jax.ShapeDtypeStruct((t, d), F32),
        grid=(t // tm,),
        in_specs=[pl.BlockSpec((tm, d), lambda i: (i, 0)), _const_spec((1, d)),
                  _const_spec(w.shape), _const_spec((1, 2 * d))],
        out_specs=pl.BlockSpec((tm, d), lambda i: (i, 0)),
        compiler_params=_params(("parallel",)),
        name="conv_in",
    )(x2, g.reshape(1, d), w, bias.reshape(1, 2 * d))


def _conv_out_body(u_ref, halo_ref, x_ref, dw_ref, bdw_ref, lng_ref, lnb_ref, w_ref, bo_ref, o_ref,
                   ubuf_ref, cbuf_ref, *, ts):
    i = pl.program_id(1)
    n_slab = ubuf_ref.shape[0]
    halo = jnp.where(i > 0, halo_ref[0], 0.0)
    for lb in range(n_slab):
        ls = slice(lb * LANES, (lb + 1) * LANES)
        ubuf_ref[lb, 0:CONV_HALO, :] = halo[:, ls]
        ubuf_ref[lb, CONV_HALO:CONV_HALO + ts, :] = u_ref[0, :, ls]
        ubuf_ref[lb, CONV_HALO + ts:, :] = jnp.zeros((SUBLANES, LANES), F32)
    lead = CONV_HALO - (CONV_WIDTH - 1)

    def slab(lb, carry):
        for r0 in range(0, ts, CONV_STRIP):
            acc = None
            for sh in range(SUBLANES):
                taps = [k for k in range(CONV_WIDTH) if (lead + k) % SUBLANES == sh]
                if not taps:
                    continue
                win = ubuf_ref[lb, pl.ds(r0 + sh, CONV_STRIP + CONV_HALO), :]
                for k in taps:
                    a8 = (lead + k) // SUBLANES * SUBLANES
                    term = win[a8:a8 + CONV_STRIP] * dw_ref[lb, k:k + 1, :]
                    acc = term if acc is None else acc + term
            cbuf_ref[lb, r0:r0 + CONV_STRIP, :] = acc + bdw_ref[lb]
        return carry

    lax.fori_loop(0, n_slab, slab, 0)
    c = jnp.concatenate([cbuf_ref[lb] for lb in range(n_slab)], axis=-1)
    mu = jnp.mean(c, axis=-1, keepdims=True)
    var = jnp.mean(jnp.square(c - mu), axis=-1, keepdims=True)
    n = (c - mu) * lax.rsqrt(var + EPS) * lng_ref[...] + lnb_ref[...]
    a = (n * jax.nn.sigmoid(n)).astype(BF16)
    o_ref[0] = x_ref[0] + _dot(a, w_ref[...]) + bo_ref[...]


def _conv_out(u, x, dw, b_dw, ln_g, ln_b, w_out, b_out, ts=ROW_TILE):
    b, s, d = x.shape
    hb = ts // CONV_HALO
    n_slab = d // LANES
    dw_pad = jnp.zeros((CONV_HALO, d), F32).at[:CONV_WIDTH].set(dw)
    dw_slab = dw_pad.reshape(CONV_HALO, n_slab, LANES).transpose(1, 0, 2)
    bdw_slab = b_dw.reshape(n_slab, 1, LANES)
    row = lambda v: v.reshape(1, d)
    return pl.pallas_call(
        functools.partial(_conv_out_body, ts=ts),
        out_shape=jax.ShapeDtypeStruct((b, s, d), F32),
        grid=(b, s // ts),
        in_specs=[pl.BlockSpec((1, ts, d), lambda bi, i: (bi, i, 0)),
                  pl.BlockSpec((1, CONV_HALO, d), lambda bi, i: (bi, jnp.maximum(i * hb - 1, 0), 0)),
                  pl.BlockSpec((1, ts, d), lambda bi, i: (bi, i, 0)),
                  _const_spec(dw_slab.shape), _const_spec(bdw_slab.shape), _const_spec((1, d)),
                  _const_spec((1, d)), _const_spec(w_out.shape), _const_spec((1, d))],
        out_specs=pl.BlockSpec((1, ts, d), lambda bi, i: (bi, i, 0)),
        scratch_shapes=[pltpu.VMEM((n_slab, ts + CONV_HALO + SUBLANES, LANES), F32),
                        pltpu.VMEM((n_slab, ts, LANES), F32)],
        compiler_params=_params(("parallel", "parallel")),
        name="conv_out",
    )(u, u, x, dw_slab, bdw_slab, row(ln_g), row(ln_b), w_out, row(b_out))


def _gla_in_body(x_ref, g_ref, wqk_ref, wv_ref, wog_ref, wgd_ref, qk_ref, v_ref, og_ref, gd_ref):
    h = _rms(x_ref[...], g_ref[...]).astype(BF16)
    qk_ref[...] = _dot(h, wqk_ref[...])
    v_ref[...] = _dot(h, wv_ref[...]).astype(BF16)
    og_ref[...] = _dot(h, wog_ref[...])
    gd_ref[...] = _dot(h, wgd_ref[...])


def _gla_in(x2, g, w_qk, w_v, w_og, w_gd, tm=ROW_TILE):
    t, d = x2.shape
    widths = (w_qk.shape[1], w_v.shape[1], w_og.shape[1], w_gd.shape[1])
    dtypes = (F32, BF16, F32, F32)
    return pl.pallas_call(
        _gla_in_body,
        out_shape=[jax.ShapeDtypeStruct((t, n), dt) for n, dt in zip(widths, dtypes)],
        grid=(t // tm,),
        in_specs=[pl.BlockSpec((tm, d), lambda i: (i, 0)), _const_spec((1, d)),
                  _const_spec(w_qk.shape), _const_spec(w_v.shape), _const_spec(w_og.shape),
                  _const_spec(w_gd.shape)],
        out_specs=[pl.BlockSpec((tm, n), lambda i: (i, 0)) for n in widths],
        compiler_params=_params(("parallel",)),
        name="gla_in",
    )(x2, g.reshape(1, d), w_qk, w_v, w_og, w_gd)


def _gla_body(q_ref, k_ref, v_ref, og_ref, gd_ref, wup_ref, bg_ref, ng_ref, x_ref, wo_ref, o_ref,
              state_ref, obuf_ref):
    @pl.when(pl.program_id(1) == 0)
    def _():
        state_ref[...] = jnp.zeros_like(state_ref)

    c, dk, dv = GLA_CHUNK, GLA_DK, GLA_DV
    ga = _dot(gd_ref[0].astype(BF16), wup_ref[...]) + bg_ref[...]
    log_a = (jnp.minimum(ga, 0.0) - jnp.log1p(jnp.exp(-jnp.abs(ga)))) / GLA_TAU
    ri = lax.broadcasted_iota(jnp.int32, (c, c), 0)
    ci = lax.broadcasted_iota(jnp.int32, (c, c), 1)
    causal = ci <= ri
    tril = jnp.where(causal, 1.0, 0.0).astype(BF16)
    scale = dk ** -0.5
    n_chunks = GLA_ROWS // c
    chunk_rows = [slice(ch * c, (ch + 1) * c) for ch in range(n_chunks)]
    cases = [(ch, h) for ch in range(n_chunks) for h in range(GLA_HEADS)]
    hi = log_a.astype(BF16)
    r1 = log_a - hi.astype(F32)
    mid = r1.astype(BF16)
    lo = (r1 - mid.astype(F32)).astype(BF16)
    q_t, k_t, k_dec, decay = [], [], [], []
    for rows in chunk_rows:
        bcum = _dot(tril, hi[rows]) + _dot(tril, mid[rows]) + _dot(tril, lo[rows])
        b_last = bcum[c - 1:c, :]
        k = k_ref[0, rows, :]
        q_t.append((q_ref[0, rows, :] * scale * jnp.exp(bcum)).astype(BF16))
        k_t.append((k * jnp.exp(-bcum)).astype(BF16))
        k_dec.append((k * jnp.exp(b_last - bcum)).astype(BF16))
        decay.append(jnp.exp(b_last))
    ks = [slice(h * dk, (h + 1) * dk) for h in range(GLA_HEADS)]
    vs = [slice(h * dv, (h + 1) * dv) for h in range(GLA_HEADS)]
    v = {(ch, h): v_ref[0, chunk_rows[ch], vs[h]] for ch, h in cases}
    a = {(ch, h): jnp.where(causal, _dot_nt(q_t[ch][:, ks[h]], k_t[ch][:, ks[h]]), 0.0).astype(BF16)
         for ch, h in cases}
    kv = {(ch, h): _dot_tn(v[ch, h], k_dec[ch][:, ks[h]]) for ch, h in cases}
    o_intra = {(ch, h): _dot(a[ch, h], v[ch, h]) for ch, h in cases}
    state = [state_ref[h] for h in range(GLA_HEADS)]
    for ch, h in cases:
        o = o_intra[ch, h] + _dot_nt(q_t[ch][:, ks[h]], state[h].astype(BF16))
        state[h] = state[h] * decay[ch][:, ks[h]] + kv[ch, h]
        o = _rms(o, ng_ref[...])
        og = og_ref[0, chunk_rows[ch], vs[h]]
        obuf_ref[chunk_rows[ch], vs[h]] = (o * (og * jax.nn.sigmoid(og))).astype(BF16)
    for h in range(GLA_HEADS):
        state_ref[h] = state[h]
    o_ref[0] = x_ref[0] + _dot(obuf_ref[...], wo_ref[...])


def _gla_core(qk, v, og, gd, w_up, b_gate, norm_g, x, w_o):
    b, s, d = x.shape
    hh, dk, dv, rt = GLA_HEADS, GLA_DK, GLA_DV, GLA_ROWS
    return pl.pallas_call(
        _gla_body,
        out_shape=jax.ShapeDtypeStruct((b, s, d), F32),
        grid=(b, s // rt),
        in_specs=[pl.BlockSpec((1, rt, hh * dk), lambda bi, i: (bi, i, 0)),
                  pl.BlockSpec((1, rt, hh * dk), lambda bi, i: (bi, i, 1)),
                  pl.BlockSpec((1, rt, hh * dv), lambda bi, i: (bi, i, 0)),
                  pl.BlockSpec((1, rt, hh * dv), lambda bi, i: (bi, i, 0)),
                  pl.BlockSpec((1, rt, LANES), lambda bi, i: (bi, i, 0)),
                  _const_spec((LANES, hh * dk)), _const_spec((1, hh * dk)), _const_spec((1, dv)),
                  pl.BlockSpec((1, rt, d), lambda bi, i: (bi, i, 0)), _const_spec(w_o.shape)],
        out_specs=pl.BlockSpec((1, rt, d), lambda bi, i: (bi, i, 0)),
        scratch_shapes=[pltpu.VMEM((hh, dv, dk), F32), pltpu.VMEM((rt, hh * dv), BF16)],
        compiler_params=_params(("parallel", "arbitrary")),
        name="gla_core",
    )(qk, qk, v, og, gd, w_up, b_gate.reshape(1, hh * dk), norm_g.reshape(1, dv), x, w_o)


def _gla_mixer(x, g, w_in, w_gate_up, b_gate, norm_g, w_o):
    b, s, d = x.shape
    qw, vw = GLA_HEADS * GLA_DK, GLA_HEADS * GLA_DV
    x2 = x.reshape(b * s, d)
    w_gd = jnp.zeros((d, LANES), BF16).at[:, :GLA_GATE_RANK].set(w_in[:, 2 * qw + 2 * vw:].astype(BF16))
    w_up = jnp.zeros((LANES, qw), BF16).at[:GLA_GATE_RANK].set(w_gate_up.astype(BF16))
    qk, v, og, gd = _gla_in(x2, g, w_in[:, :2 * qw].astype(BF16), w_in[:, 2 * qw:2 * qw + vw].astype(BF16),
                            w_in[:, 2 * qw + vw:2 * qw + 2 * vw].astype(BF16), w_gd)
    r3 = lambda a: a.reshape(b, s, a.shape[-1])
    return _gla_core(r3(qk), r3(v), r3(og), r3(gd), w_up, b_gate, norm_g, x, w_o.astype(BF16))


def _nsa_in_body(x_ref, g_ref, wqt_ref, wk_ref, wvt_ref, wc_ref, wgt_ref,
                 qt_ref, kas_ref, kaw_ref, vast_ref, vawt_ref, c_ref, gtt_ref, *, ts):
    i = pl.program_id(1)
    qt, groups = NSA_QT, NSA_KV_HEADS
    h = _rms(x_ref[0], g_ref[...]).astype(BF16)
    q_all = (_dot_nt(wqt_ref[...], h) * (NSA_HEAD_DIM ** -0.5 * LOG2E)).astype(BF16)
    hw = NSA_REP * NSA_HEAD_DIM
    for gi in range(groups):
        for tt in range(ts // qt):
            qt_ref[0, gi, tt] = q_all[gi * hw:(gi + 1) * hw, tt * qt:(tt + 1) * qt]
    k = _dot(h, wk_ref[...])
    gw = groups * LANES
    lane = lax.broadcasted_iota(jnp.int32, (ts, LANES), 1)
    t = i * ts + lax.broadcasted_iota(jnp.int32, (ts, LANES), 0)
    onehot = jnp.where(lane - NSA_HEAD_DIM == t // NSA_SEL_BLOCK, 1.0, 0.0)
    for gi in range(groups):
        ls = slice(gi * LANES, (gi + 1) * LANES)
        kas_ref[0, gi] = (k[:, ls] + onehot).astype(BF16)
        kaw_ref[0, gi] = k[:, gw + gi * LANES:gw + (gi + 1) * LANES].astype(BF16)
    vt = _dot_nt(wvt_ref[...], h)
    ones = jnp.where(lax.broadcasted_iota(jnp.int32, (LANES, qt), 0) >= NSA_HEAD_DIM, 1.0, 0.0)
    for gi in range(groups):
        for tt in range(ts // qt):
            cs = slice(tt * qt, (tt + 1) * qt)
            vast_ref[0, gi, tt] = (vt[gi * LANES:(gi + 1) * LANES, cs] + ones).astype(BF16)
            vawt_ref[0, gi, tt] = (vt[gw + gi * LANES:gw + (gi + 1) * LANES, cs] + ones).astype(BF16)
    c = _dot(h, wc_ref[...])
    for gi in range(groups):
        c_ref[0, gi] = c[:, gi * LANES:(gi + 1) * LANES]
    gtt_ref[0] = _dot_nt(wgt_ref[...], h)


def _nsa_in(x, g, w_qt, w_k, w_vt, w_c, w_gt, ts=ROW_TILE):
    b, s, d = x.shape
    groups, qt = NSA_KV_HEADS, NSA_QT
    gw = groups * LANES
    nkt = s // qt
    tpb = ts // qt
    seq_spec = lambda n: pl.BlockSpec((1, ts, n), lambda bi, i: (bi, i, 0))
    rows_spec = lambda n: pl.BlockSpec((1, n, ts), lambda bi, i: (bi, 0, i))
    vt_spec = pl.BlockSpec((1, groups, tpb, LANES, qt), lambda bi, i: (bi, 0, i, 0, 0))
    vt_shape = jax.ShapeDtypeStruct((b, groups, nkt, LANES, qt), BF16)
    k_spec = pl.BlockSpec((1, groups, ts, LANES), lambda bi, i: (bi, 0, i, 0))
    k_shape = jax.ShapeDtypeStruct((b, groups, s, LANES), BF16)
    return pl.pallas_call(
        functools.partial(_nsa_in_body, ts=ts),
        out_shape=[jax.ShapeDtypeStruct((b, groups, nkt, w_qt.shape[0] // groups, qt), BF16),
                   k_shape, k_shape,
                   vt_shape, vt_shape,
                   jax.ShapeDtypeStruct((b, groups, s, LANES), F32),
                   jax.ShapeDtypeStruct((b, w_gt.shape[0], s), F32)],
        grid=(b, s // ts),
        in_specs=[pl.BlockSpec((1, ts, d), lambda bi, i: (bi, i, 0)), _const_spec((1, d)),
                  _const_spec(w_qt.shape), _const_spec(w_k.shape), _const_spec(w_vt.shape),
                  _const_spec(w_c.shape), _const_spec(w_gt.shape)],
        out_specs=[pl.BlockSpec((1, groups, tpb, w_qt.shape[0] // groups, qt), lambda bi, i: (bi, 0, i, 0, 0)),
                   k_spec, k_spec, vt_spec, vt_spec,
                   k_spec, rows_spec(w_gt.shape[0])],
        compiler_params=_params(("parallel", "parallel")),
        name="nsa_in",
    )(x, g.reshape(1, d), w_qt, w_k, w_vt, w_c, w_gt)


def _nsa_cmp_body(c_ref, pos_ref, w1t_ref, w1b_ref, w2k_ref, w2vt_ref, kc_ref, vct_ref):
    stride = NSA_CMP_STRIDE
    nch = c_ref.shape[2] // stride
    u = v = None
    for p in range(stride):
        x = c_ref[0, 0, pl.ds(p, nch, stride=stride), :]
        up = _dot((x + pos_ref[p:p + 1, :]).astype(BF16), w1t_ref[p])
        vp = _dot((x + pos_ref[stride + p:stride + p + 1, :]).astype(BF16), w1b_ref[p])
        u = up if u is None else u + up
        v = vp if v is None else v + vp
    hid = jax.nn.gelu(u + pltpu.roll(v, nch - 1, 0), approximate=True).astype(BF16)
    hw = hid.shape[-1] // 2
    kc_ref[0, 0] = _dot(hid[:, :hw], w2k_ref[...])
    vct_ref[0, 0] = _dot_nt(w2vt_ref[...], hid[:, hw:])


def _nsa_compress(c, cmp_pos, w1, w2):
    b, g, s, _ = c.shape
    dh, hidden = w2.shape[-1], w2.shape[1]
    stride = NSA_CMP_STRIDE
    nch = s // stride
    pos = jnp.concatenate([cmp_pos[0], cmp_pos[1]], axis=-1)
    w1r = w1.astype(BF16).reshape(2, NSA_CMP_BLOCK, dh, hidden)
    zero = jnp.zeros((NSA_CMP_BLOCK, dh, hidden), BF16)
    w1bd = jnp.concatenate([jnp.concatenate([w1r[0], zero], axis=-1),
                            jnp.concatenate([zero, w1r[1]], axis=-1)], axis=1)
    return pl.pallas_call(
        _nsa_cmp_body,
        out_shape=[jax.ShapeDtypeStruct((b, g, nch, dh), F32), jax.ShapeDtypeStruct((b, g, dh, nch), F32)],
        grid=(b, g),
        in_specs=[pl.BlockSpec((1, 1, s, 2 * dh), lambda bi, gi: (bi, gi, 0, 0)),
                  _const_spec(pos.shape), _const_spec((stride, 2 * dh, 2 * hidden)),
                  _const_spec((stride, 2 * dh, 2 * hidden)), _const_spec((hidden, dh)), _const_spec((dh, hidden))],
        out_specs=[pl.BlockSpec((1, 1, nch, dh), lambda bi, gi: (bi, gi, 0, 0)),
                   pl.BlockSpec((1, 1, dh, nch), lambda bi, gi: (bi, gi, 0, 0))],
        compiler_params=_params(("parallel", "parallel")),
        name="nsa_compress",
    )(c, pos, w1bd[:stride], w1bd[stride:], w2[0].astype(BF16), w2[1].astype(BF16).T)


def _nsa_attn_body(qt_ref, kc_ref, vct_ref, kas_ref, vast_ref, kaw_ref, vawt_ref, gtt_ref, ovt_ref, o_ref,
                   qaug_ref, m_ref, acc_ref, p_ref, alpha_ref, s_ref, rank_ref, *, n_slc):
    qt, rep, dh = NSA_QT, NSA_REP, NSA_HEAD_DIM
    cols = rep * qt
    qi = pl.program_id(2)
    s0 = qi * qt
    q = jnp.concatenate([qt_ref[0, 0, 0, r * dh:(r + 1) * dh, :] for r in range(rep)], axis=1)
    t_col = s0 + (lax.broadcasted_iota(jnp.int32, (1, cols), 1) & (qt - 1))

    n_pad = kc_ref.shape[2]
    sc = _dot(kc_ref[0, 0].astype(BF16), q)
    cmp_end = lax.broadcasted_iota(jnp.int32, (n_pad, 1), 0) * NSA_CMP_STRIDE + (NSA_CMP_BLOCK - 1)
    sc = jnp.where(cmp_end <= t_col, sc, NEG)
    e = jnp.exp2(sc - jnp.max(sc, axis=0, keepdims=True))
    p_cmp = e * (1.0 / jnp.sum(e, axis=0, keepdims=True))
    p_cmp = jnp.where(t_col >= NSA_CMP_BLOCK - 1, p_cmp, 0.0).astype(BF16)
    o_cmp = _dot(vct_ref[0, 0].astype(BF16), p_cmp)

    heads = [slice(r * qt, (r + 1) * qt) for r in range(rep)]

    wt = NSA_WINDOW // qt + 1
    lo = jnp.clip(qi - (wt - 1), 0, kaw_ref.shape[2] // qt - wt)
    kw = kaw_ref[0, 0, pl.ds(pl.multiple_of(lo * qt, qt), wt * qt), 0:dh]
    kpos = lo * qt + lax.broadcasted_iota(jnp.int32, (wt * qt, 1), 0)
    t_q = t_col[:, 0:qt]
    wbias = jnp.where((kpos <= t_q) & (kpos > t_q - NSA_WINDOW), 0.0, NEG)
    w_scores = [_dot(kw, q[:, cs]) for cs in heads]
    w_probs = []
    for s in w_scores:
        s = s + wbias
        w_probs.append(jnp.exp2(s - jnp.max(s, axis=0, keepdims=True)).astype(BF16))
    o_win = []
    for p in w_probs:
        pv = _dot(vawt_ref[0, 0, lo], p[0:qt])
        for u in range(1, wt):
            pv = pv + _dot(vawt_ref[0, 0, lo + u], p[u * qt:(u + 1) * qt])
        o_win.append(pv[0:dh] / pv[dh:dh + 1])
    o_win = jnp.concatenate(o_win, axis=1)

    imp = _dot(ovt_ref[...], p_cmp[:, 0:qt])
    for r in range(1, rep):
        imp = imp + _dot(ovt_ref[...], p_cmp[:, r * qt:(r + 1) * qt])
    imp_t = imp[:n_slc]
    blk = lax.broadcasted_iota(jnp.int32, (n_slc, 1), 0)
    cur = t_q // NSA_SEL_BLOCK
    forced = (blk == 0) | (blk == cur) | (blk == cur - 1)
    visible = blk * NSA_SEL_BLOCK <= t_q
    imp_t = jnp.where(forced, NSA_FORCE, imp_t)
    imp_t = jnp.where(visible, imp_t, -1.0)
    groups = [imp_t[k:k + SUBLANES] for k in range(0, n_slc, SUBLANES)]
    sub = lax.broadcasted_iota(jnp.int32, (SUBLANES, 1), 0)
    rank_ref[...] = jnp.zeros_like(rank_ref)
    last_blk = (s0 + qt - 1) // NSA_SEL_BLOCK
    for gi, own in enumerate(groups):
        @pl.when(gi * SUBLANES <= last_blk)
        def _(gi=gi, own=own):
            incs = [jnp.zeros((SUBLANES, qt), F32) for _ in groups]
            for si in range(SUBLANES):
                row = own[si:si + 1, :]
                for k, blk_imp in enumerate(groups):
                    if k < gi:
                        inc = jnp.where(row > blk_imp, 1.0, 0.0)
                    elif k > gi:
                        inc = jnp.where(row >= blk_imp, 1.0, 0.0)
                    else:
                        inc = jnp.where(sub > si, jnp.where(row >= blk_imp, 1.0, 0.0),
                                        jnp.where(row > blk_imp, 1.0, 0.0))
                    incs[k] = incs[k] + inc
            for k, inc in enumerate(incs):
                rank_ref[k * SUBLANES:(k + 1) * SUBLANES, :] += inc
    sel = (rank_ref[0:n_slc, :] < float(min(NSA_N_SELECT, n_slc))) & visible
    bias_parts = [jnp.where(sel, 0.0, NEG)]
    if LANES - dh - n_slc:
        bias_parts.append(jnp.zeros((LANES - dh - n_slc, qt), F32))
    bias = jnp.concatenate(bias_parts, axis=0).astype(BF16)
    qaug_ref[0:dh, :] = q
    for r in range(rep):
        qaug_ref[dh:, r * qt:(r + 1) * qt] = bias

    def score(ka_ref, j, nt):
        start = j * (nt * qt)
        if not isinstance(start, int):
            start = pl.multiple_of(start, nt * qt)
        ka = ka_ref[0, 0, pl.ds(start, nt * qt), :]
        return [_dot(ka, qaug_ref[:, cs]) for cs in heads]

    def flush(vat_ref, jp, nt):
        pvs = []
        for cs in heads:
            pv = _dot(vat_ref[0, 0, jp * nt], p_ref[0:qt, cs])
            for u in range(1, nt):
                pv = pv + _dot(vat_ref[0, 0, jp * nt + u], p_ref[u * qt:(u + 1) * qt, cs])
            pvs.append(pv)
        return pvs

    def accumulate(pvs):
        for cs, pv in zip(heads, pvs):
            acc_ref[:, cs] = acc_ref[:, cs] * alpha_ref[:, cs] + pv

    def step(ka_ref, vat_ref, j, nt, mask_fn, has_next):
        rows = nt * qt
        nxt = score(ka_ref, j + 1, nt) if has_next else None
        pvs = flush(vat_ref, jnp.maximum(j - 1, 0), nt)
        probs, alphas = [], []
        for cs in heads:
            s = s_ref[0:rows, cs]
            if mask_fn is not None:
                kpos = j * rows + lax.broadcasted_iota(jnp.int32, (rows, 1), 0)
                s = jnp.where(mask_fn(kpos, t_col[:, cs]), s, NEG)
            m_old = m_ref[:, cs]
            m_new = jnp.maximum(m_old, jnp.max(s, axis=0, keepdims=True))
            probs.append(jnp.exp2(s - m_new).astype(BF16))
            alphas.append(jnp.exp2(m_old - m_new))
            m_ref[:, cs] = m_new
        accumulate(pvs)
        for cs, p, alpha in zip(heads, probs, alphas):
            p_ref[0:rows, cs] = p
            alpha_ref[:, cs] = alpha
        if has_next:
            for cs, s in zip(heads, nxt):
                s_ref[0:rows, cs] = s

    def attend(ka_ref, vat_ref, nt, lo, hi, mask_fn, last_mask_fn):
        rows = nt * qt
        m_ref[...] = jnp.full_like(m_ref, NEG)
        acc_ref[...] = jnp.zeros_like(acc_ref)
        p_ref[0:rows, :] = jnp.zeros((rows, cols), BF16)
        alpha_ref[...] = jnp.ones_like(alpha_ref)
        for cs, s in zip(heads, score(ka_ref, lo, nt)):
            s_ref[0:rows, cs] = s

        def body(jj, carry):
            step(ka_ref, vat_ref, lo + 2 * jj, nt, mask_fn, True)
            step(ka_ref, vat_ref, lo + 2 * jj + 1, nt, mask_fn, True)
            return carry

        n_full = hi - lo
        lax.fori_loop(0, n_full // 2, body, 0)

        @pl.when(n_full % 2 == 1)
        def _():
            step(ka_ref, vat_ref, hi - 1, nt, mask_fn, True)

        step(ka_ref, vat_ref, hi, nt, last_mask_fn, False)
        accumulate(flush(vat_ref, hi, nt))
        acc = acc_ref[...]
        return acc[0:dh] / acc[dh:dh + 1]

    o_slc = attend(kas_ref, vast_ref, NSA_SEL_SPAN, 0, qi // NSA_SEL_SPAN, None, lambda kpos, t: kpos <= t)

    gates = jax.nn.sigmoid(gtt_ref[0])
    for r in range(rep):
        cs = slice(r * qt, (r + 1) * qt)
        o = (gates[3 * r:3 * r + 1] * o_cmp[:, cs] + gates[3 * r + 1:3 * r + 2] * o_slc[:, cs]
             + gates[3 * r + 2:3 * r + 3] * o_win[:, cs])
        o_ref[0, 0, 0, r * dh:(r + 1) * dh, :] = o.astype(BF16)


def _nsa_attn(q_t, kcmp, vcmp_t, kas, vas_t, kaw, vaw_t, gates_t, ov_t):
    b, g, nkt, _, qt = q_t.shape
    rep, dh = NSA_REP, NSA_HEAD_DIM
    s = nkt * qt
    n_slc = s // NSA_SEL_BLOCK
    nch = kcmp.shape[2]
    k_spec = pl.BlockSpec((1, 1, s, LANES), lambda bi, gi, i: (bi, gi, 0, 0))
    vt_spec = pl.BlockSpec((1, 1, nkt, LANES, qt), lambda bi, gi, i: (bi, gi, 0, 0, 0))
    q_spec = pl.BlockSpec((1, 1, 1, rep * dh, qt), lambda bi, gi, i: (bi, gi, i, 0, 0))
    return pl.pallas_call(
        functools.partial(_nsa_attn_body, n_slc=n_slc),
        out_shape=jax.ShapeDtypeStruct(q_t.shape, BF16),
        grid=(b, g, s // qt),
        in_specs=[q_spec,
                  pl.BlockSpec((1, 1, nch, dh), lambda bi, gi, i: (bi, gi, 0, 0)),
                  pl.BlockSpec((1, 1, dh, nch), lambda bi, gi, i: (bi, gi, 0, 0)),
                  k_spec, vt_spec, k_spec, vt_spec,
                  pl.BlockSpec((1, NSA_GATE_ROWS, qt), lambda bi, gi, i: (bi, gi, i)),
                  _const_spec(ov_t.shape)],
        out_specs=q_spec,
        scratch_shapes=[pltpu.VMEM((LANES, rep * qt), BF16), pltpu.VMEM((1, rep * qt), F32),
                        pltpu.VMEM((LANES, rep * qt), F32), pltpu.VMEM((NSA_SEL_SPAN * qt, rep * qt), BF16),
                        pltpu.VMEM((1, rep * qt), F32), pltpu.VMEM((NSA_SEL_SPAN * qt, rep * qt), F32),
                        pltpu.VMEM((LANES - dh, qt), F32)],
        compiler_params=_params(("parallel", "parallel", "parallel")),
        name="nsa_attn",
    )(q_t, kcmp, vcmp_t, kas, vas_t, kaw, vaw_t, gates_t, ov_t)


def _nsa_overlap_t(s):
    n_cmp = (s - NSA_CMP_BLOCK) // NSA_CMP_STRIDE + 1
    n_slc = s // NSA_SEL_BLOCK
    cs = np.arange(n_cmp) * NSA_CMP_STRIDE
    ss = np.arange(n_slc) * NSA_SEL_BLOCK
    ov = np.clip(np.minimum(cs[:, None] + NSA_CMP_BLOCK, ss[None, :] + NSA_SEL_BLOCK)
                 - np.maximum(cs[:, None], ss[None, :]), 0, None) / NSA_CMP_BLOCK
    out = np.zeros((LANES, s // NSA_CMP_STRIDE), np.float32)
    out[:n_slc, :n_cmp] = ov.T
    return jnp.asarray(out, BF16)


def _mm_res_t_body(at_ref, w_ref, x_ref, o_ref):
    groups, tiles, kw, qt = at_ref.shape[1:]
    for tt in range(tiles):
        rows = slice(tt * qt, (tt + 1) * qt)
        y = x_ref[0, rows, :]
        for gi in range(groups):
            y = y + _dot_tn(at_ref[0, gi, tt], w_ref[gi * kw:(gi + 1) * kw, :])
        o_ref[0, rows, :] = y


def _mm_res_t(a_t, w, x, ts=ROW_TILE):
    b, groups, nkt, kw, qt = a_t.shape
    s, n = nkt * qt, w.shape[1]
    tpb = ts // qt
    return pl.pallas_call(
        _mm_res_t_body,
        out_shape=jax.ShapeDtypeStruct((b, s, n), F32),
        grid=(b, s // ts),
        in_specs=[pl.BlockSpec((1, groups, tpb, kw, qt), lambda bi, i: (bi, 0, i, 0, 0)),
                  _const_spec(w.shape), pl.BlockSpec((1, ts, n), lambda bi, i: (bi, i, 0))],
        out_specs=pl.BlockSpec((1, ts, n), lambda bi, i: (bi, i, 0)),
        compiler_params=_params(("parallel", "parallel")),
        name="mm_res_t",
    )(a_t, w, x)


def _nsa_mixer(x, g, w_in, cmp_pos, cmp_w1, cmp_w2, w_o):
    b, s, d = x.shape
    hh, gg, rep, dh = NSA_HEADS, NSA_KV_HEADS, NSA_REP, NSA_HEAD_DIM
    kvw = gg * dh
    assert s % (NSA_SEL_SPAN * NSA_QT) == 0 and s % ROW_TILE == 0
    assert s // NSA_SEL_BLOCK <= LANES - dh and (s // NSA_SEL_BLOCK) % SUBLANES == 0
    wb = w_in.astype(BF16)
    parts = [wb[:, d + i * kvw:d + (i + 1) * kvw].reshape(d, gg, dh) for i in range(6)]
    zeros = jnp.zeros((d, gg, LANES - dh), BF16)
    padded = lambda p: jnp.concatenate([p, zeros], axis=-1).reshape(d, gg * LANES)
    w_k = jnp.concatenate([padded(parts[2]), padded(parts[4])], axis=-1)
    w_vt = jnp.concatenate([padded(parts[3]), padded(parts[5])], axis=-1).T
    w_c = jnp.concatenate([parts[0], parts[1]], axis=-1).reshape(d, gg * 2 * dh)
    w_g = wb[:, d + 6 * kvw:].reshape(d, gg, 3 * rep)
    w_gt = jnp.concatenate([w_g, jnp.zeros((d, gg, NSA_GATE_ROWS - 3 * rep), BF16)], axis=-1)
    w_gt = w_gt.reshape(d, gg * NSA_GATE_ROWS).T
    q_t, kas, kaw, vas_t, vaw_t, c, gates_t = _nsa_in(x, g, wb[:, :d].T, w_k, w_vt, w_c, w_gt)

    k_cmp, v_cmp_t = _nsa_compress(c, cmp_pos, cmp_w1, cmp_w2)
    o_t = _nsa_attn(q_t, k_cmp, v_cmp_t, kas, vas_t, kaw, vaw_t, gates_t, _nsa_overlap_t(s))
    return _mm_res_t(o_t, w_o.astype(BF16), x)


def kernel(x, mem, ffn1_norm, ffn1_w_in, ffn1_w_out, mix_norm, xattn_norm, mem_norm, xattn_w_q, xattn_w_kv, xattn_w_o, ffn2_norm, ffn2_w_in, ffn2_w_out, pool_w, pool_b, pool_scale, nsa_w_in, nsa_cmp_pos, nsa_cmp_w1, nsa_cmp_w2, nsa_w_o, gla_w_in, gla_w_gate_up, gla_b_gate, gla_norm, gla_w_o, conv_w_in, conv_b_in, conv_dw, conv_b_dw, conv_ln_g, conv_ln_b, conv_w_out, conv_b_out, final_norm):
    b, s, d = x.shape
    n_mem = mem.shape[1]
    depth = ffn1_norm.shape[0]
    n_mixers = 4
    flat = lambda a: a.reshape(b * s, d)
    cube = lambda a: a.reshape(b, s, d)
    mem2 = mem.reshape(b * n_mem, d)
    ffn1_w_in, ffn1_w_out, ffn2_w_in, ffn2_w_out, xattn_w_q, xattn_w_kv, xattn_w_o = (
        w.astype(BF16) for w in (ffn1_w_in, ffn1_w_out, ffn2_w_in, ffn2_w_out, xattn_w_q, xattn_w_kv, xattn_w_o))
    for i in range(depth):
        m, j = i % n_mixers, i // n_mixers
        x = cube(_ffn(flat(x), ffn1_norm[i], ffn1_w_in, ffn1_w_out, i))
        if m == 0:
            x = _pool_mixer(x, mix_norm[i], pool_w[j].astype(BF16), pool_b[j], pool_scale[j])
        elif m == 1:
            x = _nsa_mixer(x, mix_norm[i], nsa_w_in[j], nsa_cmp_pos[j], nsa_cmp_w1[j], nsa_cmp_w2[j], nsa_w_o[j])
        elif m == 2:
            x = _gla_mixer(x, mix_norm[i], gla_w_in[j], gla_w_gate_up[j], gla_b_gate[j], gla_norm[j], gla_w_o[j])
        else:
            u = _conv_in(flat(x), mix_norm[i], conv_w_in[j].astype(BF16), conv_b_in[j])
            x = _conv_out(cube(u), x, conv_dw[j], conv_b_dw[j], conv_ln_g[j], conv_ln_b[j],
                          conv_w_out[j].astype(BF16), conv_b_out[j])
        kv = _rms_mm(mem2, mem_norm[i], xattn_w_kv, i, BF16, tm=n_mem)
        x = _xattn(x, xattn_norm[i], xattn_w_q, kv.reshape(b, n_mem, 2 * d), xattn_w_o, i)
        final_g = final_norm if i == depth - 1 else None
        x = cube(_ffn(flat(x), ffn2_norm[i], ffn2_w_in, ffn2_w_out, i, final_g=final_g))
    return x
```

```python
import functools

import numpy as np
import jax
import jax.numpy as jnp
from jax import lax
from jax.experimental import pallas as pl
from jax.experimental.pallas import tpu as pltpu

F32 = jnp.float32
BF16 = jnp.bfloat16

EPS = 1e-6
NEG = -1e30

V7X_VMEM_BYTES = 64 * 1024 * 1024
VMEM_LIMIT = V7X_VMEM_BYTES - 8 * 1024 * 1024
LANES = 128
SUBLANES = 8

POOL_WINDOWS = (2, 4, 8, 16)
POOL_HALO = 16

NSA_HEADS = 16
NSA_KV_HEADS = 4
NSA_REP = NSA_HEADS // NSA_KV_HEADS
NSA_HEAD_DIM = 64
NSA_CMP_BLOCK = 32
NSA_CMP_STRIDE = 16
NSA_SEL_BLOCK = 64
NSA_N_SELECT = 16
NSA_WINDOW = 512
NSA_FORCE = 1e4
NSA_QT = 256
NSA_GATE_ROWS = 16
NSA_SEL_SPAN = 2
LOG2E = 1.4426950408889634

GLA_HEADS = 4
GLA_DK = 128
GLA_DV = 256
GLA_GATE_RANK = 16
GLA_TAU = 16.0
GLA_CHUNK = 64
GLA_ROWS = 512

CONV_WIDTH = 31
CONV_HALO = 32
CONV_STRIP = 128

XATTN_HEADS = 4

ROW_TILE = 1024
FFN_ROW_TILE = 512
FFN_CHUNKS = 1


def _params(sem):
    return pltpu.CompilerParams(dimension_semantics=sem, vmem_limit_bytes=VMEM_LIMIT)


def _rms(x, g):
    return x * lax.rsqrt(jnp.mean(x * x, axis=-1, keepdims=True) + EPS) * g


def _dot(a, b):
    return jnp.dot(a, b, preferred_element_type=F32)


def _dot_nt(a, b):
    return lax.dot_general(a, b, (((1,), (1,)), ((), ())), preferred_element_type=F32)


def _dot_tn(a, b):
    return lax.dot_general(a, b, (((0,), (0,)), ((), ())), preferred_element_type=F32)


def _const_spec(shape):
    nd = len(shape)
    return pl.BlockSpec(shape, lambda *_: (0,) * nd, pipeline_mode=pl.Buffered(1))


def _layer_spec(stack_shape, layer):
    return pl.BlockSpec((None,) + tuple(stack_shape[1:]), lambda *_: (layer, 0, 0), pipeline_mode=pl.Buffered(1))


def _ffn_body(x_ref, g_ref, win_ref, wout_ref, *rest, d_ff, final):
    if final:
        fg_ref, o_ref = rest
    else:
        (o_ref,) = rest
    x = x_ref[...]
    h = _rms(x, g_ref[...]).astype(BF16)
    fc = d_ff // FFN_CHUNKS
    y = None
    for c in range(FFN_CHUNKS):
        g = _dot(h, win_ref[:, c * fc:(c + 1) * fc])
        u = _dot(h, win_ref[:, d_ff + c * fc:d_ff + (c + 1) * fc])
        a = (g * jax.nn.sigmoid(g) * u).astype(BF16)
        yc = _dot(a, wout_ref[c * fc:(c + 1) * fc, :])
        y = yc if y is None else y + yc
    out = x + 0.5 * y
    if final:
        out = _rms(out, fg_ref[...])
    o_ref[...] = out


def _ffn(x2, g, w_in, w_out, layer, final_g=None, tm=FFN_ROW_TILE):
    t, d = x2.shape
    d_ff = w_out.shape[1]
    final = final_g is not None
    in_specs = [pl.BlockSpec((tm, d), lambda i: (i, 0)), _const_spec((1, d)),
                _layer_spec(w_in.shape, layer), _layer_spec(w_out.shape, layer)]
    args = [x2, g.reshape(1, d), w_in, w_out]
    if final:
        in_specs.append(_const_spec((1, d)))
        args.append(final_g.reshape(1, d))
    return pl.pallas_call(
        functools.partial(_ffn_body, d_ff=d_ff, final=final),
        out_shape=jax.ShapeDtypeStruct((t, d), F32),
        grid=(t // tm,),
        in_specs=in_specs,
        out_specs=pl.BlockSpec((tm, d), lambda i: (i, 0)),
        compiler_params=_params(("parallel",)),
        name="ffn",
    )(*args)


def _rms_mm_body(x_ref, g_ref, w_ref, o_ref):
    h = _rms(x_ref[...], g_ref[...]).astype(BF16)
    o_ref[...] = _dot(h, w_ref[...]).astype(o_ref.dtype)


def _rms_mm(x2, g, w, layer, out_dtype, tm):
    t, d = x2.shape
    n = w.shape[2]
    return pl.pallas_call(
        _rms_mm_body,
        out_shape=jax.ShapeDtypeStruct((t, n), out_dtype),
        grid=(t // tm,),
        in_specs=[pl.BlockSpec((tm, d), lambda i: (i, 0)), _const_spec((1, d)), _layer_spec(w.shape, layer)],
        out_specs=pl.BlockSpec((tm, n), lambda i: (i, 0)),
        compiler_params=_params(("parallel",)),
        name="rms_mm",
    )(x2, g.reshape(1, d), w)


def _xattn_body(x_ref, g_ref, wq_ref, k_ref, v_ref, wo_ref, o_ref, *, heads):
    x = x_ref[0]
    d = x.shape[-1]
    dh = d // heads
    h = _rms(x, g_ref[...]).astype(BF16)
    q = (_dot(h, wq_ref[...]) * dh ** -0.5).astype(BF16)
    cols = [slice(hd * dh, (hd + 1) * dh) for hd in range(heads)]
    scores = [_dot_nt(q[:, sl], k_ref[0][:, sl]) for sl in cols]
    probs = []
    for s in scores:
        e = jnp.exp(s - jnp.max(s, axis=-1, keepdims=True))
        probs.append((e * (1.0 / jnp.sum(e, axis=-1, keepdims=True))).astype(BF16))
    o = jnp.concatenate([_dot(p, v_ref[0][:, sl]).astype(BF16) for p, sl in zip(probs, cols)], axis=-1)
    o_ref[0] = x + _dot(o, wo_ref[...])


def _xattn(x, g, w_q, kv, w_o, layer, ts=ROW_TILE):
    b, s, d = x.shape
    n = kv.shape[1]
    return pl.pallas_call(
        functools.partial(_xattn_body, heads=XATTN_HEADS),
        out_shape=jax.ShapeDtypeStruct((b, s, d), F32),
        grid=(b, s // ts),
        in_specs=[pl.BlockSpec((1, ts, d), lambda bi, i: (bi, i, 0)), _const_spec((1, d)),
                  _layer_spec(w_q.shape, layer),
                  pl.BlockSpec((1, n, d), lambda bi, i: (bi, 0, 0)),
                  pl.BlockSpec((1, n, d), lambda bi, i: (bi, 0, 1)),
                  _layer_spec(w_o.shape, layer)],
        out_specs=pl.BlockSpec((1, ts, d), lambda bi, i: (bi, i, 0)),
        compiler_params=_params(("parallel", "parallel")),
        name="xattn",
    )(x, g.reshape(1, d), w_q, kv, kv, w_o)


def _pool_body(x_ref, halo_ref, g_ref, w_ref, b_ref, sc_ref, o_ref, hbuf_ref, *, ts):
    i = pl.program_id(1)
    x = x_ref[0]
    d = x.shape[-1]
    gw = d // len(POOL_WINDOWS)
    g = g_ref[...]
    h = _rms(x, g)
    hh = _rms(halo_ref[0], g)
    hbuf_ref[0:POOL_HALO, :] = jnp.where(i > 0, hh, 0.0)
    hbuf_ref[POOL_HALO:, :] = h
    t = i * ts + lax.broadcasted_iota(jnp.int32, (ts, 1), 0)
    ys = []
    for gi, win in enumerate(POOL_WINDOWS):
        cs = slice(gi * gw, (gi + 1) * gw)
        acc = h[:, cs]
        for k in range(1, win):
            acc = acc + hbuf_ref[POOL_HALO - k:POOL_HALO - k + ts, cs]
        cnt = jnp.minimum(t + 1, win).astype(F32)
        p = acc / cnt - h[:, cs]
        ys.append(_dot(p.astype(BF16), w_ref[gi]))
    y = (jnp.concatenate(ys, axis=-1) + b_ref[...]) * sc_ref[...]
    o_ref[0] = x + y


def _pool_mixer(x, g, w, bias, scale, ts=ROW_TILE):
    b, s, d = x.shape
    hb = ts // POOL_HALO
    return pl.pallas_call(
        functools.partial(_pool_body, ts=ts),
        out_shape=jax.ShapeDtypeStruct((b, s, d), F32),
        grid=(b, s // ts),
        in_specs=[pl.BlockSpec((1, ts, d), lambda bi, i: (bi, i, 0)),
                  pl.BlockSpec((1, POOL_HALO, d), lambda bi, i: (bi, jnp.maximum(i * hb - 1, 0), 0)),
                  _const_spec((1, d)), _const_spec(w.shape), _const_spec((1, d)), _const_spec((1, d))],
        out_specs=pl.BlockSpec((1, ts, d), lambda bi, i: (bi, i, 0)),
        scratch_shapes=[pltpu.VMEM((ts + POOL_HALO, d), F32)],
        compiler_params=_params(("parallel", "parallel")),
        name="pool",
    )(x, x, g.reshape(1, d), w, bias.reshape(1, d), scale.reshape(1, d))


def _conv_in_body(x_ref, g_ref, w_ref, b_ref, o_ref):
    d = x_ref.shape[-1]
    h = _rms(x_ref[...], g_ref[...]).astype(BF16)
    ag = _dot(h, w_ref[...]) + b_ref[...]
    o_ref[...] = ag[:, :d] * jax.nn.sigmoid(ag[:, d:])


def _conv_in(x2, g, w, bias, tm=ROW_TILE):
    t, d = x2.shape
    return pl.pallas_call(
        _conv_in_body,
        out_shape=jax.ShapeDtypeStruct((t, d), F32),
        grid=(t // tm,),
        in_specs=[pl.BlockSpec((tm, d), lambda i: (i, 0)), _const_spec((1, d)),
                  _const_spec(w.shape), _const_spec((1, 2 * d))],
        out_specs=pl.BlockSpec((tm, d), lambda i: (i, 0)),
        compiler_params=_params(("parallel",)),
        name="conv_in",
    )(x2, g.reshape(1, d), w, bias.reshape(1, 2 * d))


def _conv_out_body(u_ref, halo_ref, x_ref, dw_ref, bdw_ref, lng_ref, lnb_ref, w_ref, bo_ref, o_ref,
                   ubuf_ref, cbuf_ref, *, ts):
    i = pl.program_id(1)
    n_slab = ubuf_ref.shape[0]
    halo = jnp.where(i > 0, halo_ref[0], 0.0)
    for lb in range(n_slab):
        ls = slice(lb * LANES, (lb + 1) * LANES)
        ubuf_ref[lb, 0:CONV_HALO, :] = halo[:, ls]
        ubuf_ref[lb, CONV_HALO:CONV_HALO + ts, :] = u_ref[0, :, ls]
        ubuf_ref[lb, CONV_HALO + ts:, :] = jnp.zeros((SUBLANES, LANES), F32)
    lead = CONV_HALO - (CONV_WIDTH - 1)

    def slab(lb, carry):
        for r0 in range(0, ts, CONV_STRIP):
            acc = None
            for sh in range(SUBLANES):
                taps = [k for k in range(CONV_WIDTH) if (lead + k) % SUBLANES == sh]
                if not taps:
                    continue
                win = ubuf_ref[lb, pl.ds(r0 + sh, CONV_STRIP + CONV_HALO), :]
                for k in taps:
                    a8 = (lead + k) // SUBLANES * SUBLANES
                    term = win[a8:a8 + CONV_STRIP] * dw_ref[lb, k:k + 1, :]
                    acc = term if acc is None else acc + term
            cbuf_ref[lb, r0:r0 + CONV_STRIP, :] = acc + bdw_ref[lb]
        return carry

    lax.fori_loop(0, n_slab, slab, 0)
    c = jnp.concatenate([cbuf_ref[lb] for lb in range(n_slab)], axis=-1)
    mu = jnp.mean(c, axis=-1, keepdims=True)
    var = jnp.mean(jnp.square(c - mu), axis=-1, keepdims=True)
    n = (c - mu) * lax.rsqrt(var + EPS) * lng_ref[...] + lnb_ref[...]
    a = (n * jax.nn.sigmoid(n)).astype(BF16)
    o_ref[0] = x_ref[0] + _dot(a, w_ref[...]) + bo_ref[...]


def _conv_out(u, x, dw, b_dw, ln_g, ln_b, w_out, b_out, ts=ROW_TILE):
    b, s, d = x.shape
    hb = ts // CONV_HALO
    n_slab = d // LANES
    dw_pad = jnp.zeros((CONV_HALO, d), F32).at[:CONV_WIDTH].set(dw)
    dw_slab = dw_pad.reshape(CONV_HALO, n_slab, LANES).transpose(1, 0, 2)
    bdw_slab = b_dw.reshape(n_slab, 1, LANES)
    row = lambda v: v.reshape(1, d)
    return pl.pallas_call(
        functools.partial(_conv_out_body, ts=ts),
        out_shape=jax.ShapeDtypeStruct((b, s, d), F32),
        grid=(b, s // ts),
        in_specs=[pl.BlockSpec((1, ts, d), lambda bi, i: (bi, i, 0)),
                  pl.BlockSpec((1, CONV_HALO, d), lambda bi, i: (bi, jnp.maximum(i * hb - 1, 0), 0)),
                  pl.BlockSpec((1, ts, d), lambda bi, i: (bi, i, 0)),
                  _const_spec(dw_slab.shape), _const_spec(bdw_slab.shape), _const_spec((1, d)),
                  _const_spec((1, d)), _const_spec(w_out.shape), _const_spec((1, d))],
        out_specs=pl.BlockSpec((1, ts, d), lambda bi, i: (bi, i, 0)),
        scratch_shapes=[pltpu.VMEM((n_slab, ts + CONV_HALO + SUBLANES, LANES), F32),
                        pltpu.VMEM((n_slab, ts, LANES), F32)],
        compiler_params=_params(("parallel", "parallel")),
        name="conv_out",
    )(u, u, x, dw_slab, bdw_slab, row(ln_g), row(ln_b), w_out, row(b_out))


def _gla_in_body(x_ref, g_ref, wqk_ref, wv_ref, wog_ref, wgd_ref, qk_ref, v_ref, og_ref, gd_ref):
    h = _rms(x_ref[...], g_ref[...]).astype(BF16)
    qk_ref[...] = _dot(h, wqk_ref[...])
    v_ref[...] = _dot(h, wv_ref[...]).astype(BF16)
    og_ref[...] = _dot(h, wog_ref[...])
    gd_ref[...] = _dot(h, wgd_ref[...])


def _gla_in(x2, g, w_qk, w_v, w_og, w_gd, tm=ROW_TILE):
    t, d = x2.shape
    widths = (w_qk.shape[1], w_v.shape[1], w_og.shape[1], w_gd.shape[1])
    dtypes = (F32, BF16, F32, F32)
    return pl.pallas_call(
        _gla_in_body,
        out_shape=[jax.ShapeDtypeStruct((t, n), dt) for n, dt in zip(widths, dtypes)],
        grid=(t // tm,),
        in_specs=[pl.BlockSpec((tm, d), lambda i: (i, 0)), _const_spec((1, d)),
                  _const_spec(w_qk.shape), _const_spec(w_v.shape), _const_spec(w_og.shape),
                  _const_spec(w_gd.shape)],
        out_specs=[pl.BlockSpec((tm, n), lambda i: (i, 0)) for n in widths],
        compiler_params=_params(("parallel",)),
        name="gla_in",
    )(x2, g.reshape(1, d), w_qk, w_v, w_og, w_gd)


def _gla_body(q_ref, k_ref, v_ref, og_ref, gd_ref, wup_ref, bg_ref, ng_ref, x_ref, wo_ref, o_ref,
              state_ref, obuf_ref):
    @pl.when(pl.program_id(1) == 0)
    def _():
        state_ref[...] = jnp.zeros_like(state_ref)

    c, dk, dv = GLA_CHUNK, GLA_DK, GLA_DV
    ga = _dot(gd_ref[0].astype(BF16), wup_ref[...]) + bg_ref[...]
    log_a = (jnp.minimum(ga, 0.0) - jnp.log1p(jnp.exp(-jnp.abs(ga)))) / GLA_TAU
    ri = lax.broadcasted_iota(jnp.int32, (c, c), 0)
    ci = lax.broadcasted_iota(jnp.int32, (c, c), 1)
    causal = ci <= ri
    tril = jnp.where(causal, 1.0, 0.0).astype(BF16)
    scale = dk ** -0.5
    n_chunks = GLA_ROWS // c
    chunk_rows = [slice(ch * c, (ch + 1) * c) for ch in range(n_chunks)]
    cases = [(ch, h) for ch in range(n_chunks) for h in range(GLA_HEADS)]
    hi = log_a.astype(BF16)
    r1 = log_a - hi.astype(F32)
    mid = r1.astype(BF16)
    lo = (r1 - mid.astype(F32)).astype(BF16)
    q_t, k_t, k_dec, decay = [], [], [], []
    for rows in chunk_rows:
        bcum = _dot(tril, hi[rows]) + _dot(tril, mid[rows]) + _dot(tril, lo[rows])
        b_last = bcum[c - 1:c, :]
        k = k_ref[0, rows, :]
        q_t.append((q_ref[0, rows, :] * scale * jnp.exp(bcum)).astype(BF16))
        k_t.append((k * jnp.exp(-bcum)).astype(BF16))
        k_dec.append((k * jnp.exp(b_last - bcum)).astype(BF16))
        decay.append(jnp.exp(b_last))
    ks = [slice(h * dk, (h + 1) * dk) for h in range(GLA_HEADS)]
    vs = [slice(h * dv, (h + 1) * dv) for h in range(GLA_HEADS)]
    v = {(ch, h): v_ref[0, chunk_rows[ch], vs[h]] for ch, h in cases}
    a = {(ch, h): jnp.where(causal, _dot_nt(q_t[ch][:, ks[h]], k_t[ch][:, ks[h]]), 0.0).astype(BF16)
         for ch, h in cases}
    kv = {(ch, h): _dot_tn(v[ch, h], k_dec[ch][:, ks[h]]) for ch, h in cases}
    o_intra = {(ch, h): _dot(a[ch, h], v[ch, h]) for ch, h in cases}
    state = [state_ref[h] for h in range(GLA_HEADS)]
    for ch, h in cases:
        o = o_intra[ch, h] + _dot_nt(q_t[ch][:, ks[h]], state[h].astype(BF16))
        state[h] = state[h] * decay[ch][:, ks[h]] + kv[ch, h]
        o = _rms(o, ng_ref[...])
        og = og_ref[0, chunk_rows[ch], vs[h]]
        obuf_ref[chunk_rows[ch], vs[h]] = (o * (og * jax.nn.sigmoid(og))).astype(BF16)
    for h in range(GLA_HEADS):
        state_ref[h] = state[h]
    o_ref[0] = x_ref[0] + _dot(obuf_ref[...], wo_ref[...])


def _gla_core(qk, v, og, gd, w_up, b_gate, norm_g, x, w_o):
    b, s, d = x.shape
    hh, dk, dv, rt = GLA_HEADS, GLA_DK, GLA_DV, GLA_ROWS
    return pl.pallas_call(
        _gla_body,
        out_shape=jax.ShapeDtypeStruct((b, s, d), F32),
        grid=(b, s // rt),
        in_specs=[pl.BlockSpec((1, rt, hh * dk), lambda bi, i: (bi, i, 0)),
                  pl.BlockSpec((1, rt, hh * dk), lambda bi, i: (bi, i, 1)),
                  pl.BlockSpec((1, rt, hh * dv), lambda bi, i: (bi, i, 0)),
                  pl.BlockSpec((1, rt, hh * dv), lambda bi, i: (bi, i, 0)),
                  pl.BlockSpec((1, rt, LANES), lambda bi, i: (bi, i, 0)),
                  _const_spec((LANES, hh * dk)), _const_spec((1, hh * dk)), _const_spec((1, dv)),
                  pl.BlockSpec((1, rt, d), lambda bi, i: (bi, i, 0)), _const_spec(w_o.shape)],
        out_specs=pl.BlockSpec((1, rt, d), lambda bi, i: (bi, i, 0)),
        scratch_shapes=[pltpu.VMEM((hh, dv, dk), F32), pltpu.VMEM((rt, hh * dv), BF16)],
        compiler_params=_params(("parallel", "arbitrary")),
        name="gla_core",
    )(qk, qk, v, og, gd, w_up, b_gate.reshape(1, hh * dk), norm_g.reshape(1, dv), x, w_o)


def _gla_mixer(x, g, w_in, w_gate_up, b_gate, norm_g, w_o):
    b, s, d = x.shape
    qw, vw = GLA_HEADS * GLA_DK, GLA_HEADS * GLA_DV
    x2 = x.reshape(b * s, d)
    w_gd = jnp.zeros((d, LANES), BF16).at[:, :GLA_GATE_RANK].set(w_in[:, 2 * qw + 2 * vw:].astype(BF16))
    w_up = jnp.zeros((LANES, qw), BF16).at[:GLA_GATE_RANK].set(w_gate_up.astype(BF16))
    qk, v, og, gd = _gla_in(x2, g, w_in[:, :2 * qw].astype(BF16), w_in[:, 2 * qw:2 * qw + vw].astype(BF16),
                            w_in[:, 2 * qw + vw:2 * qw + 2 * vw].astype(BF16), w_gd)
    r3 = lambda a: a.reshape(b, s, a.shape[-1])
    return _gla_core(r3(qk), r3(v), r3(og), r3(gd), w_up, b_gate, norm_g, x, w_o.astype(BF16))


def _nsa_in_body(x_ref, g_ref, wqt_ref, wk_ref, wvt_ref, wc_ref, wgt_ref,
                 qt_ref, kas_ref, kaw_ref, vast_ref, vawt_ref, c_ref, gtt_ref, *, ts):
    i = pl.program_id(1)
    qt, groups = NSA_QT, NSA_KV_HEADS
    h = _rms(x_ref[0], g_ref[...]).astype(BF16)
    q_all = (_dot_nt(wqt_ref[...], h) * (NSA_HEAD_DIM ** -0.5 * LOG2E)).astype(BF16)
    hw = NSA_REP * NSA_HEAD_DIM
    for gi in range(groups):
        for tt in range(ts // qt):
            qt_ref[0, gi, tt] = q_all[gi * hw:(gi + 1) * hw, tt * qt:(tt + 1) * qt]
    k = _dot(h, wk_ref[...])
    gw = groups * LANES
    lane = lax.broadcasted_iota(jnp.int32, (ts, LANES), 1)
    t = i * ts + lax.broadcasted_iota(jnp.int32, (ts, LANES), 0)
    onehot = jnp.where(lane - NSA_HEAD_DIM == t // NSA_SEL_BLOCK, 1.0, 0.0)
    for gi in range(groups):
        ls = slice(gi * LANES, (gi + 1) * LANES)
        kas_ref[0, gi] = (k[:, ls] + onehot).astype(BF16)
        kaw_ref[0, gi] = k[:, gw + gi * LANES:gw + (gi + 1) * LANES].astype(BF16)
    vt = _dot_nt(wvt_ref[...], h)
    ones = jnp.where(lax.broadcasted_iota(jnp.int32, (LANES, qt), 0) >= NSA_HEAD_DIM, 1.0, 0.0)
    for gi in range(groups):
        for tt in range(ts // qt):
            cs = slice(tt * qt, (tt + 1) * qt)
            vast_ref[0, gi, tt] = (vt[gi * LANES:(gi + 1) * LANES, cs] + ones).astype(BF16)
            vawt_ref[0, gi, tt] = (vt[gw + gi * LANES:gw + (gi + 1) * LANES, cs] + ones).astype(BF16)
    c = _dot(h, wc_ref[...])
    for gi in range(groups):
        c_ref[0, gi] = c[:, gi * LANES:(gi + 1) * LANES]
    gtt_ref[0] = _dot_nt(wgt_ref[...], h)


def _nsa_in(x, g, w_qt, w_k, w_vt, w_c, w_gt, ts=ROW_TILE):
    b, s, d = x.shape
    groups, qt = NSA_KV_HEADS, NSA_QT
    gw = groups * LANES
    nkt = s // qt
    tpb = ts // qt
    seq_spec = lambda n: pl.BlockSpec((1, ts, n), lambda bi, i: (bi, i, 0))
    rows_spec = lambda n: pl.BlockSpec((1, n, ts), lambda bi, i: (bi, 0, i))
    vt_spec = pl.BlockSpec((1, groups, tpb, LANES, qt), lambda bi, i: (bi, 0, i, 0, 0))
    vt_shape = jax.ShapeDtypeStruct((b, groups, nkt, LANES, qt), BF16)
    k_spec = pl.BlockSpec((1, groups, ts, LANES), lambda bi, i: (bi, 0, i, 0))
    k_shape = jax.ShapeDtypeStruct((b, groups, s, LANES), BF16)
    return pl.pallas_call(
        functools.partial(_nsa_in_body, ts=ts),
        out_shape=[jax.ShapeDtypeStruct((b, groups, nkt, w_qt.shape[0] // groups, qt), BF16),
                   k_shape, k_shape,
                   vt_shape, vt_shape,
                   jax.ShapeDtypeStruct((b, groups, s, LANES), F32),
                   jax.ShapeDtypeStruct((b, w_gt.shape[0], s), F32)],
        grid=(b, s // ts),
        in_specs=[pl.BlockSpec((1, ts, d), lambda bi, i: (bi, i, 0)), _const_spec((1, d)),
                  _const_spec(w_qt.shape), _const_spec(w_k.shape), _const_spec(w_vt.shape),
                  _const_spec(w_c.shape), _const_spec(w_gt.shape)],
        out_specs=[pl.BlockSpec((1, groups, tpb, w_qt.shape[0] // groups, qt), lambda bi, i: (bi, 0, i, 0, 0)),
                   k_spec, k_spec, vt_spec, vt_spec,
                   k_spec, rows_spec(w_gt.shape[0])],
        compiler_params=_params(("parallel", "parallel")),
        name="nsa_in",
    )(x, g.reshape(1, d), w_qt, w_k, w_vt, w_c, w_gt)


def _nsa_cmp_body(c_ref, pos_ref, w1t_ref, w1b_ref, w2k_ref, w2vt_ref, kc_ref, vct_ref):
    stride = NSA_CMP_STRIDE
    nch = c_ref.shape[2] // stride
    u = v = None
    for p in range(stride):
        x = c_ref[0, 0, pl.ds(p, nch, stride=stride), :]
        up = _dot((x + pos_ref[p:p + 1, :]).astype(BF16), w1t_ref[p])
        vp = _dot((x + pos_ref[stride + p:stride + p + 1, :]).astype(BF16), w1b_ref[p])
        u = up if u is None else u + up
        v = vp if v is None else v + vp
    hid = jax.nn.gelu(u + pltpu.roll(v, nch - 1, 0), approximate=True).astype(BF16)
    hw = hid.shape[-1] // 2
    kc_ref[0, 0] = _dot(hid[:, :hw], w2k_ref[...])
    vct_ref[0, 0] = _dot_nt(w2vt_ref[...], hid[:, hw:])


def _nsa_compress(c, cmp_pos, w1, w2):
    b, g, s, _ = c.shape
    dh, hidden = w2.shape[-1], w2.shape[1]
    stride = NSA_CMP_STRIDE
    nch = s // stride
    pos = jnp.concatenate([cmp_pos[0], cmp_pos[1]], axis=-1)
    w1r = w1.astype(BF16).reshape(2, NSA_CMP_BLOCK, dh, hidden)
    zero = jnp.zeros((NSA_CMP_BLOCK, dh, hidden), BF16)
    w1bd = jnp.concatenate([jnp.concatenate([w1r[0], zero], axis=-1),
                            jnp.concatenate([zero, w1r[1]], axis=-1)], axis=1)
    return pl.pallas_call(
        _nsa_cmp_body,
        out_shape=[jax.ShapeDtypeStruct((b, g, nch, dh), F32), jax.ShapeDtypeStruct((b, g, dh, nch), F32)],
        grid=(b, g),
        in_specs=[pl.BlockSpec((1, 1, s, 2 * dh), lambda bi, gi: (bi, gi, 0, 0)),
                  _const_spec(pos.shape), _const_spec((stride, 2 * dh, 2 * hidden)),
                  _const_spec((stride, 2 * dh, 2 * hidden)), _const_spec((hidden, dh)), _const_spec((dh, hidden))],
        out_specs=[pl.BlockSpec((1, 1, nch, dh), lambda bi, gi: (bi, gi, 0, 0)),
                   pl.BlockSpec((1, 1, dh, nch), lambda bi, gi: (bi, gi, 0, 0))],
        compiler_params=_params(("parallel", "parallel")),
        name="nsa_compress",
    )(c, pos, w1bd[:stride], w1bd[stride:], w2[0].astype(BF16), w2[1].astype(BF16).T)


def _nsa_attn_body(qt_ref, kc_ref, vct_ref, kas_ref, vast_ref, kaw_ref, vawt_ref, gtt_ref, ovt_ref, o_ref,
                   qaug_ref, m_ref, acc_ref, p_ref, alpha_ref, s_ref, rank_ref, *, n_slc):
    qt, rep, dh = NSA_QT, NSA_REP, NSA_HEAD_DIM
    cols = rep * qt
    qi = pl.program_id(2)
    s0 = qi * qt
    q = jnp.concatenate([qt_ref[0, 0, 0, r * dh:(r + 1) * dh, :] for r in range(rep)], axis=1)
    t_col = s0 + (lax.broadcasted_iota(jnp.int32, (1, cols), 1) & (qt - 1))

    n_pad = kc_ref.shape[2]
    sc = _dot(kc_ref[0, 0].astype(BF16), q)
    cmp_end = lax.broadcasted_iota(jnp.int32, (n_pad, 1), 0) * NSA_CMP_STRIDE + (NSA_CMP_BLOCK - 1)
    sc = jnp.where(cmp_end <= t_col, sc, NEG)
    e = jnp.exp2(sc - jnp.max(sc, axis=0, keepdims=True))
    p_cmp = e * (1.0 / jnp.sum(e, axis=0, keepdims=True))
    p_cmp = jnp.where(t_col >= NSA_CMP_BLOCK - 1, p_cmp, 0.0).astype(BF16)
    o_cmp = _dot(vct_ref[0, 0].astype(BF16), p_cmp)

    heads = [slice(r * qt, (r + 1) * qt) for r in range(rep)]

    wt = NSA_WINDOW // qt + 1
    lo = jnp.clip(qi - (wt - 1), 0, kaw_ref.shape[2] // qt - wt)
    kw = kaw_ref[0, 0, pl.ds(pl.multiple_of(lo * qt, qt), wt * qt), 0:dh]
    kpos = lo * qt + lax.broadcasted_iota(jnp.int32, (wt * qt, 1), 0)
    t_q = t_col[:, 0:qt]
    wbias = jnp.where((kpos <= t_q) & (kpos > t_q - NSA_WINDOW), 0.0, NEG)
    w_scores = [_dot(kw, q[:, cs]) for cs in heads]
    w_probs = []
    for s in w_scores:
        s = s + wbias
        w_probs.append(jnp.exp2(s - jnp.max(s, axis=0, keepdims=True)).astype(BF16))
    o_win = []
    for p in w_probs:
        pv = _dot(vawt_ref[0, 0, lo], p[0:qt])
        for u in range(1, wt):
            pv = pv + _dot(vawt_ref[0, 0, lo + u], p[u * qt:(u + 1) * qt])
        o_win.append(pv[0:dh] / pv[dh:dh + 1])
    o_win = jnp.concatenate(o_win, axis=1)

    imp = _dot(ovt_ref[...], p_cmp[:, 0:qt])
    for r in range(1, rep):
        imp = imp + _dot(ovt_ref[...], p_cmp[:, r * qt:(r + 1) * qt])
    imp_t = imp[:n_slc]
    blk = lax.broadcasted_iota(jnp.int32, (n_slc, 1), 0)
    cur = t_q // NSA_SEL_BLOCK
    forced = (blk == 0) | (blk == cur) | (blk == cur - 1)
    visible = blk * NSA_SEL_BLOCK <= t_q
    imp_t = jnp.where(forced, NSA_FORCE, imp_t)
    imp_t = jnp.where(visible, imp_t, -1.0)
    groups = [imp_t[k:k + SUBLANES] for k in range(0, n_slc, SUBLANES)]
    sub = lax.broadcasted_iota(jnp.int32, (SUBLANES, 1), 0)
    rank_ref[...] = jnp.zeros_like(rank_ref)
    last_blk = (s0 + qt - 1) // NSA_SEL_BLOCK
    for gi, own in enumerate(groups):
        @pl.when(gi * SUBLANES <= last_blk)
        def _(gi=gi, own=own):
            incs = [jnp.zeros((SUBLANES, qt), F32) for _ in groups]
            for si in range(SUBLANES):
                row = own[si:si + 1, :]
                for k, blk_imp in enumerate(groups):
                    if k < gi:
                        inc = jnp.where(row > blk_imp, 1.0, 0.0)
                    elif k > gi:
                        inc = jnp.where(row >= blk_imp, 1.0, 0.0)
                    else:
                        inc = jnp.where(sub > si, jnp.where(row >= blk_imp, 1.0, 0.0),
                                        jnp.where(row > blk_imp, 1.0, 0.0))
                    incs[k] = incs[k] + inc
            for k, inc in enumerate(incs):
                rank_ref[k * SUBLANES:(k + 1) * SUBLANES, :] += inc
    sel = (rank_ref[0:n_slc, :] < float(min(NSA_N_SELECT, n_slc))) & visible
    bias_parts = [jnp.where(sel, 0.0, NEG)]
    if LANES - dh - n_slc:
        bias_parts.append(jnp.zeros((LANES - dh - n_slc, qt), F32))
    bias = jnp.concatenate(bias_parts, axis=0).astype(BF16)
    qaug_ref[0:dh, :] = q
    for r in range(rep):
        qaug_ref[dh:, r * qt:(r + 1) * qt] = bias

    def score(ka_ref, j, nt):
        start = j * (nt * qt)
        if not isinstance(start, int):
            start = pl.multiple_of(start, nt * qt)
        ka = ka_ref[0, 0, pl.ds(start, nt * qt), :]
        return [_dot(ka, qaug_ref[:, cs]) for cs in heads]

    def flush(vat_ref, jp, nt):
        pvs = []
        for cs in heads:
            pv = _dot(vat_ref[0, 0, jp * nt], p_ref[0:qt, cs])
            for u in range(1, nt):
                pv = pv + _dot(vat_ref[0, 0, jp * nt + u], p_ref[u * qt:(u + 1) * qt, cs])
            pvs.append(pv)
        return pvs

    def accumulate(pvs):
        for cs, pv in zip(heads, pvs):
            acc_ref[:, cs] = acc_ref[:, cs] * alpha_ref[:, cs] + pv

    def step(ka_ref, vat_ref, j, nt, mask_fn, has_next):
        rows = nt * qt
        nxt = score(ka_ref, j + 1, nt) if has_next else None
        pvs = flush(vat_ref, jnp.maximum(j - 1, 0), nt)
        probs, alphas = [], []
        for cs in heads:
            s = s_ref[0:rows, cs]
            if mask_fn is not None:
                kpos = j * rows + lax.broadcasted_iota(jnp.int32, (rows, 1), 0)
                s = jnp.where(mask_fn(kpos, t_col[:, cs]), s, NEG)
            m_old = m_ref[:, cs]
            m_new = jnp.maximum(m_old, jnp.max(s, axis=0, keepdims=True))
            probs.append(jnp.exp2(s - m_new).astype(BF16))
            alphas.append(jnp.exp2(m_old - m_new))
            m_ref[:, cs] = m_new
        accumulate(pvs)
        for cs, p, alpha in zip(heads, probs, alphas):
            p_ref[0:rows, cs] = p
            alpha_ref[:, cs] = alpha
        if has_next:
            for cs, s in zip(heads, nxt):
                s_ref[0:rows, cs] = s

    def attend(ka_ref, vat_ref, nt, lo, hi, mask_fn, last_mask_fn):
        rows = nt * qt
        m_ref[...] = jnp.full_like(m_ref, NEG)
        acc_ref[...] = jnp.zeros_like(acc_ref)
        p_ref[0:rows, :] = jnp.zeros((rows, cols), BF16)
        alpha_ref[...] = jnp.ones_like(alpha_ref)
        for cs, s in zip(heads, score(ka_ref, lo, nt)):
            s_ref[0:rows, cs] = s

        def body(j, carry):
            step(ka_ref, vat_ref, j, nt, mask_fn, True)
            return carry

        lax.fori_loop(lo, hi, body, 0)
        step(ka_ref, vat_ref, hi, nt, last_mask_fn, False)
        accumulate(flush(vat_ref, hi, nt))
        acc = acc_ref[...]
        return acc[0:dh] / acc[dh:dh + 1]

    o_slc = attend(kas_ref, vast_ref, NSA_SEL_SPAN, 0, qi // NSA_SEL_SPAN, None, lambda kpos, t: kpos <= t)

    gates = jax.nn.sigmoid(gtt_ref[0])
    for r in range(rep):
        cs = slice(r * qt, (r + 1) * qt)
        o = (gates[3 * r:3 * r + 1] * o_cmp[:, cs] + gates[3 * r + 1:3 * r + 2] * o_slc[:, cs]
             + gates[3 * r + 2:3 * r + 3] * o_win[:, cs])
        o_ref[0, 0, 0, r * dh:(r + 1) * dh, :] = o.astype(BF16)


def _nsa_attn(q_t, kcmp, vcmp_t, kas, vas_t, kaw, vaw_t, gates_t, ov_t):
    b, g, nkt, _, qt = q_t.shape
    rep, dh = NSA_REP, NSA_HEAD_DIM
    s = nkt * qt
    n_slc = s // NSA_SEL_BLOCK
    nch = kcmp.shape[2]
    k_spec = pl.BlockSpec((1, 1, s, LANES), lambda bi, gi, i: (bi, gi, 0, 0))
    vt_spec = pl.BlockSpec((1, 1, nkt, LANES, qt), lambda bi, gi, i: (bi, gi, 0, 0, 0))
    q_spec = pl.BlockSpec((1, 1, 1, rep * dh, qt), lambda bi, gi, i: (bi, gi, i, 0, 0))
    return pl.pallas_call(
        functools.partial(_nsa_attn_body, n_slc=n_slc),
        out_shape=jax.ShapeDtypeStruct(q_t.shape, BF16),
        grid=(b, g, s // qt),
        in_specs=[q_spec,
                  pl.BlockSpec((1, 1, nch, dh), lambda bi, gi, i: (bi, gi, 0, 0)),
                  pl.BlockSpec((1, 1, dh, nch), lambda bi, gi, i: (bi, gi, 0, 0)),
                  k_spec, vt_spec, k_spec, vt_spec,
                  pl.BlockSpec((1, NSA_GATE_ROWS, qt), lambda bi, gi, i: (bi, gi, i)),
                  _const_spec(ov_t.shape)],
        out_specs=q_spec,
        scratch_shapes=[pltpu.VMEM((LANES, rep * qt), BF16), pltpu.VMEM((1, rep * qt), F32),
                        pltpu.VMEM((LANES, rep * qt), F32), pltpu.VMEM((NSA_SEL_SPAN * qt, rep * qt), BF16),
                        pltpu.VMEM((1, rep * qt), F32), pltpu.VMEM((NSA_SEL_SPAN * qt, rep * qt), F32),
                        pltpu.VMEM((LANES - dh, qt), F32)],
        compiler_params=_params(("parallel", "parallel", "parallel")),
        name="nsa_attn",
    )(q_t, kcmp, vcmp_t, kas, vas_t, kaw, vaw_t, gates_t, ov_t)


def _nsa_overlap_t(s):
    n_cmp = (s - NSA_CMP_BLOCK) // NSA_CMP_STRIDE + 1
    n_slc = s // NSA_SEL_BLOCK
    cs = np.arange(n_cmp) * NSA_CMP_STRIDE
    ss = np.arange(n_slc) * NSA_SEL_BLOCK
    ov = np.clip(np.minimum(cs[:, None] + NSA_CMP_BLOCK, ss[None, :] + NSA_SEL_BLOCK)
                 - np.maximum(cs[:, None], ss[None, :]), 0, None) / NSA_CMP_BLOCK
    out = np.zeros((LANES, s // NSA_CMP_STRIDE), np.float32)
    out[:n_slc, :n_cmp] = ov.T
    return jnp.asarray(out, BF16)


def _mm_res_t_body(at_ref, w_ref, x_ref, o_ref):
    groups, tiles, kw, qt = at_ref.shape[1:]
    for tt in range(tiles):
        rows = slice(tt * qt, (tt + 1) * qt)
        y = x_ref[0, rows, :]
        for gi in range(groups):
            y = y + _dot_tn(at_ref[0, gi, tt], w_ref[gi * kw:(gi + 1) * kw, :])
        o_ref[0, rows, :] = y


def _mm_res_t(a_t, w, x, ts=ROW_TILE):
    b, groups, nkt, kw, qt = a_t.shape
    s, n = nkt * qt, w.shape[1]
    tpb = ts // qt
    return pl.pallas_call(
        _mm_res_t_body,
        out_shape=jax.ShapeDtypeStruct((b, s, n), F32),
        grid=(b, s // ts),
        in_specs=[pl.BlockSpec((1, groups, tpb, kw, qt), lambda bi, i: (bi, 0, i, 0, 0)),
                  _const_spec(w.shape), pl.BlockSpec((1, ts, n), lambda bi, i: (bi, i, 0))],
        out_specs=pl.BlockSpec((1, ts, n), lambda bi, i: (bi, i, 0)),
        compiler_params=_params(("parallel", "parallel")),
        name="mm_res_t",
    )(a_t, w, x)


def _nsa_mixer(x, g, w_in, cmp_pos, cmp_w1, cmp_w2, w_o):
    b, s, d = x.shape
    hh, gg, rep, dh = NSA_HEADS, NSA_KV_HEADS, NSA_REP, NSA_HEAD_DIM
    kvw = gg * dh
    assert s % (NSA_SEL_SPAN * NSA_QT) == 0 and s % ROW_TILE == 0
    assert s // NSA_SEL_BLOCK <= LANES - dh and (s // NSA_SEL_BLOCK) % SUBLANES == 0
    wb = w_in.astype(BF16)
    parts = [wb[:, d + i * kvw:d + (i + 1) * kvw].reshape(d, gg, dh) for i in range(6)]
    zeros = jnp.zeros((d, gg, LANES - dh), BF16)
    padded = lambda p: jnp.concatenate([p, zeros], axis=-1).reshape(d, gg * LANES)
    w_k = jnp.concatenate([padded(parts[2]), padded(parts[4])], axis=-1)
    w_vt = jnp.concatenate([padded(parts[3]), padded(parts[5])], axis=-1).T
    w_c = jnp.concatenate([parts[0], parts[1]], axis=-1).reshape(d, gg * 2 * dh)
    w_g = wb[:, d + 6 * kvw:].reshape(d, gg, 3 * rep)
    w_gt = jnp.concatenate([w_g, jnp.zeros((d, gg, NSA_GATE_ROWS - 3 * rep), BF16)], axis=-1)
    w_gt = w_gt.reshape(d, gg * NSA_GATE_ROWS).T
    q_t, kas, kaw, vas_t, vaw_t, c, gates_t = _nsa_in(x, g, wb[:, :d].T, w_k, w_vt, w_c, w_gt)

    k_cmp, v_cmp_t = _nsa_compress(c, cmp_pos, cmp_w1, cmp_w2)
    o_t = _nsa_attn(q_t, k_cmp, v_cmp_t, kas, vas_t, kaw, vaw_t, gates_t, _nsa_overlap_t(s))
    return _mm_res_t(o_t, w_o.astype(BF16), x)


def kernel(x, mem, ffn1_norm, ffn1_w_in, ffn1_w_out, mix_norm, xattn_norm, mem_norm, xattn_w_q, xattn_w_kv, xattn_w_o, ffn2_norm, ffn2_w_in, ffn2_w_out, pool_w, pool_b, pool_scale, nsa_w_in, nsa_cmp_pos, nsa_cmp_w1, nsa_cmp_w2, nsa_w_o, gla_w_in, gla_w_gate_up, gla_b_gate, gla_norm, gla_w_o, conv_w_in, conv_b_in, conv_dw, conv_b_dw, conv_ln_g, conv_ln_b, conv_w_out, conv_b_out, final_norm):
    b, s, d = x.shape
    n_mem = mem.shape[1]
    depth = ffn1_norm.shape[0]
    n_mixers = 4
    flat = lambda a: a.reshape(b * s, d)
    cube = lambda a: a.reshape(b, s, d)
    mem2 = mem.reshape(b * n_mem, d)
    ffn1_w_in, ffn1_w_out, ffn2_w_in, ffn2_w_out, xattn_w_q, xattn_w_kv, xattn_w_o = (
        w.astype(BF16) for w in (ffn1_w_in, ffn1_w_out, ffn2_w_in, ffn2_w_out, xattn_w_q, xattn_w_kv, xattn_w_o))
    for i in range(depth):
        m, j = i % n_mixers, i // n_mixers
        x = cube(_ffn(flat(x), ffn1_norm[i], ffn1_w_in, ffn1_w_out, i))
        if m == 0:
            x = _pool_mixer(x, mix_norm[i], pool_w[j].astype(BF16), pool_b[j], pool_scale[j])
        elif m == 1:
            x = _nsa_mixer(x, mix_norm[i], nsa_w_in[j], nsa_cmp_pos[j], nsa_cmp_w1[j], nsa_cmp_w2[j], nsa_w_o[j])
        elif m == 2:
            x = _gla_mixer(x, mix_norm[i], gla_w_in[j], gla_w_gate_up[j], gla_b_gate[j], gla_norm[j], gla_w_o[j])
        else:
            u = _conv_in(flat(x), mix_norm[i], conv_w_in[j].astype(BF16), conv_b_in[j])
            x = _conv_out(cube(u), x, conv_dw[j], conv_b_dw[j], conv_ln_g[j], conv_ln_b[j],
                          conv_w_out[j].astype(BF16), conv_b_out[j])
        kv = _rms_mm(mem2, mem_norm[i], xattn_w_kv, i, BF16, tm=n_mem)
        x = _xattn(x, xattn_norm[i], xattn_w_q, kv.reshape(b, n_mem, 2 * d), xattn_w_o, i)
        final_g = final_norm if i == depth - 1 else None
        x = cube(_ffn(flat(x), ffn2_norm[i], ffn2_w_in, ffn2_w_out, i, final_g=final_g))
    return x
```

```python
import functools

import numpy as np
import jax
import jax.numpy as jnp
from jax import lax
from jax.experimental import pallas as pl
from jax.experimental.pallas import tpu as pltpu

F32 = jnp.float32
BF16 = jnp.bfloat16

EPS = 1e-6
NEG = -1e30

V7X_VMEM_BYTES = 64 * 1024 * 1024
VMEM_LIMIT = V7X_VMEM_BYTES - 8 * 1024 * 1024
LANES = 128
SUBLANES = 8

POOL_WINDOWS = (2, 4, 8, 16)
POOL_HALO = 16

NSA_HEADS = 16
NSA_KV_HEADS = 4
NSA_REP = NSA_HEADS // NSA_KV_HEADS
NSA_HEAD_DIM = 64
NSA_CMP_BLOCK = 32
NSA_CMP_STRIDE = 16
NSA_SEL_BLOCK = 64
NSA_N_SELECT = 16
NSA_WINDOW = 512
NSA_FORCE = 1e4
NSA_QT = 256
NSA_GATE_ROWS = 16
NSA_SEL_SPAN = 2
LOG2E = 1.4426950408889634

GLA_HEADS = 4
GLA_DK = 128
GLA_DV = 256
GLA_GATE_RANK = 16
GLA_TAU = 16.0
GLA_CHUNK = 64
GLA_ROWS = 512

CONV_WIDTH = 31
CONV_HALO = 32
CONV_STRIP = 128

XATTN_HEADS = 4

ROW_TILE = 1024
FFN_ROW_TILE = 512
FFN_CHUNKS = 1


def _params(sem):
    return pltpu.CompilerParams(dimension_semantics=sem, vmem_limit_bytes=VMEM_LIMIT)


def _rms(x, g):
    return x * lax.rsqrt(jnp.mean(x * x, axis=-1, keepdims=True) + EPS) * g


def _dot(a, b):
    return jnp.dot(a, b, preferred_element_type=F32)


def _dot_nt(a, b):
    return lax.dot_general(a, b, (((1,), (1,)), ((), ())), preferred_element_type=F32)


def _dot_tn(a, b):
    return lax.dot_general(a, b, (((0,), (0,)), ((), ())), preferred_element_type=F32)


def _const_spec(shape):
    nd = len(shape)
    return pl.BlockSpec(shape, lambda *_: (0,) * nd, pipeline_mode=pl.Buffered(1))


def _layer_spec(stack_shape, layer):
    return pl.BlockSpec((None,) + tuple(stack_shape[1:]), lambda *_: (layer, 0, 0), pipeline_mode=pl.Buffered(1))


def _ffn_body(x_ref, g_ref, win_ref, wout_ref, *rest, d_ff, final):
    if final:
        fg_ref, o_ref = rest
    else:
        (o_ref,) = rest
    x = x_ref[...]
    h = _rms(x, g_ref[...]).astype(BF16)
    fc = d_ff // FFN_CHUNKS
    y = None
    for c in range(FFN_CHUNKS):
        g = _dot(h, win_ref[:, c * fc:(c + 1) * fc])
        u = _dot(h, win_ref[:, d_ff + c * fc:d_ff + (c + 1) * fc])
        a = (g * jax.nn.sigmoid(g) * u).astype(BF16)
        yc = _dot(a, wout_ref[c * fc:(c + 1) * fc, :])
        y = yc if y is None else y + yc
    out = x + 0.5 * y
    if final:
        out = _rms(out, fg_ref[...])
    o_ref[...] = out


def _ffn(x2, g, w_in, w_out, layer, final_g=None, tm=FFN_ROW_TILE):
    t, d = x2.shape
    d_ff = w_out.shape[1]
    final = final_g is not None
    in_specs = [pl.BlockSpec((tm, d), lambda i: (i, 0)), _const_spec((1, d)),
                _layer_spec(w_in.shape, layer), _layer_spec(w_out.shape, layer)]
    args = [x2, g.reshape(1, d), w_in, w_out]
    if final:
        in_specs.append(_const_spec((1, d)))
        args.append(final_g.reshape(1, d))
    return pl.pallas_call(
        functools.partial(_ffn_body, d_ff=d_ff, final=final),
        out_shape=jax.ShapeDtypeStruct((t, d), F32),
        grid=(t // tm,),
        in_specs=in_specs,
        out_specs=pl.BlockSpec((tm, d), lambda i: (i, 0)),
        compiler_params=_params(("parallel",)),
        name="ffn",
    )(*args)


def _rms_mm_body(x_ref, g_ref, w_ref, o_ref):
    h = _rms(x_ref[...], g_ref[...]).astype(BF16)
    o_ref[...] = _dot(h, w_ref[...]).astype(o_ref.dtype)


def _rms_mm_layers(x2, g, w, out_dtype, tm):
    t, d = x2.shape
    layers, _, n = w.shape
    return pl.pallas_call(
        _rms_mm_body,
        out_shape=jax.ShapeDtypeStruct((layers, t, n), out_dtype),
        grid=(layers, t // tm),
        in_specs=[pl.BlockSpec((tm, d), lambda l, i: (i, 0)),
                  pl.BlockSpec((None, 1, d), lambda l, i: (l, 0, 0)),
                  pl.BlockSpec((None, d, n), lambda l, i: (l, 0, 0))],
        out_specs=pl.BlockSpec((None, tm, n), lambda l, i: (l, i, 0)),
        compiler_params=_params(("parallel", "parallel")),
        name="rms_mm",
    )(x2, g.reshape(layers, 1, d), w)


def _xattn_body(x_ref, g_ref, wq_ref, k_ref, v_ref, wo_ref, o_ref, *, heads):
    x = x_ref[0]
    d = x.shape[-1]
    dh = d // heads
    h = _rms(x, g_ref[...]).astype(BF16)
    q = (_dot(h, wq_ref[...]) * dh ** -0.5).astype(BF16)
    cols = [slice(hd * dh, (hd + 1) * dh) for hd in range(heads)]
    scores = [_dot_nt(q[:, sl], k_ref[0][:, sl]) for sl in cols]
    probs = []
    for s in scores:
        e = jnp.exp(s - jnp.max(s, axis=-1, keepdims=True))
        probs.append((e * (1.0 / jnp.sum(e, axis=-1, keepdims=True))).astype(BF16))
    o = jnp.concatenate([_dot(p, v_ref[0][:, sl]).astype(BF16) for p, sl in zip(probs, cols)], axis=-1)
    o_ref[0] = x + _dot(o, wo_ref[...])


def _xattn(x, g, w_q, kv, w_o, layer, ts=ROW_TILE):
    b, s, d = x.shape
    n = kv.shape[2]
    return pl.pallas_call(
        functools.partial(_xattn_body, heads=XATTN_HEADS),
        out_shape=jax.ShapeDtypeStruct((b, s, d), F32),
        grid=(b, s // ts),
        in_specs=[pl.BlockSpec((1, ts, d), lambda bi, i: (bi, i, 0)), _const_spec((1, d)),
                  _layer_spec(w_q.shape, layer),
                  pl.BlockSpec((None, 1, n, d), lambda bi, i: (layer, bi, 0, 0)),
                  pl.BlockSpec((None, 1, n, d), lambda bi, i: (layer, bi, 0, 1)),
                  _layer_spec(w_o.shape, layer)],
        out_specs=pl.BlockSpec((1, ts, d), lambda bi, i: (bi, i, 0)),
        compiler_params=_params(("parallel", "parallel")),
        name="xattn",
    )(x, g.reshape(1, d), w_q, kv, kv, w_o)


def _pool_body(x_ref, halo_ref, g_ref, w_ref, b_ref, sc_ref, o_ref, hbuf_ref, l1_ref, l2_ref, *, ts):
    i = pl.program_id(1)
    x = x_ref[0]
    d = x.shape[-1]
    pad = POOL_HALO
    gw = d // len(POOL_WINDOWS)
    n = ts + pad
    g = g_ref[...]
    h = _rms(x, g)
    hh = _rms(halo_ref[0], g)
    hbuf_ref[0:pad, :] = jnp.zeros((pad, d), F32)
    l1_ref[0:pad, :] = jnp.zeros((pad, d - gw), F32)
    l2_ref[0:pad, :] = jnp.zeros((pad, d - 2 * gw), F32)
    hbuf_ref[pad:2 * pad, :] = jnp.where(i > 0, hh, 0.0)
    hbuf_ref[2 * pad:, :] = h
    a1 = hbuf_ref[pad:pad + n, :] + hbuf_ref[pad - 1:pad - 1 + n, :]
    l1_ref[pad:pad + n, :] = a1[:, gw:]
    a2 = a1[:, gw:] + l1_ref[pad - 2:pad - 2 + n, :]
    l2_ref[pad:pad + n, :] = a2[:, gw:]
    a3 = a2[:, gw:] + l2_ref[pad - 4:pad - 4 + n, :]
    a3_last = a3[:, gw:]
    a4 = a3_last[SUBLANES:] + a3_last[:n - SUBLANES]
    sums = [a1[pad:, 0:gw], a2[pad:, 0:gw], a3[pad:, 0:gw], a4[pad - SUBLANES:, :]]
    t = i * ts + lax.broadcasted_iota(jnp.int32, (ts, 1), 0)
    ys = []
    for gi, win in enumerate(POOL_WINDOWS):
        cs = slice(gi * gw, (gi + 1) * gw)
        cnt = jnp.minimum(t + 1, win).astype(F32)
        p = sums[gi] / cnt - h[:, cs]
        ys.append(_dot(p.astype(BF16), w_ref[gi]))
    y = (jnp.concatenate(ys, axis=-1) + b_ref[...]) * sc_ref[...]
    o_ref[0] = x + y


def _pool_mixer(x, g, w, bias, scale, ts=ROW_TILE):
    b, s, d = x.shape
    hb = ts // POOL_HALO
    return pl.pallas_call(
        functools.partial(_pool_body, ts=ts),
        out_shape=jax.ShapeDtypeStruct((b, s, d), F32),
        grid=(b, s // ts),
        in_specs=[pl.BlockSpec((1, ts, d), lambda bi, i: (bi, i, 0)),
                  pl.BlockSpec((1, POOL_HALO, d), lambda bi, i: (bi, jnp.maximum(i * hb - 1, 0), 0)),
                  _const_spec((1, d)), _const_spec(w.shape), _const_spec((1, d)), _const_spec((1, d))],
        out_specs=pl.BlockSpec((1, ts, d), lambda bi, i: (bi, i, 0)),
        scratch_shapes=[pltpu.VMEM((ts + 2 * POOL_HALO, d), F32),
                        pltpu.VMEM((ts + 2 * POOL_HALO, d - d // len(POOL_WINDOWS)), F32),
                        pltpu.VMEM((ts + 2 * POOL_HALO, d - 2 * (d // len(POOL_WINDOWS))), F32)],
        compiler_params=_params(("parallel", "parallel")),
        name="pool",
    )(x, x, g.reshape(1, d), w, bias.reshape(1, d), scale.reshape(1, d))


def _conv_in_body(x_ref, g_ref, w_ref, b_ref, o_ref):
    d = x_ref.shape[-1]
    h = _rms(x_ref[...], g_ref[...]).astype(BF16)
    ag = _dot(h, w_ref[...]) + b_ref[...]
    o_ref[...] = ag[:, :d] * jax.nn.sigmoid(ag[:, d:])


def _conv_in(x2, g, w, bias, tm=ROW_TILE):
    t, d = x2.shape
    return pl.pallas_call(
        _conv_in_body,
        out_shape=jax.ShapeDtypeStruct((t, d), F32),
        grid=(t // tm,),
        in_specs=[pl.BlockSpec((tm, d), lambda i: (i, 0)), _const_spec((1, d)),
                  _const_spec(w.shape), _const_spec((1, 2 * d))],
        out_specs=pl.BlockSpec((tm, d), lambda i: (i, 0)),
        compiler_params=_params(("parallel",)),
        name="conv_in",
    )(x2, g.reshape(1, d), w, bias.reshape(1, 2 * d))


def _conv_out_body(u_ref, halo_ref, x_ref, dw_ref, bdw_ref, lng_ref, lnb_ref, w_ref, bo_ref, o_ref,
                   ubuf_ref, cbuf_ref, *, ts):
    i = pl.program_id(1)
    n_slab = ubuf_ref.shape[0]
    halo = jnp.where(i > 0, halo_ref[0], 0.0)
    for lb in range(n_slab):
        ls = slice(lb * LANES, (lb + 1) * LANES)
        ubuf_ref[lb, 0:CONV_HALO, :] = halo[:, ls]
        ubuf_ref[lb, CONV_HALO:CONV_HALO + ts, :] = u_ref[0, :, ls]
        ubuf_ref[lb, CONV_HALO + ts:, :] = jnp.zeros((SUBLANES, LANES), F32)
    lead = CONV_HALO - (CONV_WIDTH - 1)

    def slab(lb, carry):
        for r0 in range(0, ts, CONV_STRIP):
            acc = None
            for sh in range(SUBLANES):
                taps = [k for k in range(CONV_WIDTH) if (lead + k) % SUBLANES == sh]
                if not taps:
                    continue
                win = ubuf_ref[lb, pl.ds(r0 + sh, CONV_STRIP + CONV_HALO), :]
                for k in taps:
                    a8 = (lead + k) // SUBLANES * SUBLANES
                    term = win[a8:a8 + CONV_STRIP] * dw_ref[lb, k:k + 1, :]
                    acc = term if acc is None else acc + term
            cbuf_ref[lb, r0:r0 + CONV_STRIP, :] = acc + bdw_ref[lb]
        return carry

    lax.fori_loop(0, n_slab, slab, 0)
    c = jnp.concatenate([cbuf_ref[lb] for lb in range(n_slab)], axis=-1)
    mu = jnp.mean(c, axis=-1, keepdims=True)
    var = jnp.mean(jnp.square(c - mu), axis=-1, keepdims=True)
    n = (c - mu) * lax.rsqrt(var + EPS) * lng_ref[...] + lnb_ref[...]
    a = (n * jax.nn.sigmoid(n)).astype(BF16)
    o_ref[0] = x_ref[0] + _dot(a, w_ref[...]) + bo_ref[...]


def _conv_out(u, x, dw, b_dw, ln_g, ln_b, w_out, b_out, ts=ROW_TILE):
    b, s, d = x.shape
    hb = ts // CONV_HALO
    n_slab = d // LANES
    dw_pad = jnp.zeros((CONV_HALO, d), F32).at[:CONV_WIDTH].set(dw)
    dw_slab = dw_pad.reshape(CONV_HALO, n_slab, LANES).transpose(1, 0, 2)
    bdw_slab = b_dw.reshape(n_slab, 1, LANES)
    row = lambda v: v.reshape(1, d)
    return pl.pallas_call(
        functools.partial(_conv_out_body, ts=ts),
        out_shape=jax.ShapeDtypeStruct((b, s, d), F32),
        grid=(b, s // ts),
        in_specs=[pl.BlockSpec((1, ts, d), lambda bi, i: (bi, i, 0)),
                  pl.BlockSpec((1, CONV_HALO, d), lambda bi, i: (bi, jnp.maximum(i * hb - 1, 0), 0)),
                  pl.BlockSpec((1, ts, d), lambda bi, i: (bi, i, 0)),
                  _const_spec(dw_slab.shape), _const_spec(bdw_slab.shape), _const_spec((1, d)),
                  _const_spec((1, d)), _const_spec(w_out.shape), _const_spec((1, d))],
        out_specs=pl.BlockSpec((1, ts, d), lambda bi, i: (bi, i, 0)),
        scratch_shapes=[pltpu.VMEM((n_slab, ts + CONV_HALO + SUBLANES, LANES), F32),
                        pltpu.VMEM((n_slab, ts, LANES), F32)],
        compiler_params=_params(("parallel", "parallel")),
        name="conv_out",
    )(u, u, x, dw_slab, bdw_slab, row(ln_g), row(ln_b), w_out, row(b_out))


def _gla_in_body(x_ref, g_ref, wqk_ref, wv_ref, wog_ref, wgd_ref, qk_ref, v_ref, og_ref, gd_ref):
    h = _rms(x_ref[...], g_ref[...]).astype(BF16)
    qk_ref[...] = _dot(h, wqk_ref[...])
    v_ref[...] = _dot(h, wv_ref[...]).astype(BF16)
    og_ref[...] = _dot(h, wog_ref[...])
    gd_ref[...] = _dot(h, wgd_ref[...])


def _gla_in(x2, g, w_qk, w_v, w_og, w_gd, tm=ROW_TILE):
    t, d = x2.shape
    widths = (w_qk.shape[1], w_v.shape[1], w_og.shape[1], w_gd.shape[1])
    dtypes = (F32, BF16, F32, F32)
    return pl.pallas_call(
        _gla_in_body,
        out_shape=[jax.ShapeDtypeStruct((t, n), dt) for n, dt in zip(widths, dtypes)],
        grid=(t // tm,),
        in_specs=[pl.BlockSpec((tm, d), lambda i: (i, 0)), _const_spec((1, d)),
                  _const_spec(w_qk.shape), _const_spec(w_v.shape), _const_spec(w_og.shape),
                  _const_spec(w_gd.shape)],
        out_specs=[pl.BlockSpec((tm, n), lambda i: (i, 0)) for n in widths],
        compiler_params=_params(("parallel",)),
        name="gla_in",
    )(x2, g.reshape(1, d), w_qk, w_v, w_og, w_gd)


def _gla_body(q_ref, k_ref, v_ref, og_ref, gd_ref, wup_ref, bg_ref, ng_ref, x_ref, wo_ref, o_ref,
              state_ref, obuf_ref):
    @pl.when(pl.program_id(1) == 0)
    def _():
        state_ref[...] = jnp.zeros_like(state_ref)

    c, dk, dv = GLA_CHUNK, GLA_DK, GLA_DV
    ga = _dot(gd_ref[0].astype(BF16), wup_ref[...]) + bg_ref[...]
    log_a = (jnp.minimum(ga, 0.0) - jnp.log1p(jnp.exp(-jnp.abs(ga)))) / GLA_TAU
    ri = lax.broadcasted_iota(jnp.int32, (c, c), 0)
    ci = lax.broadcasted_iota(jnp.int32, (c, c), 1)
    causal = ci <= ri
    tril = jnp.where(causal, 1.0, 0.0).astype(BF16)
    scale = dk ** -0.5
    n_chunks = GLA_ROWS // c
    chunk_rows = [slice(ch * c, (ch + 1) * c) for ch in range(n_chunks)]
    cases = [(ch, h) for ch in range(n_chunks) for h in range(GLA_HEADS)]
    hi = log_a.astype(BF16)
    r1 = log_a - hi.astype(F32)
    mid = r1.astype(BF16)
    lo = (r1 - mid.astype(F32)).astype(BF16)
    q_t, k_t, k_dec, decay = [], [], [], []
    for rows in chunk_rows:
        bcum = _dot(tril, hi[rows]) + _dot(tril, mid[rows]) + _dot(tril, lo[rows])
        b_last = bcum[c - 1:c, :]
        k = k_ref[0, rows, :]
        q_t.append((q_ref[0, rows, :] * scale * jnp.exp(bcum)).astype(BF16))
        k_t.append((k * jnp.exp(-bcum)).astype(BF16))
        k_dec.append((k * jnp.exp(b_last - bcum)).astype(BF16))
        decay.append(jnp.exp(b_last))
    ks = [slice(h * dk, (h + 1) * dk) for h in range(GLA_HEADS)]
    vs = [slice(h * dv, (h + 1) * dv) for h in range(GLA_HEADS)]
    v = {(ch, h): v_ref[0, chunk_rows[ch], vs[h]] for ch, h in cases}
    a = {(ch, h): jnp.where(causal, _dot_nt(q_t[ch][:, ks[h]], k_t[ch][:, ks[h]]), 0.0).astype(BF16)
         for ch, h in cases}
    kv = {(ch, h): _dot_tn(v[ch, h], k_dec[ch][:, ks[h]]) for ch, h in cases}
    o_intra = {(ch, h): _dot(a[ch, h], v[ch, h]) for ch, h in cases}
    state = [state_ref[h] for h in range(GLA_HEADS)]
    for ch, h in cases:
        o = o_intra[ch, h] + _dot_nt(q_t[ch][:, ks[h]], state[h].astype(BF16))
        state[h] = state[h] * decay[ch][:, ks[h]] + kv[ch, h]
        o = _rms(o, ng_ref[...])
        og = og_ref[0, chunk_rows[ch], vs[h]]
        obuf_ref[chunk_rows[ch], vs[h]] = (o * (og * jax.nn.sigmoid(og))).astype(BF16)
    for h in range(GLA_HEADS):
        state_ref[h] = state[h]
    o_ref[0] = x_ref[0] + _dot(obuf_ref[...], wo_ref[...])


def _gla_core(qk, v, og, gd, w_up, b_gate, norm_g, x, w_o):
    b, s, d = x.shape
    hh, dk, dv, rt = GLA_HEADS, GLA_DK, GLA_DV, GLA_ROWS
    return pl.pallas_call(
        _gla_body,
        out_shape=jax.ShapeDtypeStruct((b, s, d), F32),
        grid=(b, s // rt),
        in_specs=[pl.BlockSpec((1, rt, hh * dk), lambda bi, i: (bi, i, 0)),
                  pl.BlockSpec((1, rt, hh * dk), lambda bi, i: (bi, i, 1)),
                  pl.BlockSpec((1, rt, hh * dv), lambda bi, i: (bi, i, 0)),
                  pl.BlockSpec((1, rt, hh * dv), lambda bi, i: (bi, i, 0)),
                  pl.BlockSpec((1, rt, LANES), lambda bi, i: (bi, i, 0)),
                  _const_spec((LANES, hh * dk)), _const_spec((1, hh * dk)), _const_spec((1, dv)),
                  pl.BlockSpec((1, rt, d), lambda bi, i: (bi, i, 0)), _const_spec(w_o.shape)],
        out_specs=pl.BlockSpec((1, rt, d), lambda bi, i: (bi, i, 0)),
        scratch_shapes=[pltpu.VMEM((hh, dv, dk), F32), pltpu.VMEM((rt, hh * dv), BF16)],
        compiler_params=_params(("parallel", "arbitrary")),
        name="gla_core",
    )(qk, qk, v, og, gd, w_up, b_gate.reshape(1, hh * dk), norm_g.reshape(1, dv), x, w_o)


def _gla_mixer(x, g, w_in, w_gate_up, b_gate, norm_g, w_o):
    b, s, d = x.shape
    qw, vw = GLA_HEADS * GLA_DK, GLA_HEADS * GLA_DV
    x2 = x.reshape(b * s, d)
    w_gd = jnp.zeros((d, LANES), BF16).at[:, :GLA_GATE_RANK].set(w_in[:, 2 * qw + 2 * vw:].astype(BF16))
    w_up = jnp.zeros((LANES, qw), BF16).at[:GLA_GATE_RANK].set(w_gate_up.astype(BF16))
    qk, v, og, gd = _gla_in(x2, g, w_in[:, :2 * qw].astype(BF16), w_in[:, 2 * qw:2 * qw + vw].astype(BF16),
                            w_in[:, 2 * qw + vw:2 * qw + 2 * vw].astype(BF16), w_gd)
    r3 = lambda a: a.reshape(b, s, a.shape[-1])
    return _gla_core(r3(qk), r3(v), r3(og), r3(gd), w_up, b_gate, norm_g, x, w_o.astype(BF16))


def _nsa_in_body(x_ref, g_ref, wqt_ref, wk_ref, wvt_ref, wc_ref, wgt_ref,
                 qt_ref, kas_ref, kaw_ref, vast_ref, vawt_ref, c_ref, gtt_ref, *, ts):
    i = pl.program_id(1)
    qt, groups = NSA_QT, NSA_KV_HEADS
    h = _rms(x_ref[0], g_ref[...]).astype(BF16)
    q_all = (_dot_nt(wqt_ref[...], h) * (NSA_HEAD_DIM ** -0.5 * LOG2E)).astype(BF16)
    hw = NSA_REP * NSA_HEAD_DIM
    for gi in range(groups):
        for tt in range(ts // qt):
            qt_ref[0, gi, tt] = q_all[gi * hw:(gi + 1) * hw, tt * qt:(tt + 1) * qt]
    k = _dot(h, wk_ref[...])
    gw = groups * LANES
    lane = lax.broadcasted_iota(jnp.int32, (ts, LANES), 1)
    t = i * ts + lax.broadcasted_iota(jnp.int32, (ts, LANES), 0)
    onehot = jnp.where(lane - NSA_HEAD_DIM == t // NSA_SEL_BLOCK, 1.0, 0.0)
    for gi in range(groups):
        ls = slice(gi * LANES, (gi + 1) * LANES)
        kas_ref[0, gi] = (k[:, ls] + onehot).astype(BF16)
        kaw_ref[0, gi] = k[:, gw + gi * LANES:gw + (gi + 1) * LANES].astype(BF16)
    vt = _dot_nt(wvt_ref[...], h)
    ones = jnp.where(lax.broadcasted_iota(jnp.int32, (LANES, qt), 0) >= NSA_HEAD_DIM, 1.0, 0.0)
    for gi in range(groups):
        for tt in range(ts // qt):
            cs = slice(tt * qt, (tt + 1) * qt)
            vast_ref[0, gi, tt] = (vt[gi * LANES:(gi + 1) * LANES, cs] + ones).astype(BF16)
            vawt_ref[0, gi, tt] = (vt[gw + gi * LANES:gw + (gi + 1) * LANES, cs] + ones).astype(BF16)
    c = _dot(h, wc_ref[...])
    for gi in range(groups):
        c_ref[0, gi] = c[:, gi * LANES:(gi + 1) * LANES]
    gtt_ref[0] = _dot_nt(wgt_ref[...], h)


def _nsa_in(x, g, w_qt, w_k, w_vt, w_c, w_gt, ts=ROW_TILE):
    b, s, d = x.shape
    groups, qt = NSA_KV_HEADS, NSA_QT
    gw = groups * LANES
    nkt = s // qt
    tpb = ts // qt
    seq_spec = lambda n: pl.BlockSpec((1, ts, n), lambda bi, i: (bi, i, 0))
    rows_spec = lambda n: pl.BlockSpec((1, n, ts), lambda bi, i: (bi, 0, i))
    vt_spec = pl.BlockSpec((1, groups, tpb, LANES, qt), lambda bi, i: (bi, 0, i, 0, 0))
    vt_shape = jax.ShapeDtypeStruct((b, groups, nkt, LANES, qt), BF16)
    k_spec = pl.BlockSpec((1, groups, ts, LANES), lambda bi, i: (bi, 0, i, 0))
    k_shape = jax.ShapeDtypeStruct((b, groups, s, LANES), BF16)
    return pl.pallas_call(
        functools.partial(_nsa_in_body, ts=ts),
        out_shape=[jax.ShapeDtypeStruct((b, groups, nkt, w_qt.shape[0] // groups, qt), BF16),
                   k_shape, k_shape,
                   vt_shape, vt_shape,
                   jax.ShapeDtypeStruct((b, groups, s, LANES), F32),
                   jax.ShapeDtypeStruct((b, w_gt.shape[0], s), F32)],
        grid=(b, s // ts),
        in_specs=[pl.BlockSpec((1, ts, d), lambda bi, i: (bi, i, 0)), _const_spec((1, d)),
                  _const_spec(w_qt.shape), _const_spec(w_k.shape), _const_spec(w_vt.shape),
                  _const_spec(w_c.shape), _const_spec(w_gt.shape)],
        out_specs=[pl.BlockSpec((1, groups, tpb, w_qt.shape[0] // groups, qt), lambda bi, i: (bi, 0, i, 0, 0)),
                   k_spec, k_spec, vt_spec, vt_spec,
                   k_spec, rows_spec(w_gt.shape[0])],
        compiler_params=_params(("parallel", "parallel")),
        name="nsa_in",
    )(x, g.reshape(1, d), w_qt, w_k, w_vt, w_c, w_gt)


def _nsa_cmp_body(c_ref, pos_ref, w1t_ref, w1b_ref, w2k_ref, w2vt_ref, kc_ref, vct_ref):
    stride = NSA_CMP_STRIDE
    nch = c_ref.shape[2] // stride
    u = v = None
    for p in range(stride):
        x = c_ref[0, 0, pl.ds(p, nch, stride=stride), :]
        up = _dot((x + pos_ref[p:p + 1, :]).astype(BF16), w1t_ref[p])
        vp = _dot((x + pos_ref[stride + p:stride + p + 1, :]).astype(BF16), w1b_ref[p])
        u = up if u is None else u + up
        v = vp if v is None else v + vp
    hid = jax.nn.gelu(u + pltpu.roll(v, nch - 1, 0), approximate=True).astype(BF16)
    hw = hid.shape[-1] // 2
    kc_ref[0, 0] = _dot(hid[:, :hw], w2k_ref[...])
    vct_ref[0, 0] = _dot_nt(w2vt_ref[...], hid[:, hw:])


def _nsa_compress(c, cmp_pos, w1, w2):
    b, g, s, _ = c.shape
    dh, hidden = w2.shape[-1], w2.shape[1]
    stride = NSA_CMP_STRIDE
    nch = s // stride
    pos = jnp.concatenate([cmp_pos[0], cmp_pos[1]], axis=-1)
    w1r = w1.astype(BF16).reshape(2, NSA_CMP_BLOCK, dh, hidden)
    zero = jnp.zeros((NSA_CMP_BLOCK, dh, hidden), BF16)
    w1bd = jnp.concatenate([jnp.concatenate([w1r[0], zero], axis=-1),
                            jnp.concatenate([zero, w1r[1]], axis=-1)], axis=1)
    return pl.pallas_call(
        _nsa_cmp_body,
        out_shape=[jax.ShapeDtypeStruct((b, g, nch, dh), F32), jax.ShapeDtypeStruct((b, g, dh, nch), F32)],
        grid=(b, g),
        in_specs=[pl.BlockSpec((1, 1, s, 2 * dh), lambda bi, gi: (bi, gi, 0, 0)),
                  _const_spec(pos.shape), _const_spec((stride, 2 * dh, 2 * hidden)),
                  _const_spec((stride, 2 * dh, 2 * hidden)), _const_spec((hidden, dh)), _const_spec((dh, hidden))],
        out_specs=[pl.BlockSpec((1, 1, nch, dh), lambda bi, gi: (bi, gi, 0, 0)),
                   pl.BlockSpec((1, 1, dh, nch), lambda bi, gi: (bi, gi, 0, 0))],
        compiler_params=_params(("parallel", "parallel")),
        name="nsa_compress",
    )(c, pos, w1bd[:stride], w1bd[stride:], w2[0].astype(BF16), w2[1].astype(BF16).T)


def _nsa_attn_body(qt_ref, kc_ref, vct_ref, kas_ref, vast_ref, kaw_ref, vawt_ref, gtt_ref, ovt_ref, o_ref,
                   qaug_ref, m_ref, acc_ref, p_ref, alpha_ref, s_ref, rank_ref, *, n_slc):
    qt, rep, dh = NSA_QT, NSA_REP, NSA_HEAD_DIM
    cols = rep * qt
    qi = pl.program_id(2)
    s0 = qi * qt
    q = jnp.concatenate([qt_ref[0, 0, 0, r * dh:(r + 1) * dh, :] for r in range(rep)], axis=1)
    t_col = s0 + (lax.broadcasted_iota(jnp.int32, (1, cols), 1) & (qt - 1))

    n_pad = kc_ref.shape[2]
    sc = _dot(kc_ref[0, 0].astype(BF16), q)
    cmp_end = lax.broadcasted_iota(jnp.int32, (n_pad, 1), 0) * NSA_CMP_STRIDE + (NSA_CMP_BLOCK - 1)
    sc = jnp.where(cmp_end <= t_col, sc, NEG)
    e = jnp.exp2(sc - jnp.max(sc, axis=0, keepdims=True))
    p_cmp = e * (1.0 / jnp.sum(e, axis=0, keepdims=True))
    p_cmp = jnp.where(t_col >= NSA_CMP_BLOCK - 1, p_cmp, 0.0).astype(BF16)
    o_cmp = _dot(vct_ref[0, 0].astype(BF16), p_cmp)

    heads = [slice(r * qt, (r + 1) * qt) for r in range(rep)]

    wt = NSA_WINDOW // qt + 1
    lo = jnp.clip(qi - (wt - 1), 0, kaw_ref.shape[2] // qt - wt)
    kw = kaw_ref[0, 0, pl.ds(pl.multiple_of(lo * qt, qt), wt * qt), 0:dh]
    kpos = lo * qt + lax.broadcasted_iota(jnp.int32, (wt * qt, 1), 0)
    t_q = t_col[:, 0:qt]
    wbias = jnp.where((kpos <= t_q) & (kpos > t_q - NSA_WINDOW), 0.0, NEG)
    w_scores = [_dot(kw, q[:, cs]) for cs in heads]
    w_probs = []
    for s in w_scores:
        s = s + wbias
        w_probs.append(jnp.exp2(s - jnp.max(s, axis=0, keepdims=True)).astype(BF16))
    o_win = []
    for p in w_probs:
        pv = _dot(vawt_ref[0, 0, lo], p[0:qt])
        for u in range(1, wt):
            pv = pv + _dot(vawt_ref[0, 0, lo + u], p[u * qt:(u + 1) * qt])
        o_win.append(pv[0:dh] / pv[dh:dh + 1])
    o_win = jnp.concatenate(o_win, axis=1)

    imp = _dot(ovt_ref[...], p_cmp[:, 0:qt])
    for r in range(1, rep):
        imp = imp + _dot(ovt_ref[...], p_cmp[:, r * qt:(r + 1) * qt])
    imp_t = imp[:n_slc]
    blk = lax.broadcasted_iota(jnp.int32, (n_slc, 1), 0)
    cur = t_q // NSA_SEL_BLOCK
    forced = (blk == 0) | (blk == cur) | (blk == cur - 1)
    visible = blk * NSA_SEL_BLOCK <= t_q
    imp_t = jnp.where(forced, NSA_FORCE, imp_t)
    imp_t = jnp.where(visible, imp_t, -1.0)
    groups = [imp_t[k:k + SUBLANES] for k in range(0, n_slc, SUBLANES)]
    sub = lax.broadcasted_iota(jnp.int32, (SUBLANES, 1), 0)
    rank_ref[...] = jnp.zeros_like(rank_ref)
    last_blk = (s0 + qt - 1) // NSA_SEL_BLOCK
    for gi, own in enumerate(groups):
        @pl.when(gi * SUBLANES <= last_blk)
        def _(gi=gi, own=own):
            incs = [jnp.zeros((SUBLANES, qt), F32) for _ in groups]
            for si in range(SUBLANES):
                row = own[si:si + 1, :]
                for k, blk_imp in enumerate(groups):
                    if k < gi:
                        inc = jnp.where(row > blk_imp, 1.0, 0.0)
                    elif k > gi:
                        inc = jnp.where(row >= blk_imp, 1.0, 0.0)
                    else:
                        inc = jnp.where(sub > si, jnp.where(row >= blk_imp, 1.0, 0.0),
                                        jnp.where(row > blk_imp, 1.0, 0.0))
                    incs[k] = incs[k] + inc
            for k, inc in enumerate(incs):
                rank_ref[k * SUBLANES:(k + 1) * SUBLANES, :] += inc
    sel = (rank_ref[0:n_slc, :] < float(min(NSA_N_SELECT, n_slc))) & visible
    bias_parts = [jnp.where(sel, 0.0, NEG)]
    if LANES - dh - n_slc:
        bias_parts.append(jnp.zeros((LANES - dh - n_slc, qt), F32))
    bias = jnp.concatenate(bias_parts, axis=0).astype(BF16)
    qaug_ref[0:dh, :] = q
    for r in range(rep):
        qaug_ref[dh:, r * qt:(r + 1) * qt] = bias

    def score(ka_ref, j, nt):
        start = j * (nt * qt)
        if not isinstance(start, int):
            start = pl.multiple_of(start, nt * qt)
        ka = ka_ref[0, 0, pl.ds(start, nt * qt), :]
        return [_dot(ka, qaug_ref[:, cs]) for cs in heads]

    def flush(vat_ref, jp, nt):
        pvs = []
        for cs in heads:
            pv = _dot(vat_ref[0, 0, jp * nt], p_ref[0:qt, cs])
            for u in range(1, nt):
                pv = pv + _dot(vat_ref[0, 0, jp * nt + u], p_ref[u * qt:(u + 1) * qt, cs])
            pvs.append(pv)
        return pvs

    def accumulate(pvs):
        for cs, pv in zip(heads, pvs):
            acc_ref[:, cs] = acc_ref[:, cs] * alpha_ref[:, cs] + pv

    def step(ka_ref, vat_ref, j, nt, mask_fn, has_next):
        rows = nt * qt
        nxt = score(ka_ref, j + 1, nt) if has_next else None
        pvs = flush(vat_ref, jnp.maximum(j - 1, 0), nt)
        probs, alphas = [], []
        for cs in heads:
            s = s_ref[0:rows, cs]
            if mask_fn is not None:
                kpos = j * rows + lax.broadcasted_iota(jnp.int32, (rows, 1), 0)
                s = jnp.where(mask_fn(kpos, t_col[:, cs]), s, NEG)
            m_old = m_ref[:, cs]
            m_new = jnp.maximum(m_old, jnp.max(s, axis=0, keepdims=True))
            probs.append(jnp.exp2(s - m_new).astype(BF16))
            alphas.append(jnp.exp2(m_old - m_new))
            m_ref[:, cs] = m_new
        accumulate(pvs)
        for cs, p, alpha in zip(heads, probs, alphas):
            p_ref[0:rows, cs] = p
            alpha_ref[:, cs] = alpha
        if has_next:
            for cs, s in zip(heads, nxt):
                s_ref[0:rows, cs] = s

    def attend(ka_ref, vat_ref, nt, lo, hi, mask_fn, last_mask_fn):
        rows = nt * qt
        m_ref[...] = jnp.full_like(m_ref, NEG)
        acc_ref[...] = jnp.zeros_like(acc_ref)
        p_ref[0:rows, :] = jnp.zeros((rows, cols), BF16)
        alpha_ref[...] = jnp.ones_like(alpha_ref)
        for cs, s in zip(heads, score(ka_ref, lo, nt)):
            s_ref[0:rows, cs] = s

        def body(j, carry):
            step(ka_ref, vat_ref, j, nt, mask_fn, True)
            return carry

        lax.fori_loop(lo, hi, body, 0)
        step(ka_ref, vat_ref, hi, nt, last_mask_fn, False)
        accumulate(flush(vat_ref, hi, nt))
        acc = acc_ref[...]
        return acc[0:dh] / acc[dh:dh + 1]

    o_slc = attend(kas_ref, vast_ref, NSA_SEL_SPAN, 0, qi // NSA_SEL_SPAN, None, lambda kpos, t: kpos <= t)

    gates = jax.nn.sigmoid(gtt_ref[0])
    for r in range(rep):
        cs = slice(r * qt, (r + 1) * qt)
        o = (gates[3 * r:3 * r + 1] * o_cmp[:, cs] + gates[3 * r + 1:3 * r + 2] * o_slc[:, cs]
             + gates[3 * r + 2:3 * r + 3] * o_win[:, cs])
        o_ref[0, 0, 0, r * dh:(r + 1) * dh, :] = o.astype(BF16)


def _nsa_attn(q_t, kcmp, vcmp_t, kas, vas_t, kaw, vaw_t, gates_t, ov_t):
    b, g, nkt, _, qt = q_t.shape
    rep, dh = NSA_REP, NSA_HEAD_DIM
    s = nkt * qt
    n_slc = s // NSA_SEL_BLOCK
    nch = kcmp.shape[2]
    k_spec = pl.BlockSpec((1, 1, s, LANES), lambda bi, gi, i: (bi, gi, 0, 0))
    vt_spec = pl.BlockSpec((1, 1, nkt, LANES, qt), lambda bi, gi, i: (bi, gi, 0, 0, 0))
    q_spec = pl.BlockSpec((1, 1, 1, rep * dh, qt), lambda bi, gi, i: (bi, gi, i, 0, 0))
    return pl.pallas_call(
        functools.partial(_nsa_attn_body, n_slc=n_slc),
        out_shape=jax.ShapeDtypeStruct(q_t.shape, BF16),
        grid=(b, g, s // qt),
        in_specs=[q_spec,
                  pl.BlockSpec((1, 1, nch, dh), lambda bi, gi, i: (bi, gi, 0, 0)),
                  pl.BlockSpec((1, 1, dh, nch), lambda bi, gi, i: (bi, gi, 0, 0)),
                  k_spec, vt_spec, k_spec, vt_spec,
                  pl.BlockSpec((1, NSA_GATE_ROWS, qt), lambda bi, gi, i: (bi, gi, i)),
                  _const_spec(ov_t.shape)],
        out_specs=q_spec,
        scratch_shapes=[pltpu.VMEM((LANES, rep * qt), BF16), pltpu.VMEM((1, rep * qt), F32),
                        pltpu.VMEM((LANES, rep * qt), F32), pltpu.VMEM((NSA_SEL_SPAN * qt, rep * qt), BF16),
                        pltpu.VMEM((1, rep * qt), F32), pltpu.VMEM((NSA_SEL_SPAN * qt, rep * qt), F32),
                        pltpu.VMEM((LANES - dh, qt), F32)],
        compiler_params=_params(("parallel", "parallel", "parallel")),
        name="nsa_attn",
    )(q_t, kcmp, vcmp_t, kas, vas_t, kaw, vaw_t, gates_t, ov_t)


def _nsa_overlap_t(s):
    n_cmp = (s - NSA_CMP_BLOCK) // NSA_CMP_STRIDE + 1
    n_slc = s // NSA_SEL_BLOCK
    cs = np.arange(n_cmp) * NSA_CMP_STRIDE
    ss = np.arange(n_slc) * NSA_SEL_BLOCK
    ov = np.clip(np.minimum(cs[:, None] + NSA_CMP_BLOCK, ss[None, :] + NSA_SEL_BLOCK)
                 - np.maximum(cs[:, None], ss[None, :]), 0, None) / NSA_CMP_BLOCK
    out = np.zeros((LANES, s // NSA_CMP_STRIDE), np.float32)
    out[:n_slc, :n_cmp] = ov.T
    return jnp.asarray(out, BF16)


def _mm_res_t_body(at_ref, w_ref, x_ref, o_ref):
    groups, tiles, kw, qt = at_ref.shape[1:]
    for tt in range(tiles):
        rows = slice(tt * qt, (tt + 1) * qt)
        y = x_ref[0, rows, :]
        for gi in range(groups):
            y = y + _dot_tn(at_ref[0, gi, tt], w_ref[gi * kw:(gi + 1) * kw, :])
        o_ref[0, rows, :] = y


def _mm_res_t(a_t, w, x, ts=ROW_TILE):
    b, groups, nkt, kw, qt = a_t.shape
    s, n = nkt * qt, w.shape[1]
    tpb = ts // qt
    return pl.pallas_call(
        _mm_res_t_body,
        out_shape=jax.ShapeDtypeStruct((b, s, n), F32),
        grid=(b, s // ts),
        in_specs=[pl.BlockSpec((1, groups, tpb, kw, qt), lambda bi, i: (bi, 0, i, 0, 0)),
                  _const_spec(w.shape), pl.BlockSpec((1, ts, n), lambda bi, i: (bi, i, 0))],
        out_specs=pl.BlockSpec((1, ts, n), lambda bi, i: (bi, i, 0)),
        compiler_params=_params(("parallel", "parallel")),
        name="mm_res_t",
    )(a_t, w, x)


def _nsa_mixer(x, g, w_in, cmp_pos, cmp_w1, cmp_w2, w_o):
    b, s, d = x.shape
    hh, gg, rep, dh = NSA_HEADS, NSA_KV_HEADS, NSA_REP, NSA_HEAD_DIM
    kvw = gg * dh
    assert s % (NSA_SEL_SPAN * NSA_QT) == 0 and s % ROW_TILE == 0
    assert s // NSA_SEL_BLOCK <= LANES - dh and (s // NSA_SEL_BLOCK) % SUBLANES == 0
    wb = w_in.astype(BF16)
    parts = [wb[:, d + i * kvw:d + (i + 1) * kvw].reshape(d, gg, dh) for i in range(6)]
    zeros = jnp.zeros((d, gg, LANES - dh), BF16)
    padded = lambda p: jnp.concatenate([p, zeros], axis=-1).reshape(d, gg * LANES)
    w_k = jnp.concatenate([padded(parts[2]), padded(parts[4])], axis=-1)
    w_vt = jnp.concatenate([padded(parts[3]), padded(parts[5])], axis=-1).T
    w_c = jnp.concatenate([parts[0], parts[1]], axis=-1).reshape(d, gg * 2 * dh)
    w_g = wb[:, d + 6 * kvw:].reshape(d, gg, 3 * rep)
    w_gt = jnp.concatenate([w_g, jnp.zeros((d, gg, NSA_GATE_ROWS - 3 * rep), BF16)], axis=-1)
    w_gt = w_gt.reshape(d, gg * NSA_GATE_ROWS).T
    q_t, kas, kaw, vas_t, vaw_t, c, gates_t = _nsa_in(x, g, wb[:, :d].T, w_k, w_vt, w_c, w_gt)

    k_cmp, v_cmp_t = _nsa_compress(c, cmp_pos, cmp_w1, cmp_w2)
    o_t = _nsa_attn(q_t, k_cmp, v_cmp_t, kas, vas_t, kaw, vaw_t, gates_t, _nsa_overlap_t(s))
    return _mm_res_t(o_t, w_o.astype(BF16), x)


def kernel(x, mem, ffn1_norm, ffn1_w_in, ffn1_w_out, mix_norm, xattn_norm, mem_norm, xattn_w_q, xattn_w_kv, xattn_w_o, ffn2_norm, ffn2_w_in, ffn2_w_out, pool_w, pool_b, pool_scale, nsa_w_in, nsa_cmp_pos, nsa_cmp_w1, nsa_cmp_w2, nsa_w_o, gla_w_in, gla_w_gate_up, gla_b_gate, gla_norm, gla_w_o, conv_w_in, conv_b_in, conv_dw, conv_b_dw, conv_ln_g, conv_ln_b, conv_w_out, conv_b_out, final_norm):
    b, s, d = x.shape
    n_mem = mem.shape[1]
    depth = ffn1_norm.shape[0]
    n_mixers = 4
    flat = lambda a: a.reshape(b * s, d)
    cube = lambda a: a.reshape(b, s, d)
    mem2 = mem.reshape(b * n_mem, d)
    ffn1_w_in, ffn1_w_out, ffn2_w_in, ffn2_w_out, xattn_w_q, xattn_w_kv, xattn_w_o = (
        w.astype(BF16) for w in (ffn1_w_in, ffn1_w_out, ffn2_w_in, ffn2_w_out, xattn_w_q, xattn_w_kv, xattn_w_o))
    mem_kv = _rms_mm_layers(mem2, mem_norm, xattn_w_kv, BF16, tm=n_mem).reshape(depth, b, n_mem, 2 * d)
    for i in range(depth):
        m, j = i % n_mixers, i // n_mixers
        x = cube(_ffn(flat(x), ffn1_norm[i], ffn1_w_in, ffn1_w_out, i))
        if m == 0:
            x = _pool_mixer(x, mix_norm[i], pool_w[j].astype(BF16), pool_b[j], pool_scale[j])
        elif m == 1:
            x = _nsa_mixer(x, mix_norm[i], nsa_w_in[j], nsa_cmp_pos[j], nsa_cmp_w1[j], nsa_cmp_w2[j], nsa_w_o[j])
        elif m == 2:
            x = _gla_mixer(x, mix_norm[i], gla_w_in[j], gla_w_gate_up[j], gla_b_gate[j], gla_norm[j], gla_w_o[j])
        else:
            u = _conv_in(flat(x), mix_norm[i], conv_w_in[j].astype(BF16), conv_b_in[j])
            x = _conv_out(cube(u), x, conv_dw[j], conv_b_dw[j], conv_ln_g[j], conv_ln_b[j],
                          conv_w_out[j].astype(BF16), conv_b_out[j])
        x = _xattn(x, xattn_norm[i], xattn_w_q, mem_kv, xattn_w_o, i)
        final_g = final_norm if i == depth - 1 else None
        x = cube(_ffn(flat(x), ffn2_norm[i], ffn2_w_in, ffn2_w_out, i, final_g=final_g))
    return x
```

```python
import functools

import numpy as np
import jax
import jax.numpy as jnp
from jax import lax
from jax.experimental import pallas as pl
from jax.experimental.pallas import tpu as pltpu

F32 = jnp.float32
BF16 = jnp.bfloat16

EPS = 1e-6
NEG = -1e30

V7X_VMEM_BYTES = 64 * 1024 * 1024
VMEM_LIMIT = V7X_VMEM_BYTES - 8 * 1024 * 1024
LANES = 128
SUBLANES = 8

POOL_WINDOWS = (2, 4, 8, 16)
POOL_HALO = 16

NSA_HEADS = 16
NSA_KV_HEADS = 4
NSA_REP = NSA_HEADS // NSA_KV_HEADS
NSA_HEAD_DIM = 64
NSA_CMP_BLOCK = 32
NSA_CMP_STRIDE = 16
NSA_SEL_BLOCK = 64
NSA_N_SELECT = 16
NSA_WINDOW = 512
NSA_FORCE = 1e4
NSA_QT = 256
NSA_GATE_ROWS = 16
NSA_SEL_SPAN = 2
LOG2E = 1.4426950408889634

GLA_HEADS = 4
GLA_DK = 128
GLA_DV = 256
GLA_GATE_RANK = 16
GLA_TAU = 16.0
GLA_CHUNK = 64
GLA_ROWS = 512

CONV_WIDTH = 31
CONV_HALO = 32
CONV_STRIP = 128

XATTN_HEADS = 4

ROW_TILE = 1024
FFN_ROW_TILE = 512
FFN_CHUNKS = 1


def _params(sem):
    return pltpu.CompilerParams(dimension_semantics=sem, vmem_limit_bytes=VMEM_LIMIT)


def _rms(x, g):
    return x * lax.rsqrt(jnp.mean(x * x, axis=-1, keepdims=True) + EPS) * g


def _dot(a, b):
    return jnp.dot(a, b, preferred_element_type=F32)


def _dot_nt(a, b):
    return lax.dot_general(a, b, (((1,), (1,)), ((), ())), preferred_element_type=F32)


def _dot_tn(a, b):
    return lax.dot_general(a, b, (((0,), (0,)), ((), ())), preferred_element_type=F32)


def _const_spec(shape):
    nd = len(shape)
    return pl.BlockSpec(shape, lambda *_: (0,) * nd, pipeline_mode=pl.Buffered(1))


def _layer_spec(stack_shape, layer):
    return pl.BlockSpec((None,) + tuple(stack_shape[1:]), lambda *_: (layer, 0, 0), pipeline_mode=pl.Buffered(1))


def _ffn_body(x_ref, g_ref, win_ref, wout_ref, *rest, d_ff, final):
    if final:
        fg_ref, o_ref = rest
    else:
        (o_ref,) = rest
    x = x_ref[...]
    h = _rms(x, g_ref[...]).astype(BF16)
    fc = d_ff // FFN_CHUNKS
    y = None
    for c in range(FFN_CHUNKS):
        g = _dot(h, win_ref[:, c * fc:(c + 1) * fc])
        u = _dot(h, win_ref[:, d_ff + c * fc:d_ff + (c + 1) * fc])
        a = (g * jax.nn.sigmoid(g) * u).astype(BF16)
        yc = _dot(a, wout_ref[c * fc:(c + 1) * fc, :])
        y = yc if y is None else y + yc
    out = x + 0.5 * y
    if final:
        out = _rms(out, fg_ref[...])
    o_ref[...] = out


def _ffn(x2, g, w_in, w_out, layer, final_g=None, tm=FFN_ROW_TILE):
    t, d = x2.shape
    d_ff = w_out.shape[1]
    final = final_g is not None
    in_specs = [pl.BlockSpec((tm, d), lambda i: (i, 0)), _const_spec((1, d)),
                _layer_spec(w_in.shape, layer), _layer_spec(w_out.shape, layer)]
    args = [x2, g.reshape(1, d), w_in, w_out]
    if final:
        in_specs.append(_const_spec((1, d)))
        args.append(final_g.reshape(1, d))
    return pl.pallas_call(
        functools.partial(_ffn_body, d_ff=d_ff, final=final),
        out_shape=jax.ShapeDtypeStruct((t, d), F32),
        grid=(t // tm,),
        in_specs=in_specs,
        out_specs=pl.BlockSpec((tm, d), lambda i: (i, 0)),
        compiler_params=_params(("parallel",)),
        name="ffn",
    )(*args)


def _rms_mm_body(x_ref, g_ref, w_ref, o_ref):
    h = _rms(x_ref[...], g_ref[...]).astype(BF16)
    o_ref[...] = _dot(h, w_ref[...]).astype(o_ref.dtype)


def _rms_mm_layers(x2, g, w, out_dtype, tm):
    t, d = x2.shape
    layers, _, n = w.shape
    return pl.pallas_call(
        _rms_mm_body,
        out_shape=jax.ShapeDtypeStruct((layers, t, n), out_dtype),
        grid=(layers, t // tm),
        in_specs=[pl.BlockSpec((tm, d), lambda l, i: (i, 0)),
                  pl.BlockSpec((None, 1, d), lambda l, i: (l, 0, 0)),
                  pl.BlockSpec((None, d, n), lambda l, i: (l, 0, 0))],
        out_specs=pl.BlockSpec((None, tm, n), lambda l, i: (l, i, 0)),
        compiler_params=_params(("parallel", "parallel")),
        name="rms_mm",
    )(x2, g.reshape(layers, 1, d), w)


def _xattn_body(x_ref, g_ref, wq_ref, k_ref, v_ref, wo_ref, o_ref, *, heads):
    x = x_ref[0]
    d = x.shape[-1]
    dh = d // heads
    h = _rms(x, g_ref[...]).astype(BF16)
    q = (_dot(h, wq_ref[...]) * dh ** -0.5).astype(BF16)
    cols = [slice(hd * dh, (hd + 1) * dh) for hd in range(heads)]
    scores = [_dot_nt(q[:, sl], k_ref[0][:, sl]) for sl in cols]
    probs = []
    for s in scores:
        e = jnp.exp(s - jnp.max(s, axis=-1, keepdims=True))
        probs.append((e * (1.0 / jnp.sum(e, axis=-1, keepdims=True))).astype(BF16))
    o = jnp.concatenate([_dot(p, v_ref[0][:, sl]).astype(BF16) for p, sl in zip(probs, cols)], axis=-1)
    o_ref[0] = x + _dot(o, wo_ref[...])


def _xattn(x, g, w_q, kv, w_o, layer, ts=ROW_TILE):
    b, s, d = x.shape
    n = kv.shape[2]
    return pl.pallas_call(
        functools.partial(_xattn_body, heads=XATTN_HEADS),
        out_shape=jax.ShapeDtypeStruct((b, s, d), F32),
        grid=(b, s // ts),
        in_specs=[pl.BlockSpec((1, ts, d), lambda bi, i: (bi, i, 0)), _const_spec((1, d)),
                  _layer_spec(w_q.shape, layer),
                  pl.BlockSpec((None, 1, n, d), lambda bi, i: (layer, bi, 0, 0)),
                  pl.BlockSpec((None, 1, n, d), lambda bi, i: (layer, bi, 0, 1)),
                  _layer_spec(w_o.shape, layer)],
        out_specs=pl.BlockSpec((1, ts, d), lambda bi, i: (bi, i, 0)),
        compiler_params=_params(("parallel", "parallel")),
        name="xattn",
    )(x, g.reshape(1, d), w_q, kv, kv, w_o)


def _pool_body(x_ref, halo_ref, g_ref, w_ref, b_ref, sc_ref, o_ref, hbuf_ref, l1_ref, l2_ref, *, ts):
    i = pl.program_id(1)
    x = x_ref[0]
    d = x.shape[-1]
    pad = POOL_HALO
    gw = d // len(POOL_WINDOWS)
    n = ts + pad
    g = g_ref[...]
    h = _rms(x, g)
    hh = _rms(halo_ref[0], g)
    hbuf_ref[0:pad, :] = jnp.zeros((pad, d), F32)
    l1_ref[0:pad, :] = jnp.zeros((pad, d - gw), F32)
    l2_ref[0:pad, :] = jnp.zeros((pad, d - 2 * gw), F32)
    hbuf_ref[pad:2 * pad, :] = jnp.where(i > 0, hh, 0.0)
    hbuf_ref[2 * pad:, :] = h
    a1 = hbuf_ref[pad:pad + n, :] + hbuf_ref[pad - 1:pad - 1 + n, :]
    l1_ref[pad:pad + n, :] = a1[:, gw:]
    a2 = a1[:, gw:] + l1_ref[pad - 2:pad - 2 + n, :]
    l2_ref[pad:pad + n, :] = a2[:, gw:]
    a3 = a2[:, gw:] + l2_ref[pad - 4:pad - 4 + n, :]
    a3_last = a3[:, gw:]
    a4 = a3_last[SUBLANES:] + a3_last[:n - SUBLANES]
    sums = [a1[pad:, 0:gw], a2[pad:, 0:gw], a3[pad:, 0:gw], a4[pad - SUBLANES:, :]]
    t = i * ts + lax.broadcasted_iota(jnp.int32, (ts, 1), 0)
    ys = []
    for gi, win in enumerate(POOL_WINDOWS):
        cs = slice(gi * gw, (gi + 1) * gw)
        cnt = jnp.minimum(t + 1, win).astype(F32)
        p = sums[gi] / cnt - h[:, cs]
        ys.append(_dot(p.astype(BF16), w_ref[gi]))
    y = (jnp.concatenate(ys, axis=-1) + b_ref[...]) * sc_ref[...]
    o_ref[0] = x + y


def _pool_mixer(x, g, w, bias, scale, ts=ROW_TILE):
    b, s, d = x.shape
    hb = ts // POOL_HALO
    return pl.pallas_call(
        functools.partial(_pool_body, ts=ts),
        out_shape=jax.ShapeDtypeStruct((b, s, d), F32),
        grid=(b, s // ts),
        in_specs=[pl.BlockSpec((1, ts, d), lambda bi, i: (bi, i, 0)),
                  pl.BlockSpec((1, POOL_HALO, d), lambda bi, i: (bi, jnp.maximum(i * hb - 1, 0), 0)),
                  _const_spec((1, d)), _const_spec(w.shape), _const_spec((1, d)), _const_spec((1, d))],
        out_specs=pl.BlockSpec((1, ts, d), lambda bi, i: (bi, i, 0)),
        scratch_shapes=[pltpu.VMEM((ts + 2 * POOL_HALO, d), F32),
                        pltpu.VMEM((ts + 2 * POOL_HALO, d - d // len(POOL_WINDOWS)), F32),
                        pltpu.VMEM((ts + 2 * POOL_HALO, d - 2 * (d // len(POOL_WINDOWS))), F32)],
        compiler_params=_params(("parallel", "parallel")),
        name="pool",
    )(x, x, g.reshape(1, d), w, bias.reshape(1, d), scale.reshape(1, d))


def _conv_in_body(x_ref, g_ref, w_ref, b_ref, o_ref):
    d = x_ref.shape[-1]
    h = _rms(x_ref[...], g_ref[...]).astype(BF16)
    gate = _dot(h, w_ref[:, d:]) + b_ref[:, d:]
    a = _dot(h, w_ref[:, :d]) + b_ref[:, :d]
    o_ref[...] = a * jax.nn.sigmoid(gate)


def _conv_in(x2, g, w, bias, tm=ROW_TILE):
    t, d = x2.shape
    return pl.pallas_call(
        _conv_in_body,
        out_shape=jax.ShapeDtypeStruct((t, d), F32),
        grid=(t // tm,),
        in_specs=[pl.BlockSpec((tm, d), lambda i: (i, 0)), _const_spec((1, d)),
                  _const_spec(w.shape), _const_spec((1, 2 * d))],
        out_specs=pl.BlockSpec((tm, d), lambda i: (i, 0)),
        compiler_params=_params(("parallel",)),
        name="conv_in",
    )(x2, g.reshape(1, d), w, bias.reshape(1, 2 * d))


def _conv_out_body(u_ref, halo_ref, x_ref, dw_ref, bdw_ref, lng_ref, lnb_ref, w_ref, bo_ref, o_ref,
                   ubuf_ref, cbuf_ref, *, ts):
    i = pl.program_id(1)
    n_slab = ubuf_ref.shape[0]
    halo = jnp.where(i > 0, halo_ref[0], 0.0)
    for lb in range(n_slab):
        ls = slice(lb * LANES, (lb + 1) * LANES)
        ubuf_ref[lb, 0:CONV_HALO, :] = halo[:, ls]
        ubuf_ref[lb, CONV_HALO:CONV_HALO + ts, :] = u_ref[0, :, ls]
        ubuf_ref[lb, CONV_HALO + ts:, :] = jnp.zeros((SUBLANES, LANES), F32)
    lead = CONV_HALO - (CONV_WIDTH - 1)

    def slab(lb, carry):
        for r0 in range(0, ts, CONV_STRIP):
            acc = None
            for sh in range(SUBLANES):
                taps = [k for k in range(CONV_WIDTH) if (lead + k) % SUBLANES == sh]
                if not taps:
                    continue
                win = ubuf_ref[lb, pl.ds(r0 + sh, CONV_STRIP + CONV_HALO), :]
                for k in taps:
                    a8 = (lead + k) // SUBLANES * SUBLANES
                    term = win[a8:a8 + CONV_STRIP] * dw_ref[lb, k:k + 1, :]
                    acc = term if acc is None else acc + term
            cbuf_ref[lb, r0:r0 + CONV_STRIP, :] = acc + bdw_ref[lb]
        return carry

    lax.fori_loop(0, n_slab, slab, 0)
    c = jnp.concatenate([cbuf_ref[lb] for lb in range(n_slab)], axis=-1)
    mu = jnp.mean(c, axis=-1, keepdims=True)
    var = jnp.mean(jnp.square(c - mu), axis=-1, keepdims=True)
    n = (c - mu) * lax.rsqrt(var + EPS) * lng_ref[...] + lnb_ref[...]
    a = (n * jax.nn.sigmoid(n)).astype(BF16)
    o_ref[0] = x_ref[0] + _dot(a, w_ref[...]) + bo_ref[...]


def _conv_out(u, x, dw, b_dw, ln_g, ln_b, w_out, b_out, ts=ROW_TILE):
    b, s, d = x.shape
    hb = ts // CONV_HALO
    n_slab = d // LANES
    dw_pad = jnp.zeros((CONV_HALO, d), F32).at[:CONV_WIDTH].set(dw)
    dw_slab = dw_pad.reshape(CONV_HALO, n_slab, LANES).transpose(1, 0, 2)
    bdw_slab = b_dw.reshape(n_slab, 1, LANES)
    row = lambda v: v.reshape(1, d)
    return pl.pallas_call(
        functools.partial(_conv_out_body, ts=ts),
        out_shape=jax.ShapeDtypeStruct((b, s, d), F32),
        grid=(b, s // ts),
        in_specs=[pl.BlockSpec((1, ts, d), lambda bi, i: (bi, i, 0)),
                  pl.BlockSpec((1, CONV_HALO, d), lambda bi, i: (bi, jnp.maximum(i * hb - 1, 0), 0)),
                  pl.BlockSpec((1, ts, d), lambda bi, i: (bi, i, 0)),
                  _const_spec(dw_slab.shape), _const_spec(bdw_slab.shape), _const_spec((1, d)),
                  _const_spec((1, d)), _const_spec(w_out.shape), _const_spec((1, d))],
        out_specs=pl.BlockSpec((1, ts, d), lambda bi, i: (bi, i, 0)),
        scratch_shapes=[pltpu.VMEM((n_slab, ts + CONV_HALO + SUBLANES, LANES), F32),
                        pltpu.VMEM((n_slab, ts, LANES), F32)],
        compiler_params=_params(("parallel", "parallel")),
        name="conv_out",
    )(u, u, x, dw_slab, bdw_slab, row(ln_g), row(ln_b), w_out, row(b_out))


def _gla_in_body(x_ref, g_ref, wqk_ref, wv_ref, wog_ref, wgd_ref, qk_ref, v_ref, og_ref, gd_ref):
    h = _rms(x_ref[...], g_ref[...]).astype(BF16)
    qk_ref[...] = _dot(h, wqk_ref[...])
    v_ref[...] = _dot(h, wv_ref[...]).astype(BF16)
    og_ref[...] = _dot(h, wog_ref[...])
    gd_ref[...] = _dot(h, wgd_ref[...])


def _gla_in(x2, g, w_qk, w_v, w_og, w_gd, tm=ROW_TILE):
    t, d = x2.shape
    widths = (w_qk.shape[1], w_v.shape[1], w_og.shape[1], w_gd.shape[1])
    dtypes = (F32, BF16, F32, F32)
    return pl.pallas_call(
        _gla_in_body,
        out_shape=[jax.ShapeDtypeStruct((t, n), dt) for n, dt in zip(widths, dtypes)],
        grid=(t // tm,),
        in_specs=[pl.BlockSpec((tm, d), lambda i: (i, 0)), _const_spec((1, d)),
                  _const_spec(w_qk.shape), _const_spec(w_v.shape), _const_spec(w_og.shape),
                  _const_spec(w_gd.shape)],
        out_specs=[pl.BlockSpec((tm, n), lambda i: (i, 0)) for n in widths],
        compiler_params=_params(("parallel",)),
        name="gla_in",
    )(x2, g.reshape(1, d), w_qk, w_v, w_og, w_gd)


def _gla_body(q_ref, k_ref, v_ref, og_ref, gd_ref, wup_ref, bg_ref, ng_ref, x_ref, wo_ref, o_ref,
              state_ref, obuf_ref):
    @pl.when(pl.program_id(1) == 0)
    def _():
        state_ref[...] = jnp.zeros_like(state_ref)

    c, dk, dv = GLA_CHUNK, GLA_DK, GLA_DV
    ga = _dot(gd_ref[0].astype(BF16), wup_ref[...]) + bg_ref[...]
    log_a = (jnp.minimum(ga, 0.0) - jnp.log1p(jnp.exp(-jnp.abs(ga)))) / GLA_TAU
    ri = lax.broadcasted_iota(jnp.int32, (c, c), 0)
    ci = lax.broadcasted_iota(jnp.int32, (c, c), 1)
    causal = ci <= ri
    tril = jnp.where(causal, 1.0, 0.0).astype(BF16)
    scale = dk ** -0.5
    n_chunks = GLA_ROWS // c
    chunk_rows = [slice(ch * c, (ch + 1) * c) for ch in range(n_chunks)]
    cases = [(ch, h) for ch in range(n_chunks) for h in range(GLA_HEADS)]
    hi = log_a.astype(BF16)
    r1 = log_a - hi.astype(F32)
    mid = r1.astype(BF16)
    lo = (r1 - mid.astype(F32)).astype(BF16)
    q_t, k_t, k_dec, decay = [], [], [], []
    for rows in chunk_rows:
        bcum = _dot(tril, hi[rows]) + _dot(tril, mid[rows]) + _dot(tril, lo[rows])
        b_last = bcum[c - 1:c, :]
        k = k_ref[0, rows, :]
        q_t.append((q_ref[0, rows, :] * scale * jnp.exp(bcum)).astype(BF16))
        k_t.append((k * jnp.exp(-bcum)).astype(BF16))
        k_dec.append((k * jnp.exp(b_last - bcum)).astype(BF16))
        decay.append(jnp.exp(b_last))
    ks = [slice(h * dk, (h + 1) * dk) for h in range(GLA_HEADS)]
    vs = [slice(h * dv, (h + 1) * dv) for h in range(GLA_HEADS)]
    v = {(ch, h): v_ref[0, chunk_rows[ch], vs[h]] for ch, h in cases}
    a = {(ch, h): jnp.where(causal, _dot_nt(q_t[ch][:, ks[h]], k_t[ch][:, ks[h]]), 0.0).astype(BF16)
         for ch, h in cases}
    kv = {(ch, h): _dot_tn(v[ch, h], k_dec[ch][:, ks[h]]) for ch, h in cases}
    o_intra = {(ch, h): _dot(a[ch, h], v[ch, h]) for ch, h in cases}
    state = [state_ref[h] for h in range(GLA_HEADS)]
    for ch, h in cases:
        o = o_intra[ch, h] + _dot_nt(q_t[ch][:, ks[h]], state[h].astype(BF16))
        state[h] = state[h] * decay[ch][:, ks[h]] + kv[ch, h]
        o = _rms(o, ng_ref[...])
        og = og_ref[0, chunk_rows[ch], vs[h]]
        obuf_ref[chunk_rows[ch], vs[h]] = (o * (og * jax.nn.sigmoid(og))).astype(BF16)
    for h in range(GLA_HEADS):
        state_ref[h] = state[h]
    o_ref[0] = x_ref[0] + _dot(obuf_ref[...], wo_ref[...])


def _gla_core(qk, v, og, gd, w_up, b_gate, norm_g, x, w_o):
    b, s, d = x.shape
    hh, dk, dv, rt = GLA_HEADS, GLA_DK, GLA_DV, GLA_ROWS
    return pl.pallas_call(
        _gla_body,
        out_shape=jax.ShapeDtypeStruct((b, s, d), F32),
        grid=(b, s // rt),
        in_specs=[pl.BlockSpec((1, rt, hh * dk), lambda bi, i: (bi, i, 0)),
                  pl.BlockSpec((1, rt, hh * dk), lambda bi, i: (bi, i, 1)),
                  pl.BlockSpec((1, rt, hh * dv), lambda bi, i: (bi, i, 0)),
                  pl.BlockSpec((1, rt, hh * dv), lambda bi, i: (bi, i, 0)),
                  pl.BlockSpec((1, rt, LANES), lambda bi, i: (bi, i, 0)),
                  _const_spec((LANES, hh * dk)), _const_spec((1, hh * dk)), _const_spec((1, dv)),
                  pl.BlockSpec((1, rt, d), lambda bi, i: (bi, i, 0)), _const_spec(w_o.shape)],
        out_specs=pl.BlockSpec((1, rt, d), lambda bi, i: (bi, i, 0)),
        scratch_shapes=[pltpu.VMEM((hh, dv, dk), F32), pltpu.VMEM((rt, hh * dv), BF16)],
        compiler_params=_params(("parallel", "arbitrary")),
        name="gla_core",
    )(qk, qk, v, og, gd, w_up, b_gate.reshape(1, hh * dk), norm_g.reshape(1, dv), x, w_o)


def _gla_mixer(x, g, w_in, w_gate_up, b_gate, norm_g, w_o):
    b, s, d = x.shape
    qw, vw = GLA_HEADS * GLA_DK, GLA_HEADS * GLA_DV
    x2 = x.reshape(b * s, d)
    w_gd = jnp.zeros((d, LANES), BF16).at[:, :GLA_GATE_RANK].set(w_in[:, 2 * qw + 2 * vw:].astype(BF16))
    w_up = jnp.zeros((LANES, qw), BF16).at[:GLA_GATE_RANK].set(w_gate_up.astype(BF16))
    qk, v, og, gd = _gla_in(x2, g, w_in[:, :2 * qw].astype(BF16), w_in[:, 2 * qw:2 * qw + vw].astype(BF16),
                            w_in[:, 2 * qw + vw:2 * qw + 2 * vw].astype(BF16), w_gd)
    r3 = lambda a: a.reshape(b, s, a.shape[-1])
    return _gla_core(r3(qk), r3(v), r3(og), r3(gd), w_up, b_gate, norm_g, x, w_o.astype(BF16))


def _nsa_in_body(x_ref, g_ref, wqt_ref, wk_ref, wvt_ref, wc_ref, wgt_ref,
                 qt_ref, kas_ref, kaw_ref, vast_ref, vawt_ref, c_ref, gtt_ref, *, ts):
    i = pl.program_id(1)
    qt, groups = NSA_QT, NSA_KV_HEADS
    h = _rms(x_ref[0], g_ref[...]).astype(BF16)
    q_all = (_dot_nt(wqt_ref[...], h) * (NSA_HEAD_DIM ** -0.5 * LOG2E)).astype(BF16)
    hw = NSA_REP * NSA_HEAD_DIM
    for gi in range(groups):
        for tt in range(ts // qt):
            qt_ref[0, gi, tt] = q_all[gi * hw:(gi + 1) * hw, tt * qt:(tt + 1) * qt]
    k = _dot(h, wk_ref[...])
    gw = groups * LANES
    lane = lax.broadcasted_iota(jnp.int32, (ts, LANES), 1)
    t = i * ts + lax.broadcasted_iota(jnp.int32, (ts, LANES), 0)
    onehot = jnp.where(lane - NSA_HEAD_DIM == t // NSA_SEL_BLOCK, 1.0, 0.0)
    for gi in range(groups):
        ls = slice(gi * LANES, (gi + 1) * LANES)
        kas_ref[0, gi] = (k[:, ls] + onehot).astype(BF16)
        kaw_ref[0, gi] = k[:, gw + gi * LANES:gw + (gi + 1) * LANES].astype(BF16)
    vt = _dot_nt(wvt_ref[...], h)
    ones = jnp.where(lax.broadcasted_iota(jnp.int32, (LANES, qt), 0) >= NSA_HEAD_DIM, 1.0, 0.0)
    for gi in range(groups):
        for tt in range(ts // qt):
            cs = slice(tt * qt, (tt + 1) * qt)
            vast_ref[0, gi, tt] = (vt[gi * LANES:(gi + 1) * LANES, cs] + ones).astype(BF16)
            vawt_ref[0, gi, tt] = (vt[gw + gi * LANES:gw + (gi + 1) * LANES, cs] + ones).astype(BF16)
    c = _dot(h, wc_ref[...])
    for gi in range(groups):
        c_ref[0, gi] = c[:, gi * LANES:(gi + 1) * LANES]
    gtt_ref[0] = _dot_nt(wgt_ref[...], h)


def _nsa_in(x, g, w_qt, w_k, w_vt, w_c, w_gt, ts=ROW_TILE):
    b, s, d = x.shape
    groups, qt = NSA_KV_HEADS, NSA_QT
    gw = groups * LANES
    nkt = s // qt
    tpb = ts // qt
    seq_spec = lambda n: pl.BlockSpec((1, ts, n), lambda bi, i: (bi, i, 0))
    rows_spec = lambda n: pl.BlockSpec((1, n, ts), lambda bi, i: (bi, 0, i))
    vt_spec = pl.BlockSpec((1, groups, tpb, LANES, qt), lambda bi, i: (bi, 0, i, 0, 0))
    vt_shape = jax.ShapeDtypeStruct((b, groups, nkt, LANES, qt), BF16)
    k_spec = pl.BlockSpec((1, groups, ts, LANES), lambda bi, i: (bi, 0, i, 0))
    k_shape = jax.ShapeDtypeStruct((b, groups, s, LANES), BF16)
    return pl.pallas_call(
        functools.partial(_nsa_in_body, ts=ts),
        out_shape=[jax.ShapeDtypeStruct((b, groups, nkt, w_qt.shape[0] // groups, qt), BF16),
                   k_shape, k_shape,
                   vt_shape, vt_shape,
                   jax.ShapeDtypeStruct((b, groups, s, LANES), F32),
                   jax.ShapeDtypeStruct((b, w_gt.shape[0], s), F32)],
        grid=(b, s // ts),
        in_specs=[pl.BlockSpec((1, ts, d), lambda bi, i: (bi, i, 0)), _const_spec((1, d)),
                  _const_spec(w_qt.shape), _const_spec(w_k.shape), _const_spec(w_vt.shape),
                  _const_spec(w_c.shape), _const_spec(w_gt.shape)],
        out_specs=[pl.BlockSpec((1, groups, tpb, w_qt.shape[0] // groups, qt), lambda bi, i: (bi, 0, i, 0, 0)),
                   k_spec, k_spec, vt_spec, vt_spec,
                   k_spec, rows_spec(w_gt.shape[0])],
        compiler_params=_params(("parallel", "parallel")),
        name="nsa_in",
    )(x, g.reshape(1, d), w_qt, w_k, w_vt, w_c, w_gt)


def _nsa_cmp_body(c_ref, pos_ref, w1t_ref, w1b_ref, w2k_ref, w2vt_ref, kc_ref, vct_ref):
    stride = NSA_CMP_STRIDE
    nch = c_ref.shape[2] // stride
    u = v = None
    for p in range(stride):
        x = c_ref[0, 0, pl.ds(p, nch, stride=stride), :]
        up = _dot((x + pos_ref[p:p + 1, :]).astype(BF16), w1t_ref[p])
        vp = _dot((x + pos_ref[stride + p:stride + p + 1, :]).astype(BF16), w1b_ref[p])
        u = up if u is None else u + up
        v = vp if v is None else v + vp
    hid = jax.nn.gelu(u + pltpu.roll(v, nch - 1, 0), approximate=True).astype(BF16)
    hw = hid.shape[-1] // 2
    kc_ref[0, 0] = _dot(hid[:, :hw], w2k_ref[...])
    vct_ref[0, 0] = _dot_nt(w2vt_ref[...], hid[:, hw:])


def _nsa_compress(c, cmp_pos, w1, w2):
    b, g, s, _ = c.shape
    dh, hidden = w2.shape[-1], w2.shape[1]
    stride = NSA_CMP_STRIDE
    nch = s // stride
    pos = jnp.concatenate([cmp_pos[0], cmp_pos[1]], axis=-1)
    w1r = w1.astype(BF16).reshape(2, NSA_CMP_BLOCK, dh, hidden)
    zero = jnp.zeros((NSA_CMP_BLOCK, dh, hidden), BF16)
    w1bd = jnp.concatenate([jnp.concatenate([w1r[0], zero], axis=-1),
                            jnp.concatenate([zero, w1r[1]], axis=-1)], axis=1)
    return pl.pallas_call(
        _nsa_cmp_body,
        out_shape=[jax.ShapeDtypeStruct((b, g, nch, dh), F32), jax.ShapeDtypeStruct((b, g, dh, nch), F32)],
        grid=(b, g),
        in_specs=[pl.BlockSpec((1, 1, s, 2 * dh), lambda bi, gi: (bi, gi, 0, 0)),
                  _const_spec(pos.shape), _const_spec((stride, 2 * dh, 2 * hidden)),
                  _const_spec((stride, 2 * dh, 2 * hidden)), _const_spec((hidden, dh)), _const_spec((dh, hidden))],
        out_specs=[pl.BlockSpec((1, 1, nch, dh), lambda bi, gi: (bi, gi, 0, 0)),
                   pl.BlockSpec((1, 1, dh, nch), lambda bi, gi: (bi, gi, 0, 0))],
        compiler_params=_params(("parallel", "parallel")),
        name="nsa_compress",
    )(c, pos, w1bd[:stride], w1bd[stride:], w2[0].astype(BF16), w2[1].astype(BF16).T)


def _nsa_attn_body(qt_ref, kc_ref, vct_ref, kas_ref, vast_ref, kaw_ref, vawt_ref, gtt_ref, ovt_ref, o_ref,
                   qaug_ref, m_ref, acc_ref, p_ref, alpha_ref, s_ref, rank_ref, *, n_slc):
    qt, rep, dh = NSA_QT, NSA_REP, NSA_HEAD_DIM
    cols = rep * qt
    qi = pl.program_id(2)
    s0 = qi * qt
    q = jnp.concatenate([qt_ref[0, 0, 0, r * dh:(r + 1) * dh, :] for r in range(rep)], axis=1)
    t_col = s0 + (lax.broadcasted_iota(jnp.int32, (1, cols), 1) & (qt - 1))

    n_pad = kc_ref.shape[2]
    sc = _dot(kc_ref[0, 0].astype(BF16), q)
    cmp_end = lax.broadcasted_iota(jnp.int32, (n_pad, 1), 0) * NSA_CMP_STRIDE + (NSA_CMP_BLOCK - 1)
    sc = jnp.where(cmp_end <= t_col, sc, NEG)
    e = jnp.exp2(sc - jnp.max(sc, axis=0, keepdims=True))
    p_cmp = e * (1.0 / jnp.sum(e, axis=0, keepdims=True))
    p_cmp = jnp.where(t_col >= NSA_CMP_BLOCK - 1, p_cmp, 0.0).astype(BF16)
    o_cmp = _dot(vct_ref[0, 0].astype(BF16), p_cmp)

    heads = [slice(r * qt, (r + 1) * qt) for r in range(rep)]

    wt = NSA_WINDOW // qt + 1
    lo = jnp.clip(qi - (wt - 1), 0, kaw_ref.shape[2] // qt - wt)
    kw = kaw_ref[0, 0, pl.ds(pl.multiple_of(lo * qt, qt), wt * qt), 0:dh]
    kpos = lo * qt + lax.broadcasted_iota(jnp.int32, (wt * qt, 1), 0)
    t_q = t_col[:, 0:qt]
    wbias = jnp.where((kpos <= t_q) & (kpos > t_q - NSA_WINDOW), 0.0, NEG)
    w_scores = [_dot(kw, q[:, cs]) for cs in heads]
    w_probs = []
    for s in w_scores:
        s = s + wbias
        w_probs.append(jnp.exp2(s - jnp.max(s, axis=0, keepdims=True)).astype(BF16))
    o_win = []
    for p in w_probs:
        pv = _dot(vawt_ref[0, 0, lo], p[0:qt])
        for u in range(1, wt):
            pv = pv + _dot(vawt_ref[0, 0, lo + u], p[u * qt:(u + 1) * qt])
        o_win.append(pv[0:dh] / pv[dh:dh + 1])
    o_win = jnp.concatenate(o_win, axis=1)

    imp = _dot(ovt_ref[...], p_cmp[:, 0:qt])
    for r in range(1, rep):
        imp = imp + _dot(ovt_ref[...], p_cmp[:, r * qt:(r + 1) * qt])
    imp_t = imp[:n_slc]
    blk = lax.broadcasted_iota(jnp.int32, (n_slc, 1), 0)
    cur = t_q // NSA_SEL_BLOCK
    forced = (blk == 0) | (blk == cur) | (blk == cur - 1)
    visible = blk * NSA_SEL_BLOCK <= t_q
    imp_t = jnp.where(forced, NSA_FORCE, imp_t)
    imp_t = jnp.where(visible, imp_t, -1.0)
    groups = [imp_t[k:k + SUBLANES] for k in range(0, n_slc, SUBLANES)]
    sub = lax.broadcasted_iota(jnp.int32, (SUBLANES, 1), 0)
    rank_ref[...] = jnp.zeros_like(rank_ref)
    last_blk = (s0 + qt - 1) // NSA_SEL_BLOCK
    for gi, own in enumerate(groups):
        @pl.when(gi * SUBLANES <= last_blk)
        def _(gi=gi, own=own):
            incs = [jnp.zeros((SUBLANES, qt), F32) for _ in groups]
            for si in range(SUBLANES):
                row = own[si:si + 1, :]
                for k, blk_imp in enumerate(groups):
                    if k < gi:
                        inc = jnp.where(row > blk_imp, 1.0, 0.0)
                    elif k > gi:
                        inc = jnp.where(row >= blk_imp, 1.0, 0.0)
                    else:
                        inc = jnp.where(sub > si, jnp.where(row >= blk_imp, 1.0, 0.0),
                                        jnp.where(row > blk_imp, 1.0, 0.0))
                    incs[k] = incs[k] + inc
            for k, inc in enumerate(incs):
                rank_ref[k * SUBLANES:(k + 1) * SUBLANES, :] += inc
    sel = (rank_ref[0:n_slc, :] < float(min(NSA_N_SELECT, n_slc))) & visible
    bias_parts = [jnp.where(sel, 0.0, NEG)]
    if LANES - dh - n_slc:
        bias_parts.append(jnp.zeros((LANES - dh - n_slc, qt), F32))
    bias = jnp.concatenate(bias_parts, axis=0).astype(BF16)
    qaug_ref[0:dh, :] = q
    for r in range(rep):
        qaug_ref[dh:, r * qt:(r + 1) * qt] = bias

    def score(ka_ref, j, nt):
        start = j * (nt * qt)
        if not isinstance(start, int):
            start = pl.multiple_of(start, nt * qt)
        ka = ka_ref[0, 0, pl.ds(start, nt * qt), :]
        return [_dot(ka, qaug_ref[:, cs]) for cs in heads]

    def flush(vat_ref, jp, nt):
        pvs = []
        for cs in heads:
            pv = _dot(vat_ref[0, 0, jp * nt], p_ref[0:qt, cs])
            for u in range(1, nt):
                pv = pv + _dot(vat_ref[0, 0, jp * nt + u], p_ref[u * qt:(u + 1) * qt, cs])
            pvs.append(pv)
        return pvs

    def accumulate(pvs):
        for cs, pv in zip(heads, pvs):
            acc_ref[:, cs] = acc_ref[:, cs] * alpha_ref[:, cs] + pv

    def step(ka_ref, vat_ref, j, nt, mask_fn, has_next):
        rows = nt * qt
        nxt = score(ka_ref, j + 1, nt) if has_next else None
        pvs = flush(vat_ref, jnp.maximum(j - 1, 0), nt)
        probs, alphas = [], []
        for cs in heads:
            s = s_ref[0:rows, cs]
            if mask_fn is not None:
                kpos = j * rows + lax.broadcasted_iota(jnp.int32, (rows, 1), 0)
                s = jnp.where(mask_fn(kpos, t_col[:, cs]), s, NEG)
            m_old = m_ref[:, cs]
            m_new = jnp.maximum(m_old, jnp.max(s, axis=0, keepdims=True))
            probs.append(jnp.exp2(s - m_new).astype(BF16))
            alphas.append(jnp.exp2(m_old - m_new))
            m_ref[:, cs] = m_new
        accumulate(pvs)
        for cs, p, alpha in zip(heads, probs, alphas):
            p_ref[0:rows, cs] = p
            alpha_ref[:, cs] = alpha
        if has_next:
            for cs, s in zip(heads, nxt):
                s_ref[0:rows, cs] = s

    def attend(ka_ref, vat_ref, nt, lo, hi, mask_fn, last_mask_fn):
        rows = nt * qt
        m_ref[...] = jnp.full_like(m_ref, NEG)
        acc_ref[...] = jnp.zeros_like(acc_ref)
        p_ref[0:rows, :] = jnp.zeros((rows, cols), BF16)
        alpha_ref[...] = jnp.ones_like(alpha_ref)
        for cs, s in zip(heads, score(ka_ref, lo, nt)):
            s_ref[0:rows, cs] = s

        def body(j, carry):
            step(ka_ref, vat_ref, j, nt, mask_fn, True)
            return carry

        lax.fori_loop(lo, hi, body, 0)
        step(ka_ref, vat_ref, hi, nt, last_mask_fn, False)
        accumulate(flush(vat_ref, hi, nt))
        acc = acc_ref[...]
        return acc[0:dh] / acc[dh:dh + 1]

    o_slc = attend(kas_ref, vast_ref, NSA_SEL_SPAN, 0, qi // NSA_SEL_SPAN, None, lambda kpos, t: kpos <= t)

    gates = jax.nn.sigmoid(gtt_ref[0])
    for r in range(rep):
        cs = slice(r * qt, (r + 1) * qt)
        o = (gates[3 * r:3 * r + 1] * o_cmp[:, cs] + gates[3 * r + 1:3 * r + 2] * o_slc[:, cs]
             + gates[3 * r + 2:3 * r + 3] * o_win[:, cs])
        o_ref[0, 0, 0, r * dh:(r + 1) * dh, :] = o.astype(BF16)


def _nsa_attn(q_t, kcmp, vcmp_t, kas, vas_t, kaw, vaw_t, gates_t, ov_t):
    b, g, nkt, _, qt = q_t.shape
    rep, dh = NSA_REP, NSA_HEAD_DIM
    s = nkt * qt
    n_slc = s // NSA_SEL_BLOCK
    nch = kcmp.shape[2]
    k_spec = pl.BlockSpec((1, 1, s, LANES), lambda bi, gi, i: (bi, gi, 0, 0))
    vt_spec = pl.BlockSpec((1, 1, nkt, LANES, qt), lambda bi, gi, i: (bi, gi, 0, 0, 0))
    q_spec = pl.BlockSpec((1, 1, 1, rep * dh, qt), lambda bi, gi, i: (bi, gi, i, 0, 0))
    return pl.pallas_call(
        functools.partial(_nsa_attn_body, n_slc=n_slc),
        out_shape=jax.ShapeDtypeStruct(q_t.shape, BF16),
        grid=(b, g, s // qt),
        in_specs=[q_spec,
                  pl.BlockSpec((1, 1, nch, dh), lambda bi, gi, i: (bi, gi, 0, 0)),
                  pl.BlockSpec((1, 1, dh, nch), lambda bi, gi, i: (bi, gi, 0, 0)),
                  k_spec, vt_spec, k_spec, vt_spec,
                  pl.BlockSpec((1, NSA_GATE_ROWS, qt), lambda bi, gi, i: (bi, gi, i)),
                  _const_spec(ov_t.shape)],
        out_specs=q_spec,
        scratch_shapes=[pltpu.VMEM((LANES, rep * qt), BF16), pltpu.VMEM((1, rep * qt), F32),
                        pltpu.VMEM((LANES, rep * qt), F32), pltpu.VMEM((NSA_SEL_SPAN * qt, rep * qt), BF16),
                        pltpu.VMEM((1, rep * qt), F32), pltpu.VMEM((NSA_SEL_SPAN * qt, rep * qt), F32),
                        pltpu.VMEM((LANES - dh, qt), F32)],
        compiler_params=_params(("parallel", "parallel", "parallel")),
        name="nsa_attn",
    )(q_t, kcmp, vcmp_t, kas, vas_t, kaw, vaw_t, gates_t, ov_t)


def _nsa_overlap_t(s):
    n_cmp = (s - NSA_CMP_BLOCK) // NSA_CMP_STRIDE + 1
    n_slc = s // NSA_SEL_BLOCK
    cs = np.arange(n_cmp) * NSA_CMP_STRIDE
    ss = np.arange(n_slc) * NSA_SEL_BLOCK
    ov = np.clip(np.minimum(cs[:, None] + NSA_CMP_BLOCK, ss[None, :] + NSA_SEL_BLOCK)
                 - np.maximum(cs[:, None], ss[None, :]), 0, None) / NSA_CMP_BLOCK
    out = np.zeros((LANES, s // NSA_CMP_STRIDE), np.float32)
    out[:n_slc, :n_cmp] = ov.T
    return jnp.asarray(out, BF16)


def _mm_res_t_body(at_ref, w_ref, x_ref, o_ref):
    groups, tiles, kw, qt = at_ref.shape[1:]
    for tt in range(tiles):
        rows = slice(tt * qt, (tt + 1) * qt)
        y = x_ref[0, rows, :]
        for gi in range(groups):
            y = y + _dot_tn(at_ref[0, gi, tt], w_ref[gi * kw:(gi + 1) * kw, :])
        o_ref[0, rows, :] = y


def _mm_res_t(a_t, w, x, ts=ROW_TILE):
    b, groups, nkt, kw, qt = a_t.shape
    s, n = nkt * qt, w.shape[1]
    tpb = ts // qt
    return pl.pallas_call(
        _mm_res_t_body,
        out_shape=jax.ShapeDtypeStruct((b, s, n), F32),
        grid=(b, s // ts),
        in_specs=[pl.BlockSpec((1, groups, tpb, kw, qt), lambda bi, i: (bi, 0, i, 0, 0)),
                  _const_spec(w.shape), pl.BlockSpec((1, ts, n), lambda bi, i: (bi, i, 0))],
        out_specs=pl.BlockSpec((1, ts, n), lambda bi, i: (bi, i, 0)),
        compiler_params=_params(("parallel", "parallel")),
        name="mm_res_t",
    )(a_t, w, x)


def _nsa_mixer(x, g, w_in, cmp_pos, cmp_w1, cmp_w2, w_o):
    b, s, d = x.shape
    hh, gg, rep, dh = NSA_HEADS, NSA_KV_HEADS, NSA_REP, NSA_HEAD_DIM
    kvw = gg * dh
    assert s % (NSA_SEL_SPAN * NSA_QT) == 0 and s % ROW_TILE == 0
    assert s // NSA_SEL_BLOCK <= LANES - dh and (s // NSA_SEL_BLOCK) % SUBLANES == 0
    wb = w_in.astype(BF16)
    parts = [wb[:, d + i * kvw:d + (i + 1) * kvw].reshape(d, gg, dh) for i in range(6)]
    zeros = jnp.zeros((d, gg, LANES - dh), BF16)
    padded = lambda p: jnp.concatenate([p, zeros], axis=-1).reshape(d, gg * LANES)
    w_k = jnp.concatenate([padded(parts[2]), padded(parts[4])], axis=-1)
    w_vt = jnp.concatenate([padded(parts[3]), padded(parts[5])], axis=-1).T
    w_c = jnp.concatenate([parts[0], parts[1]], axis=-1).reshape(d, gg * 2 * dh)
    w_g = wb[:, d + 6 * kvw:].reshape(d, gg, 3 * rep)
    w_gt = jnp.concatenate([w_g, jnp.zeros((d, gg, NSA_GATE_ROWS - 3 * rep), BF16)], axis=-1)
    w_gt = w_gt.reshape(d, gg * NSA_GATE_ROWS).T
    q_t, kas, kaw, vas_t, vaw_t, c, gates_t = _nsa_in(x, g, wb[:, :d].T, w_k, w_vt, w_c, w_gt)

    k_cmp, v_cmp_t = _nsa_compress(c, cmp_pos, cmp_w1, cmp_w2)
    o_t = _nsa_attn(q_t, k_cmp, v_cmp_t, kas, vas_t, kaw, vaw_t, gates_t, _nsa_overlap_t(s))
    return _mm_res_t(o_t, w_o.astype(BF16), x)


def kernel(x, mem, ffn1_norm, ffn1_w_in, ffn1_w_out, mix_norm, xattn_norm, mem_norm, xattn_w_q, xattn_w_kv, xattn_w_o, ffn2_norm, ffn2_w_in, ffn2_w_out, pool_w, pool_b, pool_scale, nsa_w_in, nsa_cmp_pos, nsa_cmp_w1, nsa_cmp_w2, nsa_w_o, gla_w_in, gla_w_gate_up, gla_b_gate, gla_norm, gla_w_o, conv_w_in, conv_b_in, conv_dw, conv_b_dw, conv_ln_g, conv_ln_b, conv_w_out, conv_b_out, final_norm):
    b, s, d = x.shape
    n_mem = mem.shape[1]
    depth = ffn1_norm.shape[0]
    n_mixers = 4
    flat = lambda a: a.reshape(b * s, d)
    cube = lambda a: a.reshape(b, s, d)
    mem2 = mem.reshape(b * n_mem, d)
    ffn1_w_in, ffn1_w_out, ffn2_w_in, ffn2_w_out, xattn_w_q, xattn_w_kv, xattn_w_o = (
        w.astype(BF16) for w in (ffn1_w_in, ffn1_w_out, ffn2_w_in, ffn2_w_out, xattn_w_q, xattn_w_kv, xattn_w_o))
    mem_kv = _rms_mm_layers(mem2, mem_norm, xattn_w_kv, BF16, tm=b * n_mem).reshape(depth, b, n_mem, 2 * d)
    for i in range(depth):
        m, j = i % n_mixers, i // n_mixers
        x = cube(_ffn(flat(x), ffn1_norm[i], ffn1_w_in, ffn1_w_out, i))
        if m == 0:
            x = _pool_mixer(x, mix_norm[i], pool_w[j].astype(BF16), pool_b[j], pool_scale[j])
        elif m == 1:
            x = _nsa_mixer(x, mix_norm[i], nsa_w_in[j], nsa_cmp_pos[j], nsa_cmp_w1[j], nsa_cmp_w2[j], nsa_w_o[j])
        elif m == 2:
            x = _gla_mixer(x, mix_norm[i], gla_w_in[j], gla_w_gate_up[j], gla_b_gate[j], gla_norm[j], gla_w_o[j])
        else:
            u = _conv_in(flat(x), mix_norm[i], conv_w_in[j].astype(BF16), conv_b_in[j])
            x = _conv_out(cube(u), x, conv_dw[j], conv_b_dw[j], conv_ln_g[j], conv_ln_b[j],
                          conv_w_out[j].astype(BF16), conv_b_out[j])
        x = _xattn(x, xattn_norm[i], xattn_w_q, mem_kv, xattn_w_o, i)
        final_g = final_norm if i == depth - 1 else None
        x = cube(_ffn(flat(x), ffn2_norm[i], ffn2_w_in, ffn2_w_out, i, final_g=final_g))
    return x
```

```python
import functools

import numpy as np
import jax
import jax.numpy as jnp
from jax import lax
from jax.experimental import pallas as pl
from jax.experimental.pallas import tpu as pltpu

F32 = jnp.float32
BF16 = jnp.bfloat16

EPS = 1e-6
NEG = -1e30

V7X_VMEM_BYTES = 64 * 1024 * 1024
VMEM_LIMIT = V7X_VMEM_BYTES - 8 * 1024 * 1024
LANES = 128
SUBLANES = 8

POOL_WINDOWS = (2, 4, 8, 16)
POOL_HALO = 16

NSA_HEADS = 16
NSA_KV_HEADS = 4
NSA_REP = NSA_HEADS // NSA_KV_HEADS
NSA_HEAD_DIM = 64
NSA_CMP_BLOCK = 32
NSA_CMP_STRIDE = 16
NSA_SEL_BLOCK = 64
NSA_N_SELECT = 16
NSA_WINDOW = 512
NSA_FORCE = 1e4
NSA_QT = 256
NSA_GATE_ROWS = 16
NSA_SEL_SPAN = 2
LOG2E = 1.4426950408889634

GLA_HEADS = 4
GLA_DK = 128
GLA_DV = 256
GLA_GATE_RANK = 16
GLA_TAU = 16.0
GLA_CHUNK = 64
GLA_ROWS = 512

CONV_WIDTH = 31
CONV_HALO = 32
CONV_STRIP = 128

XATTN_HEADS = 4

ROW_TILE = 1024
FFN_ROW_TILE = 512
FFN_CHUNKS = 1


def _params(sem):
    return pltpu.CompilerParams(dimension_semantics=sem, vmem_limit_bytes=VMEM_LIMIT)


def _rms(x, g):
    return x * lax.rsqrt(jnp.mean(x * x, axis=-1, keepdims=True) + EPS) * g


def _dot(a, b):
    return jnp.dot(a, b, preferred_element_type=F32)


def _dot_nt(a, b):
    return lax.dot_general(a, b, (((1,), (1,)), ((), ())), preferred_element_type=F32)


def _dot_tn(a, b):
    return lax.dot_general(a, b, (((0,), (0,)), ((), ())), preferred_element_type=F32)


def _const_spec(shape):
    nd = len(shape)
    return pl.BlockSpec(shape, lambda *_: (0,) * nd, pipeline_mode=pl.Buffered(1))


def _layer_spec(stack_shape, layer):
    return pl.BlockSpec((None,) + tuple(stack_shape[1:]), lambda *_: (layer, 0, 0), pipeline_mode=pl.Buffered(1))


def _ffn_body(x_ref, g_ref, win_ref, wout_ref, *rest, d_ff, final):
    if final:
        fg_ref, o_ref = rest
    else:
        (o_ref,) = rest
    x = x_ref[...]
    h = _rms(x, g_ref[...]).astype(BF16)
    fc = d_ff // FFN_CHUNKS
    y = None
    for c in range(FFN_CHUNKS):
        g = _dot(h, win_ref[:, c * fc:(c + 1) * fc])
        u = _dot(h, win_ref[:, d_ff + c * fc:d_ff + (c + 1) * fc])
        a = (g * jax.nn.sigmoid(g) * u).astype(BF16)
        yc = _dot(a, wout_ref[c * fc:(c + 1) * fc, :].astype(BF16))
        y = yc if y is None else y + yc
    out = x + 0.5 * y
    if final:
        out = _rms(out, fg_ref[...])
    o_ref[...] = out


def _ffn(x2, g, w_in, w_out, layer, final_g=None, tm=FFN_ROW_TILE):
    t, d = x2.shape
    d_ff = w_out.shape[1]
    final = final_g is not None
    in_specs = [pl.BlockSpec((tm, d), lambda i: (i, 0)), _const_spec((1, d)),
                _layer_spec(w_in.shape, layer), _layer_spec(w_out.shape, layer)]
    args = [x2, g.reshape(1, d), w_in, w_out]
    if final:
        in_specs.append(_const_spec((1, d)))
        args.append(final_g.reshape(1, d))
    return pl.pallas_call(
        functools.partial(_ffn_body, d_ff=d_ff, final=final),
        out_shape=jax.ShapeDtypeStruct((t, d), F32),
        grid=(t // tm,),
        in_specs=in_specs,
        out_specs=pl.BlockSpec((tm, d), lambda i: (i, 0)),
        compiler_params=_params(("parallel",)),
        name="ffn",
    )(*args)


def _rms_mm_body(x_ref, g_ref, w_ref, o_ref):
    h = _rms(x_ref[...], g_ref[...]).astype(BF16)
    o_ref[...] = _dot(h, w_ref[...]).astype(o_ref.dtype)


def _rms_mm_layers(x2, g, w, out_dtype, tm):
    t, d = x2.shape
    layers, _, n = w.shape
    return pl.pallas_call(
        _rms_mm_body,
        out_shape=jax.ShapeDtypeStruct((layers, t, n), out_dtype),
        grid=(layers, t // tm),
        in_specs=[pl.BlockSpec((tm, d), lambda l, i: (i, 0)),
                  pl.BlockSpec((None, 1, d), lambda l, i: (l, 0, 0)),
                  pl.BlockSpec((None, d, n), lambda l, i: (l, 0, 0))],
        out_specs=pl.BlockSpec((None, tm, n), lambda l, i: (l, i, 0)),
        compiler_params=_params(("parallel", "parallel")),
        name="rms_mm",
    )(x2, g.reshape(layers, 1, d), w)


def _xattn_body(x_ref, g_ref, wq_ref, k_ref, v_ref, wo_ref, o_ref, *, heads):
    x = x_ref[0]
    d = x.shape[-1]
    dh = d // heads
    h = _rms(x, g_ref[...]).astype(BF16)
    q = (_dot(h, wq_ref[...]) * dh ** -0.5).astype(BF16)
    cols = [slice(hd * dh, (hd + 1) * dh) for hd in range(heads)]
    scores = [_dot_nt(q[:, sl], k_ref[0][:, sl]) for sl in cols]
    probs = []
    for s in scores:
        e = jnp.exp(s - jnp.max(s, axis=-1, keepdims=True))
        probs.append((e * (1.0 / jnp.sum(e, axis=-1, keepdims=True))).astype(BF16))
    o = jnp.concatenate([_dot(p, v_ref[0][:, sl]).astype(BF16) for p, sl in zip(probs, cols)], axis=-1)
    o_ref[0] = x + _dot(o, wo_ref[...])


def _xattn(x, g, w_q, kv, w_o, layer, ts=ROW_TILE):
    b, s, d = x.shape
    n = kv.shape[2]
    return pl.pallas_call(
        functools.partial(_xattn_body, heads=XATTN_HEADS),
        out_shape=jax.ShapeDtypeStruct((b, s, d), F32),
        grid=(b, s // ts),
        in_specs=[pl.BlockSpec((1, ts, d), lambda bi, i: (bi, i, 0)), _const_spec((1, d)),
                  _layer_spec(w_q.shape, layer),
                  pl.BlockSpec((None, 1, n, d), lambda bi, i: (layer, bi, 0, 0)),
                  pl.BlockSpec((None, 1, n, d), lambda bi, i: (layer, bi, 0, 1)),
                  _layer_spec(w_o.shape, layer)],
        out_specs=pl.BlockSpec((1, ts, d), lambda bi, i: (bi, i, 0)),
        compiler_params=_params(("parallel", "parallel")),
        name="xattn",
    )(x, g.reshape(1, d), w_q, kv, kv, w_o)


def _pool_body(x_ref, halo_ref, g_ref, w_ref, b_ref, sc_ref, o_ref, hbuf_ref, l1_ref, l2_ref, *, ts):
    i = pl.program_id(1)
    x = x_ref[0]
    d = x.shape[-1]
    pad = POOL_HALO
    gw = d // len(POOL_WINDOWS)
    n = ts + pad
    g = g_ref[...]
    h = _rms(x, g)
    hh = _rms(halo_ref[0], g)
    hbuf_ref[0:pad, :] = jnp.zeros((pad, d), F32)
    l1_ref[0:pad, :] = jnp.zeros((pad, d - gw), F32)
    l2_ref[0:pad, :] = jnp.zeros((pad, d - 2 * gw), F32)
    hbuf_ref[pad:2 * pad, :] = jnp.where(i > 0, hh, 0.0)
    hbuf_ref[2 * pad:, :] = h
    a1 = hbuf_ref[pad:pad + n, :] + hbuf_ref[pad - 1:pad - 1 + n, :]
    l1_ref[pad:pad + n, :] = a1[:, gw:]
    a2 = a1[:, gw:] + l1_ref[pad - 2:pad - 2 + n, :]
    l2_ref[pad:pad + n, :] = a2[:, gw:]
    a3 = a2[:, gw:] + l2_ref[pad - 4:pad - 4 + n, :]
    a3_last = a3[:, gw:]
    a4 = a3_last[SUBLANES:] + a3_last[:n - SUBLANES]
    sums = [a1[pad:, 0:gw], a2[pad:, 0:gw], a3[pad:, 0:gw], a4[pad - SUBLANES:, :]]
    t = i * ts + lax.broadcasted_iota(jnp.int32, (ts, 1), 0)
    ys = []
    for gi, win in enumerate(POOL_WINDOWS):
        cs = slice(gi * gw, (gi + 1) * gw)
        cnt = jnp.minimum(t + 1, win).astype(F32)
        p = sums[gi] / cnt - h[:, cs]
        ys.append(_dot(p.astype(BF16), w_ref[gi]))
    y = (jnp.concatenate(ys, axis=-1) + b_ref[...]) * sc_ref[...]
    o_ref[0] = x + y


def _pool_mixer(x, g, w, bias, scale, ts=ROW_TILE):
    b, s, d = x.shape
    hb = ts // POOL_HALO
    return pl.pallas_call(
        functools.partial(_pool_body, ts=ts),
        out_shape=jax.ShapeDtypeStruct((b, s, d), F32),
        grid=(b, s // ts),
        in_specs=[pl.BlockSpec((1, ts, d), lambda bi, i: (bi, i, 0)),
                  pl.BlockSpec((1, POOL_HALO, d), lambda bi, i: (bi, jnp.maximum(i * hb - 1, 0), 0)),
                  _const_spec((1, d)), _const_spec(w.shape), _const_spec((1, d)), _const_spec((1, d))],
        out_specs=pl.BlockSpec((1, ts, d), lambda bi, i: (bi, i, 0)),
        scratch_shapes=[pltpu.VMEM((ts + 2 * POOL_HALO, d), F32),
                        pltpu.VMEM((ts + 2 * POOL_HALO, d - d // len(POOL_WINDOWS)), F32),
                        pltpu.VMEM((ts + 2 * POOL_HALO, d - 2 * (d // len(POOL_WINDOWS))), F32)],
        compiler_params=_params(("parallel", "parallel")),
        name="pool",
    )(x, x, g.reshape(1, d), w, bias.reshape(1, d), scale.reshape(1, d))


def _conv_in_body(x_ref, g_ref, w_ref, b_ref, o_ref):
    d = x_ref.shape[-1]
    h = _rms(x_ref[...], g_ref[...]).astype(BF16)
    gate = _dot(h, w_ref[:, d:]) + b_ref[:, d:]
    a = _dot(h, w_ref[:, :d]) + b_ref[:, :d]
    o_ref[...] = a * jax.nn.sigmoid(gate)


def _conv_in(x2, g, w, bias, tm=ROW_TILE):
    t, d = x2.shape
    return pl.pallas_call(
        _conv_in_body,
        out_shape=jax.ShapeDtypeStruct((t, d), F32),
        grid=(t // tm,),
        in_specs=[pl.BlockSpec((tm, d), lambda i: (i, 0)), _const_spec((1, d)),
                  _const_spec(w.shape), _const_spec((1, 2 * d))],
        out_specs=pl.BlockSpec((tm, d), lambda i: (i, 0)),
        compiler_params=_params(("parallel",)),
        name="conv_in",
    )(x2, g.reshape(1, d), w, bias.reshape(1, 2 * d))


def _conv_out_body(u_ref, halo_ref, x_ref, dw_ref, bdw_ref, lng_ref, lnb_ref, w_ref, bo_ref, o_ref,
                   ubuf_ref, cbuf_ref, *, ts):
    i = pl.program_id(1)
    n_slab = ubuf_ref.shape[0]
    halo = jnp.where(i > 0, halo_ref[0], 0.0)
    for lb in range(n_slab):
        ls = slice(lb * LANES, (lb + 1) * LANES)
        ubuf_ref[lb, 0:CONV_HALO, :] = halo[:, ls]
        ubuf_ref[lb, CONV_HALO:CONV_HALO + ts, :] = u_ref[0, :, ls]
        ubuf_ref[lb, CONV_HALO + ts:, :] = jnp.zeros((SUBLANES, LANES), F32)
    lead = CONV_HALO - (CONV_WIDTH - 1)

    def slab(lb, carry):
        for r0 in range(0, ts, CONV_STRIP):
            acc = None
            for sh in range(SUBLANES):
                taps = [k for k in range(CONV_WIDTH) if (lead + k) % SUBLANES == sh]
                if not taps:
                    continue
                win = ubuf_ref[lb, pl.ds(r0 + sh, CONV_STRIP + CONV_HALO), :]
                for k in taps:
                    a8 = (lead + k) // SUBLANES * SUBLANES
                    term = win[a8:a8 + CONV_STRIP] * dw_ref[lb, k:k + 1, :]
                    acc = term if acc is None else acc + term
            cbuf_ref[lb, r0:r0 + CONV_STRIP, :] = acc + bdw_ref[lb]
        return carry

    lax.fori_loop(0, n_slab, slab, 0)
    c = jnp.concatenate([cbuf_ref[lb] for lb in range(n_slab)], axis=-1)
    mu = jnp.mean(c, axis=-1, keepdims=True)
    var = jnp.mean(jnp.square(c - mu), axis=-1, keepdims=True)
    n = (c - mu) * lax.rsqrt(var + EPS) * lng_ref[...] + lnb_ref[...]
    a = (n * jax.nn.sigmoid(n)).astype(BF16)
    o_ref[0] = x_ref[0] + _dot(a, w_ref[...]) + bo_ref[...]


def _conv_out(u, x, dw, b_dw, ln_g, ln_b, w_out, b_out, ts=ROW_TILE):
    b, s, d = x.shape
    hb = ts // CONV_HALO
    n_slab = d // LANES
    dw_pad = jnp.zeros((CONV_HALO, d), F32).at[:CONV_WIDTH].set(dw)
    dw_slab = dw_pad.reshape(CONV_HALO, n_slab, LANES).transpose(1, 0, 2)
    bdw_slab = b_dw.reshape(n_slab, 1, LANES)
    row = lambda v: v.reshape(1, d)
    return pl.pallas_call(
        functools.partial(_conv_out_body, ts=ts),
        out_shape=jax.ShapeDtypeStruct((b, s, d), F32),
        grid=(b, s // ts),
        in_specs=[pl.BlockSpec((1, ts, d), lambda bi, i: (bi, i, 0)),
                  pl.BlockSpec((1, CONV_HALO, d), lambda bi, i: (bi, jnp.maximum(i * hb - 1, 0), 0)),
                  pl.BlockSpec((1, ts, d), lambda bi, i: (bi, i, 0)),
                  _const_spec(dw_slab.shape), _const_spec(bdw_slab.shape), _const_spec((1, d)),
                  _const_spec((1, d)), _const_spec(w_out.shape), _const_spec((1, d))],
        out_specs=pl.BlockSpec((1, ts, d), lambda bi, i: (bi, i, 0)),
        scratch_shapes=[pltpu.VMEM((n_slab, ts + CONV_HALO + SUBLANES, LANES), F32),
                        pltpu.VMEM((n_slab, ts, LANES), F32)],
        compiler_params=_params(("parallel", "parallel")),
        name="conv_out",
    )(u, u, x, dw_slab, bdw_slab, row(ln_g), row(ln_b), w_out, row(b_out))


def _gla_in_body(x_ref, g_ref, wqk_ref, wv_ref, wog_ref, wgd_ref, qk_ref, v_ref, og_ref, gd_ref):
    h = _rms(x_ref[...], g_ref[...]).astype(BF16)
    qk_ref[...] = _dot(h, wqk_ref[...])
    v_ref[...] = _dot(h, wv_ref[...]).astype(BF16)
    og_ref[...] = _dot(h, wog_ref[...])
    gd_ref[...] = _dot(h, wgd_ref[...])


def _gla_in(x2, g, w_qk, w_v, w_og, w_gd, tm=ROW_TILE):
    t, d = x2.shape
    widths = (w_qk.shape[1], w_v.shape[1], w_og.shape[1], w_gd.shape[1])
    dtypes = (F32, BF16, F32, F32)
    return pl.pallas_call(
        _gla_in_body,
        out_shape=[jax.ShapeDtypeStruct((t, n), dt) for n, dt in zip(widths, dtypes)],
        grid=(t // tm,),
        in_specs=[pl.BlockSpec((tm, d), lambda i: (i, 0)), _const_spec((1, d)),
                  _const_spec(w_qk.shape), _const_spec(w_v.shape), _const_spec(w_og.shape),
                  _const_spec(w_gd.shape)],
        out_specs=[pl.BlockSpec((tm, n), lambda i: (i, 0)) for n in widths],
        compiler_params=_params(("parallel",)),
        name="gla_in",
    )(x2, g.reshape(1, d), w_qk, w_v, w_og, w_gd)


def _gla_body(q_ref, k_ref, v_ref, og_ref, gd_ref, wup_ref, bg_ref, ng_ref, x_ref, wo_ref, o_ref,
              state_ref, obuf_ref):
    @pl.when(pl.program_id(1) == 0)
    def _():
        state_ref[...] = jnp.zeros_like(state_ref)

    c, dk, dv = GLA_CHUNK, GLA_DK, GLA_DV
    ga = _dot(gd_ref[0].astype(BF16), wup_ref[...]) + bg_ref[...]
    log_a = (jnp.minimum(ga, 0.0) - jnp.log1p(jnp.exp(-jnp.abs(ga)))) / GLA_TAU
    ri = lax.broadcasted_iota(jnp.int32, (c, c), 0)
    ci = lax.broadcasted_iota(jnp.int32, (c, c), 1)
    causal = ci <= ri
    tril = jnp.where(causal, 1.0, 0.0).astype(BF16)
    scale = dk ** -0.5
    n_chunks = GLA_ROWS // c
    chunk_rows = [slice(ch * c, (ch + 1) * c) for ch in range(n_chunks)]
    cases = [(ch, h) for ch in range(n_chunks) for h in range(GLA_HEADS)]
    hi = log_a.astype(BF16)
    r1 = log_a - hi.astype(F32)
    mid = r1.astype(BF16)
    lo = (r1 - mid.astype(F32)).astype(BF16)
    q_t, k_t, k_dec, decay = [], [], [], []
    for rows in chunk_rows:
        bcum = _dot(tril, hi[rows]) + _dot(tril, mid[rows]) + _dot(tril, lo[rows])
        b_last = bcum[c - 1:c, :]
        k = k_ref[0, rows, :]
        q_t.append((q_ref[0, rows, :] * scale * jnp.exp(bcum)).astype(BF16))
        k_t.append((k * jnp.exp(-bcum)).astype(BF16))
        k_dec.append((k * jnp.exp(b_last - bcum)).astype(BF16))
        decay.append(jnp.exp(b_last))
    ks = [slice(h * dk, (h + 1) * dk) for h in range(GLA_HEADS)]
    vs = [slice(h * dv, (h + 1) * dv) for h in range(GLA_HEADS)]
    v = {(ch, h): v_ref[0, chunk_rows[ch], vs[h]] for ch, h in cases}
    a = {(ch, h): jnp.where(causal, _dot_nt(q_t[ch][:, ks[h]], k_t[ch][:, ks[h]]), 0.0).astype(BF16)
         for ch, h in cases}
    kv = {(ch, h): _dot_tn(v[ch, h], k_dec[ch][:, ks[h]]) for ch, h in cases}
    o_intra = {(ch, h): _dot(a[ch, h], v[ch, h]) for ch, h in cases}
    state = [state_ref[h] for h in range(GLA_HEADS)]
    for ch, h in cases:
        o = o_intra[ch, h] + _dot_nt(q_t[ch][:, ks[h]], state[h].astype(BF16))
        state[h] = state[h] * decay[ch][:, ks[h]] + kv[ch, h]
        o = _rms(o, ng_ref[...])
        og = og_ref[0, chunk_rows[ch], vs[h]]
        obuf_ref[chunk_rows[ch], vs[h]] = (o * (og * jax.nn.sigmoid(og))).astype(BF16)
    for h in range(GLA_HEADS):
        state_ref[h] = state[h]
    o_ref[0] = x_ref[0] + _dot(obuf_ref[...], wo_ref[...])


def _gla_core(qk, v, og, gd, w_up, b_gate, norm_g, x, w_o):
    b, s, d = x.shape
    hh, dk, dv, rt = GLA_HEADS, GLA_DK, GLA_DV, GLA_ROWS
    return pl.pallas_call(
        _gla_body,
        out_shape=jax.ShapeDtypeStruct((b, s, d), F32),
        grid=(b, s // rt),
        in_specs=[pl.BlockSpec((1, rt, hh * dk), lambda bi, i: (bi, i, 0)),
                  pl.BlockSpec((1, rt, hh * dk), lambda bi, i: (bi, i, 1)),
                  pl.BlockSpec((1, rt, hh * dv), lambda bi, i: (bi, i, 0)),
                  pl.BlockSpec((1, rt, hh * dv), lambda bi, i: (bi, i, 0)),
                  pl.BlockSpec((1, rt, LANES), lambda bi, i: (bi, i, 0)),
                  _const_spec((LANES, hh * dk)), _const_spec((1, hh * dk)), _const_spec((1, dv)),
                  pl.BlockSpec((1, rt, d), lambda bi, i: (bi, i, 0)), _const_spec(w_o.shape)],
        out_specs=pl.BlockSpec((1, rt, d), lambda bi, i: (bi, i, 0)),
        scratch_shapes=[pltpu.VMEM((hh, dv, dk), F32), pltpu.VMEM((rt, hh * dv), BF16)],
        compiler_params=_params(("parallel", "arbitrary")),
        name="gla_core",
    )(qk, qk, v, og, gd, w_up, b_gate.reshape(1, hh * dk), norm_g.reshape(1, dv), x, w_o)


def _gla_mixer(x, g, w_in, w_gate_up, b_gate, norm_g, w_o):
    b, s, d = x.shape
    qw, vw = GLA_HEADS * GLA_DK, GLA_HEADS * GLA_DV
    x2 = x.reshape(b * s, d)
    w_gd = jnp.zeros((d, LANES), BF16).at[:, :GLA_GATE_RANK].set(w_in[:, 2 * qw + 2 * vw:].astype(BF16))
    w_up = jnp.zeros((LANES, qw), BF16).at[:GLA_GATE_RANK].set(w_gate_up.astype(BF16))
    qk, v, og, gd = _gla_in(x2, g, w_in[:, :2 * qw].astype(BF16), w_in[:, 2 * qw:2 * qw + vw].astype(BF16),
                            w_in[:, 2 * qw + vw:2 * qw + 2 * vw].astype(BF16), w_gd)
    r3 = lambda a: a.reshape(b, s, a.shape[-1])
    return _gla_core(r3(qk), r3(v), r3(og), r3(gd), w_up, b_gate, norm_g, x, w_o.astype(BF16))


def _nsa_in_body(x_ref, g_ref, wqt_ref, wk_ref, wvt_ref, wc_ref, wgt_ref,
                 qt_ref, kas_ref, kaw_ref, vast_ref, vawt_ref, c_ref, gtt_ref, *, ts):
    i = pl.program_id(1)
    qt, groups = NSA_QT, NSA_KV_HEADS
    h = _rms(x_ref[0], g_ref[...]).astype(BF16)
    q_all = (_dot_nt(wqt_ref[...], h) * (NSA_HEAD_DIM ** -0.5 * LOG2E)).astype(BF16)
    hw = NSA_REP * NSA_HEAD_DIM
    for gi in range(groups):
        for tt in range(ts // qt):
            qt_ref[0, gi, tt] = q_all[gi * hw:(gi + 1) * hw, tt * qt:(tt + 1) * qt]
    k = _dot(h, wk_ref[...])
    gw = groups * LANES
    lane = lax.broadcasted_iota(jnp.int32, (ts, LANES), 1)
    t = i * ts + lax.broadcasted_iota(jnp.int32, (ts, LANES), 0)
    onehot = jnp.where(lane - NSA_HEAD_DIM == t // NSA_SEL_BLOCK, 1.0, 0.0)
    for gi in range(groups):
        ls = slice(gi * LANES, (gi + 1) * LANES)
        kas_ref[0, gi] = (k[:, ls] + onehot).astype(BF16)
        kaw_ref[0, gi] = k[:, gw + gi * LANES:gw + (gi + 1) * LANES].astype(BF16)
    vt = _dot_nt(wvt_ref[...], h)
    ones = jnp.where(lax.broadcasted_iota(jnp.int32, (LANES, qt), 0) >= NSA_HEAD_DIM, 1.0, 0.0)
    for gi in range(groups):
        for tt in range(ts // qt):
            cs = slice(tt * qt, (tt + 1) * qt)
            vast_ref[0, gi, tt] = (vt[gi * LANES:(gi + 1) * LANES, cs] + ones).astype(BF16)
            vawt_ref[0, gi, tt] = (vt[gw + gi * LANES:gw + (gi + 1) * LANES, cs] + ones).astype(BF16)
    c = _dot(h, wc_ref[...])
    for gi in range(groups):
        c_ref[0, gi] = c[:, gi * LANES:(gi + 1) * LANES]
    gtt_ref[0] = _dot_nt(wgt_ref[...], h)


def _nsa_in(x, g, w_qt, w_k, w_vt, w_c, w_gt, ts=ROW_TILE):
    b, s, d = x.shape
    groups, qt = NSA_KV_HEADS, NSA_QT
    gw = groups * LANES
    nkt = s // qt
    tpb = ts // qt
    seq_spec = lambda n: pl.BlockSpec((1, ts, n), lambda bi, i: (bi, i, 0))
    rows_spec = lambda n: pl.BlockSpec((1, n, ts), lambda bi, i: (bi, 0, i))
    vt_spec = pl.BlockSpec((1, groups, tpb, LANES, qt), lambda bi, i: (bi, 0, i, 0, 0))
    vt_shape = jax.ShapeDtypeStruct((b, groups, nkt, LANES, qt), BF16)
    k_spec = pl.BlockSpec((1, groups, ts, LANES), lambda bi, i: (bi, 0, i, 0))
    k_shape = jax.ShapeDtypeStruct((b, groups, s, LANES), BF16)
    return pl.pallas_call(
        functools.partial(_nsa_in_body, ts=ts),
        out_shape=[jax.ShapeDtypeStruct((b, groups, nkt, w_qt.shape[0] // groups, qt), BF16),
                   k_shape, k_shape,
                   vt_shape, vt_shape,
                   jax.ShapeDtypeStruct((b, groups, s, LANES), F32),
                   jax.ShapeDtypeStruct((b, w_gt.shape[0], s), F32)],
        grid=(b, s // ts),
        in_specs=[pl.BlockSpec((1, ts, d), lambda bi, i: (bi, i, 0)), _const_spec((1, d)),
                  _const_spec(w_qt.shape), _const_spec(w_k.shape), _const_spec(w_vt.shape),
                  _const_spec(w_c.shape), _const_spec(w_gt.shape)],
        out_specs=[pl.BlockSpec((1, groups, tpb, w_qt.shape[0] // groups, qt), lambda bi, i: (bi, 0, i, 0, 0)),
                   k_spec, k_spec, vt_spec, vt_spec,
                   k_spec, rows_spec(w_gt.shape[0])],
        compiler_params=_params(("parallel", "parallel")),
        name="nsa_in",
    )(x, g.reshape(1, d), w_qt, w_k, w_vt, w_c, w_gt)


def _nsa_cmp_body(c_ref, pos_ref, w1t_ref, w1b_ref, w2k_ref, w2vt_ref, kc_ref, vct_ref):
    stride = NSA_CMP_STRIDE
    nch = c_ref.shape[2] // stride
    u = v = None
    for p in range(stride):
        x = c_ref[0, 0, pl.ds(p, nch, stride=stride), :]
        up = _dot((x + pos_ref[p:p + 1, :]).astype(BF16), w1t_ref[p])
        vp = _dot((x + pos_ref[stride + p:stride + p + 1, :]).astype(BF16), w1b_ref[p])
        u = up if u is None else u + up
        v = vp if v is None else v + vp
    hid = jax.nn.gelu(u + pltpu.roll(v, nch - 1, 0), approximate=True).astype(BF16)
    hw = hid.shape[-1] // 2
    kc_ref[0, 0] = _dot(hid[:, :hw], w2k_ref[...])
    vct_ref[0, 0] = _dot_nt(w2vt_ref[...], hid[:, hw:])


def _nsa_compress(c, cmp_pos, w1, w2):
    b, g, s, _ = c.shape
    dh, hidden = w2.shape[-1], w2.shape[1]
    stride = NSA_CMP_STRIDE
    nch = s // stride
    pos = jnp.concatenate([cmp_pos[0], cmp_pos[1]], axis=-1)
    w1r = w1.astype(BF16).reshape(2, NSA_CMP_BLOCK, dh, hidden)
    zero = jnp.zeros((NSA_CMP_BLOCK, dh, hidden), BF16)
    w1bd = jnp.concatenate([jnp.concatenate([w1r[0], zero], axis=-1),
                            jnp.concatenate([zero, w1r[1]], axis=-1)], axis=1)
    return pl.pallas_call(
        _nsa_cmp_body,
        out_shape=[jax.ShapeDtypeStruct((b, g, nch, dh), F32), jax.ShapeDtypeStruct((b, g, dh, nch), F32)],
        grid=(b, g),
        in_specs=[pl.BlockSpec((1, 1, s, 2 * dh), lambda bi, gi: (bi, gi, 0, 0)),
                  _const_spec(pos.shape), _const_spec((stride, 2 * dh, 2 * hidden)),
                  _const_spec((stride, 2 * dh, 2 * hidden)), _const_spec((hidden, dh)), _const_spec((dh, hidden))],
        out_specs=[pl.BlockSpec((1, 1, nch, dh), lambda bi, gi: (bi, gi, 0, 0)),
                   pl.BlockSpec((1, 1, dh, nch), lambda bi, gi: (bi, gi, 0, 0))],
        compiler_params=_params(("parallel", "parallel")),
        name="nsa_compress",
    )(c, pos, w1bd[:stride], w1bd[stride:], w2[0].astype(BF16), w2[1].astype(BF16).T)


def _nsa_attn_body(qt_ref, kc_ref, vct_ref, kas_ref, vast_ref, kaw_ref, vawt_ref, gtt_ref, ovt_ref, o_ref,
                   qaug_ref, m_ref, acc_ref, p_ref, alpha_ref, s_ref, rank_ref, *, n_slc):
    qt, rep, dh = NSA_QT, NSA_REP, NSA_HEAD_DIM
    cols = rep * qt
    qi = pl.program_id(2)
    s0 = qi * qt
    q = jnp.concatenate([qt_ref[0, 0, 0, r * dh:(r + 1) * dh, :] for r in range(rep)], axis=1)
    t_col = s0 + (lax.broadcasted_iota(jnp.int32, (1, cols), 1) & (qt - 1))

    n_pad = kc_ref.shape[2]
    sc = _dot(kc_ref[0, 0].astype(BF16), q)
    cmp_end = lax.broadcasted_iota(jnp.int32, (n_pad, 1), 0) * NSA_CMP_STRIDE + (NSA_CMP_BLOCK - 1)
    sc = jnp.where(cmp_end <= t_col, sc, NEG)
    e = jnp.exp2(sc - jnp.max(sc, axis=0, keepdims=True))
    p_cmp = e * (1.0 / jnp.sum(e, axis=0, keepdims=True))
    p_cmp = jnp.where(t_col >= NSA_CMP_BLOCK - 1, p_cmp, 0.0).astype(BF16)
    o_cmp = _dot(vct_ref[0, 0].astype(BF16), p_cmp)

    heads = [slice(r * qt, (r + 1) * qt) for r in range(rep)]

    wt = NSA_WINDOW // qt + 1
    lo = jnp.clip(qi - (wt - 1), 0, kaw_ref.shape[2] // qt - wt)
    kw = kaw_ref[0, 0, pl.ds(pl.multiple_of(lo * qt, qt), wt * qt), 0:dh]
    kpos = lo * qt + lax.broadcasted_iota(jnp.int32, (wt * qt, 1), 0)
    t_q = t_col[:, 0:qt]
    wbias = jnp.where((kpos <= t_q) & (kpos > t_q - NSA_WINDOW), 0.0, NEG)
    w_scores = [_dot(kw, q[:, cs]) for cs in heads]
    w_probs = []
    for s in w_scores:
        s = s + wbias
        w_probs.append(jnp.exp2(s - jnp.max(s, axis=0, keepdims=True)).astype(BF16))
    o_win = []
    for p in w_probs:
        pv = _dot(vawt_ref[0, 0, lo], p[0:qt])
        for u in range(1, wt):
            pv = pv + _dot(vawt_ref[0, 0, lo + u], p[u * qt:(u + 1) * qt])
        o_win.append(pv[0:dh] / pv[dh:dh + 1])
    o_win = jnp.concatenate(o_win, axis=1)

    imp = _dot(ovt_ref[...], p_cmp[:, 0:qt])
    for r in range(1, rep):
        imp = imp + _dot(ovt_ref[...], p_cmp[:, r * qt:(r + 1) * qt])
    imp_t = imp[:n_slc]
    blk = lax.broadcasted_iota(jnp.int32, (n_slc, 1), 0)
    cur = t_q // NSA_SEL_BLOCK
    forced = (blk == 0) | (blk == cur) | (blk == cur - 1)
    visible = blk * NSA_SEL_BLOCK <= t_q
    imp_t = jnp.where(forced, NSA_FORCE, imp_t)
    imp_t = jnp.where(visible, imp_t, -1.0)
    groups = [imp_t[k:k + SUBLANES] for k in range(0, n_slc, SUBLANES)]
    sub = lax.broadcasted_iota(jnp.int32, (SUBLANES, 1), 0)
    rank_ref[...] = jnp.zeros_like(rank_ref)
    last_blk = (s0 + qt - 1) // NSA_SEL_BLOCK
    for gi, own in enumerate(groups):
        @pl.when(gi * SUBLANES <= last_blk)
        def _(gi=gi, own=own):
            incs = [jnp.zeros((SUBLANES, qt), F32) for _ in groups]
            for si in range(SUBLANES):
                row = own[si:si + 1, :]
                for k, blk_imp in enumerate(groups):
                    if k < gi:
                        inc = jnp.where(row > blk_imp, 1.0, 0.0)
                    elif k > gi:
                        inc = jnp.where(row >= blk_imp, 1.0, 0.0)
                    else:
                        inc = jnp.where(sub > si, jnp.where(row >= blk_imp, 1.0, 0.0),
                                        jnp.where(row > blk_imp, 1.0, 0.0))
                    incs[k] = incs[k] + inc
            for k, inc in enumerate(incs):
                rank_ref[k * SUBLANES:(k + 1) * SUBLANES, :] += inc
    sel = (rank_ref[0:n_slc, :] < float(min(NSA_N_SELECT, n_slc))) & visible
    bias_parts = [jnp.where(sel, 0.0, NEG)]
    if LANES - dh - n_slc:
        bias_parts.append(jnp.zeros((LANES - dh - n_slc, qt), F32))
    bias = jnp.concatenate(bias_parts, axis=0).astype(BF16)
    qaug_ref[0:dh, :] = q
    for r in range(rep):
        qaug_ref[dh:, r * qt:(r + 1) * qt] = bias

    def score(ka_ref, j, nt):
        start = j * (nt * qt)
        if not isinstance(start, int):
            start = pl.multiple_of(start, nt * qt)
        ka = ka_ref[0, 0, pl.ds(start, nt * qt), :]
        return [_dot(ka, qaug_ref[:, cs]) for cs in heads]

    def flush(vat_ref, jp, nt):
        pvs = []
        for cs in heads:
            pv = _dot(vat_ref[0, 0, jp * nt], p_ref[0:qt, cs])
            for u in range(1, nt):
                pv = pv + _dot(vat_ref[0, 0, jp * nt + u], p_ref[u * qt:(u + 1) * qt, cs])
            pvs.append(pv)
        return pvs

    def accumulate(pvs):
        for cs, pv in zip(heads, pvs):
            acc_ref[:, cs] = acc_ref[:, cs] * alpha_ref[:, cs] + pv

    def step(ka_ref, vat_ref, j, nt, mask_fn, has_next):
        rows = nt * qt
        nxt = score(ka_ref, j + 1, nt) if has_next else None
        pvs = flush(vat_ref, jnp.maximum(j - 1, 0), nt)
        probs, alphas = [], []
        for cs in heads:
            s = s_ref[0:rows, cs]
            if mask_fn is not None:
                kpos = j * rows + lax.broadcasted_iota(jnp.int32, (rows, 1), 0)
                s = jnp.where(mask_fn(kpos, t_col[:, cs]), s, NEG)
            m_old = m_ref[:, cs]
            m_new = jnp.maximum(m_old, jnp.max(s, axis=0, keepdims=True))
            probs.append(jnp.exp2(s - m_new).astype(BF16))
            alphas.append(jnp.exp2(m_old - m_new))
            m_ref[:, cs] = m_new
        accumulate(pvs)
        for cs, p, alpha in zip(heads, probs, alphas):
            p_ref[0:rows, cs] = p
            alpha_ref[:, cs] = alpha
        if has_next:
            for cs, s in zip(heads, nxt):
                s_ref[0:rows, cs] = s

    def attend(ka_ref, vat_ref, nt, lo, hi, mask_fn, last_mask_fn):
        rows = nt * qt
        m_ref[...] = jnp.full_like(m_ref, NEG)
        acc_ref[...] = jnp.zeros_like(acc_ref)
        p_ref[0:rows, :] = jnp.zeros((rows, cols), BF16)
        alpha_ref[...] = jnp.ones_like(alpha_ref)
        for cs, s in zip(heads, score(ka_ref, lo, nt)):
            s_ref[0:rows, cs] = s

        def body(j, carry):
            step(ka_ref, vat_ref, j, nt, mask_fn, True)
            return carry

        lax.fori_loop(lo, hi, body, 0)
        step(ka_ref, vat_ref, hi, nt, last_mask_fn, False)
        accumulate(flush(vat_ref, hi, nt))
        acc = acc_ref[...]
        return acc[0:dh] / acc[dh:dh + 1]

    o_slc = attend(kas_ref, vast_ref, NSA_SEL_SPAN, 0, qi // NSA_SEL_SPAN, None, lambda kpos, t: kpos <= t)

    gates = jax.nn.sigmoid(gtt_ref[0])
    for r in range(rep):
        cs = slice(r * qt, (r + 1) * qt)
        o = (gates[3 * r:3 * r + 1] * o_cmp[:, cs] + gates[3 * r + 1:3 * r + 2] * o_slc[:, cs]
             + gates[3 * r + 2:3 * r + 3] * o_win[:, cs])
        o_ref[0, 0, 0, r * dh:(r + 1) * dh, :] = o.astype(BF16)


def _nsa_attn(q_t, kcmp, vcmp_t, kas, vas_t, kaw, vaw_t, gates_t, ov_t):
    b, g, nkt, _, qt = q_t.shape
    rep, dh = NSA_REP, NSA_HEAD_DIM
    s = nkt * qt
    n_slc = s // NSA_SEL_BLOCK
    nch = kcmp.shape[2]
    k_spec = pl.BlockSpec((1, 1, s, LANES), lambda bi, gi, i: (bi, gi, 0, 0))
    vt_spec = pl.BlockSpec((1, 1, nkt, LANES, qt), lambda bi, gi, i: (bi, gi, 0, 0, 0))
    q_spec = pl.BlockSpec((1, 1, 1, rep * dh, qt), lambda bi, gi, i: (bi, gi, i, 0, 0))
    return pl.pallas_call(
        functools.partial(_nsa_attn_body, n_slc=n_slc),
        out_shape=jax.ShapeDtypeStruct(q_t.shape, BF16),
        grid=(b, g, s // qt),
        in_specs=[q_spec,
                  pl.BlockSpec((1, 1, nch, dh), lambda bi, gi, i: (bi, gi, 0, 0)),
                  pl.BlockSpec((1, 1, dh, nch), lambda bi, gi, i: (bi, gi, 0, 0)),
                  k_spec, vt_spec, k_spec, vt_spec,
                  pl.BlockSpec((1, NSA_GATE_ROWS, qt), lambda bi, gi, i: (bi, gi, i)),
                  _const_spec(ov_t.shape)],
        out_specs=q_spec,
        scratch_shapes=[pltpu.VMEM((LANES, rep * qt), BF16), pltpu.VMEM((1, rep * qt), F32),
                        pltpu.VMEM((LANES, rep * qt), F32), pltpu.VMEM((NSA_SEL_SPAN * qt, rep * qt), BF16),
                        pltpu.VMEM((1, rep * qt), F32), pltpu.VMEM((NSA_SEL_SPAN * qt, rep * qt), F32),
                        pltpu.VMEM((LANES - dh, qt), F32)],
        compiler_params=_params(("parallel", "parallel", "parallel")),
        name="nsa_attn",
    )(q_t, kcmp, vcmp_t, kas, vas_t, kaw, vaw_t, gates_t, ov_t)


def _nsa_overlap_t(s):
    n_cmp = (s - NSA_CMP_BLOCK) // NSA_CMP_STRIDE + 1
    n_slc = s // NSA_SEL_BLOCK
    cs = np.arange(n_cmp) * NSA_CMP_STRIDE
    ss = np.arange(n_slc) * NSA_SEL_BLOCK
    ov = np.clip(np.minimum(cs[:, None] + NSA_CMP_BLOCK, ss[None, :] + NSA_SEL_BLOCK)
                 - np.maximum(cs[:, None], ss[None, :]), 0, None) / NSA_CMP_BLOCK
    out = np.zeros((LANES, s // NSA_CMP_STRIDE), np.float32)
    out[:n_slc, :n_cmp] = ov.T
    return jnp.asarray(out, BF16)


def _mm_res_t_body(at_ref, w_ref, x_ref, o_ref):
    groups, tiles, kw, qt = at_ref.shape[1:]
    for tt in range(tiles):
        rows = slice(tt * qt, (tt + 1) * qt)
        y = x_ref[0, rows, :]
        for gi in range(groups):
            y = y + _dot_tn(at_ref[0, gi, tt], w_ref[gi * kw:(gi + 1) * kw, :])
        o_ref[0, rows, :] = y


def _mm_res_t(a_t, w, x, ts=ROW_TILE):
    b, groups, nkt, kw, qt = a_t.shape
    s, n = nkt * qt, w.shape[1]
    tpb = ts // qt
    return pl.pallas_call(
        _mm_res_t_body,
        out_shape=jax.ShapeDtypeStruct((b, s, n), F32),
        grid=(b, s // ts),
        in_specs=[pl.BlockSpec((1, groups, tpb, kw, qt), lambda bi, i: (bi, 0, i, 0, 0)),
                  _const_spec(w.shape), pl.BlockSpec((1, ts, n), lambda bi, i: (bi, i, 0))],
        out_specs=pl.BlockSpec((1, ts, n), lambda bi, i: (bi, i, 0)),
        compiler_params=_params(("parallel", "parallel")),
        name="mm_res_t",
    )(a_t, w, x)


def _nsa_mixer(x, g, w_in, cmp_pos, cmp_w1, cmp_w2, w_o):
    b, s, d = x.shape
    hh, gg, rep, dh = NSA_HEADS, NSA_KV_HEADS, NSA_REP, NSA_HEAD_DIM
    kvw = gg * dh
    assert s % (NSA_SEL_SPAN * NSA_QT) == 0 and s % ROW_TILE == 0
    assert s // NSA_SEL_BLOCK <= LANES - dh and (s // NSA_SEL_BLOCK) % SUBLANES == 0
    wb = w_in.astype(BF16)
    parts = [wb[:, d + i * kvw:d + (i + 1) * kvw].reshape(d, gg, dh) for i in range(6)]
    zeros = jnp.zeros((d, gg, LANES - dh), BF16)
    padded = lambda p: jnp.concatenate([p, zeros], axis=-1).reshape(d, gg * LANES)
    w_k = jnp.concatenate([padded(parts[2]), padded(parts[4])], axis=-1)
    w_vt = jnp.concatenate([padded(parts[3]), padded(parts[5])], axis=-1).T
    w_c = jnp.concatenate([parts[0], parts[1]], axis=-1).reshape(d, gg * 2 * dh)
    w_g = wb[:, d + 6 * kvw:].reshape(d, gg, 3 * rep)
    w_gt = jnp.concatenate([w_g, jnp.zeros((d, gg, NSA_GATE_ROWS - 3 * rep), BF16)], axis=-1)
    w_gt = w_gt.reshape(d, gg * NSA_GATE_ROWS).T
    q_t, kas, kaw, vas_t, vaw_t, c, gates_t = _nsa_in(x, g, wb[:, :d].T, w_k, w_vt, w_c, w_gt)

    k_cmp, v_cmp_t = _nsa_compress(c, cmp_pos, cmp_w1, cmp_w2)
    o_t = _nsa_attn(q_t, k_cmp, v_cmp_t, kas, vas_t, kaw, vaw_t, gates_t, _nsa_overlap_t(s))
    return _mm_res_t(o_t, w_o.astype(BF16), x)


def kernel(x, mem, ffn1_norm, ffn1_w_in, ffn1_w_out, mix_norm, xattn_norm, mem_norm, xattn_w_q, xattn_w_kv, xattn_w_o, ffn2_norm, ffn2_w_in, ffn2_w_out, pool_w, pool_b, pool_scale, nsa_w_in, nsa_cmp_pos, nsa_cmp_w1, nsa_cmp_w2, nsa_w_o, gla_w_in, gla_w_gate_up, gla_b_gate, gla_norm, gla_w_o, conv_w_in, conv_b_in, conv_dw, conv_b_dw, conv_ln_g, conv_ln_b, conv_w_out, conv_b_out, final_norm):
    b, s, d = x.shape
    n_mem = mem.shape[1]
    depth = ffn1_norm.shape[0]
    n_mixers = 4
    flat = lambda a: a.reshape(b * s, d)
    cube = lambda a: a.reshape(b, s, d)
    mem2 = mem.reshape(b * n_mem, d)
    ffn1_w_in, ffn2_w_in, xattn_w_q, xattn_w_kv, xattn_w_o = (
        w.astype(BF16) for w in (ffn1_w_in, ffn2_w_in, xattn_w_q, xattn_w_kv, xattn_w_o))
    mem_kv = _rms_mm_layers(mem2, mem_norm, xattn_w_kv, BF16, tm=b * n_mem).reshape(depth, b, n_mem, 2 * d)
    for i in range(depth):
        m, j = i % n_mixers, i // n_mixers
        x = cube(_ffn(flat(x), ffn1_norm[i], ffn1_w_in, ffn1_w_out, i))
        if m == 0:
            x = _pool_mixer(x, mix_norm[i], pool_w[j].astype(BF16), pool_b[j], pool_scale[j])
        elif m == 1:
            x = _nsa_mixer(x, mix_norm[i], nsa_w_in[j], nsa_cmp_pos[j], nsa_cmp_w1[j], nsa_cmp_w2[j], nsa_w_o[j])
        elif m == 2:
            x = _gla_mixer(x, mix_norm[i], gla_w_in[j], gla_w_gate_up[j], gla_b_gate[j], gla_norm[j], gla_w_o[j])
        else:
            u = _conv_in(flat(x), mix_norm[i], conv_w_in[j].astype(BF16), conv_b_in[j])
            x = _conv_out(cube(u), x, conv_dw[j], conv_b_dw[j], conv_ln_g[j], conv_ln_b[j],
                          conv_w_out[j].astype(BF16), conv_b_out[j])
        x = _xattn(x, xattn_norm[i], xattn_w_q, mem_kv, xattn_w_o, i)
        final_g = final_norm if i == depth - 1 else None
        x = cube(_ffn(flat(x), ffn2_norm[i], ffn2_w_in, ffn2_w_out, i, final_g=final_g))
    return x
```
